```python
import math
import jax, jax.numpy as jnp
from jax import lax
import numpy as np

D_MODEL = 2048
BATCH = 4
SEQ = 2048
DEPTH = 2
DEC_BATCH = 128
DEC_SEQ = 8
PAST_LEN = 16384
PAGE_SIZE = 128

HEAD_DIM = 128
H_A = 6
H_B = 6
H_C = 4
W_A = H_A * HEAD_DIM
W_B = H_B * HEAD_DIM
W_C = H_C * HEAD_DIM
MIX = W_A + W_B + W_C
N_HEADS_ALL = H_A + H_B + H_C
N_META = 16
CONV_W = 4
CHUNK_A = 64
CHUNK_B = 64
CHUNK_C = 16
EPS = 1e-6
NEG_BIG = -1e30
EXP_CLIP = 60.0
SPLIT_SIZES = (W_A, W_A, W_A, W_A, H_A, H_A,
               W_B, W_B, W_B, H_B, H_B,
               W_C, W_C, W_C,
               MIX)
SPLIT_POINTS = tuple(int(s) for s in np.cumsum(SPLIT_SIZES)[:-1])
N_IN = int(sum(SPLIT_SIZES))

kernel_name = 'hymba_mlstm_gdn_hgrn2_step'


def _rmsnorm(x, w):
    xf = x.astype(jnp.float32)
    y = xf * lax.rsqrt(jnp.mean(xf * xf, axis=-1, keepdims=True) + EPS)
    return (y * w.astype(jnp.float32)).astype(x.dtype)


def _to_chunks(x, c):
    b, l = x.shape[:2]
    x = x.reshape((b, l // c, c) + x.shape[2:])
    return jnp.swapaxes(jnp.moveaxis(x, 1, 0), 2, 3)


def _from_chunks(x):
    nc, b, h, c = x.shape[:4]
    x = jnp.moveaxis(jnp.swapaxes(x, 2, 3), 0, 1)
    return x.reshape((b, nc * c) + x.shape[3:])


def _mlstm_segment(q, k, v, i_pre, f_pre, state, chunk):
    c = math.gcd(q.shape[1], chunk)
    qc = _to_chunks(q, c)
    kc = _to_chunks(k * (HEAD_DIM ** -0.5), c)
    vc = _to_chunks(v, c)
    lic = _to_chunks(i_pre, c)
    lfc = _to_chunks(jax.nn.log_sigmoid(f_pre), c)
    causal = jnp.tril(jnp.ones((c, c), dtype=bool))

    def step(carry, inp):
        C, n, m = carry
        q_, k_, v_, li, lf = inp
        F = jnp.cumsum(lf, axis=-1)
        Dm = jnp.where(causal, F[..., :, None] - F[..., None, :] + li[..., None, :], NEG_BIG)
        inter = F + m[..., None]
        mt = jnp.maximum(inter, jnp.max(Dm, axis=-1))
        a = jnp.exp(inter - mt)
        P = jnp.exp(Dm - mt[..., None]) * jnp.einsum('bhtd,bhsd->bhts', q_, k_)
        num = a[..., None] * jnp.einsum('bhtd,bhde->bhte', q_, C) + jnp.einsum('bhts,bhse->bhte', P, v_)
        den = a * jnp.einsum('bhtd,bhd->bht', q_, n) + jnp.sum(P, axis=-1)
        h = num / jnp.maximum(jnp.abs(den), jnp.exp(jnp.minimum(-mt, EXP_CLIP)))[..., None]
        m_new = mt[..., -1]
        a_s = jnp.exp(F[..., -1] + m - m_new)
        w = jnp.exp(F[..., -1:] - F + li - m_new[..., None])
        C_new = a_s[..., None, None] * C + jnp.einsum('bhs,bhsd,bhse->bhde', w, k_, v_)
        n_new = a_s[..., None] * n + jnp.einsum('bhs,bhsd->bhd', w, k_)
        return (C_new, n_new, m_new), h

    new_state, h = lax.scan(step, state, (qc, kc, vc, lic, lfc))
    return _from_chunks(h), new_state


def _gdn_segment(q, k, v, g, beta, S0, chunk):
    c = math.gcd(q.shape[1], chunk)
    qc, kc, vc = _to_chunks(q, c), _to_chunks(k, c), _to_chunks(v, c)
    gc, bc = _to_chunks(g, c), _to_chunks(beta, c)
    G = jnp.cumsum(gc, axis=-1)
    incl = jnp.tril(jnp.ones((c, c), dtype=bool))
    strict = jnp.tril(jnp.ones((c, c), dtype=bool), -1)
    decay = jnp.exp(jnp.where(incl, G[..., :, None] - G[..., None, :], NEG_BIG))
    A = jnp.where(strict, bc[..., None] * decay * jnp.einsum('nbhtd,nbhsd->nbhts', kc, kc), 0.0)
    M = A + jnp.eye(c, dtype=A.dtype)
    rhs = jnp.concatenate([bc[..., None] * vc, (bc * jnp.exp(G))[..., None] * kc], axis=-1)
    sol = lax.linalg.triangular_solve(M, rhs, left_side=True, lower=True)
    U, Wk = sol[..., :HEAD_DIM], sol[..., HEAD_DIM:]
    Aqk = decay * jnp.einsum('nbhtd,nbhsd->nbhts', qc, kc)

    def step(S, inp):
        q_, k_, U_, Wk_, Aqk_, G_ = inp
        W = U_ - jnp.einsum('bhtd,bhde->bhte', Wk_, S)
        o = jnp.exp(G_)[..., None] * jnp.einsum('bhtd,bhde->bhte', q_, S) + jnp.einsum('bhts,bhse->bhte', Aqk_, W)
        Ge = G_[..., -1:]
        S_new = jnp.exp(Ge)[..., None] * S + jnp.einsum('bhsd,bhse->bhde', k_ * jnp.exp(Ge - G_)[..., None], W)
        return S_new, o

    S_new, o = lax.scan(step, S0, (qc, kc, U, Wk, Aqk, G))
    return _from_chunks(o), S_new


def _hgrn_segment(q, k, v, logf, S0, chunk):
    c = math.gcd(q.shape[1], chunk)
    qc, kc, vc = _to_chunks(q, c), _to_chunks(k, c), _to_chunks(v, c)
    bcum = jnp.cumsum(_to_chunks(logf, c), axis=-2)
    incl = jnp.tril(jnp.ones((c, c), dtype=bool))[..., None]

    def step(S, inp):
        q_, k_, v_, b_ = inp
        diff = jnp.where(incl, b_[..., :, None, :] - b_[..., None, :, :], NEG_BIG)
        A = jnp.einsum('bhtd,bhsd,bhtsd->bhts', q_, k_, jnp.exp(diff))
        o = jnp.einsum('bhtd,bhde->bhte', q_ * jnp.exp(b_), S) + jnp.einsum('bhts,bhse->bhte', A, v_)
        be = b_[..., -1:, :]
        S_new = jnp.exp(be[..., 0, :])[..., None] * S + jnp.einsum('bhsd,bhse->bhde', k_ * jnp.exp(be - b_), v_)
        return S_new, o

    S_new, o = lax.scan(step, S0, (qc, kc, vc, bcum))
    return _from_chunks(o), S_new


def _l2norm(x):
    return x * lax.rsqrt(jnp.sum(x * x, axis=-1, keepdims=True) + EPS)


def _layer(x, state, seg_lens, norm_w, w_in, mlstm_b, A_log, dt_bias, conv_w, lb, out_norm_w, w_out):
    f32 = jnp.float32
    C_a, n_a, m_a, S_b, conv_b, S_c = state
    bsz, L, _ = x.shape
    h = _rmsnorm(x, norm_w)
    proj = jnp.einsum('bld,de->ble', h, w_in)
    (aq, ak, av, ao, ai, af, bq, bk, bv, ba, bb, cq, cf, ci, z) = jnp.split(proj, SPLIT_POINTS, axis=-1)

    def heads(t, nh):
        return t.astype(f32).reshape(bsz, L, nh, HEAD_DIM)

    mlstm_b = mlstm_b.astype(f32)
    qa, ka, va = heads(aq, H_A), heads(ak, H_A), heads(av, H_A)
    i_pre = ai.astype(f32) + mlstm_b[0]
    f_pre = af.astype(f32) + mlstm_b[1]
    u = jnp.concatenate([bq, bk, bv], axis=-1)
    u_ext = jnp.concatenate([conv_b.astype(u.dtype), u], axis=1)
    conv = u_ext[:, 0:L].astype(f32) * conv_w[0].astype(f32)
    for j in range(1, CONV_W):
        conv = conv + u_ext[:, j:j + L].astype(f32) * conv_w[j].astype(f32)
    conv = jax.nn.silu(conv)
    new_conv = u_ext[:, -(CONV_W - 1):]
    qb = _l2norm(heads(conv[..., :W_B], H_B)) * (HEAD_DIM ** -0.5)
    kb = _l2norm(heads(conv[..., W_B:2 * W_B], H_B))
    vb = heads(conv[..., 2 * W_B:], H_B)
    g_b = -jnp.exp(A_log.astype(f32)) * jax.nn.softplus(ba.astype(f32) + dt_bias.astype(f32))
    beta_b = jax.nn.sigmoid(bb.astype(f32))
    lbh = lb.reshape(H_C, HEAD_DIM)
    cfh = heads(cf, H_C)
    logf_c = jax.nn.log_sigmoid(cfh) + jnp.log1p(lbh * jnp.exp(jnp.minimum(-cfh, EXP_CLIP)))
    kc = (1.0 - lbh) * jax.nn.sigmoid(-cfh)
    qc = jax.nn.silu(heads(cq, H_C))
    vc = heads(ci, H_C)

    st_a = (C_a.astype(f32), n_a.astype(f32), m_a.astype(f32))
    st_b = S_b.astype(f32)
    st_c = S_c.astype(f32)
    outs_a, outs_b, outs_c = [], [], []
    off = 0
    for ls in seg_lens:
        sl = slice(off, off + ls)
        o, st_a = _mlstm_segment(qa[:, sl], ka[:, sl], va[:, sl], i_pre[:, sl], f_pre[:, sl], st_a, CHUNK_A)
        outs_a.append(o)
        o, st_b = _gdn_segment(qb[:, sl], kb[:, sl], vb[:, sl], g_b[:, sl], beta_b[:, sl], st_b, CHUNK_B)
        outs_b.append(o)
        o, st_c = _hgrn_segment(qc[:, sl], kc[:, sl], vc[:, sl], logf_c[:, sl], st_c, CHUNK_C)
        outs_c.append(o)
        off += ls
    ha = jnp.concatenate(outs_a, axis=1) * jax.nn.sigmoid(heads(ao, H_A))
    hb = jnp.concatenate(outs_b, axis=1)
    hc = jnp.concatenate(outs_c, axis=1)
    hcat = jnp.concatenate([ha, hb, hc], axis=2)
    hcat = hcat * lax.rsqrt(jnp.mean(hcat * hcat, axis=-1, keepdims=True) + EPS)
    y = hcat.reshape(bsz, L, MIX) * out_norm_w.astype(f32) * jax.nn.silu(z.astype(f32))
    out = jnp.einsum('ble,ed->bld', y.astype(x.dtype), w_out)
    return x + out, (st_a[0], st_a[1], st_a[2], st_b, new_conv, st_c)


def _zero_state(bsz, dtype):
    f32 = jnp.float32
    return (jnp.zeros((bsz, H_A, HEAD_DIM, HEAD_DIM), f32), jnp.zeros((bsz, H_A, HEAD_DIM), f32),
            jnp.zeros((bsz, H_A), f32), jnp.zeros((bsz, H_B, HEAD_DIM, HEAD_DIM), f32),
            jnp.zeros((bsz, CONV_W - 1, 3 * W_B), dtype), jnp.zeros((bsz, H_C, HEAD_DIM, HEAD_DIM), f32))


def setup_inputs(seed: int = 0) -> dict:
    key = jax.random.key(seed)
    ks = jax.random.split(key, 24)
    f32 = jnp.float32

    def nrm(k, shape, s):
        return jax.random.normal(k, shape, f32) * s

    dt = jnp.exp(jax.random.uniform(ks[13], (DEPTH, H_B), f32, math.log(1e-3), math.log(1e-1)))
    mlstm_gate_b = jnp.stack([nrm(ks[10], (DEPTH, H_A), 0.1),
                              jax.random.uniform(ks[11], (DEPTH, H_A), f32, 3.0, 6.0)], axis=1)
    return {
        'x_prompt': nrm(ks[0], (BATCH, SEQ, D_MODEL), 1.0),
        'x_sample': nrm(ks[1], (DEC_BATCH, DEC_SEQ, D_MODEL), 1.0),
        'state_mlstm_C': nrm(ks[2], (DEPTH, DEC_BATCH, H_A, HEAD_DIM, HEAD_DIM), HEAD_DIM ** -0.5),
        'state_mlstm_n': nrm(ks[3], (DEPTH, DEC_BATCH, H_A, HEAD_DIM), HEAD_DIM ** -0.5),
        'state_mlstm_m': nrm(ks[4], (DEPTH, DEC_BATCH, H_A), 1.0),
        'state_gdn_S': nrm(ks[5], (DEPTH, DEC_BATCH, H_B, HEAD_DIM, HEAD_DIM), 0.5),
        'state_gdn_conv': nrm(ks[6], (DEPTH, DEC_BATCH, CONV_W - 1, 3 * W_B), 1.0),
        'state_hgrn_S': nrm(ks[7], (DEPTH, DEC_BATCH, H_C, HEAD_DIM, HEAD_DIM), 1.0),
        'meta_tokens': nrm(ks[8], (N_META, D_MODEL), 1.0),
        'norm_w': 1.0 + nrm(ks[9], (DEPTH, D_MODEL), 0.1),
        'w_in': nrm(ks[12], (DEPTH, D_MODEL, N_IN), D_MODEL ** -0.5),
        'mlstm_gate_b': mlstm_gate_b,
        'gdn_A_log': jnp.log(jax.random.uniform(ks[14], (DEPTH, H_B), f32, 1.0, 16.0)),
        'gdn_dt_bias': dt + jnp.log(-jnp.expm1(-dt)),
        'gdn_conv_w': nrm(ks[15], (DEPTH, CONV_W, 3 * W_B), CONV_W ** -0.5),
        'hgrn_lower_bounds': nrm(ks[16], (DEPTH, W_C), 1.0),
        'out_norm_w': 1.0 + nrm(ks[17], (DEPTH, MIX), 0.1),
        'w_out': nrm(ks[18], (DEPTH, MIX, D_MODEL), MIX ** -0.5),
        'final_norm_w': 1.0 + nrm(ks[19], (D_MODEL,), 0.1),
    }


def reference(x_prompt, x_sample, state_mlstm_C, state_mlstm_n, state_mlstm_m, state_gdn_S, state_gdn_conv,
              state_hgrn_S, meta_tokens, norm_w, w_in, mlstm_gate_b, gdn_A_log, gdn_dt_bias, gdn_conv_w,
              hgrn_lower_bounds, out_norm_w, w_out, final_norm_w):
    bsz, seq_len, _ = x_prompt.shape
    dec_seq = x_sample.shape[1]
    sm = jax.nn.softmax(hgrn_lower_bounds.astype(jnp.float32), axis=0)
    lb_all = jnp.cumsum(sm, axis=0) - sm[0]
    meta = jnp.broadcast_to(meta_tokens.astype(x_prompt.dtype)[None], (bsz, N_META, D_MODEL))
    xp = jnp.concatenate([meta, x_prompt], axis=1)
    xs = x_sample
    new_p = [[] for _ in range(6)]
    new_s = [[] for _ in range(6)]
    for l in range(DEPTH):
        params = (norm_w[l], w_in[l], mlstm_gate_b[l], gdn_A_log[l], gdn_dt_bias[l], gdn_conv_w[l], lb_all[l],
                  out_norm_w[l], w_out[l])
        xp, st_p = _layer(xp, _zero_state(bsz, xp.dtype), (N_META, seq_len), *params)
        past = (state_mlstm_C[l], state_mlstm_n[l], state_mlstm_m[l], state_gdn_S[l], state_gdn_conv[l],
                state_hgrn_S[l])
        xs, st_s = _layer(xs, past, (dec_seq,), *params)
        for j in range(6):
            new_p[j].append(st_p[j])
            new_s[j].append(st_s[j])
    y_prompt = _rmsnorm(xp[:, N_META:], final_norm_w)
    y_sample = _rmsnorm(xs, final_norm_w)
    mC_p, mn_p, mm_p, gS_p, gc_p, hS_p = [jnp.stack(a, axis=0) for a in new_p]
    mC_s, mn_s, mm_s, gS_s, gc_s, hS_s = [jnp.stack(a, axis=0) for a in new_s]
    return (y_prompt, y_sample, mC_p, mn_p, mm_p, gS_p, gc_p, hS_p, mC_s, mn_s, mm_s, gS_s, gc_s, hS_s)
```

```python
import functools
import math

import jax
import jax.numpy as jnp
from jax import lax
from jax.experimental import pallas as pl
from jax.experimental.pallas import tpu as pltpu

F32 = jnp.float32
BF16 = jnp.bfloat16
HIGHEST = lax.Precision.HIGHEST

D_MODEL = 2048
HEAD_DIM = 128
H_A, H_B, H_C = 6, 6, 4
W_A, W_B, W_C = H_A * HEAD_DIM, H_B * HEAD_DIM, H_C * HEAD_DIM
MIX = W_A + W_B + W_C
N_META = 16
CONV_W = 4
EPS = 1e-6
NEG_BIG = -1e30
EXP_CLIP = 60.0
QK_SCALE = HEAD_DIM ** -0.5

TILE_R = 128
LANES = 128
CONV_PAD = 8

OFF_CQ, OFF_CF, OFF_CI = 0, W_C, 2 * W_C
OFF_AQ = 3 * W_C
OFF_AK, OFF_AV, OFF_AO = OFF_AQ + W_A, OFF_AQ + 2 * W_A, OFF_AQ + 3 * W_A
OFF_BQ = OFF_AQ + 4 * W_A
OFF_BK, OFF_BV = OFF_BQ + W_B, OFF_BQ + 2 * W_B
OFF_ZA = OFF_BQ + 3 * W_B
OFF_ZB = OFF_ZA + W_A
OFF_G = OFF_ZB + W_B
OFF_ZC = OFF_G + 2 * LANES
N_PACK = OFF_ZC + W_C
G_AI, G_AF, G_BA, G_BB = 0, H_A, 2 * H_A, 2 * H_A + H_B

VMEM_LIMIT = 56 * 1024 * 1024


def _sigmoid(x):
    return 1.0 / (1.0 + jnp.exp(-x))


def _silu(x):
    return x * _sigmoid(x)


def _softplus(x):
    return jnp.maximum(x, 0.0) + jnp.log1p(jnp.exp(-jnp.abs(x)))


def _log_sigmoid(x):
    return -_softplus(-x)


def _dot(a, b):
    return jnp.dot(a.astype(BF16), b.astype(BF16), preferred_element_type=F32)


def _dot_nt(a, b):
    return lax.dot_general(a.astype(BF16), b.astype(BF16), (((1,), (1,)), ((), ())),
                           preferred_element_type=F32)


def _dot_exact(sel, x):
    return jnp.dot(sel, x, precision=HIGHEST, preferred_element_type=F32)


def _log2(n):
    k = int(math.log2(n))
    assert (1 << k) == n
    return k


def _seq_masks(nseq, c):
    r = nseq * c
    row = lax.broadcasted_iota(jnp.int32, (r, r), 0)
    col = lax.broadcasted_iota(jnp.int32, (r, r), 1)
    if nseq == 1:
        same = row >= 0
    else:
        k = _log2(c)
        same = (row >> k) == (col >> k)
    return row, col, same


def _head_norm_gate(h, onw, z):
    hn = h * lax.rsqrt(jnp.mean(h * h, axis=1, keepdims=True) + EPS)
    return hn * onw * _silu(z)


def _rows(x, i, c):
    return x[i * c:(i + 1) * c]


def _cat_rows(parts):
    return parts[0] if len(parts) == 1 else jnp.concatenate(parts, axis=0)


def _inproj_kernel(x_ref, nw_ref, w_ref, o_ref, h_scr):
    @pl.when(pl.program_id(1) == 0)
    def _():
        x = x_ref[...]
        ms = jnp.mean(x * x, axis=1, keepdims=True)
        h_scr[...] = (x * lax.rsqrt(ms + EPS) * nw_ref[...]).astype(BF16)

    o_ref[...] = jnp.dot(h_scr[...], w_ref[...], preferred_element_type=F32)


def _inproj(x, norm_w3, w_pack, layer, tm, tn):
    t = x.shape[0]
    return pl.pallas_call(
        _inproj_kernel,
        grid=(t // tm, N_PACK // tn),
        in_specs=[
            pl.BlockSpec((tm, D_MODEL), lambda i, j: (i, 0)),
            pl.BlockSpec((None, 1, D_MODEL), lambda i, j: (layer, 0, 0)),
            pl.BlockSpec((None, D_MODEL, tn), lambda i, j: (layer, 0, j)),
        ],
        out_specs=pl.BlockSpec((tm, tn), lambda i, j: (i, j)),
        out_shape=jax.ShapeDtypeStruct((t, N_PACK), F32),
        scratch_shapes=[pltpu.VMEM((tm, D_MODEL), BF16)],
        compiler_params=pltpu.CompilerParams(
            dimension_semantics=("parallel", "arbitrary"), vmem_limit_bytes=VMEM_LIMIT),
        name="inproj",
    )(x, norm_w3, w_pack)


def _outproj_kernel(ya_ref, yb_ref, yc_ref, x_ref, w_ref, nw_ref, o_ref, *, final):
    acc = jnp.dot(ya_ref[...], w_ref[0:W_A, :], preferred_element_type=F32)
    acc = acc + jnp.dot(yb_ref[...], w_ref[W_A:W_A + W_B, :], preferred_element_type=F32)
    acc = acc + jnp.dot(yc_ref[...], w_ref[W_A + W_B:MIX, :], preferred_element_type=F32)
    xn = x_ref[...] + acc
    if final:
        ms = jnp.mean(xn * xn, axis=1, keepdims=True)
        xn = xn * lax.rsqrt(ms + EPS) * nw_ref[...]
    o_ref[...] = xn


def _outproj(ya, yb, yc, x, w_out, layer, final_nw, tm, row_blk0, n_blk, final):
    rows = lambda i: (row_blk0 + i, 0)
    return pl.pallas_call(
        functools.partial(_outproj_kernel, final=final),
        grid=(n_blk,),
        in_specs=[
            pl.BlockSpec((tm, W_A), rows),
            pl.BlockSpec((tm, W_B), rows),
            pl.BlockSpec((tm, W_C), rows),
            pl.BlockSpec((tm, D_MODEL), rows),
            pl.BlockSpec((None, MIX, D_MODEL), lambda i: (layer, 0, 0)),
            pl.BlockSpec((1, D_MODEL), lambda i: (0, 0)),
        ],
        out_specs=pl.BlockSpec((tm, D_MODEL), lambda i: (i, 0)),
        out_shape=jax.ShapeDtypeStruct((n_blk * tm, D_MODEL), F32),
        compiler_params=pltpu.CompilerParams(
            dimension_semantics=("parallel",), vmem_limit_bytes=VMEM_LIMIT),
        name="outproj_final" if final else "outproj",
    )(ya, yb, yc, x, w_out, final_nw)


def _mlstm_tile(q_ref, k_ref, v_ref, o_ref, z_ref, g_ref, gb_ref, onw_ref, y_ref,
                m_exp, valid, get_c, get_n, put_c, put_n, put_m, nseq, c):
    r = nseq * c
    row, col, same = _seq_masks(nseq, c)
    incl = same & (col <= row)
    g = g_ref[...] + gb_ref[...]
    li = g
    lf = _log_sigmoid(g)
    if valid is not None:
        li = jnp.where(valid, li, NEG_BIG)
        lf = jnp.where(valid, lf, 0.0)
    fcum = _dot_exact(incl.astype(F32), lf)
    if nseq == 1:
        flast = jnp.broadcast_to(fcum[r - 1:r, :], (r, LANES))
    else:
        flast = _dot_exact(same.astype(F32), lf)
    fcum_t = fcum.T
    li_t = li.T
    lane = lax.broadcasted_iota(jnp.int32, (r, LANES), 1)
    m_new_all = jnp.zeros((r, LANES), F32)

    for h in range(H_A):
        hs = slice(h * HEAD_DIM, (h + 1) * HEAD_DIM)
        fc = fcum[:, G_AF + h:G_AF + h + 1]
        fr = fcum_t[G_AF + h:G_AF + h + 1, :]
        lic = li[:, G_AI + h:G_AI + h + 1]
        lir = li_t[G_AI + h:G_AI + h + 1, :]
        fl = flast[:, G_AF + h:G_AF + h + 1]
        mcol = m_exp[:, h:h + 1]
        dm = jnp.where(incl, fc - fr + lir, NEG_BIG)
        inter = fc + mcol
        mt = jnp.maximum(inter, jnp.max(dm, axis=1, keepdims=True))
        if nseq == 1:
            m_new = jnp.broadcast_to(mt[r - 1:r, :], (r, 1))
        else:
            dl = jnp.where(same, fl - fr + lir, NEG_BIG)
            m_new = jnp.maximum(fl + mcol, jnp.max(dl, axis=1, keepdims=True))
        a = jnp.exp(inter - mt)
        qh = q_ref[:, hs]
        kh = k_ref[:, hs] * QK_SCALE
        vh = v_ref[:, hs]
        p = jnp.exp(dm - mt) * _dot_nt(qh, kh)
        pv = _dot(p, vh)
        psum = jnp.sum(p, axis=1, keepdims=True)
        qc = _cat_rows([_dot(_rows(qh, i, c), get_c(i, h)) for i in range(nseq)])
        qn = _cat_rows([jnp.sum(_rows(qh, i, c) * get_n(i, h), axis=1, keepdims=True)
                        for i in range(nseq)])
        num = a * qc + pv
        den = a * qn + psum
        hh = num / jnp.maximum(jnp.abs(den), jnp.exp(jnp.minimum(-mt, EXP_CLIP)))
        hh = hh * _sigmoid(o_ref[:, hs])
        y_ref[:, hs] = _head_norm_gate(hh, onw_ref[:, hs], z_ref[:, hs]).astype(BF16)

        w = jnp.exp(fl - fc + lic - m_new)
        a_s = jnp.exp(fl + mcol - m_new)
        wv = w * vh
        wk = w * kh
        kh_t = kh.T.astype(BF16)
        for i in range(nseq):
            a_i = a_s[i * c:i * c + 1, :]
            if nseq == 1:
                upd = jnp.dot(kh_t, wv.astype(BF16), preferred_element_type=F32)
            else:
                seq_rows = (lax.broadcasted_iota(jnp.int32, (r, 1), 0) >> _log2(c)) == i
                upd = jnp.dot(kh_t, jnp.where(seq_rows, wv, 0.0).astype(BF16),
                              preferred_element_type=F32)
            put_c(i, h, a_i * get_c(i, h) + upd)
            put_n(i, h, a_i * get_n(i, h) + jnp.sum(_rows(wk, i, c), axis=0, keepdims=True))
        m_new_all = jnp.where(lane == h, m_new, m_new_all)
    put_m(m_new_all)


def _mlstm_prompt_kernel(q_ref, k_ref, v_ref, o_ref, z_ref, g_ref, gb_ref, onw_ref,
                         y_ref, c_out, n_out, m_out, c_scr, n_scr, m_scr):
    j = pl.program_id(1)

    @pl.when(j == 0)
    def _():
        c_scr[...] = jnp.zeros_like(c_scr)
        n_scr[...] = jnp.zeros_like(n_scr)
        m_scr[...] = jnp.zeros_like(m_scr)

    tok = lax.broadcasted_iota(jnp.int32, (TILE_R, LANES), 0) + j * TILE_R
    valid = tok >= TILE_R - N_META
    m_exp = jnp.broadcast_to(m_scr[0:1, :], (TILE_R, LANES))

    def put_c(i, h, val):
        c_scr[h] = val

    def put_n(i, h, val):
        n_scr[h:h + 1, :] = val

    def put_m(val):
        m_scr[...] = val[0:8, :]

    _mlstm_tile(q_ref, k_ref, v_ref, o_ref, z_ref, g_ref, gb_ref, onw_ref, y_ref,
                m_exp, valid, lambda i, h: c_scr[h], lambda i, h: n_scr[h:h + 1, :],
                put_c, put_n, put_m, 1, TILE_R)

    @pl.when(j == pl.num_programs(1) - 1)
    def _():
        c_out[...] = c_scr[...]
        n_out[...] = n_scr[0:H_A, :]
        m_out[...] = m_scr[0:1, :]


def _mlstm_sample_kernel(q_ref, k_ref, v_ref, o_ref, z_ref, g_ref, gb_ref, onw_ref,
                         c_in, n_in, m_in, y_ref, c_out, n_out, m_out, *, c):
    nseq = TILE_R // c

    def put_c(i, h, val):
        c_out[i, h] = val

    def put_n(i, h, val):
        n_out[i, h:h + 1, :] = val

    def put_m(val):
        m_out[...] = val

    _mlstm_tile(q_ref, k_ref, v_ref, o_ref, z_ref, g_ref, gb_ref, onw_ref, y_ref,
                m_in[...], None, lambda i, h: c_in[i, h], lambda i, h: n_in[i, h:h + 1, :],
                put_c, put_n, put_m, nseq, c)


def _col_spec(width, off, rows):
    assert off % width == 0
    blk = off // width
    return pl.BlockSpec((TILE_R, width), lambda *ids: (rows(*ids), blk))


def _param_spec(width, off, layer):
    assert off % width == 0
    blk = off // width
    return pl.BlockSpec((None, 1, width), lambda *ids: (layer, 0, blk))


def _mlstm_prompt(proj, gate_bias, onw3, layer, bsz, n_chunk, rows):
    t = proj.shape[0]
    in_specs = [_col_spec(W_A, off, rows) for off in (OFF_AQ, OFF_AK, OFF_AV, OFF_AO, OFF_ZA)]
    in_specs += [_col_spec(LANES, OFF_G, rows), _param_spec(LANES, 0, layer),
                 _param_spec(W_A, 0, layer)]
    return pl.pallas_call(
        _mlstm_prompt_kernel,
        grid=(bsz, n_chunk),
        in_specs=in_specs,
        out_specs=[
            pl.BlockSpec((TILE_R, W_A), lambda b, j: (rows(b, j), 0)),
            pl.BlockSpec((None, H_A, HEAD_DIM, HEAD_DIM), lambda b, j: (b, 0, 0, 0)),
            pl.BlockSpec((None, H_A, HEAD_DIM), lambda b, j: (b, 0, 0)),
            pl.BlockSpec((None, 1, LANES), lambda b, j: (b, 0, 0)),
        ],
        out_shape=[
            jax.ShapeDtypeStruct((t, W_A), BF16),
            jax.ShapeDtypeStruct((bsz, H_A, HEAD_DIM, HEAD_DIM), F32),
            jax.ShapeDtypeStruct((bsz, H_A, HEAD_DIM), F32),
            jax.ShapeDtypeStruct((bsz, 1, LANES), F32),
        ],
        scratch_shapes=[pltpu.VMEM((H_A, HEAD_DIM, HEAD_DIM), F32),
                        pltpu.VMEM((8, HEAD_DIM), F32), pltpu.VMEM((8, LANES), F32)],
        compiler_params=pltpu.CompilerParams(
            dimension_semantics=("parallel", "arbitrary"), vmem_limit_bytes=VMEM_LIMIT),
        name="mlstm_prompt",
    )(proj, proj, proj, proj, proj, proj, gate_bias, onw3)


def _mlstm_sample(proj, y_prev, gate_bias, onw3, c_state, n_state, m_rows, layer, dec_b, c, rows):
    nseq = TILE_R // c
    in_specs = [_col_spec(W_A, off, rows) for off in (OFF_AQ, OFF_AK, OFF_AV, OFF_AO, OFF_ZA)]
    in_specs += [_col_spec(LANES, OFF_G, rows), _param_spec(LANES, 0, layer),
                 _param_spec(W_A, 0, layer)]
    in_specs += [
        pl.BlockSpec((None, nseq, H_A, HEAD_DIM, HEAD_DIM), lambda i: (layer, i, 0, 0, 0)),
        pl.BlockSpec((None, nseq, H_A, HEAD_DIM), lambda i: (layer, i, 0, 0)),
        pl.BlockSpec((None, TILE_R, LANES), lambda i: (layer, i, 0)),
        pl.BlockSpec(memory_space=pl.ANY),
    ]

    def body(*refs):
        _mlstm_sample_kernel(*refs[:11], *refs[12:], c=c)

    return pl.pallas_call(
        body,
        grid=(dec_b // nseq,),
        in_specs=in_specs,
        out_specs=[
            pl.BlockSpec((TILE_R, W_A), lambda i: (rows(i), 0)),
            pl.BlockSpec((nseq, H_A, HEAD_DIM, HEAD_DIM), lambda i: (i, 0, 0, 0)),
            pl.BlockSpec((nseq, H_A, HEAD_DIM), lambda i: (i, 0, 0)),
            pl.BlockSpec((TILE_R, LANES), lambda i: (i, 0)),
        ],
        out_shape=[
            jax.ShapeDtypeStruct(y_prev.shape, BF16),
            jax.ShapeDtypeStruct((dec_b, H_A, HEAD_DIM, HEAD_DIM), F32),
            jax.ShapeDtypeStruct((dec_b, H_A, HEAD_DIM), F32),
            jax.ShapeDtypeStruct((dec_b * c, LANES), F32),
        ],
        input_output_aliases={11: 0},
        compiler_params=pltpu.CompilerParams(
            dimension_semantics=("parallel",), vmem_limit_bytes=VMEM_LIMIT),
        name="mlstm_sample",
    )(proj, proj, proj, proj, proj, proj, gate_bias, onw3, c_state, n_state, m_rows, y_prev)


def _l2norm(x):
    return x * lax.rsqrt(jnp.sum(x * x, axis=1, keepdims=True) + EPS)


INV_LEAF = 16


def _unit_lower_inverse(a, row, col, c):
    leaf = min(INV_LEAF, c)
    kl = _log2(leaf)
    n = jnp.where((row >> kl) == (col >> kl), -a, 0.0)
    t = jnp.where(row == col, 1.0, 0.0) + n
    p = n
    for _ in range(kl - 1):
        p = _dot(p, p)
        t = t + _dot(p, t)
    w = leaf
    while w < c:
        kw = _log2(w)
        lower_left = (((row >> (kw + 1)) == (col >> (kw + 1)))
                      & (((row >> kw) & 1) == 1) & (((col >> kw) & 1) == 0))
        t = t - _dot(_dot(t, jnp.where(lower_left, a, 0.0)), t)
        w *= 2
    return t


def _gdn_tile(conv_win, cw_ref, z_ref, g_ref, gp_ref, onw_ref, y_ref, get_s, put_s, nseq, c):
    r = nseq * c
    row, col, same = _seq_masks(nseq, c)
    incl = same & (col <= row)
    strict = same & (col < row)
    gp = gp_ref[...]
    graw = g_ref[...]
    gdec = -jnp.exp(gp[1:2, :]) * _softplus(graw + gp[0:1, :])
    beta = _sigmoid(graw)
    gcum = _dot_exact(incl.astype(F32), gdec)
    if nseq == 1:
        glast = jnp.broadcast_to(gcum[r - 1:r, :], (r, LANES))
    else:
        glast = _dot_exact(same.astype(F32), gdec)
    gcum_t = gcum.T

    for h in range(H_B):
        hs = slice(h * HEAD_DIM, (h + 1) * HEAD_DIM)

        def conv(base):
            acc = conv_win(0, slice(base + hs.start, base + hs.stop)) * cw_ref[0:1, base + hs.start:base + hs.stop]
            for jj in range(1, CONV_W):
                acc = acc + (conv_win(jj, slice(base + hs.start, base + hs.stop))
                             * cw_ref[jj:jj + 1, base + hs.start:base + hs.stop])
            return _silu(acc)

        qh = _l2norm(conv(0)) * QK_SCALE
        kh = _l2norm(conv(W_B))
        vh = conv(2 * W_B)
        gc = gcum[:, G_BA + h:G_BA + h + 1]
        gr = gcum_t[G_BA + h:G_BA + h + 1, :]
        gl = glast[:, G_BA + h:G_BA + h + 1]
        bc = beta[:, G_BB + h:G_BB + h + 1]
        decay = jnp.exp(jnp.where(incl, gc - gr, NEG_BIG))
        kk = _dot_nt(kh, kh)
        amat = jnp.where(strict, bc * decay * kk, 0.0)
        rhs = jnp.concatenate([bc * vh, (bc * jnp.exp(gc)) * kh], axis=1)
        sol = _dot(_unit_lower_inverse(amat, row, col, c), rhs)
        u = sol[:, :HEAD_DIM]
        wk = sol[:, HEAD_DIM:]
        aqk = decay * _dot_nt(qh, kh)
        wks = _cat_rows([_dot(_rows(wk, i, c), get_s(i, h)) for i in range(nseq)])
        qs = _cat_rows([_dot(_rows(qh, i, c), get_s(i, h)) for i in range(nseq)])
        wmat = u - wks
        oh = jnp.exp(gc) * qs + _dot(aqk, wmat)
        y_ref[:, hs] = _head_norm_gate(oh, onw_ref[:, hs], z_ref[:, hs]).astype(BF16)

        kd_t = (kh * jnp.exp(gl - gc)).T.astype(BF16)
        e_last = jnp.exp(gl)
        for i in range(nseq):
            if nseq == 1:
                upd = jnp.dot(kd_t, wmat.astype(BF16), preferred_element_type=F32)
            else:
                seq_rows = (lax.broadcasted_iota(jnp.int32, (r, 1), 0) >> _log2(c)) == i
                upd = jnp.dot(kd_t, jnp.where(seq_rows, wmat, 0.0).astype(BF16),
                              preferred_element_type=F32)
            put_s(i, h, e_last[i * c:i * c + 1, :] * get_s(i, h) + upd)


def _gdn_prompt_kernel(q_ref, k_ref, v_ref, z_ref, g_ref, gp_ref, cw_ref, onw_ref,
                       y_ref, s_out, conv_out, s_scr, ext_scr):
    j = pl.program_id(1)

    @pl.when(j == 0)
    def _():
        s_scr[...] = jnp.zeros_like(s_scr)
        ext_scr[0:CONV_PAD, :] = jnp.zeros((CONV_PAD, 3 * W_B), F32)

    ext_scr[CONV_PAD:, 0:W_B] = q_ref[...]
    ext_scr[CONV_PAD:, W_B:2 * W_B] = k_ref[...]
    ext_scr[CONV_PAD:, 2 * W_B:] = v_ref[...]

    def conv_win(jj, cols):
        start = CONV_PAD - (CONV_W - 1) + jj
        return ext_scr[start:start + TILE_R, cols]

    def put_s(i, h, val):
        s_scr[h] = val

    _gdn_tile(conv_win, cw_ref, z_ref, g_ref, gp_ref, onw_ref, y_ref,
              lambda i, h: s_scr[h], put_s, 1, TILE_R)

    @pl.when(j == pl.num_programs(1) - 1)
    def _():
        s_out[...] = s_scr[...]
        conv_out[...] = ext_scr[TILE_R + CONV_PAD - (CONV_W - 1):TILE_R + CONV_PAD, :]

    ext_scr[0:CONV_PAD, :] = ext_scr[TILE_R:TILE_R + CONV_PAD, :]


def _gdn_sample_kernel(q_ref, k_ref, v_ref, z_ref, g_ref, gp_ref, cw_ref, onw_ref,
                       s_in, conv_in, y_ref, s_out, conv_out, ext_scr, *, c):
    nseq = TILE_R // c
    hist = CONV_W - 1
    ext_scr[:, CONV_PAD - hist:CONV_PAD, :] = conv_in[...]
    ext_scr[:, CONV_PAD:, 0:W_B] = q_ref[...].reshape(nseq, c, W_B)
    ext_scr[:, CONV_PAD:, W_B:2 * W_B] = k_ref[...].reshape(nseq, c, W_B)
    ext_scr[:, CONV_PAD:, 2 * W_B:] = v_ref[...].reshape(nseq, c, W_B)

    def conv_win(jj, cols):
        start = CONV_PAD - hist + jj
        return ext_scr[:, start:start + c, cols].reshape(TILE_R, cols.stop - cols.start)

    def put_s(i, h, val):
        s_out[i, h] = val

    _gdn_tile(conv_win, cw_ref, z_ref, g_ref, gp_ref, onw_ref, y_ref,
              lambda i, h: s_in[i, h], put_s, nseq, c)
    conv_out[...] = ext_scr[:, CONV_PAD + c - hist:CONV_PAD + c, :]


def _gdn_prompt(proj, gdn_par, conv_w, onw3, layer, bsz, n_chunk, rows):
    t = proj.shape[0]
    in_specs = [_col_spec(W_B, off, rows) for off in (OFF_BQ, OFF_BK, OFF_BV, OFF_ZB)]
    in_specs += [
        _col_spec(LANES, OFF_G, rows),
        pl.BlockSpec((None, 8, LANES), lambda b, j: (layer, 0, 0)),
        pl.BlockSpec((None, CONV_W, 3 * W_B), lambda b, j: (layer, 0, 0)),
        _param_spec(W_B, W_A, layer),
    ]
    return pl.pallas_call(
        _gdn_prompt_kernel,
        grid=(bsz, n_chunk),
        in_specs=in_specs,
        out_specs=[
            pl.BlockSpec((TILE_R, W_B), lambda b, j: (rows(b, j), 0)),
            pl.BlockSpec((None, H_B, HEAD_DIM, HEAD_DIM), lambda b, j: (b, 0, 0, 0)),
            pl.BlockSpec((None, CONV_W - 1, 3 * W_B), lambda b, j: (b, 0, 0)),
        ],
        out_shape=[
            jax.ShapeDtypeStruct((t, W_B), BF16),
            jax.ShapeDtypeStruct((bsz, H_B, HEAD_DIM, HEAD_DIM), F32),
            jax.ShapeDtypeStruct((bsz, CONV_W - 1, 3 * W_B), F32),
        ],
        scratch_shapes=[pltpu.VMEM((H_B, HEAD_DIM, HEAD_DIM), F32),
                        pltpu.VMEM((TILE_R + CONV_PAD, 3 * W_B), F32)],
        compiler_params=pltpu.CompilerParams(
            dimension_semantics=("parallel", "arbitrary"), vmem_limit_bytes=VMEM_LIMIT),
        name="gdn_prompt",
    )(proj, proj, proj, proj, proj, gdn_par, conv_w, onw3)


def _gdn_sample(proj, y_prev, gdn_par, conv_w, onw3, s_state, conv_state, layer, dec_b, c, rows):
    nseq = TILE_R // c
    in_specs = [_col_spec(W_B, off, rows) for off in (OFF_BQ, OFF_BK, OFF_BV, OFF_ZB)]
    in_specs += [
        _col_spec(LANES, OFF_G, rows),
        pl.BlockSpec((None, 8, LANES), lambda i: (layer, 0, 0)),
        pl.BlockSpec((None, CONV_W, 3 * W_B), lambda i: (layer, 0, 0)),
        _param_spec(W_B, W_A, layer),
        pl.BlockSpec((None, nseq, H_B, HEAD_DIM, HEAD_DIM), lambda i: (layer, i, 0, 0, 0)),
        pl.BlockSpec((None, nseq, CONV_W - 1, 3 * W_B), lambda i: (layer, i, 0, 0)),
        pl.BlockSpec(memory_space=pl.ANY),
    ]

    def body(*refs):
        _gdn_sample_kernel(*refs[:10], *refs[11:], c=c)

    return pl.pallas_call(
        body,
        grid=(dec_b // nseq,),
        in_specs=in_specs,
        out_specs=[
            pl.BlockSpec((TILE_R, W_B), lambda i: (rows(i), 0)),
            pl.BlockSpec((nseq, H_B, HEAD_DIM, HEAD_DIM), lambda i: (i, 0, 0, 0)),
            pl.BlockSpec((nseq, CONV_W - 1, 3 * W_B), lambda i: (i, 0, 0)),
        ],
        out_shape=[
            jax.ShapeDtypeStruct(y_prev.shape, BF16),
            jax.ShapeDtypeStruct((dec_b, H_B, HEAD_DIM, HEAD_DIM), F32),
            jax.ShapeDtypeStruct((dec_b, CONV_W - 1, 3 * W_B), F32),
        ],
        scratch_shapes=[pltpu.VMEM((nseq, CONV_PAD + c, 3 * W_B), F32)],
        input_output_aliases={10: 0},
        compiler_params=pltpu.CompilerParams(
            dimension_semantics=("parallel",), vmem_limit_bytes=VMEM_LIMIT),
        name="gdn_sample",
    )(proj, proj, proj, proj, proj, gdn_par, conv_w, onw3, s_state, conv_state, y_prev)


def _hgrn_tile(q_ref, f_ref, i_ref, z_ref, lbp_ref, onw_ref, y_ref, get_s, put_s,
               nseq, c, layer):
    r = nseq * c
    row, col, same = _seq_masks(nseq, c)
    incl = same & (col <= row)
    rowv = lax.broadcasted_iota(jnp.int32, (r, 1), 0)
    lbp = lbp_ref[...]
    e = jnp.exp(lbp - jnp.max(lbp, axis=0, keepdims=True))
    sm = e / jnp.sum(e, axis=0, keepdims=True)
    lb = jnp.zeros((1, W_C), F32)
    for l in range(1, layer + 1):
        lb = lb + sm[l:l + 1, :]

    cf = f_ref[...]
    logf = _log_sigmoid(cf) + jnp.log1p(lb * jnp.exp(jnp.minimum(-cf, EXP_CLIP)))
    kall = (1.0 - lb) * _sigmoid(-cf)
    bcum = _dot_exact(incl.astype(F32), logf)
    if nseq == 1:
        blast = jnp.broadcast_to(bcum[r - 1:r, :], (r, W_C))
    else:
        blast = _dot_exact(same.astype(F32), logf)

    levels = []
    w = c // 2
    while w >= 1:
        levels.append(w)
        w //= 2
    refs = []
    for w in levels:
        lw = _log2(w)
        ref_row = ((row >> (lw + 1)) << (lw + 1)) + (w - 1)
        refs.append(_dot_exact((col == ref_row).astype(F32), bcum))

    for h in range(H_C):
        hs = slice(h * HEAD_DIM, (h + 1) * HEAD_DIM)
        qh = _silu(q_ref[:, hs])
        kh = kall[:, hs]
        vh = i_ref[:, hs]
        bh = bcum[:, hs]
        amat = jnp.where(row == col, _dot_nt(qh, kh), 0.0)
        for w, ref_all in zip(levels, refs):
            lw = _log2(w)
            ref = ref_all[:, hs]
            tgt = ((rowv >> lw) & 1) == 1
            qp = qh * jnp.exp(jnp.where(tgt, bh - ref, NEG_BIG))
            kp = kh * jnp.exp(jnp.where(tgt, NEG_BIG, ref - bh))
            pair = (row >> (lw + 1)) == (col >> (lw + 1))
            amat = amat + jnp.where(pair, _dot_nt(qp, kp), 0.0)
        qe = qh * jnp.exp(bh)
        qs = _cat_rows([_dot(_rows(qe, i, c), get_s(i, h)) for i in range(nseq)])
        oh = qs + _dot(amat, vh)
        y_ref[:, hs] = _head_norm_gate(oh, onw_ref[:, hs], z_ref[:, hs]).astype(BF16)

        bl = blast[:, hs]
        kd_t = (kh * jnp.exp(bl - bh)).T.astype(BF16)
        eb_t = jnp.exp(bl).T
        for i in range(nseq):
            if nseq == 1:
                upd = jnp.dot(kd_t, vh.astype(BF16), preferred_element_type=F32)
            else:
                seq_rows = (rowv >> _log2(c)) == i
                upd = jnp.dot(kd_t, jnp.where(seq_rows, vh, 0.0).astype(BF16),
                              preferred_element_type=F32)
            put_s(i, h, eb_t[:, i * c:i * c + 1] * get_s(i, h) + upd)


def _hgrn_prompt_kernel(q_ref, f_ref, i_ref, z_ref, lbp_ref, onw_ref, y_ref, s_out, s_scr,
                        *, layer):
    j = pl.program_id(1)

    @pl.when(j == 0)
    def _():
        s_scr[...] = jnp.zeros_like(s_scr)

    def put_s(i, h, val):
        s_scr[h] = val

    _hgrn_tile(q_ref, f_ref, i_ref, z_ref, lbp_ref, onw_ref, y_ref,
               lambda i, h: s_scr[h], put_s, 1, TILE_R, layer)

    @pl.when(j == pl.num_programs(1) - 1)
    def _():
        s_out[...] = s_scr[...]


def _hgrn_sample_kernel(q_ref, f_ref, i_ref, z_ref, lbp_ref, onw_ref, s_in, y_ref, s_out,
                        *, c, layer):
    def put_s(i, h, val):
        s_out[i, h] = val

    _hgrn_tile(q_ref, f_ref, i_ref, z_ref, lbp_ref, onw_ref, y_ref,
               lambda i, h: s_in[i, h], put_s, TILE_R // c, c, layer)


def _hgrn_prompt(proj, lb_par, onw3, layer, bsz, n_chunk, rows):
    t = proj.shape[0]
    depth = lb_par.shape[0]
    in_specs = [_col_spec(W_C, off, rows) for off in (OFF_CQ, OFF_CF, OFF_CI, OFF_ZC)]
    in_specs += [pl.BlockSpec((depth, W_C), lambda b, j: (0, 0)),
                 _param_spec(W_C, W_A + W_B, layer)]
    return pl.pallas_call(
        functools.partial(_hgrn_prompt_kernel, layer=layer),
        grid=(bsz, n_chunk),
        in_specs=in_specs,
        out_specs=[
            pl.BlockSpec((TILE_R, W_C), lambda b, j: (rows(b, j), 0)),
            pl.BlockSpec((None, H_C, HEAD_DIM, HEAD_DIM), lambda b, j: (b, 0, 0, 0)),
        ],
        out_shape=[
            jax.ShapeDtypeStruct((t, W_C), BF16),
            jax.ShapeDtypeStruct((bsz, H_C, HEAD_DIM, HEAD_DIM), F32),
        ],
        scratch_shapes=[pltpu.VMEM((H_C, HEAD_DIM, HEAD_DIM), F32)],
        compiler_params=pltpu.CompilerParams(
            dimension_semantics=("parallel", "arbitrary"), vmem_limit_bytes=VMEM_LIMIT),
        name="hgrn_prompt",
    )(proj, proj, proj, proj, lb_par, onw3)


def _hgrn_sample(proj, y_prev, lb_par, onw3, s_state, layer, dec_b, c, rows):
    nseq = TILE_R // c
    depth = lb_par.shape[0]
    in_specs = [_col_spec(W_C, off, rows) for off in (OFF_CQ, OFF_CF, OFF_CI, OFF_ZC)]
    in_specs += [
        pl.BlockSpec((depth, W_C), lambda i: (0, 0)),
        _param_spec(W_C, W_A + W_B, layer),
        pl.BlockSpec((None, nseq, H_C, HEAD_DIM, HEAD_DIM), lambda i: (layer, i, 0, 0, 0)),
        pl.BlockSpec(memory_space=pl.ANY),
    ]

    def body(*refs):
        _hgrn_sample_kernel(*refs[:7], *refs[8:], c=c, layer=layer)

    return pl.pallas_call(
        body,
        grid=(dec_b // nseq,),
        in_specs=in_specs,
        out_specs=[
            pl.BlockSpec((TILE_R, W_C), lambda i: (rows(i), 0)),
            pl.BlockSpec((nseq, H_C, HEAD_DIM, HEAD_DIM), lambda i: (i, 0, 0, 0)),
        ],
        out_shape=[
            jax.ShapeDtypeStruct(y_prev.shape, BF16),
            jax.ShapeDtypeStruct((dec_b, H_C, HEAD_DIM, HEAD_DIM), F32),
        ],
        input_output_aliases={7: 0},
        compiler_params=pltpu.CompilerParams(
            dimension_semantics=("parallel",), vmem_limit_bytes=VMEM_LIMIT),
        name="hgrn_sample",
    )(proj, proj, proj, proj, lb_par, onw3, s_state, y_prev)


def _pack_w_in(w_in):
    sizes = (W_A, W_A, W_A, W_A, H_A, H_A, W_B, W_B, W_B, H_B, H_B, W_C, W_C, W_C, MIX)
    offs = [0]
    for s in sizes:
        offs.append(offs[-1] + s)
    (aq, ak, av, ao, ai, af, bq, bk, bv, ba, bb, cq, cf, ci, z) = [
        w_in[..., offs[n]:offs[n + 1]] for n in range(len(sizes))]
    za, zb, zc = z[..., :W_A], z[..., W_A:W_A + W_B], z[..., W_A + W_B:]
    n_gate = 2 * H_A + 2 * H_B
    pad = jnp.zeros(w_in.shape[:-1] + (2 * LANES - n_gate,), w_in.dtype)
    packed = jnp.concatenate([cq, cf, ci, aq, ak, av, ao, bq, bk, bv, za, zb,
                              ai, af, ba, bb, pad, zc], axis=-1)
    assert packed.shape[-1] == N_PACK
    return packed.astype(BF16)


def _lane_row(depth, pieces):
    row = jnp.zeros((depth, LANES), F32)
    for off, val in pieces:
        row = lax.dynamic_update_slice(row, val.astype(F32), (0, off))
    return row[:, None, :]


def _largest_tile(n, cap, mult):
    best = None
    for cand in range(mult, cap + 1, mult):
        if n % cand == 0:
            best = cand
    assert best is not None
    return best


def kernel(x_prompt, x_sample, state_mlstm_C, state_mlstm_n, state_mlstm_m, state_gdn_S,
           state_gdn_conv, state_hgrn_S, meta_tokens, norm_w, w_in, mlstm_gate_b, gdn_A_log,
           gdn_dt_bias, gdn_conv_w, hgrn_lower_bounds, out_norm_w, w_out, final_norm_w):
    bsz, seq_len, d = x_prompt.shape
    dec_b, dec_seq, _ = x_sample.shape
    depth = w_in.shape[0]
    assert d == D_MODEL and seq_len % TILE_R == 0 and TILE_R % dec_seq == 0
    nseq = TILE_R // dec_seq
    assert dec_b % nseq == 0

    n_main, n_dec = bsz * seq_len, dec_b * dec_seq
    t_all = n_main + n_dec + bsz * TILE_R
    slot = jnp.concatenate([jnp.zeros((TILE_R - N_META, d), x_prompt.dtype),
                            meta_tokens.astype(x_prompt.dtype)], axis=0)
    x = jnp.concatenate([x_prompt.reshape(n_main, d), x_sample.reshape(n_dec, d),
                         jnp.tile(slot, (bsz, 1))], axis=0)

    chunks = seq_len // TILE_R
    n_chunk = chunks + 1
    meta_blk0 = (n_main + n_dec) // TILE_R
    dec_blk0 = n_main // TILE_R

    def prompt_rows(b, j):
        return jnp.where(j == 0, meta_blk0 + b, b * chunks + j - 1)

    def dec_rows(i):
        return dec_blk0 + i

    w_pack = _pack_w_in(w_in)
    w_out_bf = w_out.astype(BF16)
    norm_w3 = norm_w.astype(F32)[:, None, :]
    onw3 = out_norm_w.astype(F32)[:, None, :]
    final_nw = final_norm_w.astype(F32)[None, :]
    gate_bias = _lane_row(depth, [(G_AI, mlstm_gate_b[:, 0]), (G_AF, mlstm_gate_b[:, 1])])
    gdn_par = jnp.concatenate([
        _lane_row(depth, [(G_BA, gdn_dt_bias)]), _lane_row(depth, [(G_BA, gdn_A_log)]),
        jnp.zeros((depth, 6, LANES), F32)], axis=1)
    conv_w = gdn_conv_w.astype(F32)
    lb_par = hgrn_lower_bounds.astype(F32)
    m_rows = jnp.repeat(
        jnp.pad(state_mlstm_m.astype(F32), ((0, 0), (0, 0), (0, LANES - H_A))), dec_seq, axis=1)

    tm = _largest_tile(t_all, 1024, 16)
    tn = _largest_tile(N_PACK, 1024, LANES)
    tm_out = _largest_tile(math.gcd(n_main, n_dec), 512, 16)

    new_p = [[] for _ in range(6)]
    new_s = [[] for _ in range(6)]
    y_prompt = y_sample = None
    for l in range(depth):
        proj = _inproj(x, norm_w3, w_pack, l, tm, tn)

        ya, c_p, n_p, m_p = _mlstm_prompt(proj, gate_bias, onw3, l, bsz, n_chunk, prompt_rows)
        ya, c_s, n_s, m_s = _mlstm_sample(proj, ya, gate_bias, onw3, state_mlstm_C, state_mlstm_n,
                                          m_rows, l, dec_b, dec_seq, dec_rows)
        yb, s_p, cv_p = _gdn_prompt(proj, gdn_par, conv_w, onw3, l, bsz, n_chunk, prompt_rows)
        yb, s_s, cv_s = _gdn_sample(proj, yb, gdn_par, conv_w, onw3, state_gdn_S, state_gdn_conv,
                                    l, dec_b, dec_seq, dec_rows)
        yc, h_p = _hgrn_prompt(proj, lb_par, onw3, l, bsz, n_chunk, prompt_rows)
        yc, h_s = _hgrn_sample(proj, yc, lb_par, onw3, state_hgrn_S, l, dec_b, dec_seq, dec_rows)

        for lst, val in zip(new_p, (c_p, n_p, m_p[:, 0, :H_A], s_p, cv_p, h_p)):
            lst.append(val)
        for lst, val in zip(new_s, (c_s, n_s, m_s[::dec_seq, :H_A], s_s, cv_s, h_s)):
            lst.append(val)

        if l + 1 < depth:
            x = _outproj(ya, yb, yc, x, w_out_bf, l, final_nw, tm_out, 0, t_all // tm_out, False)
        else:
            y_prompt = _outproj(ya, yb, yc, x, w_out_bf, l, final_nw, tm_out, 0,
                                n_main // tm_out, True)
            y_sample = _outproj(ya, yb, yc, x, w_out_bf, l, final_nw, tm_out,
                                n_main // tm_out, n_dec // tm_out, True)

    outs_p = [jnp.stack(a, axis=0) for a in new_p]
    outs_s = [jnp.stack(a, axis=0) for a in new_s]
    return (y_prompt.reshape(bsz, seq_len, d), y_sample.reshape(dec_b, dec_seq, d),
            *outs_p, *outs_s)
```

```python
import functools
import math

import jax
import jax.numpy as jnp
from jax import lax
from jax.experimental import pallas as pl
from jax.experimental.pallas import tpu as pltpu

F32 = jnp.float32
BF16 = jnp.bfloat16
HIGHEST = lax.Precision.HIGHEST

D_MODEL = 2048
HEAD_DIM = 128
H_A, H_B, H_C = 6, 6, 4
W_A, W_B, W_C = H_A * HEAD_DIM, H_B * HEAD_DIM, H_C * HEAD_DIM
MIX = W_A + W_B + W_C
N_META = 16
CONV_W = 4
EPS = 1e-6
NEG_BIG = -1e30
EXP_CLIP = 60.0
QK_SCALE = HEAD_DIM ** -0.5

TILE_R = 128
LANES = 128
CONV_PAD = 8

OFF_CQ, OFF_CF, OFF_CI = 0, W_C, 2 * W_C
OFF_AQ = 3 * W_C
OFF_AK, OFF_AV, OFF_AO = OFF_AQ + W_A, OFF_AQ + 2 * W_A, OFF_AQ + 3 * W_A
OFF_BQ = OFF_AQ + 4 * W_A
OFF_BK, OFF_BV = OFF_BQ + W_B, OFF_BQ + 2 * W_B
OFF_ZA = OFF_BQ + 3 * W_B
OFF_ZB = OFF_ZA + W_A
OFF_G = OFF_ZB + W_B
OFF_ZC = OFF_G + 2 * LANES
N_PACK = OFF_ZC + W_C
G_AI, G_AF, G_BA, G_BB = 0, H_A, 2 * H_A, 2 * H_A + H_B

VMEM_LIMIT = 56 * 1024 * 1024


def _sigmoid(x):
    return 1.0 / (1.0 + jnp.exp(-x))


def _silu(x):
    return x * _sigmoid(x)


def _softplus(x):
    return jnp.maximum(x, 0.0) + jnp.log1p(jnp.exp(-jnp.abs(x)))


def _log_sigmoid(x):
    return -_softplus(-x)


def _dot(a, b):
    return jnp.dot(a.astype(BF16), b.astype(BF16), preferred_element_type=F32)


def _dot_nt(a, b):
    return lax.dot_general(a.astype(BF16), b.astype(BF16), (((1,), (1,)), ((), ())),
                           preferred_element_type=F32)


def _dot_exact(sel, x):
    return jnp.dot(sel, x, precision=HIGHEST, preferred_element_type=F32)


def _log2(n):
    k = int(math.log2(n))
    assert (1 << k) == n
    return k


def _seq_masks(nseq, c):
    r = nseq * c
    row = lax.broadcasted_iota(jnp.int32, (r, r), 0)
    col = lax.broadcasted_iota(jnp.int32, (r, r), 1)
    if nseq == 1:
        same = row >= 0
    else:
        k = _log2(c)
        same = (row >> k) == (col >> k)
    return row, col, same


def _head_norm_gate(h, onw, z):
    hn = h * lax.rsqrt(jnp.mean(h * h, axis=1, keepdims=True) + EPS)
    return hn * onw * _silu(z)


def _rows(x, i, c):
    return x[i * c:(i + 1) * c]


def _cat_rows(parts):
    return parts[0] if len(parts) == 1 else jnp.concatenate(parts, axis=0)


def _inproj_kernel(x_ref, nw_ref, w_ref, o_ref, h_scr):
    @pl.when(pl.program_id(1) == 0)
    def _():
        x = x_ref[...]
        ms = jnp.mean(x * x, axis=1, keepdims=True)
        h_scr[...] = (x * lax.rsqrt(ms + EPS) * nw_ref[...]).astype(BF16)

    o_ref[...] = jnp.dot(h_scr[...], w_ref[...], preferred_element_type=F32)


def _inproj(x, norm_w3, w_pack, layer, tm, tn):
    t = x.shape[0]
    return pl.pallas_call(
        _inproj_kernel,
        grid=(t // tm, N_PACK // tn),
        in_specs=[
            pl.BlockSpec((tm, D_MODEL), lambda i, j: (i, 0)),
            pl.BlockSpec((None, 1, D_MODEL), lambda i, j: (layer, 0, 0)),
            pl.BlockSpec((None, D_MODEL, tn), lambda i, j: (layer, 0, j)),
        ],
        out_specs=pl.BlockSpec((tm, tn), lambda i, j: (i, j)),
        out_shape=jax.ShapeDtypeStruct((t, N_PACK), F32),
        scratch_shapes=[pltpu.VMEM((tm, D_MODEL), BF16)],
        compiler_params=pltpu.CompilerParams(
            dimension_semantics=("parallel", "arbitrary"), vmem_limit_bytes=VMEM_LIMIT),
        name="inproj",
    )(x, norm_w3, w_pack)


def _outproj_kernel(ya_ref, yb_ref, yc_ref, x_ref, w_ref, nw_ref, o_ref, *, final):
    acc = jnp.dot(ya_ref[...], w_ref[0:W_A, :], preferred_element_type=F32)
    acc = acc + jnp.dot(yb_ref[...], w_ref[W_A:W_A + W_B, :], preferred_element_type=F32)
    acc = acc + jnp.dot(yc_ref[...], w_ref[W_A + W_B:MIX, :], preferred_element_type=F32)
    xn = x_ref[...] + acc
    if final:
        ms = jnp.mean(xn * xn, axis=1, keepdims=True)
        xn = xn * lax.rsqrt(ms + EPS) * nw_ref[...]
    o_ref[...] = xn


def _outproj(ya, yb, yc, x, w_out, layer, final_nw, tm, row_blk0, n_blk, final):
    rows = lambda i: (row_blk0 + i, 0)
    return pl.pallas_call(
        functools.partial(_outproj_kernel, final=final),
        grid=(n_blk,),
        in_specs=[
            pl.BlockSpec((tm, W_A), rows),
            pl.BlockSpec((tm, W_B), rows),
            pl.BlockSpec((tm, W_C), rows),
            pl.BlockSpec((tm, D_MODEL), rows),
            pl.BlockSpec((None, MIX, D_MODEL), lambda i: (layer, 0, 0)),
            pl.BlockSpec((1, D_MODEL), lambda i: (0, 0)),
        ],
        out_specs=pl.BlockSpec((tm, D_MODEL), lambda i: (i, 0)),
        out_shape=jax.ShapeDtypeStruct((n_blk * tm, D_MODEL), F32),
        compiler_params=pltpu.CompilerParams(
            dimension_semantics=("parallel",), vmem_limit_bytes=VMEM_LIMIT),
        name="outproj_final" if final else "outproj",
    )(ya, yb, yc, x, w_out, final_nw)


def _mlstm_tile(q_ref, k_ref, v_ref, o_ref, z_ref, g_ref, gb_ref, onw_ref, y_ref,
                m_exp, valid, get_c, get_n, put_c, put_n, put_m, nseq, c):
    r = nseq * c
    row, col, same = _seq_masks(nseq, c)
    incl = same & (col <= row)
    g = g_ref[...] + gb_ref[...]
    li = g
    lf = _log_sigmoid(g)
    if valid is not None:
        li = jnp.where(valid, li, NEG_BIG)
        lf = jnp.where(valid, lf, 0.0)
    fcum = _dot_exact(incl.astype(F32), lf)
    if nseq == 1:
        flast = jnp.broadcast_to(fcum[r - 1:r, :], (r, LANES))
    else:
        flast = _dot_exact(same.astype(F32), lf)
    fcum_t = fcum.T
    li_t = li.T
    lane = lax.broadcasted_iota(jnp.int32, (r, LANES), 1)
    heads = range(H_A)
    hsl = [slice(h * HEAD_DIM, (h + 1) * HEAD_DIM) for h in heads]

    dm, mt, a, w, a_s = [], [], [], [], []
    m_new_all = jnp.zeros((r, LANES), F32)
    for h in heads:
        fc = fcum[:, G_AF + h:G_AF + h + 1]
        fr = fcum_t[G_AF + h:G_AF + h + 1, :]
        lic = li[:, G_AI + h:G_AI + h + 1]
        lir = li_t[G_AI + h:G_AI + h + 1, :]
        fl = flast[:, G_AF + h:G_AF + h + 1]
        mcol = m_exp[:, h:h + 1]
        dm_h = jnp.where(incl, fc - fr + lir, NEG_BIG)
        inter = fc + mcol
        mt_h = jnp.maximum(inter, jnp.max(dm_h, axis=1, keepdims=True))
        if nseq == 1:
            m_new = jnp.broadcast_to(mt_h[r - 1:r, :], (r, 1))
        else:
            dl = jnp.where(same, fl - fr + lir, NEG_BIG)
            m_new = jnp.maximum(fl + mcol, jnp.max(dl, axis=1, keepdims=True))
        dm.append(dm_h)
        mt.append(mt_h)
        a.append(jnp.exp(inter - mt_h))
        w.append(jnp.exp(fl - fc + lic - m_new))
        a_s.append(jnp.exp(fl + mcol - m_new))
        m_new_all = jnp.where(lane == h, m_new, m_new_all)
    put_m(m_new_all)

    s = [_dot_nt(q_ref[:, hs], k_ref[:, hs] * QK_SCALE) for hs in hsl]
    p = [jnp.exp(dm[h] - mt[h]) * s[h] for h in heads]
    pv = [_dot(p[h], v_ref[:, hsl[h]]) for h in heads]
    qc = [_cat_rows([_dot(_rows(q_ref[:, hsl[h]], i, c), get_c(i, h)) for i in range(nseq)])
          for h in heads]
    for h in heads:
        hs = hsl[h]
        qh = q_ref[:, hs]
        psum = jnp.sum(p[h], axis=1, keepdims=True)
        qn = _cat_rows([jnp.sum(_rows(qh, i, c) * get_n(i, h), axis=1, keepdims=True)
                        for i in range(nseq)])
        num = a[h] * qc[h] + pv[h]
        den = a[h] * qn + psum
        hh = num / jnp.maximum(jnp.abs(den), jnp.exp(jnp.minimum(-mt[h], EXP_CLIP)))
        hh = hh * _sigmoid(o_ref[:, hs])
        y_ref[:, hs] = _head_norm_gate(hh, onw_ref[:, hs], z_ref[:, hs]).astype(BF16)

    rowv = lax.broadcasted_iota(jnp.int32, (r, 1), 0)
    for h in heads:
        hs = hsl[h]
        kh = k_ref[:, hs] * QK_SCALE
        wv = w[h] * v_ref[:, hs]
        wk = w[h] * kh
        kh_t = kh.T.astype(BF16)
        for i in range(nseq):
            a_i = a_s[h][i * c:i * c + 1, :]
            if nseq == 1:
                upd = jnp.dot(kh_t, wv.astype(BF16), preferred_element_type=F32)
            else:
                seq_rows = (rowv >> _log2(c)) == i
                upd = jnp.dot(kh_t, jnp.where(seq_rows, wv, 0.0).astype(BF16),
                              preferred_element_type=F32)
            put_c(i, h, a_i * get_c(i, h) + upd)
            put_n(i, h, a_i * get_n(i, h) + jnp.sum(_rows(wk, i, c), axis=0, keepdims=True))


def _mlstm_prompt_kernel(q_ref, k_ref, v_ref, o_ref, z_ref, g_ref, gb_ref, onw_ref,
                         y_ref, c_out, n_out, m_out, c_scr, n_scr, m_scr):
    j = pl.program_id(1)

    @pl.when(j == 0)
    def _():
        c_scr[...] = jnp.zeros_like(c_scr)
        n_scr[...] = jnp.zeros_like(n_scr)
        m_scr[...] = jnp.zeros_like(m_scr)

    tok = lax.broadcasted_iota(jnp.int32, (TILE_R, LANES), 0) + j * TILE_R
    valid = tok >= TILE_R - N_META
    m_exp = jnp.broadcast_to(m_scr[0:1, :], (TILE_R, LANES))

    def put_c(i, h, val):
        c_scr[h] = val

    def put_n(i, h, val):
        n_scr[h:h + 1, :] = val

    def put_m(val):
        m_scr[...] = val[0:8, :]

    _mlstm_tile(q_ref, k_ref, v_ref, o_ref, z_ref, g_ref, gb_ref, onw_ref, y_ref,
                m_exp, valid, lambda i, h: c_scr[h], lambda i, h: n_scr[h:h + 1, :],
                put_c, put_n, put_m, 1, TILE_R)

    @pl.when(j == pl.num_programs(1) - 1)
    def _():
        c_out[...] = c_scr[...]
        n_out[...] = n_scr[0:H_A, :]
        m_out[...] = m_scr[0:1, :]


def _mlstm_sample_kernel(q_ref, k_ref, v_ref, o_ref, z_ref, g_ref, gb_ref, onw_ref,
                         c_in, n_in, m_in, y_ref, c_out, n_out, m_out, *, c):
    nseq = TILE_R // c

    def put_c(i, h, val):
        c_out[i, h] = val

    def put_n(i, h, val):
        n_out[i, h:h + 1, :] = val

    def put_m(val):
        m_out[...] = val

    _mlstm_tile(q_ref, k_ref, v_ref, o_ref, z_ref, g_ref, gb_ref, onw_ref, y_ref,
                m_in[...], None, lambda i, h: c_in[i, h], lambda i, h: n_in[i, h:h + 1, :],
                put_c, put_n, put_m, nseq, c)


def _col_spec(width, off, rows):
    assert off % width == 0
    blk = off // width
    return pl.BlockSpec((TILE_R, width), lambda *ids: (rows(*ids), blk))


def _param_spec(width, off, layer):
    assert off % width == 0
    blk = off // width
    return pl.BlockSpec((None, 1, width), lambda *ids: (layer, 0, blk))


def _mlstm_prompt(proj, gate_bias, onw3, layer, bsz, n_chunk, rows):
    t = proj.shape[0]
    in_specs = [_col_spec(W_A, off, rows) for off in (OFF_AQ, OFF_AK, OFF_AV, OFF_AO, OFF_ZA)]
    in_specs += [_col_spec(LANES, OFF_G, rows), _param_spec(LANES, 0, layer),
                 _param_spec(W_A, 0, layer)]
    return pl.pallas_call(
        _mlstm_prompt_kernel,
        grid=(bsz, n_chunk),
        in_specs=in_specs,
        out_specs=[
            pl.BlockSpec((TILE_R, W_A), lambda b, j: (rows(b, j), 0)),
            pl.BlockSpec((None, H_A, HEAD_DIM, HEAD_DIM), lambda b, j: (b, 0, 0, 0)),
            pl.BlockSpec((None, H_A, HEAD_DIM), lambda b, j: (b, 0, 0)),
            pl.BlockSpec((None, 1, LANES), lambda b, j: (b, 0, 0)),
        ],
        out_shape=[
            jax.ShapeDtypeStruct((t, W_A), BF16),
            jax.ShapeDtypeStruct((bsz, H_A, HEAD_DIM, HEAD_DIM), F32),
            jax.ShapeDtypeStruct((bsz, H_A, HEAD_DIM), F32),
            jax.ShapeDtypeStruct((bsz, 1, LANES), F32),
        ],
        scratch_shapes=[pltpu.VMEM((H_A, HEAD_DIM, HEAD_DIM), F32),
                        pltpu.VMEM((8, HEAD_DIM), F32), pltpu.VMEM((8, LANES), F32)],
        compiler_params=pltpu.CompilerParams(
            dimension_semantics=("parallel", "arbitrary"), vmem_limit_bytes=VMEM_LIMIT),
        name="mlstm_prompt",
    )(proj, proj, proj, proj, proj, proj, gate_bias, onw3)


def _in_place(operands, n_blocked, targets):
    specs, aliases = [], {}
    operands = list(operands)
    for out_idx, arr in targets.items():
        if arr is not None:
            aliases[len(operands)] = out_idx
            operands.append(arr)
            specs.append(pl.BlockSpec(memory_space=pl.ANY))
    n_extra = len(specs)

    def strip(kernel):
        def body(*refs):
            kernel(*refs[:n_blocked], *refs[n_blocked + n_extra:])
        return body

    return operands, specs, aliases, strip


def _mlstm_sample(proj, y_prev, c_stack, gate_bias, onw3, c_state, n_state, m_rows, layer,
                  dec_b, c, rows):
    nseq = TILE_R // c
    depth = c_state.shape[0]
    in_specs = [_col_spec(W_A, off, rows) for off in (OFF_AQ, OFF_AK, OFF_AV, OFF_AO, OFF_ZA)]
    in_specs += [_col_spec(LANES, OFF_G, rows), _param_spec(LANES, 0, layer),
                 _param_spec(W_A, 0, layer)]
    in_specs += [
        pl.BlockSpec((None, nseq, H_A, HEAD_DIM, HEAD_DIM), lambda i: (layer, i, 0, 0, 0)),
        pl.BlockSpec((None, nseq, H_A, HEAD_DIM), lambda i: (layer, i, 0, 0)),
        pl.BlockSpec((None, TILE_R, LANES), lambda i: (layer, i, 0)),
    ]
    operands = (proj, proj, proj, proj, proj, proj, gate_bias, onw3, c_state, n_state, m_rows)
    operands, extra, aliases, strip = _in_place(operands, len(in_specs), {0: y_prev, 1: c_stack})
    return pl.pallas_call(
        strip(functools.partial(_mlstm_sample_kernel, c=c)),
        grid=(dec_b // nseq,),
        in_specs=in_specs + extra,
        out_specs=[
            pl.BlockSpec((TILE_R, W_A), lambda i: (rows(i), 0)),
            pl.BlockSpec((None, nseq, H_A, HEAD_DIM, HEAD_DIM), lambda i: (layer, i, 0, 0, 0)),
            pl.BlockSpec((nseq, H_A, HEAD_DIM), lambda i: (i, 0, 0)),
            pl.BlockSpec((TILE_R, LANES), lambda i: (i, 0)),
        ],
        out_shape=[
            jax.ShapeDtypeStruct(y_prev.shape, BF16),
            jax.ShapeDtypeStruct((depth, dec_b, H_A, HEAD_DIM, HEAD_DIM), F32),
            jax.ShapeDtypeStruct((dec_b, H_A, HEAD_DIM), F32),
            jax.ShapeDtypeStruct((dec_b * c, LANES), F32),
        ],
        input_output_aliases=aliases,
        compiler_params=pltpu.CompilerParams(
            dimension_semantics=("parallel",), vmem_limit_bytes=VMEM_LIMIT),
        name="mlstm_sample",
    )(*operands)


def _l2norm(x):
    return x * lax.rsqrt(jnp.sum(x * x, axis=1, keepdims=True) + EPS)


INV_LEAF = 16


def _unit_lower_inverse(a_list, row, col, c):
    leaf = min(INV_LEAF, c)
    kl = _log2(leaf)
    leaf_mask = (row >> kl) == (col >> kl)
    eye = jnp.where(row == col, 1.0, 0.0)
    p = [jnp.where(leaf_mask, -a, 0.0) for a in a_list]
    t = [eye + n for n in p]
    for _ in range(kl - 1):
        p = [_dot(x, x) for x in p]
        t = [ti + _dot(pi, ti) for pi, ti in zip(p, t)]
    w = leaf
    while w < c:
        kw = _log2(w)
        lower_left = (((row >> (kw + 1)) == (col >> (kw + 1)))
                      & (((row >> kw) & 1) == 1) & (((col >> kw) & 1) == 0))
        left = [_dot(ti, jnp.where(lower_left, a, 0.0)) for ti, a in zip(t, a_list)]
        t = [ti - _dot(li, ti) for li, ti in zip(left, t)]
        w *= 2
    return t


def _gdn_tile(conv_win, cw_ref, z_ref, g_ref, gp_ref, onw_ref, y_ref, get_s, put_s, nseq, c):
    r = nseq * c
    row, col, same = _seq_masks(nseq, c)
    incl = same & (col <= row)
    strict = same & (col < row)
    gp = gp_ref[...]
    graw = g_ref[...]
    gdec = -jnp.exp(gp[1:2, :]) * _softplus(graw + gp[0:1, :])
    beta = _sigmoid(graw)
    gcum = _dot_exact(incl.astype(F32), gdec)
    if nseq == 1:
        glast = jnp.broadcast_to(gcum[r - 1:r, :], (r, LANES))
    else:
        glast = _dot_exact(same.astype(F32), gdec)
    gcum_t = gcum.T

    heads = range(H_B)
    hsl = [slice(h * HEAD_DIM, (h + 1) * HEAD_DIM) for h in heads]
    rowv = lax.broadcasted_iota(jnp.int32, (r, 1), 0)

    def conv(lo):
        acc = conv_win(0, slice(lo, lo + HEAD_DIM)) * cw_ref[0:1, lo:lo + HEAD_DIM]
        for jj in range(1, CONV_W):
            acc = acc + conv_win(jj, slice(lo, lo + HEAD_DIM)) * cw_ref[jj:jj + 1, lo:lo + HEAD_DIM]
        return _silu(acc)

    q = [_l2norm(conv(hs.start)) * QK_SCALE for hs in hsl]
    k = [_l2norm(conv(W_B + hs.start)) for hs in hsl]
    v = [conv(2 * W_B + hs.start) for hs in hsl]
    gc = [gcum[:, G_BA + h:G_BA + h + 1] for h in heads]
    gl = [glast[:, G_BA + h:G_BA + h + 1] for h in heads]
    bc = [beta[:, G_BB + h:G_BB + h + 1] for h in heads]
    decay = [jnp.exp(jnp.where(incl, gc[h] - gcum_t[G_BA + h:G_BA + h + 1, :], NEG_BIG))
             for h in heads]
    kk = [_dot_nt(k[h], k[h]) for h in heads]
    qk = [_dot_nt(q[h], k[h]) for h in heads]
    amat = [jnp.where(strict, bc[h] * decay[h] * kk[h], 0.0) for h in heads]
    tinv = _unit_lower_inverse(amat, row, col, c)
    sol = [_dot(tinv[h], jnp.concatenate([bc[h] * v[h], (bc[h] * jnp.exp(gc[h])) * k[h]], axis=1))
           for h in heads]
    wks = [_cat_rows([_dot(_rows(sol[h][:, HEAD_DIM:], i, c), get_s(i, h)) for i in range(nseq)])
           for h in heads]
    qs = [_cat_rows([_dot(_rows(q[h], i, c), get_s(i, h)) for i in range(nseq)]) for h in heads]
    wmat = [sol[h][:, :HEAD_DIM] - wks[h] for h in heads]
    aw = [_dot(decay[h] * qk[h], wmat[h]) for h in heads]
    for h in heads:
        hs = hsl[h]
        oh = jnp.exp(gc[h]) * qs[h] + aw[h]
        y_ref[:, hs] = _head_norm_gate(oh, onw_ref[:, hs], z_ref[:, hs]).astype(BF16)
    for h in heads:
        kd_t = (k[h] * jnp.exp(gl[h] - gc[h])).T.astype(BF16)
        e_last = jnp.exp(gl[h])
        for i in range(nseq):
            if nseq == 1:
                upd = jnp.dot(kd_t, wmat[h].astype(BF16), preferred_element_type=F32)
            else:
                seq_rows = (rowv >> _log2(c)) == i
                upd = jnp.dot(kd_t, jnp.where(seq_rows, wmat[h], 0.0).astype(BF16),
                              preferred_element_type=F32)
            put_s(i, h, e_last[i * c:i * c + 1, :] * get_s(i, h) + upd)


def _gdn_prompt_kernel(q_ref, k_ref, v_ref, z_ref, g_ref, gp_ref, cw_ref, onw_ref,
                       y_ref, s_out, conv_out, s_scr, ext_scr):
    j = pl.program_id(1)

    @pl.when(j == 0)
    def _():
        s_scr[...] = jnp.zeros_like(s_scr)
        ext_scr[0:CONV_PAD, :] = jnp.zeros((CONV_PAD, 3 * W_B), F32)

    ext_scr[CONV_PAD:, 0:W_B] = q_ref[...]
    ext_scr[CONV_PAD:, W_B:2 * W_B] = k_ref[...]
    ext_scr[CONV_PAD:, 2 * W_B:] = v_ref[...]

    def conv_win(jj, cols):
        start = CONV_PAD - (CONV_W - 1) + jj
        return ext_scr[start:start + TILE_R, cols]

    def put_s(i, h, val):
        s_scr[h] = val

    _gdn_tile(conv_win, cw_ref, z_ref, g_ref, gp_ref, onw_ref, y_ref,
              lambda i, h: s_scr[h], put_s, 1, TILE_R)

    @pl.when(j == pl.num_programs(1) - 1)
    def _():
        s_out[...] = s_scr[...]
        conv_out[...] = ext_scr[TILE_R + CONV_PAD - (CONV_W - 1):TILE_R + CONV_PAD, :]

    ext_scr[0:CONV_PAD, :] = ext_scr[TILE_R:TILE_R + CONV_PAD, :]


def _gdn_sample_kernel(q_ref, k_ref, v_ref, z_ref, g_ref, gp_ref, cw_ref, onw_ref,
                       s_in, conv_in, y_ref, s_out, conv_out, ext_scr, *, c):
    nseq = TILE_R // c
    hist = CONV_W - 1
    ext_scr[:, CONV_PAD - hist:CONV_PAD, :] = conv_in[...]
    ext_scr[:, CONV_PAD:, 0:W_B] = q_ref[...].reshape(nseq, c, W_B)
    ext_scr[:, CONV_PAD:, W_B:2 * W_B] = k_ref[...].reshape(nseq, c, W_B)
    ext_scr[:, CONV_PAD:, 2 * W_B:] = v_ref[...].reshape(nseq, c, W_B)

    def conv_win(jj, cols):
        start = CONV_PAD - hist + jj
        return ext_scr[:, start:start + c, cols].reshape(TILE_R, cols.stop - cols.start)

    def put_s(i, h, val):
        s_out[i, h] = val

    _gdn_tile(conv_win, cw_ref, z_ref, g_ref, gp_ref, onw_ref, y_ref,
              lambda i, h: s_in[i, h], put_s, nseq, c)
    conv_out[...] = ext_scr[:, CONV_PAD + c - hist:CONV_PAD + c, :]


def _gdn_prompt(proj, gdn_par, conv_w, onw3, layer, bsz, n_chunk, rows):
    t = proj.shape[0]
    in_specs = [_col_spec(W_B, off, rows) for off in (OFF_BQ, OFF_BK, OFF_BV, OFF_ZB)]
    in_specs += [
        _col_spec(LANES, OFF_G, rows),
        pl.BlockSpec((None, 8, LANES), lambda b, j: (layer, 0, 0)),
        pl.BlockSpec((None, CONV_W, 3 * W_B), lambda b, j: (layer, 0, 0)),
        _param_spec(W_B, W_A, layer),
    ]
    return pl.pallas_call(
        _gdn_prompt_kernel,
        grid=(bsz, n_chunk),
        in_specs=in_specs,
        out_specs=[
            pl.BlockSpec((TILE_R, W_B), lambda b, j: (rows(b, j), 0)),
            pl.BlockSpec((None, H_B, HEAD_DIM, HEAD_DIM), lambda b, j: (b, 0, 0, 0)),
            pl.BlockSpec((None, CONV_W - 1, 3 * W_B), lambda b, j: (b, 0, 0)),
        ],
        out_shape=[
            jax.ShapeDtypeStruct((t, W_B), BF16),
            jax.ShapeDtypeStruct((bsz, H_B, HEAD_DIM, HEAD_DIM), F32),
            jax.ShapeDtypeStruct((bsz, CONV_W - 1, 3 * W_B), F32),
        ],
        scratch_shapes=[pltpu.VMEM((H_B, HEAD_DIM, HEAD_DIM), F32),
                        pltpu.VMEM((TILE_R + CONV_PAD, 3 * W_B), F32)],
        compiler_params=pltpu.CompilerParams(
            dimension_semantics=("parallel", "arbitrary"), vmem_limit_bytes=VMEM_LIMIT),
        name="gdn_prompt",
    )(proj, proj, proj, proj, proj, gdn_par, conv_w, onw3)


def _gdn_sample(proj, y_prev, s_stack, gdn_par, conv_w, onw3, s_state, conv_state, layer,
                dec_b, c, rows):
    nseq = TILE_R // c
    depth = s_state.shape[0]
    in_specs = [_col_spec(W_B, off, rows) for off in (OFF_BQ, OFF_BK, OFF_BV, OFF_ZB)]
    in_specs += [
        _col_spec(LANES, OFF_G, rows),
        pl.BlockSpec((None, 8, LANES), lambda i: (layer, 0, 0)),
        pl.BlockSpec((None, CONV_W, 3 * W_B), lambda i: (layer, 0, 0)),
        _param_spec(W_B, W_A, layer),
        pl.BlockSpec((None, nseq, H_B, HEAD_DIM, HEAD_DIM), lambda i: (layer, i, 0, 0, 0)),
        pl.BlockSpec((None, nseq, CONV_W - 1, 3 * W_B), lambda i: (layer, i, 0, 0)),
    ]
    operands = (proj, proj, proj, proj, proj, gdn_par, conv_w, onw3, s_state, conv_state)
    operands, extra, aliases, strip = _in_place(operands, len(in_specs), {0: y_prev, 1: s_stack})
    return pl.pallas_call(
        strip(functools.partial(_gdn_sample_kernel, c=c)),
        grid=(dec_b // nseq,),
        in_specs=in_specs + extra,
        out_specs=[
            pl.BlockSpec((TILE_R, W_B), lambda i: (rows(i), 0)),
            pl.BlockSpec((None, nseq, H_B, HEAD_DIM, HEAD_DIM), lambda i: (layer, i, 0, 0, 0)),
            pl.BlockSpec((nseq, CONV_W - 1, 3 * W_B), lambda i: (i, 0, 0)),
        ],
        out_shape=[
            jax.ShapeDtypeStruct(y_prev.shape, BF16),
            jax.ShapeDtypeStruct((depth, dec_b, H_B, HEAD_DIM, HEAD_DIM), F32),
            jax.ShapeDtypeStruct((dec_b, CONV_W - 1, 3 * W_B), F32),
        ],
        scratch_shapes=[pltpu.VMEM((nseq, CONV_PAD + c, 3 * W_B), F32)],
        input_output_aliases=aliases,
        compiler_params=pltpu.CompilerParams(
            dimension_semantics=("parallel",), vmem_limit_bytes=VMEM_LIMIT),
        name="gdn_sample",
    )(*operands)


def _hgrn_tile(q_ref, f_ref, i_ref, z_ref, lbp_ref, onw_ref, y_ref, get_s, put_s,
               nseq, c, layer):
    r = nseq * c
    row, col, same = _seq_masks(nseq, c)
    incl = same & (col <= row)
    rowv = lax.broadcasted_iota(jnp.int32, (r, 1), 0)
    lbp = lbp_ref[...]
    e = jnp.exp(lbp - jnp.max(lbp, axis=0, keepdims=True))
    sm = e / jnp.sum(e, axis=0, keepdims=True)
    lb = jnp.zeros((1, W_C), F32)
    for l in range(1, layer + 1):
        lb = lb + sm[l:l + 1, :]

    cf = f_ref[...]
    logf = _log_sigmoid(cf) + jnp.log1p(lb * jnp.exp(jnp.minimum(-cf, EXP_CLIP)))
    kall = (1.0 - lb) * _sigmoid(-cf)
    bcum = _dot_exact(incl.astype(F32), logf)
    if nseq == 1:
        blast = jnp.broadcast_to(bcum[r - 1:r, :], (r, W_C))
    else:
        blast = _dot_exact(same.astype(F32), logf)

    levels = []
    w = c // 2
    while w >= 1:
        levels.append(w)
        w //= 2
    refs = []
    for w in levels:
        lw = _log2(w)
        ref_row = ((row >> (lw + 1)) << (lw + 1)) + (w - 1)
        refs.append(_dot_exact((col == ref_row).astype(F32), bcum))

    heads = range(H_C)
    hsl = [slice(h * HEAD_DIM, (h + 1) * HEAD_DIM) for h in heads]
    q = [_silu(q_ref[:, hs]) for hs in hsl]
    k = [kall[:, hs] for hs in hsl]
    b = [bcum[:, hs] for hs in hsl]
    diag = row == col
    amat = [jnp.where(diag, _dot_nt(q[h], k[h]), 0.0) for h in heads]
    for w, ref_all in zip(levels, refs):
        lw = _log2(w)
        tgt = ((rowv >> lw) & 1) == 1
        pair = (row >> (lw + 1)) == (col >> (lw + 1))
        qp = [q[h] * jnp.exp(jnp.where(tgt, b[h] - ref_all[:, hsl[h]], NEG_BIG)) for h in heads]
        kp = [k[h] * jnp.exp(jnp.where(tgt, NEG_BIG, ref_all[:, hsl[h]] - b[h])) for h in heads]
        amat = [amat[h] + jnp.where(pair, _dot_nt(qp[h], kp[h]), 0.0) for h in heads]
    qs = [_cat_rows([_dot(_rows(q[h] * jnp.exp(b[h]), i, c), get_s(i, h)) for i in range(nseq)])
          for h in heads]
    av = [_dot(amat[h], i_ref[:, hsl[h]]) for h in heads]
    for h in heads:
        hs = hsl[h]
        y_ref[:, hs] = _head_norm_gate(qs[h] + av[h], onw_ref[:, hs], z_ref[:, hs]).astype(BF16)
    for h in heads:
        hs = hsl[h]
        vh = i_ref[:, hs]
        bl = blast[:, hs]
        kd_t = (k[h] * jnp.exp(bl - b[h])).T.astype(BF16)
        eb_t = jnp.exp(bl).T
        for i in range(nseq):
            if nseq == 1:
                upd = jnp.dot(kd_t, vh.astype(BF16), preferred_element_type=F32)
            else:
                seq_rows = (rowv >> _log2(c)) == i
                upd = jnp.dot(kd_t, jnp.where(seq_rows, vh, 0.0).astype(BF16),
                              preferred_element_type=F32)
            put_s(i, h, eb_t[:, i * c:i * c + 1] * get_s(i, h) + upd)


def _hgrn_prompt_kernel(q_ref, f_ref, i_ref, z_ref, lbp_ref, onw_ref, y_ref, s_out, s_scr,
                        *, layer):
    j = pl.program_id(1)

    @pl.when(j == 0)
    def _():
        s_scr[...] = jnp.zeros_like(s_scr)

    def put_s(i, h, val):
        s_scr[h] = val

    _hgrn_tile(q_ref, f_ref, i_ref, z_ref, lbp_ref, onw_ref, y_ref,
               lambda i, h: s_scr[h], put_s, 1, TILE_R, layer)

    @pl.when(j == pl.num_programs(1) - 1)
    def _():
        s_out[...] = s_scr[...]


def _hgrn_sample_kernel(q_ref, f_ref, i_ref, z_ref, lbp_ref, onw_ref, s_in, y_ref, s_out,
                        *, c, layer):
    def put_s(i, h, val):
        s_out[i, h] = val

    _hgrn_tile(q_ref, f_ref, i_ref, z_ref, lbp_ref, onw_ref, y_ref,
               lambda i, h: s_in[i, h], put_s, TILE_R // c, c, layer)


def _hgrn_prompt(proj, lb_par, onw3, layer, bsz, n_chunk, rows):
    t = proj.shape[0]
    depth = lb_par.shape[0]
    in_specs = [_col_spec(W_C, off, rows) for off in (OFF_CQ, OFF_CF, OFF_CI, OFF_ZC)]
    in_specs += [pl.BlockSpec((depth, W_C), lambda b, j: (0, 0)),
                 _param_spec(W_C, W_A + W_B, layer)]
    return pl.pallas_call(
        functools.partial(_hgrn_prompt_kernel, layer=layer),
        grid=(bsz, n_chunk),
        in_specs=in_specs,
        out_specs=[
            pl.BlockSpec((TILE_R, W_C), lambda b, j: (rows(b, j), 0)),
            pl.BlockSpec((None, H_C, HEAD_DIM, HEAD_DIM), lambda b, j: (b, 0, 0, 0)),
        ],
        out_shape=[
            jax.ShapeDtypeStruct((t, W_C), BF16),
            jax.ShapeDtypeStruct((bsz, H_C, HEAD_DIM, HEAD_DIM), F32),
        ],
        scratch_shapes=[pltpu.VMEM((H_C, HEAD_DIM, HEAD_DIM), F32)],
        compiler_params=pltpu.CompilerParams(
            dimension_semantics=("parallel", "arbitrary"), vmem_limit_bytes=VMEM_LIMIT),
        name="hgrn_prompt",
    )(proj, proj, proj, proj, lb_par, onw3)


def _hgrn_sample(proj, y_prev, s_stack, lb_par, onw3, s_state, layer, dec_b, c, rows):
    nseq = TILE_R // c
    depth = lb_par.shape[0]
    in_specs = [_col_spec(W_C, off, rows) for off in (OFF_CQ, OFF_CF, OFF_CI, OFF_ZC)]
    in_specs += [
        pl.BlockSpec((depth, W_C), lambda i: (0, 0)),
        _param_spec(W_C, W_A + W_B, layer),
        pl.BlockSpec((None, nseq, H_C, HEAD_DIM, HEAD_DIM), lambda i: (layer, i, 0, 0, 0)),
    ]
    operands = (proj, proj, proj, proj, lb_par, onw3, s_state)
    operands, extra, aliases, strip = _in_place(operands, len(in_specs), {0: y_prev, 1: s_stack})
    return pl.pallas_call(
        strip(functools.partial(_hgrn_sample_kernel, c=c, layer=layer)),
        grid=(dec_b // nseq,),
        in_specs=in_specs + extra,
        out_specs=[
            pl.BlockSpec((TILE_R, W_C), lambda i: (rows(i), 0)),
            pl.BlockSpec((None, nseq, H_C, HEAD_DIM, HEAD_DIM), lambda i: (layer, i, 0, 0, 0)),
        ],
        out_shape=[
            jax.ShapeDtypeStruct(y_prev.shape, BF16),
            jax.ShapeDtypeStruct((depth, dec_b, H_C, HEAD_DIM, HEAD_DIM), F32),
        ],
        input_output_aliases=aliases,
        compiler_params=pltpu.CompilerParams(
            dimension_semantics=("parallel",), vmem_limit_bytes=VMEM_LIMIT),
        name="hgrn_sample",
    )(*operands)


def _pack_w_in(w_in):
    sizes = (W_A, W_A, W_A, W_A, H_A, H_A, W_B, W_B, W_B, H_B, H_B, W_C, W_C, W_C, MIX)
    offs = [0]
    for s in sizes:
        offs.append(offs[-1] + s)
    (aq, ak, av, ao, ai, af, bq, bk, bv, ba, bb, cq, cf, ci, z) = [
        w_in[..., offs[n]:offs[n + 1]] for n in range(len(sizes))]
    za, zb, zc = z[..., :W_A], z[..., W_A:W_A + W_B], z[..., W_A + W_B:]
    n_gate = 2 * H_A + 2 * H_B
    pad = jnp.zeros(w_in.shape[:-1] + (2 * LANES - n_gate,), w_in.dtype)
    packed = jnp.concatenate([cq, cf, ci, aq, ak, av, ao, bq, bk, bv, za, zb,
                              ai, af, ba, bb, pad, zc], axis=-1)
    assert packed.shape[-1] == N_PACK
    return packed.astype(BF16)


def _lane_row(depth, pieces):
    row = jnp.zeros((depth, LANES), F32)
    for off, val in pieces:
        row = lax.dynamic_update_slice(row, val.astype(F32), (0, off))
    return row[:, None, :]


def _largest_tile(n, cap, mult):
    best = None
    for cand in range(mult, cap + 1, mult):
        if n % cand == 0:
            best = cand
    assert best is not None
    return best


def kernel(x_prompt, x_sample, state_mlstm_C, state_mlstm_n, state_mlstm_m, state_gdn_S,
           state_gdn_conv, state_hgrn_S, meta_tokens, norm_w, w_in, mlstm_gate_b, gdn_A_log,
           gdn_dt_bias, gdn_conv_w, hgrn_lower_bounds, out_norm_w, w_out, final_norm_w):
    bsz, seq_len, d = x_prompt.shape
    dec_b, dec_seq, _ = x_sample.shape
    depth = w_in.shape[0]
    assert d == D_MODEL and seq_len % TILE_R == 0 and TILE_R % dec_seq == 0
    nseq = TILE_R // dec_seq
    assert dec_b % nseq == 0

    n_main, n_dec = bsz * seq_len, dec_b * dec_seq
    t_all = n_main + n_dec + bsz * TILE_R
    slot = jnp.concatenate([jnp.zeros((TILE_R - N_META, d), x_prompt.dtype),
                            meta_tokens.astype(x_prompt.dtype)], axis=0)
    x = jnp.concatenate([x_prompt.reshape(n_main, d), x_sample.reshape(n_dec, d),
                         jnp.tile(slot, (bsz, 1))], axis=0)

    chunks = seq_len // TILE_R
    n_chunk = chunks + 1
    meta_blk0 = (n_main + n_dec) // TILE_R
    dec_blk0 = n_main // TILE_R

    def prompt_rows(b, j):
        return jnp.where(j == 0, meta_blk0 + b, b * chunks + j - 1)

    def dec_rows(i):
        return dec_blk0 + i

    w_pack = _pack_w_in(w_in)
    w_out_bf = w_out.astype(BF16)
    norm_w3 = norm_w.astype(F32)[:, None, :]
    onw3 = out_norm_w.astype(F32)[:, None, :]
    final_nw = final_norm_w.astype(F32)[None, :]
    gate_bias = _lane_row(depth, [(G_AI, mlstm_gate_b[:, 0]), (G_AF, mlstm_gate_b[:, 1])])
    gdn_par = jnp.concatenate([
        _lane_row(depth, [(G_BA, gdn_dt_bias)]), _lane_row(depth, [(G_BA, gdn_A_log)]),
        jnp.zeros((depth, 6, LANES), F32)], axis=1)
    conv_w = gdn_conv_w.astype(F32)
    lb_par = hgrn_lower_bounds.astype(F32)
    m_rows = jnp.repeat(
        jnp.pad(state_mlstm_m.astype(F32), ((0, 0), (0, 0), (0, LANES - H_A))), dec_seq, axis=1)

    tm = _largest_tile(t_all, 1024, 16)
    tn = _largest_tile(N_PACK, 1024, LANES)
    tm_out = _largest_tile(math.gcd(n_main, n_dec), 512, 16)

    new_p = [[] for _ in range(6)]
    new_s = [[] for _ in range(3)]
    c_s = s_s = h_s = None
    y_prompt = y_sample = None
    for l in range(depth):
        proj = _inproj(x, norm_w3, w_pack, l, tm, tn)

        ya, c_p, n_p, m_p = _mlstm_prompt(proj, gate_bias, onw3, l, bsz, n_chunk, prompt_rows)
        ya, c_s, n_s, m_s = _mlstm_sample(proj, ya, c_s, gate_bias, onw3, state_mlstm_C,
                                          state_mlstm_n, m_rows, l, dec_b, dec_seq, dec_rows)
        yb, s_p, cv_p = _gdn_prompt(proj, gdn_par, conv_w, onw3, l, bsz, n_chunk, prompt_rows)
        yb, s_s, cv_s = _gdn_sample(proj, yb, s_s, gdn_par, conv_w, onw3, state_gdn_S,
                                    state_gdn_conv, l, dec_b, dec_seq, dec_rows)
        yc, h_p = _hgrn_prompt(proj, lb_par, onw3, l, bsz, n_chunk, prompt_rows)
        yc, h_s = _hgrn_sample(proj, yc, h_s, lb_par, onw3, state_hgrn_S, l, dec_b, dec_seq,
                               dec_rows)

        for lst, val in zip(new_p, (c_p, n_p, m_p[:, 0, :H_A], s_p, cv_p, h_p)):
            lst.append(val)
        for lst, val in zip(new_s, (n_s, m_s[::dec_seq, :H_A], cv_s)):
            lst.append(val)

        if l + 1 < depth:
            x = _outproj(ya, yb, yc, x, w_out_bf, l, final_nw, tm_out, 0, t_all // tm_out, False)
        else:
            y_prompt = _outproj(ya, yb, yc, x, w_out_bf, l, final_nw, tm_out, 0,
                                n_main // tm_out, True)
            y_sample = _outproj(ya, yb, yc, x, w_out_bf, l, final_nw, tm_out,
                                n_main // tm_out, n_dec // tm_out, True)

    outs_p = [jnp.stack(a, axis=0) for a in new_p]
    n_all, m_all, cv_all = [jnp.stack(a, axis=0) for a in new_s]
    return (y_prompt.reshape(bsz, seq_len, d), y_sample.reshape(dec_b, dec_seq, d),
            *outs_p, c_s, n_all, m_all, s_s, cv_all, h_s)
```

```python
import functools
import math

import jax
import jax.numpy as jnp
from jax import lax
from jax.experimental import pallas as pl
from jax.experimental.pallas import tpu as pltpu

F32 = jnp.float32
BF16 = jnp.bfloat16

D_MODEL = 2048
HEAD_DIM = 128
H_A, H_B, H_C = 6, 6, 4
W_A, W_B, W_C = H_A * HEAD_DIM, H_B * HEAD_DIM, H_C * HEAD_DIM
MIX = W_A + W_B + W_C
N_META = 16
CONV_W = 4
EPS = 1e-6
NEG_BIG = -1e30
EXP_CLIP = 60.0
QK_SCALE = HEAD_DIM ** -0.5

TILE_R = 128
LANES = 128
CONV_PAD = 8

OFF_CQ, OFF_CF, OFF_CI = 0, W_C, 2 * W_C
OFF_AQ = 3 * W_C
OFF_AK, OFF_AV, OFF_AO = OFF_AQ + W_A, OFF_AQ + 2 * W_A, OFF_AQ + 3 * W_A
OFF_BQ = OFF_AQ + 4 * W_A
OFF_BK, OFF_BV = OFF_BQ + W_B, OFF_BQ + 2 * W_B
OFF_ZA = OFF_BQ + 3 * W_B
OFF_ZB = OFF_ZA + W_A
OFF_G = OFF_ZB + W_B
OFF_ZC = OFF_G + 2 * LANES
N_PACK = OFF_ZC + W_C
G_AI, G_AF, G_BA, G_BB = 0, H_A, 2 * H_A, 2 * H_A + H_B

VMEM_LIMIT = 56 * 1024 * 1024


def _sigmoid(x):
    return 1.0 / (1.0 + jnp.exp(-x))


def _silu(x):
    return x * _sigmoid(x)


def _softplus(x):
    return jnp.maximum(x, 0.0) + jnp.log1p(jnp.exp(-jnp.abs(x)))


def _log_sigmoid(x):
    return -_softplus(-x)


def _dot(a, b):
    return jnp.dot(a.astype(BF16), b.astype(BF16), preferred_element_type=F32)


def _dot_nt(a, b):
    return lax.dot_general(a.astype(BF16), b.astype(BF16), (((1,), (1,)), ((), ())),
                           preferred_element_type=F32)


def _dot_exact(sel, x):
    hi = x.astype(BF16)
    r1 = x - hi.astype(F32)
    mid = r1.astype(BF16)
    lo = (r1 - mid.astype(F32)).astype(BF16)
    s = sel.astype(BF16)
    return (jnp.dot(s, hi, preferred_element_type=F32)
            + jnp.dot(s, mid, preferred_element_type=F32)
            + jnp.dot(s, lo, preferred_element_type=F32))


def _log2(n):
    k = int(math.log2(n))
    assert (1 << k) == n
    return k


def _seq_masks(nseq, c):
    r = nseq * c
    row = lax.broadcasted_iota(jnp.int32, (r, r), 0)
    col = lax.broadcasted_iota(jnp.int32, (r, r), 1)
    if nseq == 1:
        same = row >= 0
    else:
        k = _log2(c)
        same = (row >> k) == (col >> k)
    return row, col, same


def _head_norm_gate(h, onw, z):
    hn = h * lax.rsqrt(jnp.mean(h * h, axis=1, keepdims=True) + EPS)
    return hn * onw * _silu(z)


def _rows(x, i, c):
    return x[i * c:(i + 1) * c]


def _cat_rows(parts):
    return parts[0] if len(parts) == 1 else jnp.concatenate(parts, axis=0)


def _inproj_kernel(x_ref, nw_ref, w_ref, o_ref, h_scr):
    @pl.when(pl.program_id(1) == 0)
    def _():
        x = x_ref[...]
        ms = jnp.mean(x * x, axis=1, keepdims=True)
        h_scr[...] = (x * lax.rsqrt(ms + EPS) * nw_ref[...]).astype(BF16)

    o_ref[...] = jnp.dot(h_scr[...], w_ref[...], preferred_element_type=F32)


def _inproj(x, norm_w3, w_pack, layer, tm, tn):
    t = x.shape[0]
    return pl.pallas_call(
        _inproj_kernel,
        grid=(t // tm, N_PACK // tn),
        in_specs=[
            pl.BlockSpec((tm, D_MODEL), lambda i, j: (i, 0)),
            pl.BlockSpec((None, 1, D_MODEL), lambda i, j: (layer, 0, 0)),
            pl.BlockSpec((None, D_MODEL, tn), lambda i, j: (layer, 0, j)),
        ],
        out_specs=pl.BlockSpec((tm, tn), lambda i, j: (i, j)),
        out_shape=jax.ShapeDtypeStruct((t, N_PACK), F32),
        scratch_shapes=[pltpu.VMEM((tm, D_MODEL), BF16)],
        compiler_params=pltpu.CompilerParams(
            dimension_semantics=("parallel", "arbitrary"), vmem_limit_bytes=VMEM_LIMIT),
        name="inproj",
    )(x, norm_w3, w_pack)


def _outproj_kernel(ya_ref, yb_ref, yc_ref, x_ref, w_ref, nw_ref, o_ref, *, final):
    acc = jnp.dot(ya_ref[...], w_ref[0:W_A, :], preferred_element_type=F32)
    acc = acc + jnp.dot(yb_ref[...], w_ref[W_A:W_A + W_B, :], preferred_element_type=F32)
    acc = acc + jnp.dot(yc_ref[...], w_ref[W_A + W_B:MIX, :], preferred_element_type=F32)
    xn = x_ref[...] + acc
    if final:
        ms = jnp.mean(xn * xn, axis=1, keepdims=True)
        xn = xn * lax.rsqrt(ms + EPS) * nw_ref[...]
    o_ref[...] = xn


def _outproj(ya, yb, yc, x, w_out, layer, final_nw, tm, row_blk0, n_blk, final):
    rows = lambda i: (row_blk0 + i, 0)
    return pl.pallas_call(
        functools.partial(_outproj_kernel, final=final),
        grid=(n_blk,),
        in_specs=[
            pl.BlockSpec((tm, W_A), rows),
            pl.BlockSpec((tm, W_B), rows),
            pl.BlockSpec((tm, W_C), rows),
            pl.BlockSpec((tm, D_MODEL), rows),
            pl.BlockSpec((None, MIX, D_MODEL), lambda i: (layer, 0, 0)),
            pl.BlockSpec((1, D_MODEL), lambda i: (0, 0)),
        ],
        out_specs=pl.BlockSpec((tm, D_MODEL), lambda i: (i, 0)),
        out_shape=jax.ShapeDtypeStruct((n_blk * tm, D_MODEL), F32),
        compiler_params=pltpu.CompilerParams(
            dimension_semantics=("parallel",), vmem_limit_bytes=VMEM_LIMIT),
        name="outproj_final" if final else "outproj",
    )(ya, yb, yc, x, w_out, final_nw)


def _mlstm_tile(q_ref, k_ref, v_ref, o_ref, z_ref, g_ref, gb_ref, onw_ref, y_ref,
                m_exp, valid, get_c, get_n, put_c, put_n, put_m, nseq, c):
    r = nseq * c
    row, col, same = _seq_masks(nseq, c)
    incl = same & (col <= row)
    g = g_ref[...] + gb_ref[...]
    li = g
    lf = _log_sigmoid(g)
    if valid is not None:
        li = jnp.where(valid, li, NEG_BIG)
        lf = jnp.where(valid, lf, 0.0)
    fcum = _dot_exact(incl.astype(F32), lf)
    if nseq == 1:
        flast = jnp.broadcast_to(fcum[r - 1:r, :], (r, LANES))
    else:
        flast = _dot_exact(same.astype(F32), lf)
    fcum_t = fcum.T
    li_t = li.T
    lane = lax.broadcasted_iota(jnp.int32, (r, LANES), 1)
    heads = range(H_A)
    hsl = [slice(h * HEAD_DIM, (h + 1) * HEAD_DIM) for h in heads]

    dm, mt, a, w, a_s = [], [], [], [], []
    m_new_all = jnp.zeros((r, LANES), F32)
    for h in heads:
        fc = fcum[:, G_AF + h:G_AF + h + 1]
        fr = fcum_t[G_AF + h:G_AF + h + 1, :]
        lic = li[:, G_AI + h:G_AI + h + 1]
        lir = li_t[G_AI + h:G_AI + h + 1, :]
        fl = flast[:, G_AF + h:G_AF + h + 1]
        mcol = m_exp[:, h:h + 1]
        dm_h = jnp.where(incl, fc - fr + lir, NEG_BIG)
        inter = fc + mcol
        mt_h = jnp.maximum(inter, jnp.max(dm_h, axis=1, keepdims=True))
        if nseq == 1:
            m_new = jnp.broadcast_to(mt_h[r - 1:r, :], (r, 1))
        else:
            dl = jnp.where(same, fl - fr + lir, NEG_BIG)
            m_new = jnp.maximum(fl + mcol, jnp.max(dl, axis=1, keepdims=True))
        dm.append(dm_h)
        mt.append(mt_h)
        a.append(jnp.exp(inter - mt_h))
        w.append(jnp.exp(fl - fc + lic - m_new))
        a_s.append(jnp.exp(fl + mcol - m_new))
        m_new_all = jnp.where(lane == h, m_new, m_new_all)
    put_m(m_new_all)

    s = [_dot_nt(q_ref[:, hs], k_ref[:, hs] * QK_SCALE) for hs in hsl]
    p = [jnp.exp(dm[h] - mt[h]) * s[h] for h in heads]
    pv = [_dot(p[h], v_ref[:, hsl[h]]) for h in heads]
    qc = [_cat_rows([_dot(_rows(q_ref[:, hsl[h]], i, c), get_c(i, h)) for i in range(nseq)])
          for h in heads]
    for h in heads:
        hs = hsl[h]
        qh = q_ref[:, hs]
        psum = jnp.sum(p[h], axis=1, keepdims=True)
        qn = _cat_rows([jnp.sum(_rows(qh, i, c) * get_n(i, h), axis=1, keepdims=True)
                        for i in range(nseq)])
        num = a[h] * qc[h] + pv[h]
        den = a[h] * qn + psum
        hh = num / jnp.maximum(jnp.abs(den), jnp.exp(jnp.minimum(-mt[h], EXP_CLIP)))
        hh = hh * _sigmoid(o_ref[:, hs])
        y_ref[:, hs] = _head_norm_gate(hh, onw_ref[:, hs], z_ref[:, hs]).astype(BF16)

    rowv = lax.broadcasted_iota(jnp.int32, (r, 1), 0)
    for h in heads:
        hs = hsl[h]
        kh = k_ref[:, hs] * QK_SCALE
        wv = w[h] * v_ref[:, hs]
        wk = w[h] * kh
        kh_t = kh.T.astype(BF16)
        for i in range(nseq):
            a_i = a_s[h][i * c:i * c + 1, :]
            if nseq == 1:
                upd = jnp.dot(kh_t, wv.astype(BF16), preferred_element_type=F32)
            else:
                seq_rows = (rowv >> _log2(c)) == i
                upd = jnp.dot(kh_t, jnp.where(seq_rows, wv, 0.0).astype(BF16),
                              preferred_element_type=F32)
            put_c(i, h, a_i * get_c(i, h) + upd)
            put_n(i, h, a_i * get_n(i, h) + jnp.sum(_rows(wk, i, c), axis=0, keepdims=True))


def _mlstm_prompt_kernel(q_ref, k_ref, v_ref, o_ref, z_ref, g_ref, gb_ref, onw_ref,
                         y_ref, c_out, n_out, m_out, c_scr, n_scr, m_scr):
    j = pl.program_id(1)

    @pl.when(j == 0)
    def _():
        c_scr[...] = jnp.zeros_like(c_scr)
        n_scr[...] = jnp.zeros_like(n_scr)
        m_scr[...] = jnp.zeros_like(m_scr)

    tok = lax.broadcasted_iota(jnp.int32, (TILE_R, LANES), 0) + j * TILE_R
    valid = tok >= TILE_R - N_META
    m_exp = jnp.broadcast_to(m_scr[0:1, :], (TILE_R, LANES))

    def put_c(i, h, val):
        c_scr[h] = val

    def put_n(i, h, val):
        n_scr[h:h + 1, :] = val

    def put_m(val):
        m_scr[...] = val[0:8, :]

    _mlstm_tile(q_ref, k_ref, v_ref, o_ref, z_ref, g_ref, gb_ref, onw_ref, y_ref,
                m_exp, valid, lambda i, h: c_scr[h], lambda i, h: n_scr[h:h + 1, :],
                put_c, put_n, put_m, 1, TILE_R)

    @pl.when(j == pl.num_programs(1) - 1)
    def _():
        c_out[...] = c_scr[...]
        n_out[...] = n_scr[0:H_A, :]
        m_out[...] = m_scr[0:1, :]


def _mlstm_sample_kernel(q_ref, k_ref, v_ref, o_ref, z_ref, g_ref, gb_ref, onw_ref,
                         c_in, n_in, m_in, y_ref, c_out, n_out, m_out, *, c):
    nseq = TILE_R // c

    def put_c(i, h, val):
        c_out[i, h] = val

    def put_n(i, h, val):
        n_out[i, h:h + 1, :] = val

    def put_m(val):
        m_out[...] = val

    _mlstm_tile(q_ref, k_ref, v_ref, o_ref, z_ref, g_ref, gb_ref, onw_ref, y_ref,
                m_in[...], None, lambda i, h: c_in[i, h], lambda i, h: n_in[i, h:h + 1, :],
                put_c, put_n, put_m, nseq, c)


def _col_spec(width, off, rows):
    assert off % width == 0
    blk = off // width
    return pl.BlockSpec((TILE_R, width), lambda *ids: (rows(*ids), blk))


def _param_spec(width, off, layer):
    assert off % width == 0
    blk = off // width
    return pl.BlockSpec((None, 1, width), lambda *ids: (layer, 0, blk))


def _mlstm_prompt(proj, gate_bias, onw3, layer, bsz, n_chunk, rows):
    t = proj.shape[0]
    in_specs = [_col_spec(W_A, off, rows) for off in (OFF_AQ, OFF_AK, OFF_AV, OFF_AO, OFF_ZA)]
    in_specs += [_col_spec(LANES, OFF_G, rows), _param_spec(LANES, 0, layer),
                 _param_spec(W_A, 0, layer)]
    return pl.pallas_call(
        _mlstm_prompt_kernel,
        grid=(bsz, n_chunk),
        in_specs=in_specs,
        out_specs=[
            pl.BlockSpec((TILE_R, W_A), lambda b, j: (rows(b, j), 0)),
            pl.BlockSpec((None, H_A, HEAD_DIM, HEAD_DIM), lambda b, j: (b, 0, 0, 0)),
            pl.BlockSpec((None, H_A, HEAD_DIM), lambda b, j: (b, 0, 0)),
            pl.BlockSpec((None, 1, LANES), lambda b, j: (b, 0, 0)),
        ],
        out_shape=[
            jax.ShapeDtypeStruct((t, W_A), BF16),
            jax.ShapeDtypeStruct((bsz, H_A, HEAD_DIM, HEAD_DIM), F32),
            jax.ShapeDtypeStruct((bsz, H_A, HEAD_DIM), F32),
            jax.ShapeDtypeStruct((bsz, 1, LANES), F32),
        ],
        scratch_shapes=[pltpu.VMEM((H_A, HEAD_DIM, HEAD_DIM), F32),
                        pltpu.VMEM((8, HEAD_DIM), F32), pltpu.VMEM((8, LANES), F32)],
        compiler_params=pltpu.CompilerParams(
            dimension_semantics=("parallel", "arbitrary"), vmem_limit_bytes=VMEM_LIMIT),
        name="mlstm_prompt",
    )(proj, proj, proj, proj, proj, proj, gate_bias, onw3)


def _in_place(operands, n_blocked, targets):
    specs, aliases = [], {}
    operands = list(operands)
    for out_idx, arr in targets.items():
        if arr is not None:
            aliases[len(operands)] = out_idx
            operands.append(arr)
            specs.append(pl.BlockSpec(memory_space=pl.ANY))
    n_extra = len(specs)

    def strip(kernel):
        def body(*refs):
            kernel(*refs[:n_blocked], *refs[n_blocked + n_extra:])
        return body

    return operands, specs, aliases, strip


def _mlstm_sample(proj, y_prev, c_stack, gate_bias, onw3, c_state, n_state, m_rows, layer,
                  dec_b, c, rows):
    nseq = TILE_R // c
    depth = c_state.shape[0]
    in_specs = [_col_spec(W_A, off, rows) for off in (OFF_AQ, OFF_AK, OFF_AV, OFF_AO, OFF_ZA)]
    in_specs += [_col_spec(LANES, OFF_G, rows), _param_spec(LANES, 0, layer),
                 _param_spec(W_A, 0, layer)]
    in_specs += [
        pl.BlockSpec((None, nseq, H_A, HEAD_DIM, HEAD_DIM), lambda i: (layer, i, 0, 0, 0)),
        pl.BlockSpec((None, nseq, H_A, HEAD_DIM), lambda i: (layer, i, 0, 0)),
        pl.BlockSpec((None, TILE_R, LANES), lambda i: (layer, i, 0)),
    ]
    operands = (proj, proj, proj, proj, proj, proj, gate_bias, onw3, c_state, n_state, m_rows)
    operands, extra, aliases, strip = _in_place(operands, len(in_specs), {0: y_prev, 1: c_stack})
    return pl.pallas_call(
        strip(functools.partial(_mlstm_sample_kernel, c=c)),
        grid=(dec_b // nseq,),
        in_specs=in_specs + extra,
        out_specs=[
            pl.BlockSpec((TILE_R, W_A), lambda i: (rows(i), 0)),
            pl.BlockSpec((None, nseq, H_A, HEAD_DIM, HEAD_DIM), lambda i: (layer, i, 0, 0, 0)),
            pl.BlockSpec((nseq, H_A, HEAD_DIM), lambda i: (i, 0, 0)),
            pl.BlockSpec((TILE_R, LANES), lambda i: (i, 0)),
        ],
        out_shape=[
            jax.ShapeDtypeStruct(y_prev.shape, BF16),
            jax.ShapeDtypeStruct((depth, dec_b, H_A, HEAD_DIM, HEAD_DIM), F32),
            jax.ShapeDtypeStruct((dec_b, H_A, HEAD_DIM), F32),
            jax.ShapeDtypeStruct((dec_b * c, LANES), F32),
        ],
        input_output_aliases=aliases,
        compiler_params=pltpu.CompilerParams(
            dimension_semantics=("parallel",), vmem_limit_bytes=VMEM_LIMIT),
        name="mlstm_sample",
    )(*operands)


def _l2norm(x):
    return x * lax.rsqrt(jnp.sum(x * x, axis=1, keepdims=True) + EPS)


INV_LEAF = 16


def _unit_lower_inverse(a_list, row, col, c):
    leaf = min(INV_LEAF, c)
    kl = _log2(leaf)
    leaf_mask = (row >> kl) == (col >> kl)
    eye = jnp.where(row == col, 1.0, 0.0)
    p = [jnp.where(leaf_mask, -a, 0.0) for a in a_list]
    t = [eye + n for n in p]
    for _ in range(kl - 1):
        p = [_dot(x, x) for x in p]
        t = [ti + _dot(pi, ti) for pi, ti in zip(p, t)]
    w = leaf
    while w < c:
        kw = _log2(w)
        lower_left = (((row >> (kw + 1)) == (col >> (kw + 1)))
                      & (((row >> kw) & 1) == 1) & (((col >> kw) & 1) == 0))
        left = [_dot(ti, jnp.where(lower_left, a, 0.0)) for ti, a in zip(t, a_list)]
        t = [ti - _dot(li, ti) for li, ti in zip(left, t)]
        w *= 2
    return t


def _gdn_tile(conv_win, cw_ref, z_ref, g_ref, gp_ref, onw_ref, y_ref, get_s, put_s, nseq, c):
    r = nseq * c
    row, col, same = _seq_masks(nseq, c)
    incl = same & (col <= row)
    strict = same & (col < row)
    gp = gp_ref[...]
    graw = g_ref[...]
    gdec = -jnp.exp(gp[1:2, :]) * _softplus(graw + gp[0:1, :])
    beta = _sigmoid(graw)
    gcum = _dot_exact(incl.astype(F32), gdec)
    if nseq == 1:
        glast = jnp.broadcast_to(gcum[r - 1:r, :], (r, LANES))
    else:
        glast = _dot_exact(same.astype(F32), gdec)
    gcum_t = gcum.T

    heads = range(H_B)
    hsl = [slice(h * HEAD_DIM, (h + 1) * HEAD_DIM) for h in heads]
    rowv = lax.broadcasted_iota(jnp.int32, (r, 1), 0)

    def conv(lo):
        acc = conv_win(0, slice(lo, lo + HEAD_DIM)) * cw_ref[0:1, lo:lo + HEAD_DIM]
        for jj in range(1, CONV_W):
            acc = acc + conv_win(jj, slice(lo, lo + HEAD_DIM)) * cw_ref[jj:jj + 1, lo:lo + HEAD_DIM]
        return _silu(acc)

    q = [_l2norm(conv(hs.start)) * QK_SCALE for hs in hsl]
    k = [_l2norm(conv(W_B + hs.start)) for hs in hsl]
    v = [conv(2 * W_B + hs.start) for hs in hsl]
    gc = [gcum[:, G_BA + h:G_BA + h + 1] for h in heads]
    gl = [glast[:, G_BA + h:G_BA + h + 1] for h in heads]
    bc = [beta[:, G_BB + h:G_BB + h + 1] for h in heads]
    decay = [jnp.exp(jnp.where(incl, gc[h] - gcum_t[G_BA + h:G_BA + h + 1, :], NEG_BIG))
             for h in heads]
    kk = [_dot_nt(k[h], k[h]) for h in heads]
    qk = [_dot_nt(q[h], k[h]) for h in heads]
    amat = [jnp.where(strict, bc[h] * decay[h] * kk[h], 0.0) for h in heads]
    tinv = _unit_lower_inverse(amat, row, col, c)
    sol = [_dot(tinv[h], jnp.concatenate([bc[h] * v[h], (bc[h] * jnp.exp(gc[h])) * k[h]], axis=1))
           for h in heads]
    wks = [_cat_rows([_dot(_rows(sol[h][:, HEAD_DIM:], i, c), get_s(i, h)) for i in range(nseq)])
           for h in heads]
    qs = [_cat_rows([_dot(_rows(q[h], i, c), get_s(i, h)) for i in range(nseq)]) for h in heads]
    wmat = [sol[h][:, :HEAD_DIM] - wks[h] for h in heads]
    aw = [_dot(decay[h] * qk[h], wmat[h]) for h in heads]
    for h in heads:
        hs = hsl[h]
        oh = jnp.exp(gc[h]) * qs[h] + aw[h]
        y_ref[:, hs] = _head_norm_gate(oh, onw_ref[:, hs], z_ref[:, hs]).astype(BF16)
    for h in heads:
        kd_t = (k[h] * jnp.exp(gl[h] - gc[h])).T.astype(BF16)
        e_last = jnp.exp(gl[h])
        for i in range(nseq):
            if nseq == 1:
                upd = jnp.dot(kd_t, wmat[h].astype(BF16), preferred_element_type=F32)
            else:
                seq_rows = (rowv >> _log2(c)) == i
                upd = jnp.dot(kd_t, jnp.where(seq_rows, wmat[h], 0.0).astype(BF16),
                              preferred_element_type=F32)
            put_s(i, h, e_last[i * c:i * c + 1, :] * get_s(i, h) + upd)


def _gdn_prompt_kernel(q_ref, k_ref, v_ref, z_ref, g_ref, gp_ref, cw_ref, onw_ref,
                       y_ref, s_out, conv_out, s_scr, ext_scr):
    j = pl.program_id(1)

    @pl.when(j == 0)
    def _():
        s_scr[...] = jnp.zeros_like(s_scr)
        ext_scr[0:CONV_PAD, :] = jnp.zeros((CONV_PAD, 3 * W_B), F32)

    ext_scr[CONV_PAD:, 0:W_B] = q_ref[...]
    ext_scr[CONV_PAD:, W_B:2 * W_B] = k_ref[...]
    ext_scr[CONV_PAD:, 2 * W_B:] = v_ref[...]

    def conv_win(jj, cols):
        start = CONV_PAD - (CONV_W - 1) + jj
        return ext_scr[start:start + TILE_R, cols]

    def put_s(i, h, val):
        s_scr[h] = val

    _gdn_tile(conv_win, cw_ref, z_ref, g_ref, gp_ref, onw_ref, y_ref,
              lambda i, h: s_scr[h], put_s, 1, TILE_R)

    @pl.when(j == pl.num_programs(1) - 1)
    def _():
        s_out[...] = s_scr[...]
        conv_out[...] = ext_scr[TILE_R + CONV_PAD - (CONV_W - 1):TILE_R + CONV_PAD, :]

    ext_scr[0:CONV_PAD, :] = ext_scr[TILE_R:TILE_R + CONV_PAD, :]


def _gdn_sample_kernel(q_ref, k_ref, v_ref, z_ref, g_ref, gp_ref, cw_ref, onw_ref,
                       s_in, conv_in, y_ref, s_out, conv_out, ext_scr, *, c):
    nseq = TILE_R // c
    hist = CONV_W - 1
    ext_scr[:, CONV_PAD - hist:CONV_PAD, :] = conv_in[...]
    ext_scr[:, CONV_PAD:, 0:W_B] = q_ref[...].reshape(nseq, c, W_B)
    ext_scr[:, CONV_PAD:, W_B:2 * W_B] = k_ref[...].reshape(nseq, c, W_B)
    ext_scr[:, CONV_PAD:, 2 * W_B:] = v_ref[...].reshape(nseq, c, W_B)

    def conv_win(jj, cols):
        start = CONV_PAD - hist + jj
        return ext_scr[:, start:start + c, cols].reshape(TILE_R, cols.stop - cols.start)

    def put_s(i, h, val):
        s_out[i, h] = val

    _gdn_tile(conv_win, cw_ref, z_ref, g_ref, gp_ref, onw_ref, y_ref,
              lambda i, h: s_in[i, h], put_s, nseq, c)
    conv_out[...] = ext_scr[:, CONV_PAD + c - hist:CONV_PAD + c, :]


def _gdn_prompt(proj, gdn_par, conv_w, onw3, layer, bsz, n_chunk, rows):
    t = proj.shape[0]
    in_specs = [_col_spec(W_B, off, rows) for off in (OFF_BQ, OFF_BK, OFF_BV, OFF_ZB)]
    in_specs += [
        _col_spec(LANES, OFF_G, rows),
        pl.BlockSpec((None, 8, LANES), lambda b, j: (layer, 0, 0)),
        pl.BlockSpec((None, CONV_W, 3 * W_B), lambda b, j: (layer, 0, 0)),
        _param_spec(W_B, W_A, layer),
    ]
    return pl.pallas_call(
        _gdn_prompt_kernel,
        grid=(bsz, n_chunk),
        in_specs=in_specs,
        out_specs=[
            pl.BlockSpec((TILE_R, W_B), lambda b, j: (rows(b, j), 0)),
            pl.BlockSpec((None, H_B, HEAD_DIM, HEAD_DIM), lambda b, j: (b, 0, 0, 0)),
            pl.BlockSpec((None, CONV_W - 1, 3 * W_B), lambda b, j: (b, 0, 0)),
        ],
        out_shape=[
            jax.ShapeDtypeStruct((t, W_B), BF16),
            jax.ShapeDtypeStruct((bsz, H_B, HEAD_DIM, HEAD_DIM), F32),
            jax.ShapeDtypeStruct((bsz, CONV_W - 1, 3 * W_B), F32),
        ],
        scratch_shapes=[pltpu.VMEM((H_B, HEAD_DIM, HEAD_DIM), F32),
                        pltpu.VMEM((TILE_R + CONV_PAD, 3 * W_B), F32)],
        compiler_params=pltpu.CompilerParams(
            dimension_semantics=("parallel", "arbitrary"), vmem_limit_bytes=VMEM_LIMIT),
        name="gdn_prompt",
    )(proj, proj, proj, proj, proj, gdn_par, conv_w, onw3)


def _gdn_sample(proj, y_prev, s_stack, gdn_par, conv_w, onw3, s_state, conv_state, layer,
                dec_b, c, rows):
    nseq = TILE_R // c
    depth = s_state.shape[0]
    in_specs = [_col_spec(W_B, off, rows) for off in (OFF_BQ, OFF_BK, OFF_BV, OFF_ZB)]
    in_specs += [
        _col_spec(LANES, OFF_G, rows),
        pl.BlockSpec((None, 8, LANES), lambda i: (layer, 0, 0)),
        pl.BlockSpec((None, CONV_W, 3 * W_B), lambda i: (layer, 0, 0)),
        _param_spec(W_B, W_A, layer),
        pl.BlockSpec((None, nseq, H_B, HEAD_DIM, HEAD_DIM), lambda i: (layer, i, 0, 0, 0)),
        pl.BlockSpec((None, nseq, CONV_W - 1, 3 * W_B), lambda i: (layer, i, 0, 0)),
    ]
    operands = (proj, proj, proj, proj, proj, gdn_par, conv_w, onw3, s_state, conv_state)
    operands, extra, aliases, strip = _in_place(operands, len(in_specs), {0: y_prev, 1: s_stack})
    return pl.pallas_call(
        strip(functools.partial(_gdn_sample_kernel, c=c)),
        grid=(dec_b // nseq,),
        in_specs=in_specs + extra,
        out_specs=[
            pl.BlockSpec((TILE_R, W_B), lambda i: (rows(i), 0)),
            pl.BlockSpec((None, nseq, H_B, HEAD_DIM, HEAD_DIM), lambda i: (layer, i, 0, 0, 0)),
            pl.BlockSpec((nseq, CONV_W - 1, 3 * W_B), lambda i: (i, 0, 0)),
        ],
        out_shape=[
            jax.ShapeDtypeStruct(y_prev.shape, BF16),
            jax.ShapeDtypeStruct((depth, dec_b, H_B, HEAD_DIM, HEAD_DIM), F32),
            jax.ShapeDtypeStruct((dec_b, CONV_W - 1, 3 * W_B), F32),
        ],
        scratch_shapes=[pltpu.VMEM((nseq, CONV_PAD + c, 3 * W_B), F32)],
        input_output_aliases=aliases,
        compiler_params=pltpu.CompilerParams(
            dimension_semantics=("parallel",), vmem_limit_bytes=VMEM_LIMIT),
        name="gdn_sample",
    )(*operands)


def _hgrn_tile(q_ref, f_ref, i_ref, z_ref, lbp_ref, onw_ref, y_ref, get_s, put_s,
               nseq, c, layer):
    r = nseq * c
    row, col, same = _seq_masks(nseq, c)
    incl = same & (col <= row)
    rowv = lax.broadcasted_iota(jnp.int32, (r, 1), 0)
    lbp = lbp_ref[...]
    e = jnp.exp(lbp - jnp.max(lbp, axis=0, keepdims=True))
    sm = e / jnp.sum(e, axis=0, keepdims=True)
    lb = jnp.zeros((1, W_C), F32)
    for l in range(1, layer + 1):
        lb = lb + sm[l:l + 1, :]

    cf = f_ref[...]
    logf = _log_sigmoid(cf) + jnp.log1p(lb * jnp.exp(jnp.minimum(-cf, EXP_CLIP)))
    kall = (1.0 - lb) * _sigmoid(-cf)
    bcum = _dot_exact(incl.astype(F32), logf)
    if nseq == 1:
        blast = jnp.broadcast_to(bcum[r - 1:r, :], (r, W_C))
    else:
        blast = _dot_exact(same.astype(F32), logf)

    levels = []
    w = c // 2
    while w >= 1:
        levels.append(w)
        w //= 2
    rowf = lax.broadcasted_iota(jnp.int32, (r, W_C), 0)
    refs = []
    for w in levels:
        if 2 * w >= 8:
            refs.append(_cat_rows([
                jnp.broadcast_to(bcum[g * 2 * w + w - 1:g * 2 * w + w, :], (2 * w, W_C))
                for g in range(r // (2 * w))]))
        else:
            pos = rowf & (2 * w - 1)
            ref = bcum
            for off in range(2 * w):
                if off != w - 1:
                    ref = jnp.where(pos == off, pltpu.roll(bcum, (off - (w - 1)) % r, axis=0), ref)
            refs.append(ref)

    heads = range(H_C)
    hsl = [slice(h * HEAD_DIM, (h + 1) * HEAD_DIM) for h in heads]
    q = [_silu(q_ref[:, hs]) for hs in hsl]
    k = [kall[:, hs] for hs in hsl]
    b = [bcum[:, hs] for hs in hsl]
    diag = row == col
    amat = [jnp.where(diag, _dot_nt(q[h], k[h]), 0.0) for h in heads]
    for w, ref_all in zip(levels, refs):
        lw = _log2(w)
        tgt = ((rowv >> lw) & 1) == 1
        pair = (row >> (lw + 1)) == (col >> (lw + 1))
        qp = [q[h] * jnp.exp(jnp.where(tgt, b[h] - ref_all[:, hsl[h]], NEG_BIG)) for h in heads]
        kp = [k[h] * jnp.exp(jnp.where(tgt, NEG_BIG, ref_all[:, hsl[h]] - b[h])) for h in heads]
        amat = [amat[h] + jnp.where(pair, _dot_nt(qp[h], kp[h]), 0.0) for h in heads]
    qs = [_cat_rows([_dot(_rows(q[h] * jnp.exp(b[h]), i, c), get_s(i, h)) for i in range(nseq)])
          for h in heads]
    av = [_dot(amat[h], i_ref[:, hsl[h]]) for h in heads]
    for h in heads:
        hs = hsl[h]
        y_ref[:, hs] = _head_norm_gate(qs[h] + av[h], onw_ref[:, hs], z_ref[:, hs]).astype(BF16)
    for h in heads:
        hs = hsl[h]
        vh = i_ref[:, hs]
        bl = blast[:, hs]
        kd_t = (k[h] * jnp.exp(bl - b[h])).T.astype(BF16)
        eb_t = jnp.exp(bl).T
        for i in range(nseq):
            if nseq == 1:
                upd = jnp.dot(kd_t, vh.astype(BF16), preferred_element_type=F32)
            else:
                seq_rows = (rowv >> _log2(c)) == i
                upd = jnp.dot(kd_t, jnp.where(seq_rows, vh, 0.0).astype(BF16),
                              preferred_element_type=F32)
            put_s(i, h, eb_t[:, i * c:i * c + 1] * get_s(i, h) + upd)


def _hgrn_prompt_kernel(q_ref, f_ref, i_ref, z_ref, lbp_ref, onw_ref, y_ref, s_out, s_scr,
                        *, layer):
    j = pl.program_id(1)

    @pl.when(j == 0)
    def _():
        s_scr[...] = jnp.zeros_like(s_scr)

    def put_s(i, h, val):
        s_scr[h] = val

    _hgrn_tile(q_ref, f_ref, i_ref, z_ref, lbp_ref, onw_ref, y_ref,
               lambda i, h: s_scr[h], put_s, 1, TILE_R, layer)

    @pl.when(j == pl.num_programs(1) - 1)
    def _():
        s_out[...] = s_scr[...]


def _hgrn_sample_kernel(q_ref, f_ref, i_ref, z_ref, lbp_ref, onw_ref, s_in, y_ref, s_out,
                        *, c, layer):
    def put_s(i, h, val):
        s_out[i, h] = val

    _hgrn_tile(q_ref, f_ref, i_ref, z_ref, lbp_ref, onw_ref, y_ref,
               lambda i, h: s_in[i, h], put_s, TILE_R // c, c, layer)


def _hgrn_prompt(proj, lb_par, onw3, layer, bsz, n_chunk, rows):
    t = proj.shape[0]
    depth = lb_par.shape[0]
    in_specs = [_col_spec(W_C, off, rows) for off in (OFF_CQ, OFF_CF, OFF_CI, OFF_ZC)]
    in_specs += [pl.BlockSpec((depth, W_C), lambda b, j: (0, 0)),
                 _param_spec(W_C, W_A + W_B, layer)]
    return pl.pallas_call(
        functools.partial(_hgrn_prompt_kernel, layer=layer),
        grid=(bsz, n_chunk),
        in_specs=in_specs,
        out_specs=[
            pl.BlockSpec((TILE_R, W_C), lambda b, j: (rows(b, j), 0)),
            pl.BlockSpec((None, H_C, HEAD_DIM, HEAD_DIM), lambda b, j: (b, 0, 0, 0)),
        ],
        out_shape=[
            jax.ShapeDtypeStruct((t, W_C), BF16),
            jax.ShapeDtypeStruct((bsz, H_C, HEAD_DIM, HEAD_DIM), F32),
        ],
        scratch_shapes=[pltpu.VMEM((H_C, HEAD_DIM, HEAD_DIM), F32)],
        compiler_params=pltpu.CompilerParams(
            dimension_semantics=("parallel", "arbitrary"), vmem_limit_bytes=VMEM_LIMIT),
        name="hgrn_prompt",
    )(proj, proj, proj, proj, lb_par, onw3)


def _hgrn_sample(proj, y_prev, s_stack, lb_par, onw3, s_state, layer, dec_b, c, rows):
    nseq = TILE_R // c
    depth = lb_par.shape[0]
    in_specs = [_col_spec(W_C, off, rows) for off in (OFF_CQ, OFF_CF, OFF_CI, OFF_ZC)]
    in_specs += [
        pl.BlockSpec((depth, W_C), lambda i: (0, 0)),
        _param_spec(W_C, W_A + W_B, layer),
        pl.BlockSpec((None, nseq, H_C, HEAD_DIM, HEAD_DIM), lambda i: (layer, i, 0, 0, 0)),
    ]
    operands = (proj, proj, proj, proj, lb_par, onw3, s_state)
    operands, extra, aliases, strip = _in_place(operands, len(in_specs), {0: y_prev, 1: s_stack})
    return pl.pallas_call(
        strip(functools.partial(_hgrn_sample_kernel, c=c, layer=layer)),
        grid=(dec_b // nseq,),
        in_specs=in_specs + extra,
        out_specs=[
            pl.BlockSpec((TILE_R, W_C), lambda i: (rows(i), 0)),
            pl.BlockSpec((None, nseq, H_C, HEAD_DIM, HEAD_DIM), lambda i: (layer, i, 0, 0, 0)),
        ],
        out_shape=[
            jax.ShapeDtypeStruct(y_prev.shape, BF16),
            jax.ShapeDtypeStruct((depth, dec_b, H_C, HEAD_DIM, HEAD_DIM), F32),
        ],
        input_output_aliases=aliases,
        compiler_params=pltpu.CompilerParams(
            dimension_semantics=("parallel",), vmem_limit_bytes=VMEM_LIMIT),
        name="hgrn_sample",
    )(*operands)


def _pack_w_in(w_in):
    sizes = (W_A, W_A, W_A, W_A, H_A, H_A, W_B, W_B, W_B, H_B, H_B, W_C, W_C, W_C, MIX)
    offs = [0]
    for s in sizes:
        offs.append(offs[-1] + s)
    (aq, ak, av, ao, ai, af, bq, bk, bv, ba, bb, cq, cf, ci, z) = [
        w_in[..., offs[n]:offs[n + 1]] for n in range(len(sizes))]
    za, zb, zc = z[..., :W_A], z[..., W_A:W_A + W_B], z[..., W_A + W_B:]
    n_gate = 2 * H_A + 2 * H_B
    pad = jnp.zeros(w_in.shape[:-1] + (2 * LANES - n_gate,), w_in.dtype)
    packed = jnp.concatenate([cq, cf, ci, aq, ak, av, ao, bq, bk, bv, za, zb,
                              ai, af, ba, bb, pad, zc], axis=-1)
    assert packed.shape[-1] == N_PACK
    return packed.astype(BF16)


def _lane_row(depth, pieces):
    row = jnp.zeros((depth, LANES), F32)
    for off, val in pieces:
        row = lax.dynamic_update_slice(row, val.astype(F32), (0, off))
    return row[:, None, :]


def _largest_tile(n, cap, mult):
    best = None
    for cand in range(mult, cap + 1, mult):
        if n % cand == 0:
            best = cand
    assert best is not None
    return best


def kernel(x_prompt, x_sample, state_mlstm_C, state_mlstm_n, state_mlstm_m, state_gdn_S,
           state_gdn_conv, state_hgrn_S, meta_tokens, norm_w, w_in, mlstm_gate_b, gdn_A_log,
           gdn_dt_bias, gdn_conv_w, hgrn_lower_bounds, out_norm_w, w_out, final_norm_w):
    bsz, seq_len, d = x_prompt.shape
    dec_b, dec_seq, _ = x_sample.shape
    depth = w_in.shape[0]
    assert d == D_MODEL and seq_len % TILE_R == 0 and TILE_R % dec_seq == 0
    nseq = TILE_R // dec_seq
    assert dec_b % nseq == 0

    n_main, n_dec = bsz * seq_len, dec_b * dec_seq
    t_all = n_main + n_dec + bsz * TILE_R
    slot = jnp.concatenate([jnp.zeros((TILE_R - N_META, d), x_prompt.dtype),
                            meta_tokens.astype(x_prompt.dtype)], axis=0)
    x = jnp.concatenate([x_prompt.reshape(n_main, d), x_sample.reshape(n_dec, d),
                         jnp.tile(slot, (bsz, 1))], axis=0)

    chunks = seq_len // TILE_R
    n_chunk = chunks + 1
    meta_blk0 = (n_main + n_dec) // TILE_R
    dec_blk0 = n_main // TILE_R

    def prompt_rows(b, j):
        return jnp.where(j == 0, meta_blk0 + b, b * chunks + j - 1)

    def dec_rows(i):
        return dec_blk0 + i

    w_pack = _pack_w_in(w_in)
    w_out_bf = w_out.astype(BF16)
    norm_w3 = norm_w.astype(F32)[:, None, :]
    onw3 = out_norm_w.astype(F32)[:, None, :]
    final_nw = final_norm_w.astype(F32)[None, :]
    gate_bias = _lane_row(depth, [(G_AI, mlstm_gate_b[:, 0]), (G_AF, mlstm_gate_b[:, 1])])
    gdn_par = jnp.concatenate([
        _lane_row(depth, [(G_BA, gdn_dt_bias)]), _lane_row(depth, [(G_BA, gdn_A_log)]),
        jnp.zeros((depth, 6, LANES), F32)], axis=1)
    conv_w = gdn_conv_w.astype(F32)
    lb_par = hgrn_lower_bounds.astype(F32)
    m_rows = jnp.repeat(
        jnp.pad(state_mlstm_m.astype(F32), ((0, 0), (0, 0), (0, LANES - H_A))), dec_seq, axis=1)

    tm = _largest_tile(t_all, 1024, 16)
    tn = _largest_tile(N_PACK, 1024, LANES)
    tm_out = _largest_tile(math.gcd(n_main, n_dec), 512, 16)

    new_p = [[] for _ in range(6)]
    new_s = [[] for _ in range(3)]
    c_s = s_s = h_s = None
    y_prompt = y_sample = None
    for l in range(depth):
        proj = _inproj(x, norm_w3, w_pack, l, tm, tn)

        ya, c_p, n_p, m_p = _mlstm_prompt(proj, gate_bias, onw3, l, bsz, n_chunk, prompt_rows)
        ya, c_s, n_s, m_s = _mlstm_sample(proj, ya, c_s, gate_bias, onw3, state_mlstm_C,
                                          state_mlstm_n, m_rows, l, dec_b, dec_seq, dec_rows)
        yb, s_p, cv_p = _gdn_prompt(proj, gdn_par, conv_w, onw3, l, bsz, n_chunk, prompt_rows)
        yb, s_s, cv_s = _gdn_sample(proj, yb, s_s, gdn_par, conv_w, onw3, state_gdn_S,
                                    state_gdn_conv, l, dec_b, dec_seq, dec_rows)
        yc, h_p = _hgrn_prompt(proj, lb_par, onw3, l, bsz, n_chunk, prompt_rows)
        yc, h_s = _hgrn_sample(proj, yc, h_s, lb_par, onw3, state_hgrn_S, l, dec_b, dec_seq,
                               dec_rows)

        for lst, val in zip(new_p, (c_p, n_p, m_p[:, 0, :H_A], s_p, cv_p, h_p)):
            lst.append(val)
        for lst, val in zip(new_s, (n_s, m_s[::dec_seq, :H_A], cv_s)):
            lst.append(val)

        if l + 1 < depth:
            x = _outproj(ya, yb, yc, x, w_out_bf, l, final_nw, tm_out, 0, t_all // tm_out, False)
        else:
            y_prompt = _outproj(ya, yb, yc, x, w_out_bf, l, final_nw, tm_out, 0,
                                n_main // tm_out, True)
            y_sample = _outproj(ya, yb, yc, x, w_out_bf, l, final_nw, tm_out,
                                n_main // tm_out, n_dec // tm_out, True)

    outs_p = [jnp.stack(a, axis=0) for a in new_p]
    n_all, m_all, cv_all = [jnp.stack(a, axis=0) for a in new_s]
    return (y_prompt.reshape(bsz, seq_len, d), y_sample.reshape(dec_b, dec_seq, d),
            *outs_p, c_s, n_all, m_all, s_s, cv_all, h_s)
```

```python
import functools
import math

import jax
import jax.numpy as jnp
from jax import lax
from jax.experimental import pallas as pl
from jax.experimental.pallas import tpu as pltpu

F32 = jnp.float32
BF16 = jnp.bfloat16

D_MODEL = 2048
HEAD_DIM = 128
H_A, H_B, H_C = 6, 6, 4
W_A, W_B, W_C = H_A * HEAD_DIM, H_B * HEAD_DIM, H_C * HEAD_DIM
MIX = W_A + W_B + W_C
N_META = 16
CONV_W = 4
EPS = 1e-6
NEG_BIG = -1e30
EXP_CLIP = 60.0
QK_SCALE = HEAD_DIM ** -0.5

TILE_R = 128
LANES = 128
CONV_PAD = 8

OFF_AQ = 0
OFF_AK, OFF_AV, OFF_AO = W_A, 2 * W_A, 3 * W_A
OFF_BQ = 4 * W_A
OFF_BK, OFF_BV = OFF_BQ + W_B, OFF_BQ + 2 * W_B
OFF_C = OFF_BQ + 3 * W_B
OFF_ZA = OFF_C + 3 * W_C
OFF_ZB = OFF_ZA + W_A
OFF_ZC = OFF_ZB + W_B
OFF_G = OFF_ZC + W_C
N_PACK = OFF_G + 2 * LANES
G_AI, G_AF, G_BA, G_BB = 0, H_A, 2 * H_A, 2 * H_A + H_B
TN = 1024
W_SEGMENTS = ((OFF_AQ, OFF_BQ, 0), (OFF_BQ, OFF_C, 2 * H_A), (OFF_C, OFF_G, 2 * H_A + 2 * H_B))
W_ROW_CHUNK = 256

VMEM_LIMIT = 56 * 1024 * 1024


def _sigmoid(x):
    return 1.0 / (1.0 + jnp.exp(-x))


def _silu(x):
    return x * _sigmoid(x)


def _softplus(x):
    return jnp.maximum(x, 0.0) + jnp.log1p(jnp.exp(-jnp.abs(x)))


def _log_sigmoid(x):
    return -_softplus(-x)


def _dot(a, b):
    return jnp.dot(a.astype(BF16), b.astype(BF16), preferred_element_type=F32)


def _dot_nt(a, b):
    return lax.dot_general(a.astype(BF16), b.astype(BF16), (((1,), (1,)), ((), ())),
                           preferred_element_type=F32)


def _dot_exact(sel, x):
    hi = x.astype(BF16)
    r1 = x - hi.astype(F32)
    mid = r1.astype(BF16)
    lo = (r1 - mid.astype(F32)).astype(BF16)
    s = sel.astype(BF16)
    return (jnp.dot(s, hi, preferred_element_type=F32)
            + jnp.dot(s, mid, preferred_element_type=F32)
            + jnp.dot(s, lo, preferred_element_type=F32))


def _log2(n):
    k = int(math.log2(n))
    assert (1 << k) == n
    return k


def _seq_masks(nseq, c):
    r = nseq * c
    row = lax.broadcasted_iota(jnp.int32, (r, r), 0)
    col = lax.broadcasted_iota(jnp.int32, (r, r), 1)
    if nseq == 1:
        same = row >= 0
    else:
        k = _log2(c)
        same = (row >> k) == (col >> k)
    return row, col, same


def _head_norm_gate(h, onw, z):
    hn = h * lax.rsqrt(jnp.mean(h * h, axis=1, keepdims=True) + EPS)
    return hn * onw * _silu(z)


def _rows(x, i, c):
    return x[i * c:(i + 1) * c]


def _cat_rows(parts):
    return parts[0] if len(parts) == 1 else jnp.concatenate(parts, axis=0)


def _tile_pieces(jj):
    lo, hi = jj * TN, (jj + 1) * TN
    pieces = []
    for p0, p1, shift in W_SEGMENTS:
        a, b = max(p0, lo), min(p1, hi)
        if a < b:
            pieces.append((a - lo, b - lo, shift))
    return pieces


def _fill_weight_tile(w0_ref, w1_ref, wg_ref, wb_scr, jj):
    pieces = _tile_pieces(jj)

    def chunk(ci, carry):
        rows = pl.ds(pl.multiple_of(ci * W_ROW_CHUNK, W_ROW_CHUNK), W_ROW_CHUNK)
        for d0, d1, shift in pieces:
            if shift == 0:
                wb_scr[rows, d0:d1] = w0_ref[rows, d0:d1].astype(BF16)
                continue
            end = d1 + LANES
            if end <= TN:
                src = w0_ref[rows, d0:end]
            else:
                src = jnp.concatenate([w0_ref[rows, d0:TN], w1_ref[rows, 0:end - TN]], axis=1)
            src = pltpu.roll(src, (end - d0) - shift, axis=1)
            wb_scr[rows, d0:d1] = src[:, :d1 - d0].astype(BF16)
        return carry

    lax.fori_loop(0, D_MODEL // W_ROW_CHUNK, chunk, 0)
    g0 = OFF_G - jj * TN
    if 0 <= g0 < TN:
        wb_scr[:, g0:g0 + LANES] = wg_ref[...]
        wb_scr[:, g0 + LANES:TN] = jnp.zeros((D_MODEL, TN - g0 - LANES), BF16)


def _inproj_kernel(h_ref, w0_ref, w1_ref, wg_ref, o_ref, wb_scr):
    j = pl.program_id(0)

    @pl.when(pl.program_id(1) == 0)
    def _():
        for jj in range(N_PACK // TN):
            @pl.when(j == jj)
            def _(jj=jj):
                _fill_weight_tile(w0_ref, w1_ref, wg_ref, wb_scr, jj)

    o_ref[...] = jnp.dot(h_ref[...], wb_scr[...], preferred_element_type=F32)


def _inproj(h, w_in, w_gate, layer, tm):
    t = h.shape[0]
    n_tiles = N_PACK // TN
    last_blk = (w_in.shape[2] - 1) // TN
    assert n_tiles - 1 <= last_blk
    return pl.pallas_call(
        _inproj_kernel,
        grid=(n_tiles, t // tm),
        in_specs=[
            pl.BlockSpec((tm, D_MODEL), lambda j, i: (i, 0)),
            pl.BlockSpec((None, D_MODEL, TN), lambda j, i: (layer, 0, j)),
            pl.BlockSpec((None, D_MODEL, TN), lambda j, i: (layer, 0, jnp.minimum(j + 1, last_blk))),
            pl.BlockSpec((None, D_MODEL, LANES), lambda j, i: (layer, 0, 0)),
        ],
        out_specs=pl.BlockSpec((tm, TN), lambda j, i: (i, j)),
        out_shape=jax.ShapeDtypeStruct((t, N_PACK), F32),
        scratch_shapes=[pltpu.VMEM((D_MODEL, TN), BF16)],
        compiler_params=pltpu.CompilerParams(
            dimension_semantics=("arbitrary", "arbitrary"), vmem_limit_bytes=VMEM_LIMIT),
        name="inproj",
    )(h, w_in, w_in, w_gate)


def _rmsnorm(x, w):
    ms = jnp.mean(x * x, axis=1, keepdims=True)
    return x * lax.rsqrt(ms + EPS) * w


def _prep_kernel(xp_ref, xs_ref, meta_ref, nw_ref, x_ref, h_ref, *, n_chunk, bsz):
    i = pl.program_id(0)
    d = x_ref.shape[1]

    def emit(x):
        x_ref[...] = x
        h_ref[...] = _rmsnorm(x, nw_ref[...]).astype(BF16)

    @pl.when(i == 0)
    def _():
        slot = jnp.concatenate([jnp.zeros((TILE_R - N_META, d), F32), meta_ref[...]], axis=0)
        emit(jnp.concatenate([slot] * bsz, axis=0))

    @pl.when((i > 0) & (i < n_chunk))
    def _():
        emit(xp_ref[...].reshape(bsz * TILE_R, d))

    @pl.when(i >= n_chunk)
    def _():
        emit(xs_ref[...])


def _prep(x_prompt, x_sample, meta, norm_w3, bsz, n_chunk):
    d = x_prompt.shape[-1]
    tm = bsz * TILE_R
    n_dec = x_sample.shape[0] * x_sample.shape[1]
    assert n_dec % tm == 0
    xp = x_prompt.reshape(bsz, n_chunk - 1, TILE_R, d)
    xs = x_sample.reshape(n_dec, d)
    n_blk = n_chunk + n_dec // tm
    t = n_blk * tm
    return pl.pallas_call(
        functools.partial(_prep_kernel, n_chunk=n_chunk, bsz=bsz),
        grid=(n_blk,),
        in_specs=[
            pl.BlockSpec((bsz, None, TILE_R, d), lambda i: (0, jnp.clip(i - 1, 0, n_chunk - 2), 0, 0)),
            pl.BlockSpec((tm, d), lambda i: (jnp.clip(i - n_chunk, 0, n_dec // tm - 1), 0)),
            pl.BlockSpec((N_META, d), lambda i: (0, 0)),
            pl.BlockSpec((None, 1, d), lambda i: (0, 0, 0)),
        ],
        out_specs=[pl.BlockSpec((tm, d), lambda i: (i, 0)),
                   pl.BlockSpec((tm, d), lambda i: (i, 0))],
        out_shape=[jax.ShapeDtypeStruct((t, d), F32), jax.ShapeDtypeStruct((t, d), BF16)],
        compiler_params=pltpu.CompilerParams(
            dimension_semantics=("arbitrary",), vmem_limit_bytes=VMEM_LIMIT),
        name="prep",
    )(xp, xs, meta, norm_w3)


def _outproj_residual(ya_ref, yb_ref, yc_ref, x_ref, w_ref):
    acc = jnp.dot(ya_ref[...], w_ref[0:W_A, :], preferred_element_type=F32)
    acc = acc + jnp.dot(yb_ref[...], w_ref[W_A:W_A + W_B, :], preferred_element_type=F32)
    acc = acc + jnp.dot(yc_ref[...], w_ref[W_A + W_B:MIX, :], preferred_element_type=F32)
    return x_ref[...] + acc


def _outproj_mid_kernel(ya_ref, yb_ref, yc_ref, x_ref, w_ref, nw_ref, x_out, h_out):
    xn = _outproj_residual(ya_ref, yb_ref, yc_ref, x_ref, w_ref)
    x_out[...] = xn
    h_out[...] = _rmsnorm(xn, nw_ref[...]).astype(BF16)


def _outproj_final_kernel(ya_ref, yb_ref, yc_ref, x_ref, w_ref, nw_ref, y_out):
    xn = _outproj_residual(ya_ref, yb_ref, yc_ref, x_ref, w_ref)
    y_out[...] = _rmsnorm(xn, nw_ref[...]).reshape(y_out.shape)


def _outproj_specs(tm, row_blk0, layer, nw_spec):
    rows = lambda i: (row_blk0 + i, 0)
    return [
        pl.BlockSpec((tm, W_A), rows),
        pl.BlockSpec((tm, W_B), rows),
        pl.BlockSpec((tm, W_C), rows),
        pl.BlockSpec((tm, D_MODEL), rows),
        pl.BlockSpec((None, MIX, D_MODEL), lambda i: (layer, 0, 0)),
        nw_spec,
    ]


def _outproj_mid(ya, yb, yc, x, w_out, layer, norm_w3, tm):
    t = x.shape[0]
    blk = pl.BlockSpec((tm, D_MODEL), lambda i: (i, 0))
    nw_spec = pl.BlockSpec((None, 1, D_MODEL), lambda i: (layer + 1, 0, 0))
    return pl.pallas_call(
        _outproj_mid_kernel,
        grid=(t // tm,),
        in_specs=_outproj_specs(tm, 0, layer, nw_spec),
        out_specs=[blk, blk],
        out_shape=[jax.ShapeDtypeStruct((t, D_MODEL), F32), jax.ShapeDtypeStruct((t, D_MODEL), BF16)],
        compiler_params=pltpu.CompilerParams(
            dimension_semantics=("parallel",), vmem_limit_bytes=VMEM_LIMIT),
        name="outproj",
    )(ya, yb, yc, x, w_out, norm_w3)


def _outproj_final(ya, yb, yc, x, w_out, layer, final_nw, tm, row_blk0, n_blk, out_shape, out_spec):
    nw_spec = pl.BlockSpec((1, D_MODEL), lambda i: (0, 0))
    return pl.pallas_call(
        _outproj_final_kernel,
        grid=(n_blk,),
        in_specs=_outproj_specs(tm, row_blk0, layer, nw_spec),
        out_specs=out_spec,
        out_shape=jax.ShapeDtypeStruct(out_shape, F32),
        compiler_params=pltpu.CompilerParams(
            dimension_semantics=("parallel",), vmem_limit_bytes=VMEM_LIMIT),
        name="outproj_final",
    )(ya, yb, yc, x, w_out, final_nw)


def _mlstm_tile(q_ref, k_ref, v_ref, o_ref, z_ref, g_ref, gb_ref, onw_ref, y_ref,
                m_exp, valid, get_c, get_n, put_c, put_n, put_m, nseq, c):
    r = nseq * c
    row, col, same = _seq_masks(nseq, c)
    incl = same & (col <= row)
    g = g_ref[...] + gb_ref[...]
    li = g
    lf = _log_sigmoid(g)
    if valid is not None:
        li = jnp.where(valid, li, NEG_BIG)
        lf = jnp.where(valid, lf, 0.0)
    fcum = _dot_exact(incl.astype(F32), lf)
    if nseq == 1:
        flast = jnp.broadcast_to(fcum[r - 1:r, :], (r, LANES))
    else:
        flast = _dot_exact(same.astype(F32), lf)
    fcum_t = fcum.T
    li_t = li.T
    lane = lax.broadcasted_iota(jnp.int32, (r, LANES), 1)
    heads = range(H_A)
    hsl = [slice(h * HEAD_DIM, (h + 1) * HEAD_DIM) for h in heads]

    dm, mt, a, w, a_s = [], [], [], [], []
    m_new_all = jnp.zeros((r, LANES), F32)
    for h in heads:
        fc = fcum[:, G_AF + h:G_AF + h + 1]
        fr = fcum_t[G_AF + h:G_AF + h + 1, :]
        lic = li[:, G_AI + h:G_AI + h + 1]
        lir = li_t[G_AI + h:G_AI + h + 1, :]
        fl = flast[:, G_AF + h:G_AF + h + 1]
        mcol = m_exp[:, h:h + 1]
        dm_h = jnp.where(incl, fc - fr + lir, NEG_BIG)
        inter = fc + mcol
        mt_h = jnp.maximum(inter, jnp.max(dm_h, axis=1, keepdims=True))
        if nseq == 1:
            m_new = jnp.broadcast_to(mt_h[r - 1:r, :], (r, 1))
        else:
            dl = jnp.where(same, fl - fr + lir, NEG_BIG)
            m_new = jnp.maximum(fl + mcol, jnp.max(dl, axis=1, keepdims=True))
        dm.append(dm_h)
        mt.append(mt_h)
        a.append(jnp.exp(inter - mt_h))
        w.append(jnp.exp(fl - fc + lic - m_new))
        a_s.append(jnp.exp(fl + mcol - m_new))
        m_new_all = jnp.where(lane == h, m_new, m_new_all)
    put_m(m_new_all)

    s = [_dot_nt(q_ref[:, hs], k_ref[:, hs] * QK_SCALE) for hs in hsl]
    p = [jnp.exp(dm[h] - mt[h]) * s[h] for h in heads]
    pv = [_dot(p[h], v_ref[:, hsl[h]]) for h in heads]
    qc = [_cat_rows([_dot(_rows(q_ref[:, hsl[h]], i, c), get_c(i, h)) for i in range(nseq)])
          for h in heads]
    for h in heads:
        hs = hsl[h]
        qh = q_ref[:, hs]
        psum = jnp.sum(p[h], axis=1, keepdims=True)
        qn = _cat_rows([jnp.sum(_rows(qh, i, c) * get_n(i, h), axis=1, keepdims=True)
                        for i in range(nseq)])
        num = a[h] * qc[h] + pv[h]
        den = a[h] * qn + psum
        hh = num / jnp.maximum(jnp.abs(den), jnp.exp(jnp.minimum(-mt[h], EXP_CLIP)))
        hh = hh * _sigmoid(o_ref[:, hs])
        y_ref[:, hs] = _head_norm_gate(hh, onw_ref[:, hs], z_ref[:, hs]).astype(BF16)

    rowv = lax.broadcasted_iota(jnp.int32, (r, 1), 0)
    for h in heads:
        hs = hsl[h]
        kh = k_ref[:, hs] * QK_SCALE
        wv = w[h] * v_ref[:, hs]
        wk = w[h] * kh
        kh_t = kh.T.astype(BF16)
        for i in range(nseq):
            a_i = a_s[h][i * c:i * c + 1, :]
            if nseq == 1:
                upd = jnp.dot(kh_t, wv.astype(BF16), preferred_element_type=F32)
            else:
                seq_rows = (rowv >> _log2(c)) == i
                upd = jnp.dot(kh_t, jnp.where(seq_rows, wv, 0.0).astype(BF16),
                              preferred_element_type=F32)
            put_c(i, h, a_i * get_c(i, h) + upd)
            put_n(i, h, a_i * get_n(i, h) + jnp.sum(_rows(wk, i, c), axis=0, keepdims=True))


def _mlstm_prompt_kernel(q_ref, k_ref, v_ref, o_ref, z_ref, g_ref, gb_ref, onw_ref,
                         y_ref, c_out, n_out, m_out, c_scr, n_scr, m_scr):
    j = pl.program_id(1)

    @pl.when(j == 0)
    def _():
        c_scr[...] = jnp.zeros_like(c_scr)
        n_scr[...] = jnp.zeros_like(n_scr)
        m_scr[...] = jnp.zeros_like(m_scr)

    tok = lax.broadcasted_iota(jnp.int32, (TILE_R, LANES), 0) + j * TILE_R
    valid = tok >= TILE_R - N_META
    m_exp = jnp.broadcast_to(m_scr[0:1, :], (TILE_R, LANES))

    def put_c(i, h, val):
        c_scr[h] = val

    def put_n(i, h, val):
        n_scr[h:h + 1, :] = val

    def put_m(val):
        m_scr[...] = val[0:8, :]

    _mlstm_tile(q_ref, k_ref, v_ref, o_ref, z_ref, g_ref, gb_ref, onw_ref, y_ref,
                m_exp, valid, lambda i, h: c_scr[h], lambda i, h: n_scr[h:h + 1, :],
                put_c, put_n, put_m, 1, TILE_R)

    @pl.when(j == pl.num_programs(1) - 1)
    def _():
        c_out[...] = c_scr[...]
        n_out[...] = n_scr[0:H_A, :]
        m_out[...] = m_scr[0:1, :]


def _mlstm_sample_kernel(q_ref, k_ref, v_ref, o_ref, z_ref, g_ref, gb_ref, onw_ref,
                         c_in, n_in, m_in, y_ref, c_out, n_out, m_out, *, c):
    nseq = TILE_R // c

    def put_c(i, h, val):
        c_out[i, h] = val

    def put_n(i, h, val):
        n_out[i, h:h + 1, :] = val

    def put_m(val):
        m_out[...] = val

    _mlstm_tile(q_ref, k_ref, v_ref, o_ref, z_ref, g_ref, gb_ref, onw_ref, y_ref,
                m_in[...], None, lambda i, h: c_in[i, h], lambda i, h: n_in[i, h:h + 1, :],
                put_c, put_n, put_m, nseq, c)


def _col_spec(width, off, rows):
    assert off % width == 0
    blk = off // width
    return pl.BlockSpec((TILE_R, width), lambda *ids: (rows(*ids), blk))


def _param_spec(width, off, layer):
    assert off % width == 0
    blk = off // width
    return pl.BlockSpec((None, 1, width), lambda *ids: (layer, 0, blk))


def _mlstm_prompt(proj, gate_bias, onw3, layer, bsz, n_chunk, rows):
    t = proj.shape[0]
    in_specs = [_col_spec(W_A, off, rows) for off in (OFF_AQ, OFF_AK, OFF_AV, OFF_AO, OFF_ZA)]
    in_specs += [_col_spec(LANES, OFF_G, rows), _param_spec(LANES, 0, layer),
                 _param_spec(W_A, 0, layer)]
    return pl.pallas_call(
        _mlstm_prompt_kernel,
        grid=(bsz, n_chunk),
        in_specs=in_specs,
        out_specs=[
            pl.BlockSpec((TILE_R, W_A), lambda b, j: (rows(b, j), 0)),
            pl.BlockSpec((None, H_A, HEAD_DIM, HEAD_DIM), lambda b, j: (b, 0, 0, 0)),
            pl.BlockSpec((None, H_A, HEAD_DIM), lambda b, j: (b, 0, 0)),
            pl.BlockSpec((None, 1, LANES), lambda b, j: (b, 0, 0)),
        ],
        out_shape=[
            jax.ShapeDtypeStruct((t, W_A), BF16),
            jax.ShapeDtypeStruct((bsz, H_A, HEAD_DIM, HEAD_DIM), F32),
            jax.ShapeDtypeStruct((bsz, H_A, HEAD_DIM), F32),
            jax.ShapeDtypeStruct((bsz, 1, LANES), F32),
        ],
        scratch_shapes=[pltpu.VMEM((H_A, HEAD_DIM, HEAD_DIM), F32),
                        pltpu.VMEM((8, HEAD_DIM), F32), pltpu.VMEM((8, LANES), F32)],
        compiler_params=pltpu.CompilerParams(
            dimension_semantics=("parallel", "arbitrary"), vmem_limit_bytes=VMEM_LIMIT),
        name="mlstm_prompt",
    )(proj, proj, proj, proj, proj, proj, gate_bias, onw3)


def _in_place(operands, n_blocked, targets):
    specs, aliases = [], {}
    operands = list(operands)
    for out_idx, arr in targets.items():
        if arr is not None:
            aliases[len(operands)] = out_idx
            operands.append(arr)
            specs.append(pl.BlockSpec(memory_space=pl.ANY))
    n_extra = len(specs)

    def strip(kernel):
        def body(*refs):
            kernel(*refs[:n_blocked], *refs[n_blocked + n_extra:])
        return body

    return operands, specs, aliases, strip


def _mlstm_sample(proj, y_prev, c_stack, gate_bias, onw3, c_state, n_state, m_rows, layer,
                  dec_b, c, rows):
    nseq = TILE_R // c
    depth = c_state.shape[0]
    in_specs = [_col_spec(W_A, off, rows) for off in (OFF_AQ, OFF_AK, OFF_AV, OFF_AO, OFF_ZA)]
    in_specs += [_col_spec(LANES, OFF_G, rows), _param_spec(LANES, 0, layer),
                 _param_spec(W_A, 0, layer)]
    in_specs += [
        pl.BlockSpec((None, nseq, H_A, HEAD_DIM, HEAD_DIM), lambda i: (layer, i, 0, 0, 0)),
        pl.BlockSpec((None, nseq, H_A, HEAD_DIM), lambda i: (layer, i, 0, 0)),
        pl.BlockSpec((None, TILE_R, LANES), lambda i: (layer, i, 0)),
    ]
    operands = (proj, proj, proj, proj, proj, proj, gate_bias, onw3, c_state, n_state, m_rows)
    operands, extra, aliases, strip = _in_place(operands, len(in_specs), {0: y_prev, 1: c_stack})
    return pl.pallas_call(
        strip(functools.partial(_mlstm_sample_kernel, c=c)),
        grid=(dec_b // nseq,),
        in_specs=in_specs + extra,
        out_specs=[
            pl.BlockSpec((TILE_R, W_A), lambda i: (rows(i), 0)),
            pl.BlockSpec((None, nseq, H_A, HEAD_DIM, HEAD_DIM), lambda i: (layer, i, 0, 0, 0)),
            pl.BlockSpec((nseq, H_A, HEAD_DIM), lambda i: (i, 0, 0)),
            pl.BlockSpec((TILE_R, LANES), lambda i: (i, 0)),
        ],
        out_shape=[
            jax.ShapeDtypeStruct(y_prev.shape, BF16),
            jax.ShapeDtypeStruct((depth, dec_b, H_A, HEAD_DIM, HEAD_DIM), F32),
            jax.ShapeDtypeStruct((dec_b, H_A, HEAD_DIM), F32),
            jax.ShapeDtypeStruct((dec_b * c, LANES), F32),
        ],
        input_output_aliases=aliases,
        compiler_params=pltpu.CompilerParams(
            dimension_semantics=("parallel",), vmem_limit_bytes=VMEM_LIMIT),
        name="mlstm_sample",
    )(*operands)


def _l2norm(x):
    return x * lax.rsqrt(jnp.sum(x * x, axis=1, keepdims=True) + EPS)


INV_LEAF = 16


def _unit_lower_inverse(a_list, row, col, c):
    leaf = min(INV_LEAF, c)
    kl = _log2(leaf)
    leaf_mask = (row >> kl) == (col >> kl)
    eye = jnp.where(row == col, 1.0, 0.0)
    p = [jnp.where(leaf_mask, -a, 0.0) for a in a_list]
    t = [eye + n for n in p]
    for _ in range(kl - 1):
        p = [_dot(x, x) for x in p]
        t = [ti + _dot(pi, ti) for pi, ti in zip(p, t)]
    w = leaf
    while w < c:
        kw = _log2(w)
        lower_left = (((row >> (kw + 1)) == (col >> (kw + 1)))
                      & (((row >> kw) & 1) == 1) & (((col >> kw) & 1) == 0))
        left = [_dot(ti, jnp.where(lower_left, a, 0.0)) for ti, a in zip(t, a_list)]
        t = [ti - _dot(li, ti) for li, ti in zip(left, t)]
        w *= 2
    return t


def _gdn_tile(conv_win, cw_ref, z_ref, g_ref, gp_ref, onw_ref, y_ref, get_s, put_s, nseq, c):
    r = nseq * c
    row, col, same = _seq_masks(nseq, c)
    incl = same & (col <= row)
    strict = same & (col < row)
    gp = gp_ref[...]
    graw = g_ref[...]
    gdec = -jnp.exp(gp[1:2, :]) * _softplus(graw + gp[0:1, :])
    beta = _sigmoid(graw)
    gcum = _dot_exact(incl.astype(F32), gdec)
    if nseq == 1:
        glast = jnp.broadcast_to(gcum[r - 1:r, :], (r, LANES))
    else:
        glast = _dot_exact(same.astype(F32), gdec)
    gcum_t = gcum.T

    heads = range(H_B)
    hsl = [slice(h * HEAD_DIM, (h + 1) * HEAD_DIM) for h in heads]
    rowv = lax.broadcasted_iota(jnp.int32, (r, 1), 0)

    def conv(lo):
        acc = conv_win(0, slice(lo, lo + HEAD_DIM)) * cw_ref[0:1, lo:lo + HEAD_DIM]
        for jj in range(1, CONV_W):
            acc = acc + conv_win(jj, slice(lo, lo + HEAD_DIM)) * cw_ref[jj:jj + 1, lo:lo + HEAD_DIM]
        return _silu(acc)

    q = [_l2norm(conv(hs.start)) * QK_SCALE for hs in hsl]
    k = [_l2norm(conv(W_B + hs.start)) for hs in hsl]
    v = [conv(2 * W_B + hs.start) for hs in hsl]
    gc = [gcum[:, G_BA + h:G_BA + h + 1] for h in heads]
    gl = [glast[:, G_BA + h:G_BA + h + 1] for h in heads]
    bc = [beta[:, G_BB + h:G_BB + h + 1] for h in heads]
    decay = [jnp.exp(jnp.where(incl, gc[h] - gcum_t[G_BA + h:G_BA + h + 1, :], NEG_BIG))
             for h in heads]
    kk = [_dot_nt(k[h], k[h]) for h in heads]
    qk = [_dot_nt(q[h], k[h]) for h in heads]
    amat = [jnp.where(strict, bc[h] * decay[h] * kk[h], 0.0) for h in heads]
    tinv = _unit_lower_inverse(amat, row, col, c)
    sol = [_dot(tinv[h], jnp.concatenate([bc[h] * v[h], (bc[h] * jnp.exp(gc[h])) * k[h]], axis=1))
           for h in heads]
    wks = [_cat_rows([_dot(_rows(sol[h][:, HEAD_DIM:], i, c), get_s(i, h)) for i in range(nseq)])
           for h in heads]
    qs = [_cat_rows([_dot(_rows(q[h], i, c), get_s(i, h)) for i in range(nseq)]) for h in heads]
    wmat = [sol[h][:, :HEAD_DIM] - wks[h] for h in heads]
    aw = [_dot(decay[h] * qk[h], wmat[h]) for h in heads]
    for h in heads:
        hs = hsl[h]
        oh = jnp.exp(gc[h]) * qs[h] + aw[h]
        y_ref[:, hs] = _head_norm_gate(oh, onw_ref[:, hs], z_ref[:, hs]).astype(BF16)
    for h in heads:
        kd_t = (k[h] * jnp.exp(gl[h] - gc[h])).T.astype(BF16)
        e_last = jnp.exp(gl[h])
        for i in range(nseq):
            if nseq == 1:
                upd = jnp.dot(kd_t, wmat[h].astype(BF16), preferred_element_type=F32)
            else:
                seq_rows = (rowv >> _log2(c)) == i
                upd = jnp.dot(kd_t, jnp.where(seq_rows, wmat[h], 0.0).astype(BF16),
                              preferred_element_type=F32)
            put_s(i, h, e_last[i * c:i * c + 1, :] * get_s(i, h) + upd)


def _gdn_prompt_kernel(q_ref, k_ref, v_ref, z_ref, g_ref, gp_ref, cw_ref, onw_ref,
                       y_ref, s_out, conv_out, s_scr, ext_scr):
    j = pl.program_id(1)

    @pl.when(j == 0)
    def _():
        s_scr[...] = jnp.zeros_like(s_scr)
        ext_scr[0:CONV_PAD, :] = jnp.zeros((CONV_PAD, 3 * W_B), F32)

    ext_scr[CONV_PAD:, 0:W_B] = q_ref[...]
    ext_scr[CONV_PAD:, W_B:2 * W_B] = k_ref[...]
    ext_scr[CONV_PAD:, 2 * W_B:] = v_ref[...]

    def conv_win(jj, cols):
        start = CONV_PAD - (CONV_W - 1) + jj
        return ext_scr[start:start + TILE_R, cols]

    def put_s(i, h, val):
        s_scr[h] = val

    _gdn_tile(conv_win, cw_ref, z_ref, g_ref, gp_ref, onw_ref, y_ref,
              lambda i, h: s_scr[h], put_s, 1, TILE_R)

    @pl.when(j == pl.num_programs(1) - 1)
    def _():
        s_out[...] = s_scr[...]
        conv_out[...] = ext_scr[TILE_R + CONV_PAD - (CONV_W - 1):TILE_R + CONV_PAD, :]

    ext_scr[0:CONV_PAD, :] = ext_scr[TILE_R:TILE_R + CONV_PAD, :]


def _gdn_sample_kernel(q_ref, k_ref, v_ref, z_ref, g_ref, gp_ref, cw_ref, onw_ref,
                       s_in, conv_in, y_ref, s_out, conv_out, ext_scr, *, c):
    nseq = TILE_R // c
    hist = CONV_W - 1
    ext_scr[:, CONV_PAD - hist:CONV_PAD, :] = conv_in[...]
    ext_scr[:, CONV_PAD:, 0:W_B] = q_ref[...].reshape(nseq, c, W_B)
    ext_scr[:, CONV_PAD:, W_B:2 * W_B] = k_ref[...].reshape(nseq, c, W_B)
    ext_scr[:, CONV_PAD:, 2 * W_B:] = v_ref[...].reshape(nseq, c, W_B)

    def conv_win(jj, cols):
        start = CONV_PAD - hist + jj
        return ext_scr[:, start:start + c, cols].reshape(TILE_R, cols.stop - cols.start)

    def put_s(i, h, val):
        s_out[i, h] = val

    _gdn_tile(conv_win, cw_ref, z_ref, g_ref, gp_ref, onw_ref, y_ref,
              lambda i, h: s_in[i, h], put_s, nseq, c)
    conv_out[...] = ext_scr[:, CONV_PAD + c - hist:CONV_PAD + c, :]


def _gdn_prompt(proj, gdn_par, conv_w, onw3, layer, bsz, n_chunk, rows):
    t = proj.shape[0]
    in_specs = [_col_spec(W_B, off, rows) for off in (OFF_BQ, OFF_BK, OFF_BV, OFF_ZB)]
    in_specs += [
        _col_spec(LANES, OFF_G, rows),
        pl.BlockSpec((None, 8, LANES), lambda b, j: (layer, 0, 0)),
        pl.BlockSpec((None, CONV_W, 3 * W_B), lambda b, j: (layer, 0, 0)),
        _param_spec(W_B, W_A, layer),
    ]
    return pl.pallas_call(
        _gdn_prompt_kernel,
        grid=(bsz, n_chunk),
        in_specs=in_specs,
        out_specs=[
            pl.BlockSpec((TILE_R, W_B), lambda b, j: (rows(b, j), 0)),
            pl.BlockSpec((None, H_B, HEAD_DIM, HEAD_DIM), lambda b, j: (b, 0, 0, 0)),
            pl.BlockSpec((None, CONV_W - 1, 3 * W_B), lambda b, j: (b, 0, 0)),
        ],
        out_shape=[
            jax.ShapeDtypeStruct((t, W_B), BF16),
            jax.ShapeDtypeStruct((bsz, H_B, HEAD_DIM, HEAD_DIM), F32),
            jax.ShapeDtypeStruct((bsz, CONV_W - 1, 3 * W_B), F32),
        ],
        scratch_shapes=[pltpu.VMEM((H_B, HEAD_DIM, HEAD_DIM), F32),
                        pltpu.VMEM((TILE_R + CONV_PAD, 3 * W_B), F32)],
        compiler_params=pltpu.CompilerParams(
            dimension_semantics=("parallel", "arbitrary"), vmem_limit_bytes=VMEM_LIMIT),
        name="gdn_prompt",
    )(proj, proj, proj, proj, proj, gdn_par, conv_w, onw3)


def _gdn_sample(proj, y_prev, s_stack, gdn_par, conv_w, onw3, s_state, conv_state, layer,
                dec_b, c, rows):
    nseq = TILE_R // c
    depth = s_state.shape[0]
    in_specs = [_col_spec(W_B, off, rows) for off in (OFF_BQ, OFF_BK, OFF_BV, OFF_ZB)]
    in_specs += [
        _col_spec(LANES, OFF_G, rows),
        pl.BlockSpec((None, 8, LANES), lambda i: (layer, 0, 0)),
        pl.BlockSpec((None, CONV_W, 3 * W_B), lambda i: (layer, 0, 0)),
        _param_spec(W_B, W_A, layer),
        pl.BlockSpec((None, nseq, H_B, HEAD_DIM, HEAD_DIM), lambda i: (layer, i, 0, 0, 0)),
        pl.BlockSpec((None, nseq, CONV_W - 1, 3 * W_B), lambda i: (layer, i, 0, 0)),
    ]
    operands = (proj, proj, proj, proj, proj, gdn_par, conv_w, onw3, s_state, conv_state)
    operands, extra, aliases, strip = _in_place(operands, len(in_specs), {0: y_prev, 1: s_stack})
    return pl.pallas_call(
        strip(functools.partial(_gdn_sample_kernel, c=c)),
        grid=(dec_b // nseq,),
        in_specs=in_specs + extra,
        out_specs=[
            pl.BlockSpec((TILE_R, W_B), lambda i: (rows(i), 0)),
            pl.BlockSpec((None, nseq, H_B, HEAD_DIM, HEAD_DIM), lambda i: (layer, i, 0, 0, 0)),
            pl.BlockSpec((nseq, CONV_W - 1, 3 * W_B), lambda i: (i, 0, 0)),
        ],
        out_shape=[
            jax.ShapeDtypeStruct(y_prev.shape, BF16),
            jax.ShapeDtypeStruct((depth, dec_b, H_B, HEAD_DIM, HEAD_DIM), F32),
            jax.ShapeDtypeStruct((dec_b, CONV_W - 1, 3 * W_B), F32),
        ],
        scratch_shapes=[pltpu.VMEM((nseq, CONV_PAD + c, 3 * W_B), F32)],
        input_output_aliases=aliases,
        compiler_params=pltpu.CompilerParams(
            dimension_semantics=("parallel",), vmem_limit_bytes=VMEM_LIMIT),
        name="gdn_sample",
    )(*operands)


def _hgrn_tile(q_ref, f_ref, i_ref, z_ref, lbp_ref, onw_ref, y_ref, get_s, put_s,
               nseq, c, layer):
    r = nseq * c
    row, col, same = _seq_masks(nseq, c)
    incl = same & (col <= row)
    rowv = lax.broadcasted_iota(jnp.int32, (r, 1), 0)
    lbp = lbp_ref[...]
    e = jnp.exp(lbp - jnp.max(lbp, axis=0, keepdims=True))
    sm = e / jnp.sum(e, axis=0, keepdims=True)
    lb = jnp.zeros((1, W_C), F32)
    for l in range(1, layer + 1):
        lb = lb + sm[l:l + 1, :]

    cf = f_ref[...]
    logf = _log_sigmoid(cf) + jnp.log1p(lb * jnp.exp(jnp.minimum(-cf, EXP_CLIP)))
    kall = (1.0 - lb) * _sigmoid(-cf)
    bcum = _dot_exact(incl.astype(F32), logf)
    if nseq == 1:
        blast = jnp.broadcast_to(bcum[r - 1:r, :], (r, W_C))
    else:
        blast = _dot_exact(same.astype(F32), logf)

    levels = []
    w = c // 2
    while w >= 1:
        levels.append(w)
        w //= 2
    rowf = lax.broadcasted_iota(jnp.int32, (r, W_C), 0)
    refs = []
    for w in levels:
        if 2 * w >= 8:
            refs.append(_cat_rows([
                jnp.broadcast_to(bcum[g * 2 * w + w - 1:g * 2 * w + w, :], (2 * w, W_C))
                for g in range(r // (2 * w))]))
        else:
            pos = rowf & (2 * w - 1)
            ref = bcum
            for off in range(2 * w):
                if off != w - 1:
                    ref = jnp.where(pos == off, pltpu.roll(bcum, (off - (w - 1)) % r, axis=0), ref)
            refs.append(ref)

    heads = range(H_C)
    hsl = [slice(h * HEAD_DIM, (h + 1) * HEAD_DIM) for h in heads]
    q = [_silu(q_ref[:, hs]) for hs in hsl]
    k = [kall[:, hs] for hs in hsl]
    b = [bcum[:, hs] for hs in hsl]
    diag = row == col
    amat = [jnp.where(diag, _dot_nt(q[h], k[h]), 0.0) for h in heads]
    for w, ref_all in zip(levels, refs):
        lw = _log2(w)
        tgt = ((rowv >> lw) & 1) == 1
        pair = (row >> (lw + 1)) == (col >> (lw + 1))
        qp = [q[h] * jnp.exp(jnp.where(tgt, b[h] - ref_all[:, hsl[h]], NEG_BIG)) for h in heads]
        kp = [k[h] * jnp.exp(jnp.where(tgt, NEG_BIG, ref_all[:, hsl[h]] - b[h])) for h in heads]
        amat = [amat[h] + jnp.where(pair, _dot_nt(qp[h], kp[h]), 0.0) for h in heads]
    qs = [_cat_rows([_dot(_rows(q[h] * jnp.exp(b[h]), i, c), get_s(i, h)) for i in range(nseq)])
          for h in heads]
    av = [_dot(amat[h], i_ref[:, hsl[h]]) for h in heads]
    for h in heads:
        hs = hsl[h]
        y_ref[:, hs] = _head_norm_gate(qs[h] + av[h], onw_ref[:, hs], z_ref[:, hs]).astype(BF16)
    for h in heads:
        hs = hsl[h]
        vh = i_ref[:, hs]
        bl = blast[:, hs]
        kd_t = (k[h] * jnp.exp(bl - b[h])).T.astype(BF16)
        eb_t = jnp.exp(bl).T
        for i in range(nseq):
            if nseq == 1:
                upd = jnp.dot(kd_t, vh.astype(BF16), preferred_element_type=F32)
            else:
                seq_rows = (rowv >> _log2(c)) == i
                upd = jnp.dot(kd_t, jnp.where(seq_rows, vh, 0.0).astype(BF16),
                              preferred_element_type=F32)
            put_s(i, h, eb_t[:, i * c:i * c + 1] * get_s(i, h) + upd)


class _ColView:
    def __init__(self, parts):
        self.parts = []
        pos = 0
        for ref, lo, hi in parts:
            self.parts.append((pos, ref, lo, hi - lo))
            pos += hi - lo
        self.width = pos

    def __getitem__(self, idx):
        if idx is Ellipsis:
            rows, cols = slice(None), slice(0, self.width)
        else:
            rows, cols = idx
            cols = slice(cols.start or 0, self.width if cols.stop is None else cols.stop)
        out = []
        for pos, ref, lo, n in self.parts:
            a, b = max(cols.start, pos), min(cols.stop, pos + n)
            if a < b:
                out.append(ref[rows, lo + a - pos:lo + b - pos])
        return out[0] if len(out) == 1 else jnp.concatenate(out, axis=1)


def _hgrn_views(ca_ref, cb_ref, zc_ref):
    two = 2 * W_C - ca_ref.shape[1]
    q = _ColView([(ca_ref, 0, W_C)])
    f = _ColView([(ca_ref, W_C, ca_ref.shape[1]), (cb_ref, 0, two)])
    i = _ColView([(cb_ref, two, two + W_C)])
    z = _ColView([(zc_ref, 0, W_C)])
    return q, f, i, z


def _hgrn_prompt_kernel(ca_ref, cb_ref, zc_ref, lbp_ref, onw_ref, y_ref, s_out, s_scr,
                        *, layer):
    j = pl.program_id(1)

    @pl.when(j == 0)
    def _():
        s_scr[...] = jnp.zeros_like(s_scr)

    def put_s(i, h, val):
        s_scr[h] = val

    _hgrn_tile(*_hgrn_views(ca_ref, cb_ref, zc_ref), lbp_ref, onw_ref, y_ref,
               lambda i, h: s_scr[h], put_s, 1, TILE_R, layer)

    @pl.when(j == pl.num_programs(1) - 1)
    def _():
        s_out[...] = s_scr[...]


def _hgrn_sample_kernel(ca_ref, cb_ref, zc_ref, lbp_ref, onw_ref, s_in, y_ref, s_out,
                        *, c, layer):
    def put_s(i, h, val):
        s_out[i, h] = val

    _hgrn_tile(*_hgrn_views(ca_ref, cb_ref, zc_ref), lbp_ref, onw_ref, y_ref,
               lambda i, h: s_in[i, h], put_s, TILE_R // c, c, layer)


HGRN_BLK = W_A
assert OFF_C % HGRN_BLK == 0 and 3 * W_C == 2 * HGRN_BLK and OFF_ZC % HGRN_BLK == 0


def _hgrn_prompt(proj, lb_par, onw3, layer, bsz, n_chunk, rows):
    t = proj.shape[0]
    depth = lb_par.shape[0]
    in_specs = [_col_spec(HGRN_BLK, off, rows) for off in (OFF_C, OFF_C + HGRN_BLK, OFF_ZC)]
    in_specs += [pl.BlockSpec((depth, W_C), lambda b, j: (0, 0)),
                 _param_spec(W_C, W_A + W_B, layer)]
    return pl.pallas_call(
        functools.partial(_hgrn_prompt_kernel, layer=layer),
        grid=(bsz, n_chunk),
        in_specs=in_specs,
        out_specs=[
            pl.BlockSpec((TILE_R, W_C), lambda b, j: (rows(b, j), 0)),
            pl.BlockSpec((None, H_C, HEAD_DIM, HEAD_DIM), lambda b, j: (b, 0, 0, 0)),
        ],
        out_shape=[
            jax.ShapeDtypeStruct((t, W_C), BF16),
            jax.ShapeDtypeStruct((bsz, H_C, HEAD_DIM, HEAD_DIM), F32),
        ],
        scratch_shapes=[pltpu.VMEM((H_C, HEAD_DIM, HEAD_DIM), F32)],
        compiler_params=pltpu.CompilerParams(
            dimension_semantics=("parallel", "arbitrary"), vmem_limit_bytes=VMEM_LIMIT),
        name="hgrn_prompt",
    )(proj, proj, proj, lb_par, onw3)


def _hgrn_sample(proj, y_prev, s_stack, lb_par, onw3, s_state, layer, dec_b, c, rows):
    nseq = TILE_R // c
    depth = lb_par.shape[0]
    in_specs = [_col_spec(HGRN_BLK, off, rows) for off in (OFF_C, OFF_C + HGRN_BLK, OFF_ZC)]
    in_specs += [
        pl.BlockSpec((depth, W_C), lambda i: (0, 0)),
        _param_spec(W_C, W_A + W_B, layer),
        pl.BlockSpec((None, nseq, H_C, HEAD_DIM, HEAD_DIM), lambda i: (layer, i, 0, 0, 0)),
    ]
    operands = (proj, proj, proj, lb_par, onw3, s_state)
    operands, extra, aliases, strip = _in_place(operands, len(in_specs), {0: y_prev, 1: s_stack})
    return pl.pallas_call(
        strip(functools.partial(_hgrn_sample_kernel, c=c, layer=layer)),
        grid=(dec_b // nseq,),
        in_specs=in_specs + extra,
        out_specs=[
            pl.BlockSpec((TILE_R, W_C), lambda i: (rows(i), 0)),
            pl.BlockSpec((None, nseq, H_C, HEAD_DIM, HEAD_DIM), lambda i: (layer, i, 0, 0, 0)),
        ],
        out_shape=[
            jax.ShapeDtypeStruct(y_prev.shape, BF16),
            jax.ShapeDtypeStruct((depth, dec_b, H_C, HEAD_DIM, HEAD_DIM), F32),
        ],
        input_output_aliases=aliases,
        compiler_params=pltpu.CompilerParams(
            dimension_semantics=("parallel",), vmem_limit_bytes=VMEM_LIMIT),
        name="hgrn_sample",
    )(*operands)


def _gate_weight(w_in):
    ai0 = 4 * W_A
    ba0 = ai0 + 2 * H_A + 3 * W_B
    n_gate = 2 * H_A + 2 * H_B
    pad = jnp.zeros(w_in.shape[:-1] + (LANES - n_gate,), w_in.dtype)
    g = jnp.concatenate([w_in[..., ai0:ai0 + 2 * H_A], w_in[..., ba0:ba0 + 2 * H_B], pad], axis=-1)
    return g.astype(BF16)


def _lane_row(depth, pieces):
    row = jnp.zeros((depth, LANES), F32)
    for off, val in pieces:
        row = lax.dynamic_update_slice(row, val.astype(F32), (0, off))
    return row[:, None, :]


def kernel(x_prompt, x_sample, state_mlstm_C, state_mlstm_n, state_mlstm_m, state_gdn_S,
           state_gdn_conv, state_hgrn_S, meta_tokens, norm_w, w_in, mlstm_gate_b, gdn_A_log,
           gdn_dt_bias, gdn_conv_w, hgrn_lower_bounds, out_norm_w, w_out, final_norm_w):
    bsz, seq_len, d = x_prompt.shape
    dec_b, dec_seq, _ = x_sample.shape
    depth = w_in.shape[0]
    assert d == D_MODEL and seq_len % TILE_R == 0 and TILE_R % dec_seq == 0
    assert w_in.shape[2] == W_SEGMENTS[-1][1] + W_SEGMENTS[-1][2]
    nseq = TILE_R // dec_seq
    assert dec_b % nseq == 0

    chunks = seq_len // TILE_R
    n_chunk = chunks + 1
    tm = bsz * TILE_R
    n_dec = dec_b * dec_seq
    dec_blk0 = n_chunk * bsz

    def prompt_rows(b, j):
        return j * bsz + b

    def dec_rows(i):
        return dec_blk0 + i

    w_in = w_in.astype(F32)
    w_gate = _gate_weight(w_in)
    w_out_bf = w_out.astype(BF16)
    norm_w3 = norm_w.astype(F32)[:, None, :]
    onw3 = out_norm_w.astype(F32)[:, None, :]
    final_nw = final_norm_w.astype(F32)[None, :]
    gate_bias = _lane_row(depth, [(G_AI, mlstm_gate_b[:, 0]), (G_AF, mlstm_gate_b[:, 1])])
    gdn_par = jnp.concatenate([
        _lane_row(depth, [(G_BA, gdn_dt_bias)]), _lane_row(depth, [(G_BA, gdn_A_log)]),
        jnp.zeros((depth, 6, LANES), F32)], axis=1)
    conv_w = gdn_conv_w.astype(F32)
    lb_par = hgrn_lower_bounds.astype(F32)
    m_rows = jnp.repeat(
        jnp.pad(state_mlstm_m.astype(F32), ((0, 0), (0, 0), (0, LANES - H_A))), dec_seq, axis=1)

    x, h = _prep(x_prompt.astype(F32), x_sample.astype(F32), meta_tokens.astype(F32), norm_w3,
                 bsz, n_chunk)

    new_p = [[] for _ in range(6)]
    new_s = [[] for _ in range(3)]
    c_s = s_s = h_s = None
    y_prompt = y_sample = None
    for l in range(depth):
        proj = _inproj(h, w_in, w_gate, l, tm)

        ya, c_p, n_p, m_p = _mlstm_prompt(proj, gate_bias, onw3, l, bsz, n_chunk, prompt_rows)
        ya, c_s, n_s, m_s = _mlstm_sample(proj, ya, c_s, gate_bias, onw3, state_mlstm_C,
                                          state_mlstm_n, m_rows, l, dec_b, dec_seq, dec_rows)
        yb, s_p, cv_p = _gdn_prompt(proj, gdn_par, conv_w, onw3, l, bsz, n_chunk, prompt_rows)
        yb, s_s, cv_s = _gdn_sample(proj, yb, s_s, gdn_par, conv_w, onw3, state_gdn_S,
                                    state_gdn_conv, l, dec_b, dec_seq, dec_rows)
        yc, h_p = _hgrn_prompt(proj, lb_par, onw3, l, bsz, n_chunk, prompt_rows)
        yc, h_s = _hgrn_sample(proj, yc, h_s, lb_par, onw3, state_hgrn_S, l, dec_b, dec_seq,
                               dec_rows)

        for lst, val in zip(new_p, (c_p, n_p, m_p[:, 0, :H_A], s_p, cv_p, h_p)):
            lst.append(val)
        for lst, val in zip(new_s, (n_s, m_s[::dec_seq, :H_A], cv_s)):
            lst.append(val)

        if l + 1 < depth:
            x, h = _outproj_mid(ya, yb, yc, x, w_out_bf, l, norm_w3, tm)
        else:
            y_prompt = _outproj_final(
                ya, yb, yc, x, w_out_bf, l, final_nw, tm, 1, chunks, (bsz, chunks, TILE_R, d),
                pl.BlockSpec((bsz, None, TILE_R, d), lambda i: (0, i, 0, 0)))
            y_sample = _outproj_final(
                ya, yb, yc, x, w_out_bf, l, final_nw, tm, n_chunk, n_dec // tm, (n_dec, d),
                pl.BlockSpec((tm, d), lambda i: (i, 0)))

    outs_p = [jnp.stack(a, axis=0) for a in new_p]
    n_all, m_all, cv_all = [jnp.stack(a, axis=0) for a in new_s]
    return (y_prompt.reshape(bsz, seq_len, d), y_sample.reshape(dec_b, dec_seq, d),
            *outs_p, c_s, n_all, m_all, s_s, cv_all, h_s)
```

```python
import functools
import math

import jax
import jax.numpy as jnp
from jax import lax
from jax.experimental import pallas as pl
from jax.experimental.pallas import tpu as pltpu

F32 = jnp.float32
BF16 = jnp.bfloat16

D_MODEL = 2048
HEAD_DIM = 128
H_A, H_B, H_C = 6, 6, 4
W_A, W_B, W_C = H_A * HEAD_DIM, H_B * HEAD_DIM, H_C * HEAD_DIM
MIX = W_A + W_B + W_C
N_META = 16
CONV_W = 4
EPS = 1e-6
NEG_BIG = -1e30
EXP_CLIP = 60.0
QK_SCALE = HEAD_DIM ** -0.5

TILE_R = 128
LANES = 128
CONV_PAD = 8

OFF_AQ = 0
OFF_AK, OFF_AV, OFF_AO = W_A, 2 * W_A, 3 * W_A
OFF_BQ = 4 * W_A
OFF_BK, OFF_BV = OFF_BQ + W_B, OFF_BQ + 2 * W_B
OFF_C = OFF_BQ + 3 * W_B
OFF_ZA = OFF_C + 3 * W_C
OFF_ZB = OFF_ZA + W_A
OFF_ZC = OFF_ZB + W_B
OFF_G = OFF_ZC + W_C
N_PACK = OFF_G + 2 * LANES
G_AI, G_AF, G_BA, G_BB = 0, H_A, 2 * H_A, 2 * H_A + H_B
TN = 1024
W_SEGMENTS = ((OFF_AQ, OFF_BQ, 0), (OFF_BQ, OFF_C, 2 * H_A), (OFF_C, OFF_G, 2 * H_A + 2 * H_B))
W_ROW_CHUNK = 256

VMEM_LIMIT = 56 * 1024 * 1024


def _sigmoid(x):
    return 1.0 / (1.0 + jnp.exp(-x))


def _silu(x):
    return x * _sigmoid(x)


def _softplus(x):
    return jnp.maximum(x, 0.0) + jnp.log1p(jnp.exp(-jnp.abs(x)))


def _log_sigmoid(x):
    return -_softplus(-x)


def _dot(a, b):
    return jnp.dot(a.astype(BF16), b.astype(BF16), preferred_element_type=F32)


def _dot_nt(a, b):
    return lax.dot_general(a.astype(BF16), b.astype(BF16), (((1,), (1,)), ((), ())),
                           preferred_element_type=F32)


def _dot_exact(sel, x):
    hi = x.astype(BF16)
    r1 = x - hi.astype(F32)
    mid = r1.astype(BF16)
    lo = (r1 - mid.astype(F32)).astype(BF16)
    s = sel.astype(BF16)
    return (jnp.dot(s, hi, preferred_element_type=F32)
            + jnp.dot(s, mid, preferred_element_type=F32)
            + jnp.dot(s, lo, preferred_element_type=F32))


def _log2(n):
    k = int(math.log2(n))
    assert (1 << k) == n
    return k


def _seq_masks(nseq, c):
    r = nseq * c
    row = lax.broadcasted_iota(jnp.int32, (r, r), 0)
    col = lax.broadcasted_iota(jnp.int32, (r, r), 1)
    if nseq == 1:
        same = row >= 0
    else:
        k = _log2(c)
        same = (row >> k) == (col >> k)
    return row, col, same


def _head_norm_gate(h, onw, z):
    hn = h * lax.rsqrt(jnp.mean(h * h, axis=1, keepdims=True) + EPS)
    return hn * onw * _silu(z)


def _rows(x, i, c):
    return x[i * c:(i + 1) * c]


def _cat_rows(parts):
    return parts[0] if len(parts) == 1 else jnp.concatenate(parts, axis=0)


def _chunk_shift(jj, r0):
    p = jj * TN + r0
    for p0, p1, shift in W_SEGMENTS:
        if p0 <= p and p + W_ROW_CHUNK <= p1:
            return shift
    return None


def _fill_weight_tile(w0_ref, w1_ref, wg_ref, wb_scr, tail_scr, jj):
    for r0 in range(0, TN, W_ROW_CHUNK):
        shift = _chunk_shift(jj, r0)
        dst = slice(r0, r0 + W_ROW_CHUNK)
        if shift is None:
            g0 = OFF_G - jj * TN
            assert r0 <= g0 and g0 + LANES <= r0 + W_ROW_CHUNK
            wb_scr[dst, :] = jnp.zeros((W_ROW_CHUNK, D_MODEL), BF16)
            wb_scr[g0:g0 + LANES, :] = wg_ref[...].astype(BF16)
        elif r0 + shift + W_ROW_CHUNK <= TN:
            wb_scr[dst, :] = w0_ref[r0 + shift:r0 + shift + W_ROW_CHUNK, :].astype(BF16)
        else:
            tail_scr[0:W_ROW_CHUNK, :] = w0_ref[r0:TN, :]
            tail_scr[W_ROW_CHUNK:, :] = w1_ref[0:tail_scr.shape[0] - W_ROW_CHUNK, :]
            wb_scr[dst, :] = tail_scr[shift:shift + W_ROW_CHUNK, :].astype(BF16)


def _inproj_kernel(h_ref, w0_ref, w1_ref, wg_ref, o_ref, wb_scr, tail_scr):
    j = pl.program_id(0)

    @pl.when(pl.program_id(1) == 0)
    def _():
        for jj in range(N_PACK // TN):
            @pl.when(j == jj)
            def _(jj=jj):
                _fill_weight_tile(w0_ref, w1_ref, wg_ref, wb_scr, tail_scr, jj)

    o_ref[...] = lax.dot_general(h_ref[...], wb_scr[...], (((1,), (1,)), ((), ())),
                                 preferred_element_type=F32)


W_TAIL_ROWS = 32


def _inproj(h, w_t, w_gate_t, layer, tm):
    t = h.shape[0]
    n_tiles = N_PACK // TN
    last_blk = (w_t.shape[1] - 1) // TN
    assert n_tiles - 1 <= last_blk and W_TAIL_ROWS >= W_SEGMENTS[-1][2]
    return pl.pallas_call(
        _inproj_kernel,
        grid=(n_tiles, t // tm),
        in_specs=[
            pl.BlockSpec((tm, D_MODEL), lambda j, i: (i, 0)),
            pl.BlockSpec((None, TN, D_MODEL), lambda j, i: (layer, j, 0)),
            pl.BlockSpec((None, TN, D_MODEL), lambda j, i: (layer, jnp.minimum(j + 1, last_blk), 0)),
            pl.BlockSpec((None, LANES, D_MODEL), lambda j, i: (layer, 0, 0)),
        ],
        out_specs=pl.BlockSpec((tm, TN), lambda j, i: (i, j)),
        out_shape=jax.ShapeDtypeStruct((t, N_PACK), F32),
        scratch_shapes=[pltpu.VMEM((TN, D_MODEL), BF16),
                        pltpu.VMEM((W_ROW_CHUNK + W_TAIL_ROWS, D_MODEL), F32)],
        compiler_params=pltpu.CompilerParams(
            dimension_semantics=("arbitrary", "arbitrary"), vmem_limit_bytes=VMEM_LIMIT),
        name="inproj",
    )(h, w_t, w_t, w_gate_t)


def _rmsnorm(x, w):
    ms = jnp.mean(x * x, axis=1, keepdims=True)
    return x * lax.rsqrt(ms + EPS) * w


def _prep_kernel(xp_ref, xs_ref, meta_ref, nw_ref, x_ref, h_ref, *, n_chunk, bsz):
    i = pl.program_id(0)
    d = x_ref.shape[1]

    def emit(x):
        x_ref[...] = x
        h_ref[...] = _rmsnorm(x, nw_ref[...]).astype(BF16)

    @pl.when(i == 0)
    def _():
        slot = jnp.concatenate([jnp.zeros((TILE_R - N_META, d), F32), meta_ref[...]], axis=0)
        emit(jnp.concatenate([slot] * bsz, axis=0))

    @pl.when((i > 0) & (i < n_chunk))
    def _():
        emit(xp_ref[...].reshape(bsz * TILE_R, d))

    @pl.when(i >= n_chunk)
    def _():
        emit(xs_ref[...])


def _prep(x_prompt, x_sample, meta, norm_w3, bsz, n_chunk):
    d = x_prompt.shape[-1]
    tm = bsz * TILE_R
    n_dec = x_sample.shape[0] * x_sample.shape[1]
    assert n_dec % tm == 0
    xp = x_prompt.reshape(bsz, n_chunk - 1, TILE_R, d)
    xs = x_sample.reshape(n_dec, d)
    n_blk = n_chunk + n_dec // tm
    t = n_blk * tm
    return pl.pallas_call(
        functools.partial(_prep_kernel, n_chunk=n_chunk, bsz=bsz),
        grid=(n_blk,),
        in_specs=[
            pl.BlockSpec((bsz, None, TILE_R, d), lambda i: (0, jnp.clip(i - 1, 0, n_chunk - 2), 0, 0)),
            pl.BlockSpec((tm, d), lambda i: (jnp.clip(i - n_chunk, 0, n_dec // tm - 1), 0)),
            pl.BlockSpec((N_META, d), lambda i: (0, 0)),
            pl.BlockSpec((None, 1, d), lambda i: (0, 0, 0)),
        ],
        out_specs=[pl.BlockSpec((tm, d), lambda i: (i, 0)),
                   pl.BlockSpec((tm, d), lambda i: (i, 0))],
        out_shape=[jax.ShapeDtypeStruct((t, d), F32), jax.ShapeDtypeStruct((t, d), BF16)],
        compiler_params=pltpu.CompilerParams(
            dimension_semantics=("arbitrary",), vmem_limit_bytes=VMEM_LIMIT),
        name="prep",
    )(xp, xs, meta, norm_w3)


def _outproj_residual(ya_ref, yb_ref, yc_ref, x_ref, w_ref):
    acc = jnp.dot(ya_ref[...], w_ref[0:W_A, :], preferred_element_type=F32)
    acc = acc + jnp.dot(yb_ref[...], w_ref[W_A:W_A + W_B, :], preferred_element_type=F32)
    acc = acc + jnp.dot(yc_ref[...], w_ref[W_A + W_B:MIX, :], preferred_element_type=F32)
    return x_ref[...] + acc


def _outproj_mid_kernel(ya_ref, yb_ref, yc_ref, x_ref, w_ref, nw_ref, x_out, h_out):
    xn = _outproj_residual(ya_ref, yb_ref, yc_ref, x_ref, w_ref)
    x_out[...] = xn
    h_out[...] = _rmsnorm(xn, nw_ref[...]).astype(BF16)


def _outproj_final_kernel(ya_ref, yb_ref, yc_ref, x_ref, w_ref, nw_ref, y_out):
    xn = _outproj_residual(ya_ref, yb_ref, yc_ref, x_ref, w_ref)
    y_out[...] = _rmsnorm(xn, nw_ref[...]).reshape(y_out.shape)


def _outproj_specs(tm, row_blk0, layer, nw_spec):
    rows = lambda i: (row_blk0 + i, 0)
    return [
        pl.BlockSpec((tm, W_A), rows),
        pl.BlockSpec((tm, W_B), rows),
        pl.BlockSpec((tm, W_C), rows),
        pl.BlockSpec((tm, D_MODEL), rows),
        pl.BlockSpec((None, MIX, D_MODEL), lambda i: (layer, 0, 0)),
        nw_spec,
    ]


def _outproj_mid(ya, yb, yc, x, w_out, layer, norm_w3, tm):
    t = x.shape[0]
    blk = pl.BlockSpec((tm, D_MODEL), lambda i: (i, 0))
    nw_spec = pl.BlockSpec((None, 1, D_MODEL), lambda i: (layer + 1, 0, 0))
    return pl.pallas_call(
        _outproj_mid_kernel,
        grid=(t // tm,),
        in_specs=_outproj_specs(tm, 0, layer, nw_spec),
        out_specs=[blk, blk],
        out_shape=[jax.ShapeDtypeStruct((t, D_MODEL), F32), jax.ShapeDtypeStruct((t, D_MODEL), BF16)],
        compiler_params=pltpu.CompilerParams(
            dimension_semantics=("parallel",), vmem_limit_bytes=VMEM_LIMIT),
        name="outproj",
    )(ya, yb, yc, x, w_out, norm_w3)


def _outproj_final(ya, yb, yc, x, w_out, layer, final_nw, tm, row_blk0, n_blk, out_shape, out_spec):
    nw_spec = pl.BlockSpec((1, D_MODEL), lambda i: (0, 0))
    return pl.pallas_call(
        _outproj_final_kernel,
        grid=(n_blk,),
        in_specs=_outproj_specs(tm, row_blk0, layer, nw_spec),
        out_specs=out_spec,
        out_shape=jax.ShapeDtypeStruct(out_shape, F32),
        compiler_params=pltpu.CompilerParams(
            dimension_semantics=("parallel",), vmem_limit_bytes=VMEM_LIMIT),
        name="outproj_final",
    )(ya, yb, yc, x, w_out, final_nw)


def _mlstm_tile(q_ref, k_ref, v_ref, o_ref, z_ref, g_ref, gb_ref, onw_ref, y_ref,
                m_exp, valid, get_c, get_n, put_c, put_n, put_m, nseq, c):
    r = nseq * c
    row, col, same = _seq_masks(nseq, c)
    incl = same & (col <= row)
    g = g_ref[...] + gb_ref[...]
    li = g
    lf = _log_sigmoid(g)
    if valid is not None:
        li = jnp.where(valid, li, NEG_BIG)
        lf = jnp.where(valid, lf, 0.0)
    fcum = _dot_exact(incl.astype(F32), lf)
    if nseq == 1:
        flast = jnp.broadcast_to(fcum[r - 1:r, :], (r, LANES))
    else:
        flast = _dot_exact(same.astype(F32), lf)
    fcum_t = fcum.T
    li_t = li.T
    lane = lax.broadcasted_iota(jnp.int32, (r, LANES), 1)
    heads = range(H_A)
    hsl = [slice(h * HEAD_DIM, (h + 1) * HEAD_DIM) for h in heads]

    dm, mt, a, w, a_s = [], [], [], [], []
    m_new_all = jnp.zeros((r, LANES), F32)
    for h in heads:
        fc = fcum[:, G_AF + h:G_AF + h + 1]
        fr = fcum_t[G_AF + h:G_AF + h + 1, :]
        lic = li[:, G_AI + h:G_AI + h + 1]
        lir = li_t[G_AI + h:G_AI + h + 1, :]
        fl = flast[:, G_AF + h:G_AF + h + 1]
        mcol = m_exp[:, h:h + 1]
        dm_h = jnp.where(incl, fc - fr + lir, NEG_BIG)
        inter = fc + mcol
        mt_h = jnp.maximum(inter, jnp.max(dm_h, axis=1, keepdims=True))
        if nseq == 1:
            m_new = jnp.broadcast_to(mt_h[r - 1:r, :], (r, 1))
        else:
            dl = jnp.where(same, fl - fr + lir, NEG_BIG)
            m_new = jnp.maximum(fl + mcol, jnp.max(dl, axis=1, keepdims=True))
        dm.append(dm_h)
        mt.append(mt_h)
        a.append(jnp.exp(inter - mt_h))
        w.append(jnp.exp(fl - fc + lic - m_new))
        a_s.append(jnp.exp(fl + mcol - m_new))
        m_new_all = jnp.where(lane == h, m_new, m_new_all)
    put_m(m_new_all)

    s = [_dot_nt(q_ref[:, hs], k_ref[:, hs] * QK_SCALE) for hs in hsl]
    p = [jnp.exp(dm[h] - mt[h]) * s[h] for h in heads]
    pv = [_dot(p[h], v_ref[:, hsl[h]]) for h in heads]
    qc = [_cat_rows([_dot(_rows(q_ref[:, hsl[h]], i, c), get_c(i, h)) for i in range(nseq)])
          for h in heads]
    for h in heads:
        hs = hsl[h]
        qh = q_ref[:, hs]
        psum = jnp.sum(p[h], axis=1, keepdims=True)
        qn = _cat_rows([jnp.sum(_rows(qh, i, c) * get_n(i, h), axis=1, keepdims=True)
                        for i in range(nseq)])
        num = a[h] * qc[h] + pv[h]
        den = a[h] * qn + psum
        hh = num / jnp.maximum(jnp.abs(den), jnp.exp(jnp.minimum(-mt[h], EXP_CLIP)))
        hh = hh * _sigmoid(o_ref[:, hs])
        y_ref[:, hs] = _head_norm_gate(hh, onw_ref[:, hs], z_ref[:, hs]).astype(BF16)

    rowv = lax.broadcasted_iota(jnp.int32, (r, 1), 0)
    for h in heads:
        hs = hsl[h]
        kh = k_ref[:, hs] * QK_SCALE
        wv = w[h] * v_ref[:, hs]
        wk = w[h] * kh
        kh_t = kh.T.astype(BF16)
        for i in range(nseq):
            a_i = a_s[h][i * c:i * c + 1, :]
            if nseq == 1:
                upd = jnp.dot(kh_t, wv.astype(BF16), preferred_element_type=F32)
            else:
                seq_rows = (rowv >> _log2(c)) == i
                upd = jnp.dot(kh_t, jnp.where(seq_rows, wv, 0.0).astype(BF16),
                              preferred_element_type=F32)
            put_c(i, h, a_i * get_c(i, h) + upd)
            put_n(i, h, a_i * get_n(i, h) + jnp.sum(_rows(wk, i, c), axis=0, keepdims=True))


def _mlstm_prompt_kernel(q_ref, k_ref, v_ref, o_ref, z_ref, g_ref, gb_ref, onw_ref,
                         y_ref, c_out, n_out, m_out, c_scr, n_scr, m_scr):
    j = pl.program_id(1)

    @pl.when(j == 0)
    def _():
        c_scr[...] = jnp.zeros_like(c_scr)
        n_scr[...] = jnp.zeros_like(n_scr)
        m_scr[...] = jnp.zeros_like(m_scr)

    tok = lax.broadcasted_iota(jnp.int32, (TILE_R, LANES), 0) + j * TILE_R
    valid = tok >= TILE_R - N_META
    m_exp = jnp.broadcast_to(m_scr[0:1, :], (TILE_R, LANES))

    def put_c(i, h, val):
        c_scr[h] = val

    def put_n(i, h, val):
        n_scr[h:h + 1, :] = val

    def put_m(val):
        m_scr[...] = val[0:8, :]

    _mlstm_tile(q_ref, k_ref, v_ref, o_ref, z_ref, g_ref, gb_ref, onw_ref, y_ref,
                m_exp, valid, lambda i, h: c_scr[h], lambda i, h: n_scr[h:h + 1, :],
                put_c, put_n, put_m, 1, TILE_R)

    @pl.when(j == pl.num_programs(1) - 1)
    def _():
        c_out[...] = c_scr[...]
        n_out[...] = n_scr[0:H_A, :]
        m_out[...] = m_scr[0:1, :]


def _mlstm_sample_kernel(q_ref, k_ref, v_ref, o_ref, z_ref, g_ref, gb_ref, onw_ref,
                         c_in, n_in, m_in, y_ref, c_out, n_out, m_out, *, c):
    nseq = TILE_R // c

    def put_c(i, h, val):
        c_out[i, h] = val

    def put_n(i, h, val):
        n_out[i, h:h + 1, :] = val

    def put_m(val):
        m_out[...] = val

    _mlstm_tile(q_ref, k_ref, v_ref, o_ref, z_ref, g_ref, gb_ref, onw_ref, y_ref,
                m_in[...], None, lambda i, h: c_in[i, h], lambda i, h: n_in[i, h:h + 1, :],
                put_c, put_n, put_m, nseq, c)


def _col_spec(width, off, rows):
    assert off % width == 0
    blk = off // width
    return pl.BlockSpec((TILE_R, width), lambda *ids: (rows(*ids), blk))


def _param_spec(width, off, layer):
    assert off % width == 0
    blk = off // width
    return pl.BlockSpec((None, 1, width), lambda *ids: (layer, 0, blk))


def _mlstm_prompt(proj, gate_bias, onw3, layer, bsz, n_chunk, rows):
    t = proj.shape[0]
    in_specs = [_col_spec(W_A, off, rows) for off in (OFF_AQ, OFF_AK, OFF_AV, OFF_AO, OFF_ZA)]
    in_specs += [_col_spec(LANES, OFF_G, rows), _param_spec(LANES, 0, layer),
                 _param_spec(W_A, 0, layer)]
    return pl.pallas_call(
        _mlstm_prompt_kernel,
        grid=(bsz, n_chunk),
        in_specs=in_specs,
        out_specs=[
            pl.BlockSpec((TILE_R, W_A), lambda b, j: (rows(b, j), 0)),
            pl.BlockSpec((None, H_A, HEAD_DIM, HEAD_DIM), lambda b, j: (b, 0, 0, 0)),
            pl.BlockSpec((None, H_A, HEAD_DIM), lambda b, j: (b, 0, 0)),
            pl.BlockSpec((None, 1, LANES), lambda b, j: (b, 0, 0)),
        ],
        out_shape=[
            jax.ShapeDtypeStruct((t, W_A), BF16),
            jax.ShapeDtypeStruct((bsz, H_A, HEAD_DIM, HEAD_DIM), F32),
            jax.ShapeDtypeStruct((bsz, H_A, HEAD_DIM), F32),
            jax.ShapeDtypeStruct((bsz, 1, LANES), F32),
        ],
        scratch_shapes=[pltpu.VMEM((H_A, HEAD_DIM, HEAD_DIM), F32),
                        pltpu.VMEM((8, HEAD_DIM), F32), pltpu.VMEM((8, LANES), F32)],
        compiler_params=pltpu.CompilerParams(
            dimension_semantics=("parallel", "arbitrary"), vmem_limit_bytes=VMEM_LIMIT),
        name="mlstm_prompt",
    )(proj, proj, proj, proj, proj, proj, gate_bias, onw3)


def _in_place(operands, n_blocked, targets):
    specs, aliases = [], {}
    operands = list(operands)
    for out_idx, arr in targets.items():
        if arr is not None:
            aliases[len(operands)] = out_idx
            operands.append(arr)
            specs.append(pl.BlockSpec(memory_space=pl.ANY))
    n_extra = len(specs)

    def strip(kernel):
        def body(*refs):
            kernel(*refs[:n_blocked], *refs[n_blocked + n_extra:])
        return body

    return operands, specs, aliases, strip


def _mlstm_sample(proj, y_prev, c_stack, gate_bias, onw3, c_state, n_state, m_rows, layer,
                  dec_b, c, rows):
    nseq = TILE_R // c
    depth = c_state.shape[0]
    in_specs = [_col_spec(W_A, off, rows) for off in (OFF_AQ, OFF_AK, OFF_AV, OFF_AO, OFF_ZA)]
    in_specs += [_col_spec(LANES, OFF_G, rows), _param_spec(LANES, 0, layer),
                 _param_spec(W_A, 0, layer)]
    in_specs += [
        pl.BlockSpec((None, nseq, H_A, HEAD_DIM, HEAD_DIM), lambda i: (layer, i, 0, 0, 0)),
        pl.BlockSpec((None, nseq, H_A, HEAD_DIM), lambda i: (layer, i, 0, 0)),
        pl.BlockSpec((None, TILE_R, LANES), lambda i: (layer, i, 0)),
    ]
    operands = (proj, proj, proj, proj, proj, proj, gate_bias, onw3, c_state, n_state, m_rows)
    operands, extra, aliases, strip = _in_place(operands, len(in_specs), {0: y_prev, 1: c_stack})
    return pl.pallas_call(
        strip(functools.partial(_mlstm_sample_kernel, c=c)),
        grid=(dec_b // nseq,),
        in_specs=in_specs + extra,
        out_specs=[
            pl.BlockSpec((TILE_R, W_A), lambda i: (rows(i), 0)),
            pl.BlockSpec((None, nseq, H_A, HEAD_DIM, HEAD_DIM), lambda i: (layer, i, 0, 0, 0)),
            pl.BlockSpec((nseq, H_A, HEAD_DIM), lambda i: (i, 0, 0)),
            pl.BlockSpec((TILE_R, LANES), lambda i: (i, 0)),
        ],
        out_shape=[
            jax.ShapeDtypeStruct(y_prev.shape, BF16),
            jax.ShapeDtypeStruct((depth, dec_b, H_A, HEAD_DIM, HEAD_DIM), F32),
            jax.ShapeDtypeStruct((dec_b, H_A, HEAD_DIM), F32),
            jax.ShapeDtypeStruct((dec_b * c, LANES), F32),
        ],
        input_output_aliases=aliases,
        compiler_params=pltpu.CompilerParams(
            dimension_semantics=("parallel",), vmem_limit_bytes=VMEM_LIMIT),
        name="mlstm_sample",
    )(*operands)


def _l2norm(x):
    return x * lax.rsqrt(jnp.sum(x * x, axis=1, keepdims=True) + EPS)


INV_LEAF = 16


def _unit_lower_inverse(a_list, row, col, c):
    leaf = min(INV_LEAF, c)
    kl = _log2(leaf)
    leaf_mask = (row >> kl) == (col >> kl)
    eye = jnp.where(row == col, 1.0, 0.0)
    p = [jnp.where(leaf_mask, -a, 0.0) for a in a_list]
    t = [eye + n for n in p]
    for _ in range(kl - 1):
        p = [_dot(x, x) for x in p]
        t = [ti + _dot(pi, ti) for pi, ti in zip(p, t)]
    w = leaf
    while w < c:
        kw = _log2(w)
        lower_left = (((row >> (kw + 1)) == (col >> (kw + 1)))
                      & (((row >> kw) & 1) == 1) & (((col >> kw) & 1) == 0))
        left = [_dot(ti, jnp.where(lower_left, a, 0.0)) for ti, a in zip(t, a_list)]
        t = [ti - _dot(li, ti) for li, ti in zip(left, t)]
        w *= 2
    return t


def _gdn_tile(conv_win, cw_ref, z_ref, g_ref, gp_ref, onw_ref, y_ref, get_s, put_s, nseq, c):
    r = nseq * c
    row, col, same = _seq_masks(nseq, c)
    incl = same & (col <= row)
    strict = same & (col < row)
    gp = gp_ref[...]
    graw = g_ref[...]
    gdec = -jnp.exp(gp[1:2, :]) * _softplus(graw + gp[0:1, :])
    beta = _sigmoid(graw)
    gcum = _dot_exact(incl.astype(F32), gdec)
    if nseq == 1:
        glast = jnp.broadcast_to(gcum[r - 1:r, :], (r, LANES))
    else:
        glast = _dot_exact(same.astype(F32), gdec)
    gcum_t = gcum.T

    heads = range(H_B)
    hsl = [slice(h * HEAD_DIM, (h + 1) * HEAD_DIM) for h in heads]
    rowv = lax.broadcasted_iota(jnp.int32, (r, 1), 0)

    def conv(lo):
        acc = conv_win(0, slice(lo, lo + HEAD_DIM)) * cw_ref[0:1, lo:lo + HEAD_DIM]
        for jj in range(1, CONV_W):
            acc = acc + conv_win(jj, slice(lo, lo + HEAD_DIM)) * cw_ref[jj:jj + 1, lo:lo + HEAD_DIM]
        return _silu(acc)

    q = [_l2norm(conv(hs.start)) * QK_SCALE for hs in hsl]
    k = [_l2norm(conv(W_B + hs.start)) for hs in hsl]
    v = [conv(2 * W_B + hs.start) for hs in hsl]
    gc = [gcum[:, G_BA + h:G_BA + h + 1] for h in heads]
    gl = [glast[:, G_BA + h:G_BA + h + 1] for h in heads]
    bc = [beta[:, G_BB + h:G_BB + h + 1] for h in heads]
    decay = [jnp.exp(jnp.where(incl, gc[h] - gcum_t[G_BA + h:G_BA + h + 1, :], NEG_BIG))
             for h in heads]
    kk = [_dot_nt(k[h], k[h]) for h in heads]
    qk = [_dot_nt(q[h], k[h]) for h in heads]
    amat = [jnp.where(strict, bc[h] * decay[h] * kk[h], 0.0) for h in heads]
    tinv = _unit_lower_inverse(amat, row, col, c)
    sol = [_dot(tinv[h], jnp.concatenate([bc[h] * v[h], (bc[h] * jnp.exp(gc[h])) * k[h]], axis=1))
           for h in heads]
    wks = [_cat_rows([_dot(_rows(sol[h][:, HEAD_DIM:], i, c), get_s(i, h)) for i in range(nseq)])
           for h in heads]
    qs = [_cat_rows([_dot(_rows(q[h], i, c), get_s(i, h)) for i in range(nseq)]) for h in heads]
    wmat = [sol[h][:, :HEAD_DIM] - wks[h] for h in heads]
    aw = [_dot(decay[h] * qk[h], wmat[h]) for h in heads]
    for h in heads:
        hs = hsl[h]
        oh = jnp.exp(gc[h]) * qs[h] + aw[h]
        y_ref[:, hs] = _head_norm_gate(oh, onw_ref[:, hs], z_ref[:, hs]).astype(BF16)
    for h in heads:
        kd_t = (k[h] * jnp.exp(gl[h] - gc[h])).T.astype(BF16)
        e_last = jnp.exp(gl[h])
        for i in range(nseq):
            if nseq == 1:
                upd = jnp.dot(kd_t, wmat[h].astype(BF16), preferred_element_type=F32)
            else:
                seq_rows = (rowv >> _log2(c)) == i
                upd = jnp.dot(kd_t, jnp.where(seq_rows, wmat[h], 0.0).astype(BF16),
                              preferred_element_type=F32)
            put_s(i, h, e_last[i * c:i * c + 1, :] * get_s(i, h) + upd)


def _gdn_prompt_kernel(q_ref, k_ref, v_ref, z_ref, g_ref, gp_ref, cw_ref, onw_ref,
                       y_ref, s_out, conv_out, s_scr, ext_scr):
    j = pl.program_id(1)

    @pl.when(j == 0)
    def _():
        s_scr[...] = jnp.zeros_like(s_scr)
        ext_scr[0:CONV_PAD, :] = jnp.zeros((CONV_PAD, 3 * W_B), F32)

    ext_scr[CONV_PAD:, 0:W_B] = q_ref[...]
    ext_scr[CONV_PAD:, W_B:2 * W_B] = k_ref[...]
    ext_scr[CONV_PAD:, 2 * W_B:] = v_ref[...]

    def conv_win(jj, cols):
        start = CONV_PAD - (CONV_W - 1) + jj
        return ext_scr[start:start + TILE_R, cols]

    def put_s(i, h, val):
        s_scr[h] = val

    _gdn_tile(conv_win, cw_ref, z_ref, g_ref, gp_ref, onw_ref, y_ref,
              lambda i, h: s_scr[h], put_s, 1, TILE_R)

    @pl.when(j == pl.num_programs(1) - 1)
    def _():
        s_out[...] = s_scr[...]
        conv_out[...] = ext_scr[TILE_R + CONV_PAD - (CONV_W - 1):TILE_R + CONV_PAD, :]

    ext_scr[0:CONV_PAD, :] = ext_scr[TILE_R:TILE_R + CONV_PAD, :]


def _gdn_sample_kernel(q_ref, k_ref, v_ref, z_ref, g_ref, gp_ref, cw_ref, onw_ref,
                       s_in, conv_in, y_ref, s_out, conv_out, ext_scr, *, c):
    nseq = TILE_R // c
    hist = CONV_W - 1
    ext_scr[:, CONV_PAD - hist:CONV_PAD, :] = conv_in[...]
    ext_scr[:, CONV_PAD:, 0:W_B] = q_ref[...].reshape(nseq, c, W_B)
    ext_scr[:, CONV_PAD:, W_B:2 * W_B] = k_ref[...].reshape(nseq, c, W_B)
    ext_scr[:, CONV_PAD:, 2 * W_B:] = v_ref[...].reshape(nseq, c, W_B)

    def conv_win(jj, cols):
        start = CONV_PAD - hist + jj
        return ext_scr[:, start:start + c, cols].reshape(TILE_R, cols.stop - cols.start)

    def put_s(i, h, val):
        s_out[i, h] = val

    _gdn_tile(conv_win, cw_ref, z_ref, g_ref, gp_ref, onw_ref, y_ref,
              lambda i, h: s_in[i, h], put_s, nseq, c)
    conv_out[...] = ext_scr[:, CONV_PAD + c - hist:CONV_PAD + c, :]


def _gdn_prompt(proj, gdn_par, conv_w, onw3, layer, bsz, n_chunk, rows):
    t = proj.shape[0]
    in_specs = [_col_spec(W_B, off, rows) for off in (OFF_BQ, OFF_BK, OFF_BV, OFF_ZB)]
    in_specs += [
        _col_spec(LANES, OFF_G, rows),
        pl.BlockSpec((None, 8, LANES), lambda b, j: (layer, 0, 0)),
        pl.BlockSpec((None, CONV_W, 3 * W_B), lambda b, j: (layer, 0, 0)),
        _param_spec(W_B, W_A, layer),
    ]
    return pl.pallas_call(
        _gdn_prompt_kernel,
        grid=(bsz, n_chunk),
        in_specs=in_specs,
        out_specs=[
            pl.BlockSpec((TILE_R, W_B), lambda b, j: (rows(b, j), 0)),
            pl.BlockSpec((None, H_B, HEAD_DIM, HEAD_DIM), lambda b, j: (b, 0, 0, 0)),
            pl.BlockSpec((None, CONV_W - 1, 3 * W_B), lambda b, j: (b, 0, 0)),
        ],
        out_shape=[
            jax.ShapeDtypeStruct((t, W_B), BF16),
            jax.ShapeDtypeStruct((bsz, H_B, HEAD_DIM, HEAD_DIM), F32),
            jax.ShapeDtypeStruct((bsz, CONV_W - 1, 3 * W_B), F32),
        ],
        scratch_shapes=[pltpu.VMEM((H_B, HEAD_DIM, HEAD_DIM), F32),
                        pltpu.VMEM((TILE_R + CONV_PAD, 3 * W_B), F32)],
        compiler_params=pltpu.CompilerParams(
            dimension_semantics=("parallel", "arbitrary"), vmem_limit_bytes=VMEM_LIMIT),
        name="gdn_prompt",
    )(proj, proj, proj, proj, proj, gdn_par, conv_w, onw3)


def _gdn_sample(proj, y_prev, s_stack, gdn_par, conv_w, onw3, s_state, conv_state, layer,
                dec_b, c, rows):
    nseq = TILE_R // c
    depth = s_state.shape[0]
    in_specs = [_col_spec(W_B, off, rows) for off in (OFF_BQ, OFF_BK, OFF_BV, OFF_ZB)]
    in_specs += [
        _col_spec(LANES, OFF_G, rows),
        pl.BlockSpec((None, 8, LANES), lambda i: (layer, 0, 0)),
        pl.BlockSpec((None, CONV_W, 3 * W_B), lambda i: (layer, 0, 0)),
        _param_spec(W_B, W_A, layer),
        pl.BlockSpec((None, nseq, H_B, HEAD_DIM, HEAD_DIM), lambda i: (layer, i, 0, 0, 0)),
        pl.BlockSpec((None, nseq, CONV_W - 1, 3 * W_B), lambda i: (layer, i, 0, 0)),
    ]
    operands = (proj, proj, proj, proj, proj, gdn_par, conv_w, onw3, s_state, conv_state)
    operands, extra, aliases, strip = _in_place(operands, len(in_specs), {0: y_prev, 1: s_stack})
    return pl.pallas_call(
        strip(functools.partial(_gdn_sample_kernel, c=c)),
        grid=(dec_b // nseq,),
        in_specs=in_specs + extra,
        out_specs=[
            pl.BlockSpec((TILE_R, W_B), lambda i: (rows(i), 0)),
            pl.BlockSpec((None, nseq, H_B, HEAD_DIM, HEAD_DIM), lambda i: (layer, i, 0, 0, 0)),
            pl.BlockSpec((nseq, CONV_W - 1, 3 * W_B), lambda i: (i, 0, 0)),
        ],
        out_shape=[
            jax.ShapeDtypeStruct(y_prev.shape, BF16),
            jax.ShapeDtypeStruct((depth, dec_b, H_B, HEAD_DIM, HEAD_DIM), F32),
            jax.ShapeDtypeStruct((dec_b, CONV_W - 1, 3 * W_B), F32),
        ],
        scratch_shapes=[pltpu.VMEM((nseq, CONV_PAD + c, 3 * W_B), F32)],
        input_output_aliases=aliases,
        compiler_params=pltpu.CompilerParams(
            dimension_semantics=("parallel",), vmem_limit_bytes=VMEM_LIMIT),
        name="gdn_sample",
    )(*operands)


def _hgrn_tile(q_ref, f_ref, i_ref, z_ref, lbp_ref, onw_ref, y_ref, get_s, put_s,
               nseq, c, layer):
    r = nseq * c
    row, col, same = _seq_masks(nseq, c)
    incl = same & (col <= row)
    rowv = lax.broadcasted_iota(jnp.int32, (r, 1), 0)
    lbp = lbp_ref[...]
    e = jnp.exp(lbp - jnp.max(lbp, axis=0, keepdims=True))
    sm = e / jnp.sum(e, axis=0, keepdims=True)
    lb = jnp.zeros((1, W_C), F32)
    for l in range(1, layer + 1):
        lb = lb + sm[l:l + 1, :]

    cf = f_ref[...]
    logf = _log_sigmoid(cf) + jnp.log1p(lb * jnp.exp(jnp.minimum(-cf, EXP_CLIP)))
    kall = (1.0 - lb) * _sigmoid(-cf)
    bcum = _dot_exact(incl.astype(F32), logf)
    if nseq == 1:
        blast = jnp.broadcast_to(bcum[r - 1:r, :], (r, W_C))
    else:
        blast = _dot_exact(same.astype(F32), logf)

    levels = []
    w = c // 2
    while w >= 1:
        levels.append(w)
        w //= 2
    rowf = lax.broadcasted_iota(jnp.int32, (r, W_C), 0)
    refs = []
    for w in levels:
        if 2 * w >= 8:
            refs.append(_cat_rows([
                jnp.broadcast_to(bcum[g * 2 * w + w - 1:g * 2 * w + w, :], (2 * w, W_C))
                for g in range(r // (2 * w))]))
        else:
            pos = rowf & (2 * w - 1)
            ref = bcum
            for off in range(2 * w):
                if off != w - 1:
                    ref = jnp.where(pos == off, pltpu.roll(bcum, (off - (w - 1)) % r, axis=0), ref)
            refs.append(ref)

    heads = range(H_C)
    hsl = [slice(h * HEAD_DIM, (h + 1) * HEAD_DIM) for h in heads]
    q = [_silu(q_ref[:, hs]) for hs in hsl]
    k = [kall[:, hs] for hs in hsl]
    b = [bcum[:, hs] for hs in hsl]
    diag = row == col
    amat = [jnp.where(diag, _dot_nt(q[h], k[h]), 0.0) for h in heads]
    for w, ref_all in zip(levels, refs):
        lw = _log2(w)
        tgt = ((rowv >> lw) & 1) == 1
        pair = (row >> (lw + 1)) == (col >> (lw + 1))
        qp = [q[h] * jnp.exp(jnp.where(tgt, b[h] - ref_all[:, hsl[h]], NEG_BIG)) for h in heads]
        kp = [k[h] * jnp.exp(jnp.where(tgt, NEG_BIG, ref_all[:, hsl[h]] - b[h])) for h in heads]
        amat = [amat[h] + jnp.where(pair, _dot_nt(qp[h], kp[h]), 0.0) for h in heads]
    qs = [_cat_rows([_dot(_rows(q[h] * jnp.exp(b[h]), i, c), get_s(i, h)) for i in range(nseq)])
          for h in heads]
    av = [_dot(amat[h], i_ref[:, hsl[h]]) for h in heads]
    for h in heads:
        hs = hsl[h]
        y_ref[:, hs] = _head_norm_gate(qs[h] + av[h], onw_ref[:, hs], z_ref[:, hs]).astype(BF16)
    for h in heads:
        hs = hsl[h]
        vh = i_ref[:, hs]
        bl = blast[:, hs]
        kd_t = (k[h] * jnp.exp(bl - b[h])).T.astype(BF16)
        eb_t = jnp.exp(bl).T
        for i in range(nseq):
            if nseq == 1:
                upd = jnp.dot(kd_t, vh.astype(BF16), preferred_element_type=F32)
            else:
                seq_rows = (rowv >> _log2(c)) == i
                upd = jnp.dot(kd_t, jnp.where(seq_rows, vh, 0.0).astype(BF16),
                              preferred_element_type=F32)
            put_s(i, h, eb_t[:, i * c:i * c + 1] * get_s(i, h) + upd)


class _ColView:
    def __init__(self, parts):
        self.parts = []
        pos = 0
        for ref, lo, hi in parts:
            self.parts.append((pos, ref, lo, hi - lo))
            pos += hi - lo
        self.width = pos

    def __getitem__(self, idx):
        if idx is Ellipsis:
            rows, cols = slice(None), slice(0, self.width)
        else:
            rows, cols = idx
            cols = slice(cols.start or 0, self.width if cols.stop is None else cols.stop)
        out = []
        for pos, ref, lo, n in self.parts:
            a, b = max(cols.start, pos), min(cols.stop, pos + n)
            if a < b:
                out.append(ref[rows, lo + a - pos:lo + b - pos])
        return out[0] if len(out) == 1 else jnp.concatenate(out, axis=1)


def _hgrn_views(ca_ref, cb_ref, zc_ref):
    two = 2 * W_C - ca_ref.shape[1]
    q = _ColView([(ca_ref, 0, W_C)])
    f = _ColView([(ca_ref, W_C, ca_ref.shape[1]), (cb_ref, 0, two)])
    i = _ColView([(cb_ref, two, two + W_C)])
    z = _ColView([(zc_ref, 0, W_C)])
    return q, f, i, z


def _hgrn_prompt_kernel(ca_ref, cb_ref, zc_ref, lbp_ref, onw_ref, y_ref, s_out, s_scr,
                        *, layer):
    j = pl.program_id(1)

    @pl.when(j == 0)
    def _():
        s_scr[...] = jnp.zeros_like(s_scr)

    def put_s(i, h, val):
        s_scr[h] = val

    _hgrn_tile(*_hgrn_views(ca_ref, cb_ref, zc_ref), lbp_ref, onw_ref, y_ref,
               lambda i, h: s_scr[h], put_s, 1, TILE_R, layer)

    @pl.when(j == pl.num_programs(1) - 1)
    def _():
        s_out[...] = s_scr[...]


def _hgrn_sample_kernel(ca_ref, cb_ref, zc_ref, lbp_ref, onw_ref, s_in, y_ref, s_out,
                        *, c, layer):
    def put_s(i, h, val):
        s_out[i, h] = val

    _hgrn_tile(*_hgrn_views(ca_ref, cb_ref, zc_ref), lbp_ref, onw_ref, y_ref,
               lambda i, h: s_in[i, h], put_s, TILE_R // c, c, layer)


HGRN_BLK = W_A
assert OFF_C % HGRN_BLK == 0 and 3 * W_C == 2 * HGRN_BLK and OFF_ZC % HGRN_BLK == 0


def _hgrn_prompt(proj, lb_par, onw3, layer, bsz, n_chunk, rows):
    t = proj.shape[0]
    depth = lb_par.shape[0]
    in_specs = [_col_spec(HGRN_BLK, off, rows) for off in (OFF_C, OFF_C + HGRN_BLK, OFF_ZC)]
    in_specs += [pl.BlockSpec((depth, W_C), lambda b, j: (0, 0)),
                 _param_spec(W_C, W_A + W_B, layer)]
    return pl.pallas_call(
        functools.partial(_hgrn_prompt_kernel, layer=layer),
        grid=(bsz, n_chunk),
        in_specs=in_specs,
        out_specs=[
            pl.BlockSpec((TILE_R, W_C), lambda b, j: (rows(b, j), 0)),
            pl.BlockSpec((None, H_C, HEAD_DIM, HEAD_DIM), lambda b, j: (b, 0, 0, 0)),
        ],
        out_shape=[
            jax.ShapeDtypeStruct((t, W_C), BF16),
            jax.ShapeDtypeStruct((bsz, H_C, HEAD_DIM, HEAD_DIM), F32),
        ],
        scratch_shapes=[pltpu.VMEM((H_C, HEAD_DIM, HEAD_DIM), F32)],
        compiler_params=pltpu.CompilerParams(
            dimension_semantics=("parallel", "arbitrary"), vmem_limit_bytes=VMEM_LIMIT),
        name="hgrn_prompt",
    )(proj, proj, proj, lb_par, onw3)


def _hgrn_sample(proj, y_prev, s_stack, lb_par, onw3, s_state, layer, dec_b, c, rows):
    nseq = TILE_R // c
    depth = lb_par.shape[0]
    in_specs = [_col_spec(HGRN_BLK, off, rows) for off in (OFF_C, OFF_C + HGRN_BLK, OFF_ZC)]
    in_specs += [
        pl.BlockSpec((depth, W_C), lambda i: (0, 0)),
        _param_spec(W_C, W_A + W_B, layer),
        pl.BlockSpec((None, nseq, H_C, HEAD_DIM, HEAD_DIM), lambda i: (layer, i, 0, 0, 0)),
    ]
    operands = (proj, proj, proj, lb_par, onw3, s_state)
    operands, extra, aliases, strip = _in_place(operands, len(in_specs), {0: y_prev, 1: s_stack})
    return pl.pallas_call(
        strip(functools.partial(_hgrn_sample_kernel, c=c, layer=layer)),
        grid=(dec_b // nseq,),
        in_specs=in_specs + extra,
        out_specs=[
            pl.BlockSpec((TILE_R, W_C), lambda i: (rows(i), 0)),
            pl.BlockSpec((None, nseq, H_C, HEAD_DIM, HEAD_DIM), lambda i: (layer, i, 0, 0, 0)),
        ],
        out_shape=[
            jax.ShapeDtypeStruct(y_prev.shape, BF16),
            jax.ShapeDtypeStruct((depth, dec_b, H_C, HEAD_DIM, HEAD_DIM), F32),
        ],
        input_output_aliases=aliases,
        compiler_params=pltpu.CompilerParams(
            dimension_semantics=("parallel",), vmem_limit_bytes=VMEM_LIMIT),
        name="hgrn_sample",
    )(*operands)


def _gate_weight(w_t):
    ai0 = 4 * W_A
    ba0 = ai0 + 2 * H_A + 3 * W_B
    n_gate = 2 * H_A + 2 * H_B
    pad = jnp.zeros((w_t.shape[0], LANES - n_gate, w_t.shape[2]), w_t.dtype)
    return jnp.concatenate([w_t[:, ai0:ai0 + 2 * H_A], w_t[:, ba0:ba0 + 2 * H_B], pad], axis=1)


def _lane_row(depth, pieces):
    row = jnp.zeros((depth, LANES), F32)
    for off, val in pieces:
        row = lax.dynamic_update_slice(row, val.astype(F32), (0, off))
    return row[:, None, :]


def kernel(x_prompt, x_sample, state_mlstm_C, state_mlstm_n, state_mlstm_m, state_gdn_S,
           state_gdn_conv, state_hgrn_S, meta_tokens, norm_w, w_in, mlstm_gate_b, gdn_A_log,
           gdn_dt_bias, gdn_conv_w, hgrn_lower_bounds, out_norm_w, w_out, final_norm_w):
    bsz, seq_len, d = x_prompt.shape
    dec_b, dec_seq, _ = x_sample.shape
    depth = w_in.shape[0]
    assert d == D_MODEL and seq_len % TILE_R == 0 and TILE_R % dec_seq == 0
    assert w_in.shape[2] == W_SEGMENTS[-1][1] + W_SEGMENTS[-1][2]
    nseq = TILE_R // dec_seq
    assert dec_b % nseq == 0

    chunks = seq_len // TILE_R
    n_chunk = chunks + 1
    tm = bsz * TILE_R
    n_dec = dec_b * dec_seq
    dec_blk0 = n_chunk * bsz

    def prompt_rows(b, j):
        return j * bsz + b

    def dec_rows(i):
        return dec_blk0 + i

    w_t = jnp.swapaxes(w_in.astype(F32), 1, 2)
    w_gate = _gate_weight(w_t)
    w_out_bf = w_out.astype(BF16)
    norm_w3 = norm_w.astype(F32)[:, None, :]
    onw3 = out_norm_w.astype(F32)[:, None, :]
    final_nw = final_norm_w.astype(F32)[None, :]
    gate_bias = _lane_row(depth, [(G_AI, mlstm_gate_b[:, 0]), (G_AF, mlstm_gate_b[:, 1])])
    gdn_par = jnp.concatenate([
        _lane_row(depth, [(G_BA, gdn_dt_bias)]), _lane_row(depth, [(G_BA, gdn_A_log)]),
        jnp.zeros((depth, 6, LANES), F32)], axis=1)
    conv_w = gdn_conv_w.astype(F32)
    lb_par = hgrn_lower_bounds.astype(F32)
    m_rows = jnp.repeat(
        jnp.pad(state_mlstm_m.astype(F32), ((0, 0), (0, 0), (0, LANES - H_A))), dec_seq, axis=1)

    x, h = _prep(x_prompt.astype(F32), x_sample.astype(F32), meta_tokens.astype(F32), norm_w3,
                 bsz, n_chunk)

    new_p = [[] for _ in range(6)]
    new_s = [[] for _ in range(3)]
    c_s = s_s = h_s = None
    y_prompt = y_sample = None
    for l in range(depth):
        proj = _inproj(h, w_t, w_gate, l, tm)

        ya, c_p, n_p, m_p = _mlstm_prompt(proj, gate_bias, onw3, l, bsz, n_chunk, prompt_rows)
        ya, c_s, n_s, m_s = _mlstm_sample(proj, ya, c_s, gate_bias, onw3, state_mlstm_C,
                                          state_mlstm_n, m_rows, l, dec_b, dec_seq, dec_rows)
        yb, s_p, cv_p = _gdn_prompt(proj, gdn_par, conv_w, onw3, l, bsz, n_chunk, prompt_rows)
        yb, s_s, cv_s = _gdn_sample(proj, yb, s_s, gdn_par, conv_w, onw3, state_gdn_S,
                                    state_gdn_conv, l, dec_b, dec_seq, dec_rows)
        yc, h_p = _hgrn_prompt(proj, lb_par, onw3, l, bsz, n_chunk, prompt_rows)
        yc, h_s = _hgrn_sample(proj, yc, h_s, lb_par, onw3, state_hgrn_S, l, dec_b, dec_seq,
                               dec_rows)

        for lst, val in zip(new_p, (c_p, n_p, m_p[:, 0, :H_A], s_p, cv_p, h_p)):
            lst.append(val)
        for lst, val in zip(new_s, (n_s, m_s[::dec_seq, :H_A], cv_s)):
            lst.append(val)

        if l + 1 < depth:
            x, h = _outproj_mid(ya, yb, yc, x, w_out_bf, l, norm_w3, tm)
        else:
            y_prompt = _outproj_final(
                ya, yb, yc, x, w_out_bf, l, final_nw, tm, 1, chunks, (bsz, chunks, TILE_R, d),
                pl.BlockSpec((bsz, None, TILE_R, d), lambda i: (0, i, 0, 0)))
            y_sample = _outproj_final(
                ya, yb, yc, x, w_out_bf, l, final_nw, tm, n_chunk, n_dec // tm, (n_dec, d),
                pl.BlockSpec((tm, d), lambda i: (i, 0)))

    outs_p = [jnp.stack(a, axis=0) for a in new_p]
    n_all, m_all, cv_all = [jnp.stack(a, axis=0) for a in new_s]
    return (y_prompt.reshape(bsz, seq_len, d), y_sample.reshape(dec_b, dec_seq, d),
            *outs_p, c_s, n_all, m_all, s_s, cv_all, h_s)
```

```python
import functools
import math

import jax
import jax.numpy as jnp
from jax import lax
from jax.experimental import pallas as pl
from jax.experimental.pallas import tpu as pltpu

F32 = jnp.float32
BF16 = jnp.bfloat16

D_MODEL = 2048
HEAD_DIM = 128
H_A, H_B, H_C = 6, 6, 4
W_A, W_B, W_C = H_A * HEAD_DIM, H_B * HEAD_DIM, H_C * HEAD_DIM
MIX = W_A + W_B + W_C
N_META = 16
CONV_W = 4
EPS = 1e-6
NEG_BIG = -1e30
EXP_CLIP = 60.0
QK_SCALE = HEAD_DIM ** -0.5

TILE_R = 128
LANES = 128
CONV_PAD = 8

OFF_AQ = 0
OFF_AK, OFF_AV, OFF_AO = W_A, 2 * W_A, 3 * W_A
OFF_BQ = 4 * W_A
OFF_BK, OFF_BV = OFF_BQ + W_B, OFF_BQ + 2 * W_B
OFF_C = OFF_BQ + 3 * W_B
OFF_ZA = OFF_C + 3 * W_C
OFF_ZB = OFF_ZA + W_A
OFF_ZC = OFF_ZB + W_B
OFF_G = OFF_ZC + W_C
N_PACK = OFF_G + 2 * LANES
G_AI, G_AF, G_BA, G_BB = 0, H_A, 2 * H_A, 2 * H_A + H_B
TN = 1024
W_SEGMENTS = ((OFF_AQ, OFF_BQ, 0), (OFF_BQ, OFF_C, 2 * H_A), (OFF_C, OFF_G, 2 * H_A + 2 * H_B))
W_ROW_CHUNK = 256

VMEM_LIMIT = 56 * 1024 * 1024


def _sigmoid(x):
    return 1.0 / (1.0 + jnp.exp(-x))


def _silu(x):
    return x * _sigmoid(x)


def _softplus(x):
    return jnp.maximum(x, 0.0) + jnp.log1p(jnp.exp(-jnp.abs(x)))


def _log_sigmoid(x):
    return -_softplus(-x)


def _dot(a, b):
    return jnp.dot(a.astype(BF16), b.astype(BF16), preferred_element_type=F32)


def _dot_nt(a, b):
    return lax.dot_general(a.astype(BF16), b.astype(BF16), (((1,), (1,)), ((), ())),
                           preferred_element_type=F32)


def _dot_exact(sel, x):
    hi = x.astype(BF16)
    r1 = x - hi.astype(F32)
    mid = r1.astype(BF16)
    lo = (r1 - mid.astype(F32)).astype(BF16)
    s = sel.astype(BF16)
    return (jnp.dot(s, hi, preferred_element_type=F32)
            + jnp.dot(s, mid, preferred_element_type=F32)
            + jnp.dot(s, lo, preferred_element_type=F32))


def _log2(n):
    k = int(math.log2(n))
    assert (1 << k) == n
    return k


def _seq_masks(nseq, c):
    r = nseq * c
    row = lax.broadcasted_iota(jnp.int32, (r, r), 0)
    col = lax.broadcasted_iota(jnp.int32, (r, r), 1)
    if nseq == 1:
        same = row >= 0
    else:
        k = _log2(c)
        same = (row >> k) == (col >> k)
    return row, col, same


def _head_norm_gate(h, onw, z):
    hn = h * lax.rsqrt(jnp.mean(h * h, axis=1, keepdims=True) + EPS)
    return hn * onw * _silu(z)


def _rows(x, i, c):
    return x[i * c:(i + 1) * c]


def _cat_rows(parts):
    return parts[0] if len(parts) == 1 else jnp.concatenate(parts, axis=0)


def _chunk_shift(jj, r0):
    p = jj * TN + r0
    for p0, p1, shift in W_SEGMENTS:
        if p0 <= p and p + W_ROW_CHUNK <= p1:
            return shift
    return None


def _fill_weight_tile(w0_ref, w1_ref, wg_ref, wb_scr, tail_scr, jj):
    for r0 in range(0, TN, W_ROW_CHUNK):
        shift = _chunk_shift(jj, r0)
        dst = slice(r0, r0 + W_ROW_CHUNK)
        if shift is None:
            g0 = OFF_G - jj * TN
            assert r0 <= g0 and g0 + LANES <= r0 + W_ROW_CHUNK
            wb_scr[dst, :] = jnp.zeros((W_ROW_CHUNK, D_MODEL), BF16)
            wb_scr[g0:g0 + LANES, :] = wg_ref[...].astype(BF16)
        elif r0 + shift + W_ROW_CHUNK <= TN:
            wb_scr[dst, :] = w0_ref[r0 + shift:r0 + shift + W_ROW_CHUNK, :].astype(BF16)
        else:
            tail_scr[0:W_ROW_CHUNK, :] = w0_ref[r0:TN, :]
            tail_scr[W_ROW_CHUNK:, :] = w1_ref[0:tail_scr.shape[0] - W_ROW_CHUNK, :]
            wb_scr[dst, :] = tail_scr[shift:shift + W_ROW_CHUNK, :].astype(BF16)


def _inproj_kernel(h_ref, w0_ref, w1_ref, wg_ref, o_ref, wb_scr, tail_scr):
    j = pl.program_id(0)

    @pl.when(pl.program_id(1) == 0)
    def _():
        for jj in range(N_PACK // TN):
            @pl.when(j == jj)
            def _(jj=jj):
                _fill_weight_tile(w0_ref, w1_ref, wg_ref, wb_scr, tail_scr, jj)

    o_ref[...] = lax.dot_general(h_ref[...], wb_scr[...], (((1,), (1,)), ((), ())),
                                 preferred_element_type=F32)


W_TAIL_ROWS = 32


def _inproj(h, w_t, w_gate_t, layer, tm):
    t = h.shape[0]
    n_tiles = N_PACK // TN
    last_blk = (w_t.shape[1] - 1) // TN
    assert n_tiles - 1 <= last_blk and W_TAIL_ROWS >= W_SEGMENTS[-1][2]
    return pl.pallas_call(
        _inproj_kernel,
        grid=(n_tiles, t // tm),
        in_specs=[
            pl.BlockSpec((tm, D_MODEL), lambda j, i: (i, 0)),
            pl.BlockSpec((None, TN, D_MODEL), lambda j, i: (layer, j, 0)),
            pl.BlockSpec((None, TN, D_MODEL), lambda j, i: (layer, jnp.minimum(j + 1, last_blk), 0)),
            pl.BlockSpec((None, LANES, D_MODEL), lambda j, i: (layer, 0, 0)),
        ],
        out_specs=pl.BlockSpec((tm, TN), lambda j, i: (i, j)),
        out_shape=jax.ShapeDtypeStruct((t, N_PACK), F32),
        scratch_shapes=[pltpu.VMEM((TN, D_MODEL), BF16),
                        pltpu.VMEM((W_ROW_CHUNK + W_TAIL_ROWS, D_MODEL), F32)],
        compiler_params=pltpu.CompilerParams(
            dimension_semantics=("arbitrary", "arbitrary"), vmem_limit_bytes=VMEM_LIMIT),
        name="inproj",
    )(h, w_t, w_t, w_gate_t)


def _rmsnorm(x, w):
    ms = jnp.mean(x * x, axis=1, keepdims=True)
    return x * lax.rsqrt(ms + EPS) * w


def _prep_kernel(xp_ref, xs_ref, meta_ref, nw_ref, x_ref, h_ref, *, n_chunk, bsz):
    i = pl.program_id(0)
    d = x_ref.shape[1]

    def emit(x):
        x_ref[...] = x
        h_ref[...] = _rmsnorm(x, nw_ref[...]).astype(BF16)

    @pl.when(i == 0)
    def _():
        slot = jnp.concatenate([jnp.zeros((TILE_R - N_META, d), F32), meta_ref[...]], axis=0)
        emit(jnp.concatenate([slot] * bsz, axis=0))

    @pl.when((i > 0) & (i < n_chunk))
    def _():
        emit(xp_ref[...].reshape(bsz * TILE_R, d))

    @pl.when(i >= n_chunk)
    def _():
        emit(xs_ref[...])


def _prep(x_prompt, x_sample, meta, norm_w3, bsz, n_chunk):
    d = x_prompt.shape[-1]
    tm = bsz * TILE_R
    n_dec = x_sample.shape[0] * x_sample.shape[1]
    assert n_dec % tm == 0
    xp = x_prompt.reshape(bsz, n_chunk - 1, TILE_R, d)
    xs = x_sample.reshape(n_dec, d)
    n_blk = n_chunk + n_dec // tm
    t = n_blk * tm
    return pl.pallas_call(
        functools.partial(_prep_kernel, n_chunk=n_chunk, bsz=bsz),
        grid=(n_blk,),
        in_specs=[
            pl.BlockSpec((bsz, None, TILE_R, d), lambda i: (0, jnp.clip(i - 1, 0, n_chunk - 2), 0, 0)),
            pl.BlockSpec((tm, d), lambda i: (jnp.clip(i - n_chunk, 0, n_dec // tm - 1), 0)),
            pl.BlockSpec((N_META, d), lambda i: (0, 0)),
            pl.BlockSpec((None, 1, d), lambda i: (0, 0, 0)),
        ],
        out_specs=[pl.BlockSpec((tm, d), lambda i: (i, 0)),
                   pl.BlockSpec((tm, d), lambda i: (i, 0))],
        out_shape=[jax.ShapeDtypeStruct((t, d), F32), jax.ShapeDtypeStruct((t, d), BF16)],
        compiler_params=pltpu.CompilerParams(
            dimension_semantics=("arbitrary",), vmem_limit_bytes=VMEM_LIMIT),
        name="prep",
    )(xp, xs, meta, norm_w3)


def _outproj_residual(ya_ref, yb_ref, yc_ref, x_ref, w_ref):
    acc = jnp.dot(ya_ref[...], w_ref[0:W_A, :], preferred_element_type=F32)
    acc = acc + jnp.dot(yb_ref[...], w_ref[W_A:W_A + W_B, :], preferred_element_type=F32)
    acc = acc + jnp.dot(yc_ref[...], w_ref[W_A + W_B:MIX, :], preferred_element_type=F32)
    return x_ref[...] + acc


def _outproj_mid_kernel(ya_ref, yb_ref, yc_ref, x_ref, w_ref, nw_ref, x_out, h_out):
    xn = _outproj_residual(ya_ref, yb_ref, yc_ref, x_ref, w_ref)
    x_out[...] = xn
    h_out[...] = _rmsnorm(xn, nw_ref[...]).astype(BF16)


def _outproj_final_kernel(ya_ref, yb_ref, yc_ref, x_ref, w_ref, nw_ref, y_out):
    xn = _outproj_residual(ya_ref, yb_ref, yc_ref, x_ref, w_ref)
    y_out[...] = _rmsnorm(xn, nw_ref[...]).reshape(y_out.shape)


def _outproj_specs(tm, row_blk0, layer, nw_spec):
    rows = lambda i: (row_blk0 + i, 0)
    return [
        pl.BlockSpec((tm, W_A), rows),
        pl.BlockSpec((tm, W_B), rows),
        pl.BlockSpec((tm, W_C), rows),
        pl.BlockSpec((tm, D_MODEL), rows),
        pl.BlockSpec((None, MIX, D_MODEL), lambda i: (layer, 0, 0)),
        nw_spec,
    ]


def _outproj_mid(ya, yb, yc, x, w_out, layer, norm_w3, tm):
    t = x.shape[0]
    blk = pl.BlockSpec((tm, D_MODEL), lambda i: (i, 0))
    nw_spec = pl.BlockSpec((None, 1, D_MODEL), lambda i: (layer + 1, 0, 0))
    return pl.pallas_call(
        _outproj_mid_kernel,
        grid=(t // tm,),
        in_specs=_outproj_specs(tm, 0, layer, nw_spec),
        out_specs=[blk, blk],
        out_shape=[jax.ShapeDtypeStruct((t, D_MODEL), F32), jax.ShapeDtypeStruct((t, D_MODEL), BF16)],
        compiler_params=pltpu.CompilerParams(
            dimension_semantics=("parallel",), vmem_limit_bytes=VMEM_LIMIT),
        name="outproj",
    )(ya, yb, yc, x, w_out, norm_w3)


def _outproj_final(ya, yb, yc, x, w_out, layer, final_nw, tm, row_blk0, n_blk, out_shape, out_spec):
    nw_spec = pl.BlockSpec((1, D_MODEL), lambda i: (0, 0))
    return pl.pallas_call(
        _outproj_final_kernel,
        grid=(n_blk,),
        in_specs=_outproj_specs(tm, row_blk0, layer, nw_spec),
        out_specs=out_spec,
        out_shape=jax.ShapeDtypeStruct(out_shape, F32),
        compiler_params=pltpu.CompilerParams(
            dimension_semantics=("parallel",), vmem_limit_bytes=VMEM_LIMIT),
        name="outproj_final",
    )(ya, yb, yc, x, w_out, final_nw)


class _RowView:
    def __init__(self, ref, r0, n):
        self.ref, self.r0, self.n = ref, r0, n
        self.shape = (n, ref.shape[1])

    def _idx(self, idx):
        if idx is Ellipsis:
            return slice(self.r0, self.r0 + self.n), slice(None)
        rows, cols = idx
        start = self.r0 + (rows.start or 0)
        stop = self.r0 + (self.n if rows.stop is None else rows.stop)
        return slice(start, stop), cols

    def __getitem__(self, idx):
        return self.ref[self._idx(idx)]

    def __setitem__(self, idx, val):
        self.ref[self._idx(idx)] = val


def _tile_views(refs, u):
    return [_RowView(ref, u * TILE_R, TILE_R) for ref in refs]


def _interleave(tiles):
    tiles = list(tiles)
    while tiles:
        alive = []
        for t in tiles:
            try:
                next(t)
                alive.append(t)
            except StopIteration:
                pass
        tiles = alive


def _mlstm_tile(q_ref, k_ref, v_ref, o_ref, z_ref, g_ref, gb_ref, onw_ref, y_ref,
                m_exp, valid, get_c, get_n, put_c, put_n, put_m, nseq, c):
    r = nseq * c
    row, col, same = _seq_masks(nseq, c)
    incl = same & (col <= row)
    g = g_ref[...] + gb_ref[...]
    li = g
    lf = _log_sigmoid(g)
    if valid is not None:
        li = jnp.where(valid, li, NEG_BIG)
        lf = jnp.where(valid, lf, 0.0)
    fcum = _dot_exact(incl.astype(F32), lf)
    if nseq == 1:
        flast = jnp.broadcast_to(fcum[r - 1:r, :], (r, LANES))
    else:
        flast = _dot_exact(same.astype(F32), lf)
    yield
    fcum_t = fcum.T
    li_t = li.T
    lane = lax.broadcasted_iota(jnp.int32, (r, LANES), 1)
    heads = range(H_A)
    hsl = [slice(h * HEAD_DIM, (h + 1) * HEAD_DIM) for h in heads]

    dm, mt, a, w, a_s = [], [], [], [], []
    m_new_all = jnp.zeros((r, LANES), F32)
    for h in heads:
        fc = fcum[:, G_AF + h:G_AF + h + 1]
        fr = fcum_t[G_AF + h:G_AF + h + 1, :]
        lic = li[:, G_AI + h:G_AI + h + 1]
        lir = li_t[G_AI + h:G_AI + h + 1, :]
        fl = flast[:, G_AF + h:G_AF + h + 1]
        mcol = m_exp[:, h:h + 1]
        dm_h = jnp.where(incl, fc - fr + lir, NEG_BIG)
        inter = fc + mcol
        mt_h = jnp.maximum(inter, jnp.max(dm_h, axis=1, keepdims=True))
        if nseq == 1:
            m_new = jnp.broadcast_to(mt_h[r - 1:r, :], (r, 1))
        else:
            dl = jnp.where(same, fl - fr + lir, NEG_BIG)
            m_new = jnp.maximum(fl + mcol, jnp.max(dl, axis=1, keepdims=True))
        dm.append(dm_h)
        mt.append(mt_h)
        a.append(jnp.exp(inter - mt_h))
        w.append(jnp.exp(fl - fc + lic - m_new))
        a_s.append(jnp.exp(fl + mcol - m_new))
        m_new_all = jnp.where(lane == h, m_new, m_new_all)
    put_m(m_new_all)
    yield

    s = [_dot_nt(q_ref[:, hs], k_ref[:, hs] * QK_SCALE) for hs in hsl]
    yield
    p = [jnp.exp(dm[h] - mt[h]) * s[h] for h in heads]
    pv = [_dot(p[h], v_ref[:, hsl[h]]) for h in heads]
    yield
    qc = [_cat_rows([_dot(_rows(q_ref[:, hsl[h]], i, c), get_c(i, h)) for i in range(nseq)])
          for h in heads]
    yield
    for h in heads:
        hs = hsl[h]
        qh = q_ref[:, hs]
        psum = jnp.sum(p[h], axis=1, keepdims=True)
        qn = _cat_rows([jnp.sum(_rows(qh, i, c) * get_n(i, h), axis=1, keepdims=True)
                        for i in range(nseq)])
        num = a[h] * qc[h] + pv[h]
        den = a[h] * qn + psum
        hh = num / jnp.maximum(jnp.abs(den), jnp.exp(jnp.minimum(-mt[h], EXP_CLIP)))
        hh = hh * _sigmoid(o_ref[:, hs])
        y_ref[:, hs] = _head_norm_gate(hh, onw_ref[:, hs], z_ref[:, hs]).astype(BF16)
    yield

    rowv = lax.broadcasted_iota(jnp.int32, (r, 1), 0)
    for h in heads:
        hs = hsl[h]
        kh = k_ref[:, hs] * QK_SCALE
        wv = w[h] * v_ref[:, hs]
        wk = w[h] * kh
        kh_t = kh.T.astype(BF16)
        for i in range(nseq):
            a_i = a_s[h][i * c:i * c + 1, :]
            if nseq == 1:
                upd = jnp.dot(kh_t, wv.astype(BF16), preferred_element_type=F32)
            else:
                seq_rows = (rowv >> _log2(c)) == i
                upd = jnp.dot(kh_t, jnp.where(seq_rows, wv, 0.0).astype(BF16),
                              preferred_element_type=F32)
            put_c(i, h, a_i * get_c(i, h) + upd)
            put_n(i, h, a_i * get_n(i, h) + jnp.sum(_rows(wk, i, c), axis=0, keepdims=True))


def _mlstm_prompt_kernel(q_ref, k_ref, v_ref, o_ref, z_ref, g_ref, gb_ref, onw_ref,
                         y_ref, c_out, n_out, m_out, c_scr, n_scr, m_scr):
    j = pl.program_id(0)
    bsz = c_scr.shape[0]

    @pl.when(j == 0)
    def _():
        c_scr[...] = jnp.zeros_like(c_scr)
        n_scr[...] = jnp.zeros_like(n_scr)
        m_scr[...] = jnp.zeros_like(m_scr)

    tok = lax.broadcasted_iota(jnp.int32, (TILE_R, LANES), 0) + j * TILE_R
    valid = tok >= TILE_R - N_META

    def tile(u):
        q, k, v, o, z, g, y = _tile_views((q_ref, k_ref, v_ref, o_ref, z_ref, g_ref, y_ref), u)
        m_exp = jnp.broadcast_to(m_scr[u, 0:1, :], (TILE_R, LANES))

        def put_c(i, h, val):
            c_scr[u, h] = val

        def put_n(i, h, val):
            n_scr[u, h:h + 1, :] = val

        def put_m(val):
            m_scr[u] = val[0:8, :]

        return _mlstm_tile(q, k, v, o, z, g, gb_ref, onw_ref, y, m_exp, valid,
                           lambda i, h: c_scr[u, h], lambda i, h: n_scr[u, h:h + 1, :],
                           put_c, put_n, put_m, 1, TILE_R)

    _interleave(tile(u) for u in range(bsz))

    @pl.when(j == pl.num_programs(0) - 1)
    def _():
        c_out[...] = c_scr[...]
        n_out[...] = n_scr[:, 0:H_A, :]
        m_out[...] = m_scr[:, 0:1, :]


def _mlstm_sample_kernel(q_ref, k_ref, v_ref, o_ref, z_ref, g_ref, gb_ref, onw_ref,
                         c_in, n_in, m_in, y_ref, c_out, n_out, m_out, *, c):
    nseq = TILE_R // c

    def put_c(i, h, val):
        c_out[i, h] = val

    def put_n(i, h, val):
        n_out[i, h:h + 1, :] = val

    def put_m(val):
        m_out[...] = val

    _interleave([_mlstm_tile(
        q_ref, k_ref, v_ref, o_ref, z_ref, g_ref, gb_ref, onw_ref, y_ref, m_in[...], None,
        lambda i, h: c_in[i, h], lambda i, h: n_in[i, h:h + 1, :], put_c, put_n, put_m, nseq, c)])


def _col_spec(width, off, rows, n_rows=TILE_R):
    assert off % width == 0
    blk = off // width
    return pl.BlockSpec((n_rows, width), lambda *ids: (rows(*ids), blk))


def _param_spec(width, off, layer):
    assert off % width == 0
    blk = off // width
    return pl.BlockSpec((None, 1, width), lambda *ids: (layer, 0, blk))


def _whole(*shape):
    return pl.BlockSpec(shape, lambda j: (0,) * len(shape))


def _mlstm_prompt(proj, gate_bias, onw3, layer, bsz, n_chunk):
    t = proj.shape[0]
    tm = bsz * TILE_R
    rows = lambda j: j
    in_specs = [_col_spec(W_A, off, rows, tm) for off in (OFF_AQ, OFF_AK, OFF_AV, OFF_AO, OFF_ZA)]
    in_specs += [_col_spec(LANES, OFF_G, rows, tm), _param_spec(LANES, 0, layer),
                 _param_spec(W_A, 0, layer)]
    return pl.pallas_call(
        _mlstm_prompt_kernel,
        grid=(n_chunk,),
        in_specs=in_specs,
        out_specs=[
            pl.BlockSpec((tm, W_A), lambda j: (j, 0)),
            _whole(bsz, H_A, HEAD_DIM, HEAD_DIM),
            _whole(bsz, H_A, HEAD_DIM),
            _whole(bsz, 1, LANES),
        ],
        out_shape=[
            jax.ShapeDtypeStruct((t, W_A), BF16),
            jax.ShapeDtypeStruct((bsz, H_A, HEAD_DIM, HEAD_DIM), F32),
            jax.ShapeDtypeStruct((bsz, H_A, HEAD_DIM), F32),
            jax.ShapeDtypeStruct((bsz, 1, LANES), F32),
        ],
        scratch_shapes=[pltpu.VMEM((bsz, H_A, HEAD_DIM, HEAD_DIM), F32),
                        pltpu.VMEM((bsz, 8, HEAD_DIM), F32), pltpu.VMEM((bsz, 8, LANES), F32)],
        compiler_params=pltpu.CompilerParams(
            dimension_semantics=("arbitrary",), vmem_limit_bytes=VMEM_LIMIT),
        name="mlstm_prompt",
    )(proj, proj, proj, proj, proj, proj, gate_bias, onw3)


def _in_place(operands, n_blocked, targets):
    specs, aliases = [], {}
    operands = list(operands)
    for out_idx, arr in targets.items():
        if arr is not None:
            aliases[len(operands)] = out_idx
            operands.append(arr)
            specs.append(pl.BlockSpec(memory_space=pl.ANY))
    n_extra = len(specs)

    def strip(kernel):
        def body(*refs):
            kernel(*refs[:n_blocked], *refs[n_blocked + n_extra:])
        return body

    return operands, specs, aliases, strip


def _mlstm_sample(proj, y_prev, c_stack, gate_bias, onw3, c_state, n_state, m_rows, layer,
                  dec_b, c, rows):
    nseq = TILE_R // c
    depth = c_state.shape[0]
    in_specs = [_col_spec(W_A, off, rows) for off in (OFF_AQ, OFF_AK, OFF_AV, OFF_AO, OFF_ZA)]
    in_specs += [_col_spec(LANES, OFF_G, rows), _param_spec(LANES, 0, layer),
                 _param_spec(W_A, 0, layer)]
    in_specs += [
        pl.BlockSpec((None, nseq, H_A, HEAD_DIM, HEAD_DIM), lambda i: (layer, i, 0, 0, 0)),
        pl.BlockSpec((None, nseq, H_A, HEAD_DIM), lambda i: (layer, i, 0, 0)),
        pl.BlockSpec((None, TILE_R, LANES), lambda i: (layer, i, 0)),
    ]
    operands = (proj, proj, proj, proj, proj, proj, gate_bias, onw3, c_state, n_state, m_rows)
    operands, extra, aliases, strip = _in_place(operands, len(in_specs), {0: y_prev, 1: c_stack})
    return pl.pallas_call(
        strip(functools.partial(_mlstm_sample_kernel, c=c)),
        grid=(dec_b // nseq,),
        in_specs=in_specs + extra,
        out_specs=[
            pl.BlockSpec((TILE_R, W_A), lambda i: (rows(i), 0)),
            pl.BlockSpec((None, nseq, H_A, HEAD_DIM, HEAD_DIM), lambda i: (layer, i, 0, 0, 0)),
            pl.BlockSpec((nseq, H_A, HEAD_DIM), lambda i: (i, 0, 0)),
            pl.BlockSpec((TILE_R, LANES), lambda i: (i, 0)),
        ],
        out_shape=[
            jax.ShapeDtypeStruct(y_prev.shape, BF16),
            jax.ShapeDtypeStruct((depth, dec_b, H_A, HEAD_DIM, HEAD_DIM), F32),
            jax.ShapeDtypeStruct((dec_b, H_A, HEAD_DIM), F32),
            jax.ShapeDtypeStruct((dec_b * c, LANES), F32),
        ],
        input_output_aliases=aliases,
        compiler_params=pltpu.CompilerParams(
            dimension_semantics=("parallel",), vmem_limit_bytes=VMEM_LIMIT),
        name="mlstm_sample",
    )(*operands)


def _l2norm(x):
    return x * lax.rsqrt(jnp.sum(x * x, axis=1, keepdims=True) + EPS)


INV_LEAF = 16


def _unit_lower_inverse(a_list, row, col, c):
    leaf = min(INV_LEAF, c)
    kl = _log2(leaf)
    leaf_mask = (row >> kl) == (col >> kl)
    eye = jnp.where(row == col, 1.0, 0.0)
    p = [jnp.where(leaf_mask, -a, 0.0) for a in a_list]
    t = [eye + n for n in p]
    for _ in range(kl - 1):
        p = [_dot(x, x) for x in p]
        yield
        t = [ti + _dot(pi, ti) for pi, ti in zip(p, t)]
        yield
    w = leaf
    while w < c:
        kw = _log2(w)
        lower_left = (((row >> (kw + 1)) == (col >> (kw + 1)))
                      & (((row >> kw) & 1) == 1) & (((col >> kw) & 1) == 0))
        left = [_dot(ti, jnp.where(lower_left, a, 0.0)) for ti, a in zip(t, a_list)]
        yield
        t = [ti - _dot(li, ti) for li, ti in zip(left, t)]
        yield
        w *= 2
    return t


def _gdn_tile(conv_win, cw_ref, z_ref, g_ref, gp_ref, onw_ref, y_ref, get_s, put_s, nseq, c):
    r = nseq * c
    row, col, same = _seq_masks(nseq, c)
    incl = same & (col <= row)
    strict = same & (col < row)
    gp = gp_ref[...]
    graw = g_ref[...]
    gdec = -jnp.exp(gp[1:2, :]) * _softplus(graw + gp[0:1, :])
    beta = _sigmoid(graw)
    gcum = _dot_exact(incl.astype(F32), gdec)
    if nseq == 1:
        glast = jnp.broadcast_to(gcum[r - 1:r, :], (r, LANES))
    else:
        glast = _dot_exact(same.astype(F32), gdec)
    yield
    gcum_t = gcum.T

    heads = range(H_B)
    hsl = [slice(h * HEAD_DIM, (h + 1) * HEAD_DIM) for h in heads]
    rowv = lax.broadcasted_iota(jnp.int32, (r, 1), 0)

    def conv(lo):
        acc = conv_win(0, slice(lo, lo + HEAD_DIM)) * cw_ref[0:1, lo:lo + HEAD_DIM]
        for jj in range(1, CONV_W):
            acc = acc + conv_win(jj, slice(lo, lo + HEAD_DIM)) * cw_ref[jj:jj + 1, lo:lo + HEAD_DIM]
        return _silu(acc)

    q = [_l2norm(conv(hs.start)) * QK_SCALE for hs in hsl]
    k = [_l2norm(conv(W_B + hs.start)) for hs in hsl]
    v = [conv(2 * W_B + hs.start) for hs in hsl]
    yield
    gc = [gcum[:, G_BA + h:G_BA + h + 1] for h in heads]
    gl = [glast[:, G_BA + h:G_BA + h + 1] for h in heads]
    bc = [beta[:, G_BB + h:G_BB + h + 1] for h in heads]
    decay = [jnp.exp(jnp.where(incl, gc[h] - gcum_t[G_BA + h:G_BA + h + 1, :], NEG_BIG))
             for h in heads]
    kk = [_dot_nt(k[h], k[h]) for h in heads]
    qk = [_dot_nt(q[h], k[h]) for h in heads]
    yield
    amat = [jnp.where(strict, bc[h] * decay[h] * kk[h], 0.0) for h in heads]
    tinv = yield from _unit_lower_inverse(amat, row, col, c)
    sol = [_dot(tinv[h], jnp.concatenate([bc[h] * v[h], (bc[h] * jnp.exp(gc[h])) * k[h]], axis=1))
           for h in heads]
    yield
    wks = [_cat_rows([_dot(_rows(sol[h][:, HEAD_DIM:], i, c), get_s(i, h)) for i in range(nseq)])
           for h in heads]
    qs = [_cat_rows([_dot(_rows(q[h], i, c), get_s(i, h)) for i in range(nseq)]) for h in heads]
    yield
    wmat = [sol[h][:, :HEAD_DIM] - wks[h] for h in heads]
    aw = [_dot(decay[h] * qk[h], wmat[h]) for h in heads]
    yield
    for h in heads:
        hs = hsl[h]
        oh = jnp.exp(gc[h]) * qs[h] + aw[h]
        y_ref[:, hs] = _head_norm_gate(oh, onw_ref[:, hs], z_ref[:, hs]).astype(BF16)
    yield
    for h in heads:
        kd_t = (k[h] * jnp.exp(gl[h] - gc[h])).T.astype(BF16)
        e_last = jnp.exp(gl[h])
        for i in range(nseq):
            if nseq == 1:
                upd = jnp.dot(kd_t, wmat[h].astype(BF16), preferred_element_type=F32)
            else:
                seq_rows = (rowv >> _log2(c)) == i
                upd = jnp.dot(kd_t, jnp.where(seq_rows, wmat[h], 0.0).astype(BF16),
                              preferred_element_type=F32)
            put_s(i, h, e_last[i * c:i * c + 1, :] * get_s(i, h) + upd)


def _gdn_prompt_kernel(q_ref, k_ref, v_ref, z_ref, g_ref, gp_ref, cw_ref, onw_ref,
                       y_ref, s_out, conv_out, s_scr, ext_scr):
    j = pl.program_id(0)
    bsz = s_scr.shape[0]

    @pl.when(j == 0)
    def _():
        s_scr[...] = jnp.zeros_like(s_scr)
        ext_scr[:, 0:CONV_PAD, :] = jnp.zeros((bsz, CONV_PAD, 3 * W_B), F32)

    def tile(u):
        q, k, v, z, g, y = _tile_views((q_ref, k_ref, v_ref, z_ref, g_ref, y_ref), u)
        ext_scr[u, CONV_PAD:, 0:W_B] = q[...]
        ext_scr[u, CONV_PAD:, W_B:2 * W_B] = k[...]
        ext_scr[u, CONV_PAD:, 2 * W_B:] = v[...]

        def conv_win(jj, cols):
            start = CONV_PAD - (CONV_W - 1) + jj
            return ext_scr[u, start:start + TILE_R, cols]

        def put_s(i, h, val):
            s_scr[u, h] = val

        return _gdn_tile(conv_win, cw_ref, z, g, gp_ref, onw_ref, y,
                         lambda i, h: s_scr[u, h], put_s, 1, TILE_R)

    _interleave(tile(u) for u in range(bsz))

    @pl.when(j == pl.num_programs(0) - 1)
    def _():
        s_out[...] = s_scr[...]
        conv_out[...] = ext_scr[:, TILE_R + CONV_PAD - (CONV_W - 1):TILE_R + CONV_PAD, :]

    ext_scr[:, 0:CONV_PAD, :] = ext_scr[:, TILE_R:TILE_R + CONV_PAD, :]


def _gdn_sample_kernel(q_ref, k_ref, v_ref, z_ref, g_ref, gp_ref, cw_ref, onw_ref,
                       s_in, conv_in, y_ref, s_out, conv_out, ext_scr, *, c):
    nseq = TILE_R // c
    hist = CONV_W - 1
    ext_scr[:, CONV_PAD - hist:CONV_PAD, :] = conv_in[...]
    ext_scr[:, CONV_PAD:, 0:W_B] = q_ref[...].reshape(nseq, c, W_B)
    ext_scr[:, CONV_PAD:, W_B:2 * W_B] = k_ref[...].reshape(nseq, c, W_B)
    ext_scr[:, CONV_PAD:, 2 * W_B:] = v_ref[...].reshape(nseq, c, W_B)

    def conv_win(jj, cols):
        start = CONV_PAD - hist + jj
        return ext_scr[:, start:start + c, cols].reshape(TILE_R, cols.stop - cols.start)

    def put_s(i, h, val):
        s_out[i, h] = val

    _interleave([_gdn_tile(conv_win, cw_ref, z_ref, g_ref, gp_ref, onw_ref, y_ref,
                           lambda i, h: s_in[i, h], put_s, nseq, c)])
    conv_out[...] = ext_scr[:, CONV_PAD + c - hist:CONV_PAD + c, :]


def _gdn_prompt(proj, gdn_par, conv_w, onw3, layer, bsz, n_chunk):
    t = proj.shape[0]
    tm = bsz * TILE_R
    rows = lambda j: j
    in_specs = [_col_spec(W_B, off, rows, tm) for off in (OFF_BQ, OFF_BK, OFF_BV, OFF_ZB)]
    in_specs += [
        _col_spec(LANES, OFF_G, rows, tm),
        pl.BlockSpec((None, 8, LANES), lambda j: (layer, 0, 0)),
        pl.BlockSpec((None, CONV_W, 3 * W_B), lambda j: (layer, 0, 0)),
        _param_spec(W_B, W_A, layer),
    ]
    return pl.pallas_call(
        _gdn_prompt_kernel,
        grid=(n_chunk,),
        in_specs=in_specs,
        out_specs=[
            pl.BlockSpec((tm, W_B), lambda j: (j, 0)),
            _whole(bsz, H_B, HEAD_DIM, HEAD_DIM),
            _whole(bsz, CONV_W - 1, 3 * W_B),
        ],
        out_shape=[
            jax.ShapeDtypeStruct((t, W_B), BF16),
            jax.ShapeDtypeStruct((bsz, H_B, HEAD_DIM, HEAD_DIM), F32),
            jax.ShapeDtypeStruct((bsz, CONV_W - 1, 3 * W_B), F32),
        ],
        scratch_shapes=[pltpu.VMEM((bsz, H_B, HEAD_DIM, HEAD_DIM), F32),
                        pltpu.VMEM((bsz, TILE_R + CONV_PAD, 3 * W_B), F32)],
        compiler_params=pltpu.CompilerParams(
            dimension_semantics=("arbitrary",), vmem_limit_bytes=VMEM_LIMIT),
        name="gdn_prompt",
    )(proj, proj, proj, proj, proj, gdn_par, conv_w, onw3)


def _gdn_sample(proj, y_prev, s_stack, gdn_par, conv_w, onw3, s_state, conv_state, layer,
                dec_b, c, rows):
    nseq = TILE_R // c
    depth = s_state.shape[0]
    in_specs = [_col_spec(W_B, off, rows) for off in (OFF_BQ, OFF_BK, OFF_BV, OFF_ZB)]
    in_specs += [
        _col_spec(LANES, OFF_G, rows),
        pl.BlockSpec((None, 8, LANES), lambda i: (layer, 0, 0)),
        pl.BlockSpec((None, CONV_W, 3 * W_B), lambda i: (layer, 0, 0)),
        _param_spec(W_B, W_A, layer),
        pl.BlockSpec((None, nseq, H_B, HEAD_DIM, HEAD_DIM), lambda i: (layer, i, 0, 0, 0)),
        pl.BlockSpec((None, nseq, CONV_W - 1, 3 * W_B), lambda i: (layer, i, 0, 0)),
    ]
    operands = (proj, proj, proj, proj, proj, gdn_par, conv_w, onw3, s_state, conv_state)
    operands, extra, aliases, strip = _in_place(operands, len(in_specs), {0: y_prev, 1: s_stack})
    return pl.pallas_call(
        strip(functools.partial(_gdn_sample_kernel, c=c)),
        grid=(dec_b // nseq,),
        in_specs=in_specs + extra,
        out_specs=[
            pl.BlockSpec((TILE_R, W_B), lambda i: (rows(i), 0)),
            pl.BlockSpec((None, nseq, H_B, HEAD_DIM, HEAD_DIM), lambda i: (layer, i, 0, 0, 0)),
            pl.BlockSpec((nseq, CONV_W - 1, 3 * W_B), lambda i: (i, 0, 0)),
        ],
        out_shape=[
            jax.ShapeDtypeStruct(y_prev.shape, BF16),
            jax.ShapeDtypeStruct((depth, dec_b, H_B, HEAD_DIM, HEAD_DIM), F32),
            jax.ShapeDtypeStruct((dec_b, CONV_W - 1, 3 * W_B), F32),
        ],
        scratch_shapes=[pltpu.VMEM((nseq, CONV_PAD + c, 3 * W_B), F32)],
        input_output_aliases=aliases,
        compiler_params=pltpu.CompilerParams(
            dimension_semantics=("parallel",), vmem_limit_bytes=VMEM_LIMIT),
        name="gdn_sample",
    )(*operands)


def _hgrn_tile(q_ref, f_ref, i_ref, z_ref, lbp_ref, onw_ref, y_ref, get_s, put_s,
               nseq, c, layer):
    r = nseq * c
    row, col, same = _seq_masks(nseq, c)
    incl = same & (col <= row)
    rowv = lax.broadcasted_iota(jnp.int32, (r, 1), 0)
    lbp = lbp_ref[...]
    e = jnp.exp(lbp - jnp.max(lbp, axis=0, keepdims=True))
    sm = e / jnp.sum(e, axis=0, keepdims=True)
    lb = jnp.zeros((1, W_C), F32)
    for l in range(1, layer + 1):
        lb = lb + sm[l:l + 1, :]

    cf = f_ref[...]
    logf = _log_sigmoid(cf) + jnp.log1p(lb * jnp.exp(jnp.minimum(-cf, EXP_CLIP)))
    kall = (1.0 - lb) * _sigmoid(-cf)
    bcum = _dot_exact(incl.astype(F32), logf)
    if nseq == 1:
        blast = jnp.broadcast_to(bcum[r - 1:r, :], (r, W_C))
    else:
        blast = _dot_exact(same.astype(F32), logf)

    levels = []
    w = c // 2
    while w >= 1:
        levels.append(w)
        w //= 2
    rowf = lax.broadcasted_iota(jnp.int32, (r, W_C), 0)
    refs = []
    for w in levels:
        if 2 * w >= 8:
            refs.append(_cat_rows([
                jnp.broadcast_to(bcum[g * 2 * w + w - 1:g * 2 * w + w, :], (2 * w, W_C))
                for g in range(r // (2 * w))]))
        else:
            pos = rowf & (2 * w - 1)
            ref = bcum
            for off in range(2 * w):
                if off != w - 1:
                    ref = jnp.where(pos == off, pltpu.roll(bcum, (off - (w - 1)) % r, axis=0), ref)
            refs.append(ref)

    heads = range(H_C)
    hsl = [slice(h * HEAD_DIM, (h + 1) * HEAD_DIM) for h in heads]
    q = [_silu(q_ref[:, hs]) for hs in hsl]
    k = [kall[:, hs] for hs in hsl]
    b = [bcum[:, hs] for hs in hsl]
    diag = row == col
    yield
    amat = [jnp.where(diag, _dot_nt(q[h], k[h]), 0.0) for h in heads]
    for w, ref_all in zip(levels, refs):
        yield
        lw = _log2(w)
        tgt = ((rowv >> lw) & 1) == 1
        pair = (row >> (lw + 1)) == (col >> (lw + 1))
        qp = [q[h] * jnp.exp(jnp.where(tgt, b[h] - ref_all[:, hsl[h]], NEG_BIG)) for h in heads]
        kp = [k[h] * jnp.exp(jnp.where(tgt, NEG_BIG, ref_all[:, hsl[h]] - b[h])) for h in heads]
        amat = [amat[h] + jnp.where(pair, _dot_nt(qp[h], kp[h]), 0.0) for h in heads]
    yield
    qs = [_cat_rows([_dot(_rows(q[h] * jnp.exp(b[h]), i, c), get_s(i, h)) for i in range(nseq)])
          for h in heads]
    av = [_dot(amat[h], i_ref[:, hsl[h]]) for h in heads]
    yield
    for h in heads:
        hs = hsl[h]
        y_ref[:, hs] = _head_norm_gate(qs[h] + av[h], onw_ref[:, hs], z_ref[:, hs]).astype(BF16)
    yield
    for h in heads:
        hs = hsl[h]
        vh = i_ref[:, hs]
        bl = blast[:, hs]
        kd_t = (k[h] * jnp.exp(bl - b[h])).T.astype(BF16)
        eb_t = jnp.exp(bl).T
        for i in range(nseq):
            if nseq == 1:
                upd = jnp.dot(kd_t, vh.astype(BF16), preferred_element_type=F32)
            else:
                seq_rows = (rowv >> _log2(c)) == i
                upd = jnp.dot(kd_t, jnp.where(seq_rows, vh, 0.0).astype(BF16),
                              preferred_element_type=F32)
            put_s(i, h, eb_t[:, i * c:i * c + 1] * get_s(i, h) + upd)


class _ColView:
    def __init__(self, parts):
        self.parts = []
        pos = 0
        for ref, lo, hi in parts:
            self.parts.append((pos, ref, lo, hi - lo))
            pos += hi - lo
        self.width = pos

    def __getitem__(self, idx):
        if idx is Ellipsis:
            rows, cols = slice(None), slice(0, self.width)
        else:
            rows, cols = idx
            cols = slice(cols.start or 0, self.width if cols.stop is None else cols.stop)
        out = []
        for pos, ref, lo, n in self.parts:
            a, b = max(cols.start, pos), min(cols.stop, pos + n)
            if a < b:
                out.append(ref[rows, lo + a - pos:lo + b - pos])
        return out[0] if len(out) == 1 else jnp.concatenate(out, axis=1)


def _hgrn_views(ca_ref, cb_ref, zc_ref):
    two = 2 * W_C - ca_ref.shape[1]
    q = _ColView([(ca_ref, 0, W_C)])
    f = _ColView([(ca_ref, W_C, ca_ref.shape[1]), (cb_ref, 0, two)])
    i = _ColView([(cb_ref, two, two + W_C)])
    z = _ColView([(zc_ref, 0, W_C)])
    return q, f, i, z


def _hgrn_prompt_kernel(ca_ref, cb_ref, zc_ref, lbp_ref, onw_ref, y_ref, s_out, s_scr,
                        *, layer):
    j = pl.program_id(0)
    bsz = s_scr.shape[0]

    @pl.when(j == 0)
    def _():
        s_scr[...] = jnp.zeros_like(s_scr)

    def tile(u):
        ca, cb, zc, y = _tile_views((ca_ref, cb_ref, zc_ref, y_ref), u)

        def put_s(i, h, val):
            s_scr[u, h] = val

        return _hgrn_tile(*_hgrn_views(ca, cb, zc), lbp_ref, onw_ref, y,
                          lambda i, h: s_scr[u, h], put_s, 1, TILE_R, layer)

    _interleave(tile(u) for u in range(bsz))

    @pl.when(j == pl.num_programs(0) - 1)
    def _():
        s_out[...] = s_scr[...]


def _hgrn_sample_kernel(ca_ref, cb_ref, zc_ref, lbp_ref, onw_ref, s_in, y_ref, s_out,
                        *, c, layer):
    def put_s(i, h, val):
        s_out[i, h] = val

    _interleave([_hgrn_tile(*_hgrn_views(ca_ref, cb_ref, zc_ref), lbp_ref, onw_ref, y_ref,
                            lambda i, h: s_in[i, h], put_s, TILE_R // c, c, layer)])


HGRN_BLK = W_A
assert OFF_C % HGRN_BLK == 0 and 3 * W_C == 2 * HGRN_BLK and OFF_ZC % HGRN_BLK == 0


def _hgrn_prompt(proj, lb_par, onw3, layer, bsz, n_chunk):
    t = proj.shape[0]
    tm = bsz * TILE_R
    depth = lb_par.shape[0]
    rows = lambda j: j
    in_specs = [_col_spec(HGRN_BLK, off, rows, tm) for off in (OFF_C, OFF_C + HGRN_BLK, OFF_ZC)]
    in_specs += [pl.BlockSpec((depth, W_C), lambda j: (0, 0)),
                 _param_spec(W_C, W_A + W_B, layer)]
    return pl.pallas_call(
        functools.partial(_hgrn_prompt_kernel, layer=layer),
        grid=(n_chunk,),
        in_specs=in_specs,
        out_specs=[
            pl.BlockSpec((tm, W_C), lambda j: (j, 0)),
            _whole(bsz, H_C, HEAD_DIM, HEAD_DIM),
        ],
        out_shape=[
            jax.ShapeDtypeStruct((t, W_C), BF16),
            jax.ShapeDtypeStruct((bsz, H_C, HEAD_DIM, HEAD_DIM), F32),
        ],
        scratch_shapes=[pltpu.VMEM((bsz, H_C, HEAD_DIM, HEAD_DIM), F32)],
        compiler_params=pltpu.CompilerParams(
            dimension_semantics=("arbitrary",), vmem_limit_bytes=VMEM_LIMIT),
        name="hgrn_prompt",
    )(proj, proj, proj, lb_par, onw3)


def _hgrn_sample(proj, y_prev, s_stack, lb_par, onw3, s_state, layer, dec_b, c, rows):
    nseq = TILE_R // c
    depth = lb_par.shape[0]
    in_specs = [_col_spec(HGRN_BLK, off, rows) for off in (OFF_C, OFF_C + HGRN_BLK, OFF_ZC)]
    in_specs += [
        pl.BlockSpec((depth, W_C), lambda i: (0, 0)),
        _param_spec(W_C, W_A + W_B, layer),
        pl.BlockSpec((None, nseq, H_C, HEAD_DIM, HEAD_DIM), lambda i: (layer, i, 0, 0, 0)),
    ]
    operands = (proj, proj, proj, lb_par, onw3, s_state)
    operands, extra, aliases, strip = _in_place(operands, len(in_specs), {0: y_prev, 1: s_stack})
    return pl.pallas_call(
        strip(functools.partial(_hgrn_sample_kernel, c=c, layer=layer)),
        grid=(dec_b // nseq,),
        in_specs=in_specs + extra,
        out_specs=[
            pl.BlockSpec((TILE_R, W_C), lambda i: (rows(i), 0)),
            pl.BlockSpec((None, nseq, H_C, HEAD_DIM, HEAD_DIM), lambda i: (layer, i, 0, 0, 0)),
        ],
        out_shape=[
            jax.ShapeDtypeStruct(y_prev.shape, BF16),
            jax.ShapeDtypeStruct((depth, dec_b, H_C, HEAD_DIM, HEAD_DIM), F32),
        ],
        input_output_aliases=aliases,
        compiler_params=pltpu.CompilerParams(
            dimension_semantics=("parallel",), vmem_limit_bytes=VMEM_LIMIT),
        name="hgrn_sample",
    )(*operands)


def _gate_weight(w_t):
    ai0 = 4 * W_A
    ba0 = ai0 + 2 * H_A + 3 * W_B
    n_gate = 2 * H_A + 2 * H_B
    pad = jnp.zeros((w_t.shape[0], LANES - n_gate, w_t.shape[2]), w_t.dtype)
    return jnp.concatenate([w_t[:, ai0:ai0 + 2 * H_A], w_t[:, ba0:ba0 + 2 * H_B], pad], axis=1)


def _lane_row(depth, pieces):
    row = jnp.zeros((depth, LANES), F32)
    for off, val in pieces:
        row = lax.dynamic_update_slice(row, val.astype(F32), (0, off))
    return row[:, None, :]


def kernel(x_prompt, x_sample, state_mlstm_C, state_mlstm_n, state_mlstm_m, state_gdn_S,
           state_gdn_conv, state_hgrn_S, meta_tokens, norm_w, w_in, mlstm_gate_b, gdn_A_log,
           gdn_dt_bias, gdn_conv_w, hgrn_lower_bounds, out_norm_w, w_out, final_norm_w):
    bsz, seq_len, d = x_prompt.shape
    dec_b, dec_seq, _ = x_sample.shape
    depth = w_in.shape[0]
    assert d == D_MODEL and seq_len % TILE_R == 0 and TILE_R % dec_seq == 0
    assert w_in.shape[2] == W_SEGMENTS[-1][1] + W_SEGMENTS[-1][2]
    nseq = TILE_R // dec_seq
    assert dec_b % nseq == 0

    chunks = seq_len // TILE_R
    n_chunk = chunks + 1
    tm = bsz * TILE_R
    n_dec = dec_b * dec_seq
    dec_blk0 = n_chunk * bsz

    def dec_rows(i):
        return dec_blk0 + i

    w_t = jnp.swapaxes(w_in.astype(F32), 1, 2)
    w_gate = _gate_weight(w_t)
    w_out_bf = w_out.astype(BF16)
    norm_w3 = norm_w.astype(F32)[:, None, :]
    onw3 = out_norm_w.astype(F32)[:, None, :]
    final_nw = final_norm_w.astype(F32)[None, :]
    gate_bias = _lane_row(depth, [(G_AI, mlstm_gate_b[:, 0]), (G_AF, mlstm_gate_b[:, 1])])
    gdn_par = jnp.concatenate([
        _lane_row(depth, [(G_BA, gdn_dt_bias)]), _lane_row(depth, [(G_BA, gdn_A_log)]),
        jnp.zeros((depth, 6, LANES), F32)], axis=1)
    conv_w = gdn_conv_w.astype(F32)
    lb_par = hgrn_lower_bounds.astype(F32)
    m_rows = jnp.repeat(
        jnp.pad(state_mlstm_m.astype(F32), ((0, 0), (0, 0), (0, LANES - H_A))), dec_seq, axis=1)

    x, h = _prep(x_prompt.astype(F32), x_sample.astype(F32), meta_tokens.astype(F32), norm_w3,
                 bsz, n_chunk)

    new_p = [[] for _ in range(6)]
    new_s = [[] for _ in range(3)]
    c_s = s_s = h_s = None
    y_prompt = y_sample = None
    for l in range(depth):
        proj = _inproj(h, w_t, w_gate, l, tm)

        ya, c_p, n_p, m_p = _mlstm_prompt(proj, gate_bias, onw3, l, bsz, n_chunk)
        ya, c_s, n_s, m_s = _mlstm_sample(proj, ya, c_s, gate_bias, onw3, state_mlstm_C,
                                          state_mlstm_n, m_rows, l, dec_b, dec_seq, dec_rows)
        yb, s_p, cv_p = _gdn_prompt(proj, gdn_par, conv_w, onw3, l, bsz, n_chunk)
        yb, s_s, cv_s = _gdn_sample(proj, yb, s_s, gdn_par, conv_w, onw3, state_gdn_S,
                                    state_gdn_conv, l, dec_b, dec_seq, dec_rows)
        yc, h_p = _hgrn_prompt(proj, lb_par, onw3, l, bsz, n_chunk)
        yc, h_s = _hgrn_sample(proj, yc, h_s, lb_par, onw3, state_hgrn_S, l, dec_b, dec_seq,
                               dec_rows)

        for lst, val in zip(new_p, (c_p, n_p, m_p[:, 0, :H_A], s_p, cv_p, h_p)):
            lst.append(val)
        for lst, val in zip(new_s, (n_s, m_s[::dec_seq, :H_A], cv_s)):
            lst.append(val)

        if l + 1 < depth:
            x, h = _outproj_mid(ya, yb, yc, x, w_out_bf, l, norm_w3, tm)
        else:
            y_prompt = _outproj_final(
                ya, yb, yc, x, w_out_bf, l, final_nw, tm, 1, chunks, (bsz, chunks, TILE_R, d),
                pl.BlockSpec((bsz, None, TILE_R, d), lambda i: (0, i, 0, 0)))
            y_sample = _outproj_final(
                ya, yb, yc, x, w_out_bf, l, final_nw, tm, n_chunk, n_dec // tm, (n_dec, d),
                pl.BlockSpec((tm, d), lambda i: (i, 0)))

    outs_p = [jnp.stack(a, axis=0) for a in new_p]
    n_all, m_all, cv_all = [jnp.stack(a, axis=0) for a in new_s]
    return (y_prompt.reshape(bsz, seq_len, d), y_sample.reshape(dec_b, dec_seq, d),
            *outs_p, c_s, n_all, m_all, s_s, cv_all, h_s)
```

```python
import functools
import math

import jax
import jax.numpy as jnp
from jax import lax
from jax.experimental import pallas as pl
from jax.experimental.pallas import tpu as pltpu

F32 = jnp.float32
BF16 = jnp.bfloat16

D_MODEL = 2048
HEAD_DIM = 128
H_A, H_B, H_C = 6, 6, 4
W_A, W_B, W_C = H_A * HEAD_DIM, H_B * HEAD_DIM, H_C * HEAD_DIM
MIX = W_A + W_B + W_C
N_META = 16
CONV_W = 4
EPS = 1e-6
NEG_BIG = -1e30
EXP_CLIP = 60.0
QK_SCALE = HEAD_DIM ** -0.5

TILE_R = 128
LANES = 128
CONV_PAD = 8

OFF_AQ = 0
OFF_AK, OFF_AV, OFF_AO = W_A, 2 * W_A, 3 * W_A
OFF_BQ = 4 * W_A
OFF_BK, OFF_BV = OFF_BQ + W_B, OFF_BQ + 2 * W_B
OFF_C = OFF_BQ + 3 * W_B
OFF_ZA = OFF_C + 3 * W_C
OFF_ZB = OFF_ZA + W_A
OFF_ZC = OFF_ZB + W_B
OFF_G = OFF_ZC + W_C
N_PACK = OFF_G + 2 * LANES
G_AI, G_AF, G_BA, G_BB = 0, H_A, 2 * H_A, 2 * H_A + H_B
TN = 512
INPROJ_MAX_ROWS = 1280
W_SEGMENTS = ((OFF_AQ, OFF_BQ, 0), (OFF_BQ, OFF_C, 2 * H_A), (OFF_C, OFF_G, 2 * H_A + 2 * H_B))
W_ROW_CHUNK = 256

VMEM_LIMIT = 56 * 1024 * 1024


def _sigmoid(x):
    return 1.0 / (1.0 + jnp.exp(-x))


def _silu(x):
    return x * _sigmoid(x)


def _softplus(x):
    return jnp.maximum(x, 0.0) + jnp.log1p(jnp.exp(-jnp.abs(x)))


def _log_sigmoid(x):
    return -_softplus(-x)


def _dot(a, b):
    return jnp.dot(a.astype(BF16), b.astype(BF16), preferred_element_type=F32)


def _dot_nt(a, b):
    return lax.dot_general(a.astype(BF16), b.astype(BF16), (((1,), (1,)), ((), ())),
                           preferred_element_type=F32)


def _dot_exact(sel, x):
    hi = x.astype(BF16)
    r1 = x - hi.astype(F32)
    mid = r1.astype(BF16)
    lo = (r1 - mid.astype(F32)).astype(BF16)
    s = sel.astype(BF16)
    return (jnp.dot(s, hi, preferred_element_type=F32)
            + jnp.dot(s, mid, preferred_element_type=F32)
            + jnp.dot(s, lo, preferred_element_type=F32))


def _log2(n):
    k = int(math.log2(n))
    assert (1 << k) == n
    return k


def _seq_masks(nseq, c):
    r = nseq * c
    row = lax.broadcasted_iota(jnp.int32, (r, r), 0)
    col = lax.broadcasted_iota(jnp.int32, (r, r), 1)
    if nseq == 1:
        same = row >= 0
    else:
        k = _log2(c)
        same = (row >> k) == (col >> k)
    return row, col, same


def _seq_cummax(x, nseq, c):
    r = nseq * c
    pos = lax.broadcasted_iota(jnp.int32, x.shape, 0)
    if nseq > 1:
        pos = pos & (c - 1)
    k = 1
    while k < c:
        x = jnp.where(pos >= k, jnp.maximum(x, pltpu.roll(x, k, axis=0)), x)
        k *= 2
    assert x.shape[0] == r
    return x


def _head_norm_gate(h, onw, z):
    hn = h * lax.rsqrt(jnp.mean(h * h, axis=1, keepdims=True) + EPS)
    return hn * onw * _silu(z)


def _rows(x, i, c):
    return x[i * c:(i + 1) * c]


def _cat_rows(parts):
    return parts[0] if len(parts) == 1 else jnp.concatenate(parts, axis=0)


def _chunk_shift(jj, r0):
    p = jj * TN + r0
    for p0, p1, shift in W_SEGMENTS:
        if p0 <= p and p + W_ROW_CHUNK <= p1:
            return shift
    return None


def _fill_weight_tile(w0_ref, w1_ref, wg_ref, wb_scr, tail_scr, jj):
    for r0 in range(0, TN, W_ROW_CHUNK):
        shift = _chunk_shift(jj, r0)
        dst = slice(r0, r0 + W_ROW_CHUNK)
        if shift is None:
            g0 = OFF_G - jj * TN
            assert r0 <= g0 and g0 + LANES <= r0 + W_ROW_CHUNK
            wb_scr[dst, :] = jnp.zeros((W_ROW_CHUNK, D_MODEL), BF16)
            wb_scr[g0:g0 + LANES, :] = wg_ref[...].astype(BF16)
        elif r0 + shift + W_ROW_CHUNK <= TN:
            wb_scr[dst, :] = w0_ref[r0 + shift:r0 + shift + W_ROW_CHUNK, :].astype(BF16)
        else:
            tail_scr[0:W_ROW_CHUNK, :] = w0_ref[r0:TN, :]
            tail_scr[W_ROW_CHUNK:, :] = w1_ref[0:tail_scr.shape[0] - W_ROW_CHUNK, :]
            wb_scr[dst, :] = tail_scr[shift:shift + W_ROW_CHUNK, :].astype(BF16)


def _inproj_kernel(h_ref, w0_ref, w1_ref, wg_ref, o_ref, wb_scr, tail_scr):
    j = pl.program_id(0)

    @pl.when(pl.program_id(1) == 0)
    def _():
        for jj in range(N_PACK // TN):
            @pl.when(j == jj)
            def _(jj=jj):
                _fill_weight_tile(w0_ref, w1_ref, wg_ref, wb_scr, tail_scr, jj)

    o_ref[...] = lax.dot_general(h_ref[...], wb_scr[...], (((1,), (1,)), ((), ())),
                                 preferred_element_type=F32)


W_TAIL_ROWS = 32


def _inproj(h, w_t, w_gate_t, layer, tm):
    t = h.shape[0]
    n_tiles = N_PACK // TN
    last_blk = (w_t.shape[1] - 1) // TN
    assert n_tiles - 1 <= last_blk and W_TAIL_ROWS >= W_SEGMENTS[-1][2]
    return pl.pallas_call(
        _inproj_kernel,
        grid=(n_tiles, t // tm),
        in_specs=[
            pl.BlockSpec((tm, D_MODEL), lambda j, i: (i, 0)),
            pl.BlockSpec((None, TN, D_MODEL), lambda j, i: (layer, j, 0)),
            pl.BlockSpec((None, TN, D_MODEL), lambda j, i: (layer, jnp.minimum(j + 1, last_blk), 0)),
            pl.BlockSpec((None, LANES, D_MODEL), lambda j, i: (layer, 0, 0)),
        ],
        out_specs=pl.BlockSpec((tm, TN), lambda j, i: (i, j)),
        out_shape=jax.ShapeDtypeStruct((t, N_PACK), F32),
        scratch_shapes=[pltpu.VMEM((TN, D_MODEL), BF16),
                        pltpu.VMEM((W_ROW_CHUNK + W_TAIL_ROWS, D_MODEL), F32)],
        compiler_params=pltpu.CompilerParams(
            dimension_semantics=("arbitrary", "arbitrary"), vmem_limit_bytes=VMEM_LIMIT),
        name="inproj",
    )(h, w_t, w_t, w_gate_t)


def _rmsnorm(x, w):
    ms = jnp.mean(x * x, axis=1, keepdims=True)
    return x * lax.rsqrt(ms + EPS) * w


def _prep_kernel(xp_ref, xs_ref, meta_ref, nw_ref, x_ref, h_ref, *, n_chunk, bsz):
    i = pl.program_id(0)
    d = x_ref.shape[1]

    def emit(x):
        x_ref[...] = x
        h_ref[...] = _rmsnorm(x, nw_ref[...]).astype(BF16)

    @pl.when(i == 0)
    def _():
        slot = jnp.concatenate([jnp.zeros((TILE_R - N_META, d), F32), meta_ref[...]], axis=0)
        emit(jnp.concatenate([slot] * bsz, axis=0))

    @pl.when((i > 0) & (i < n_chunk))
    def _():
        emit(xp_ref[...].reshape(bsz * TILE_R, d))

    @pl.when(i >= n_chunk)
    def _():
        emit(xs_ref[...])


def _prep(x_prompt, x_sample, meta, norm_w3, bsz, n_chunk):
    d = x_prompt.shape[-1]
    tm = bsz * TILE_R
    n_dec = x_sample.shape[0] * x_sample.shape[1]
    assert n_dec % tm == 0
    xp = x_prompt.reshape(bsz, n_chunk - 1, TILE_R, d)
    xs = x_sample.reshape(n_dec, d)
    n_blk = n_chunk + n_dec // tm
    t = n_blk * tm
    return pl.pallas_call(
        functools.partial(_prep_kernel, n_chunk=n_chunk, bsz=bsz),
        grid=(n_blk,),
        in_specs=[
            pl.BlockSpec((bsz, None, TILE_R, d), lambda i: (0, jnp.clip(i - 1, 0, n_chunk - 2), 0, 0)),
            pl.BlockSpec((tm, d), lambda i: (jnp.clip(i - n_chunk, 0, n_dec // tm - 1), 0)),
            pl.BlockSpec((N_META, d), lambda i: (0, 0)),
            pl.BlockSpec((None, 1, d), lambda i: (0, 0, 0)),
        ],
        out_specs=[pl.BlockSpec((tm, d), lambda i: (i, 0)),
                   pl.BlockSpec((tm, d), lambda i: (i, 0))],
        out_shape=[jax.ShapeDtypeStruct((t, d), F32), jax.ShapeDtypeStruct((t, d), BF16)],
        compiler_params=pltpu.CompilerParams(
            dimension_semantics=("arbitrary",), vmem_limit_bytes=VMEM_LIMIT),
        name="prep",
    )(xp, xs, meta, norm_w3)


def _outproj_residual(ya_ref, yb_ref, yc_ref, x_ref, w_ref):
    acc = jnp.dot(ya_ref[...], w_ref[0:W_A, :], preferred_element_type=F32)
    acc = acc + jnp.dot(yb_ref[...], w_ref[W_A:W_A + W_B, :], preferred_element_type=F32)
    acc = acc + jnp.dot(yc_ref[...], w_ref[W_A + W_B:MIX, :], preferred_element_type=F32)
    return x_ref[...] + acc


def _outproj_mid_kernel(ya_ref, yb_ref, yc_ref, x_ref, w_ref, nw_ref, x_out, h_out):
    xn = _outproj_residual(ya_ref, yb_ref, yc_ref, x_ref, w_ref)
    x_out[...] = xn
    h_out[...] = _rmsnorm(xn, nw_ref[...]).astype(BF16)


def _outproj_final_kernel(ya_ref, yb_ref, yc_ref, x_ref, w_ref, nw_ref, y_out):
    xn = _outproj_residual(ya_ref, yb_ref, yc_ref, x_ref, w_ref)
    y_out[...] = _rmsnorm(xn, nw_ref[...]).reshape(y_out.shape)


def _outproj_specs(tm, row_blk0, layer, nw_spec):
    rows = lambda i: (row_blk0 + i, 0)
    return [
        pl.BlockSpec((tm, W_A), rows),
        pl.BlockSpec((tm, W_B), rows),
        pl.BlockSpec((tm, W_C), rows),
        pl.BlockSpec((tm, D_MODEL), rows),
        pl.BlockSpec((None, MIX, D_MODEL), lambda i: (layer, 0, 0)),
        nw_spec,
    ]


def _outproj_mid(ya, yb, yc, x, w_out, layer, norm_w3, tm):
    t = x.shape[0]
    blk = pl.BlockSpec((tm, D_MODEL), lambda i: (i, 0))
    nw_spec = pl.BlockSpec((None, 1, D_MODEL), lambda i: (layer + 1, 0, 0))
    return pl.pallas_call(
        _outproj_mid_kernel,
        grid=(t // tm,),
        in_specs=_outproj_specs(tm, 0, layer, nw_spec),
        out_specs=[blk, blk],
        out_shape=[jax.ShapeDtypeStruct((t, D_MODEL), F32), jax.ShapeDtypeStruct((t, D_MODEL), BF16)],
        compiler_params=pltpu.CompilerParams(
            dimension_semantics=("parallel",), vmem_limit_bytes=VMEM_LIMIT),
        name="outproj",
    )(ya, yb, yc, x, w_out, norm_w3)


def _outproj_final(ya, yb, yc, x, w_out, layer, final_nw, tm, row_blk0, n_blk, out_shape, out_spec):
    nw_spec = pl.BlockSpec((1, D_MODEL), lambda i: (0, 0))
    return pl.pallas_call(
        _outproj_final_kernel,
        grid=(n_blk,),
        in_specs=_outproj_specs(tm, row_blk0, layer, nw_spec),
        out_specs=out_spec,
        out_shape=jax.ShapeDtypeStruct(out_shape, F32),
        compiler_params=pltpu.CompilerParams(
            dimension_semantics=("parallel",), vmem_limit_bytes=VMEM_LIMIT),
        name="outproj_final",
    )(ya, yb, yc, x, w_out, final_nw)


class _RowView:
    def __init__(self, ref, r0, n):
        self.ref, self.r0, self.n = ref, r0, n
        self.shape = (n, ref.shape[1])

    def _idx(self, idx):
        if idx is Ellipsis:
            return slice(self.r0, self.r0 + self.n), slice(None)
        rows, cols = idx
        start = self.r0 + (rows.start or 0)
        stop = self.r0 + (self.n if rows.stop is None else rows.stop)
        return slice(start, stop), cols

    def __getitem__(self, idx):
        return self.ref[self._idx(idx)]

    def __setitem__(self, idx, val):
        self.ref[self._idx(idx)] = val


def _tile_views(refs, u):
    return [_RowView(ref, u * TILE_R, TILE_R) for ref in refs]


def _interleave(tiles):
    tiles = list(tiles)
    while tiles:
        alive = []
        for t in tiles:
            try:
                next(t)
                alive.append(t)
            except StopIteration:
                pass
        tiles = alive


def _mlstm_tile(q_ref, k_ref, v_ref, o_ref, z_ref, g_ref, gb_ref, onw_ref, y_ref,
                m_exp, valid, get_c, get_n, put_c, put_n, put_m, nseq, c):
    r = nseq * c
    row, col, same = _seq_masks(nseq, c)
    incl = same & (col <= row)
    g = g_ref[...] + gb_ref[...]
    li = g
    lf = _log_sigmoid(g)
    if valid is not None:
        li = jnp.where(valid, li, NEG_BIG)
        lf = jnp.where(valid, lf, 0.0)
    fcum = _dot_exact(incl.astype(F32), lf)
    if nseq == 1:
        flast = jnp.broadcast_to(fcum[r - 1:r, :], (r, LANES))
    else:
        flast = _dot_exact(same.astype(F32), lf)
    yield
    heads = range(H_A)
    hsl = [slice(h * HEAD_DIM, (h + 1) * HEAD_DIM) for h in heads]

    li_a = pltpu.roll(li, G_AF - G_AI, axis=1)
    m_a = pltpu.roll(m_exp, G_AF, axis=1)
    x = li_a - fcum
    mx = jnp.maximum(m_a, _seq_cummax(x, nseq, c))
    mt_all = fcum + mx
    if nseq == 1:
        m_new = jnp.broadcast_to(mt_all[r - 1:r, :], (r, LANES))
    else:
        m_new = jnp.broadcast_to(mt_all.reshape(nseq, c, LANES)[:, c - 1:c, :],
                                 (nseq, c, LANES)).reshape(r, LANES)
    a_all = jnp.exp(m_a - mx)
    w_all = jnp.exp(flast - fcum + li_a - m_new)
    as_all = jnp.exp(flast + m_a - m_new)
    low_all = jnp.exp(jnp.minimum(-mt_all, EXP_CLIP))
    put_m(pltpu.roll(m_new, LANES - G_AF, axis=1))
    x_t = x.T
    col_of = lambda arr, h: arr[:, G_AF + h:G_AF + h + 1]
    a = [col_of(a_all, h) for h in heads]
    w = [col_of(w_all, h) for h in heads]
    a_s = [col_of(as_all, h) for h in heads]
    low = [col_of(low_all, h) for h in heads]
    pexp = [jnp.exp(jnp.where(incl, x_t[G_AF + h:G_AF + h + 1, :] - col_of(mx, h), NEG_BIG))
            for h in heads]
    yield

    s = [_dot_nt(q_ref[:, hs], k_ref[:, hs] * QK_SCALE) for hs in hsl]
    yield
    p = [pexp[h] * s[h] for h in heads]
    pv = [_dot(p[h], v_ref[:, hsl[h]]) for h in heads]
    yield
    qc = [_cat_rows([_dot(_rows(q_ref[:, hsl[h]], i, c), get_c(i, h)) for i in range(nseq)])
          for h in heads]
    yield
    for h in heads:
        hs = hsl[h]
        qh = q_ref[:, hs]
        psum = jnp.sum(p[h], axis=1, keepdims=True)
        qn = _cat_rows([jnp.sum(_rows(qh, i, c) * get_n(i, h), axis=1, keepdims=True)
                        for i in range(nseq)])
        num = a[h] * qc[h] + pv[h]
        den = a[h] * qn + psum
        hh = num / jnp.maximum(jnp.abs(den), low[h])
        hh = hh * _sigmoid(o_ref[:, hs])
        y_ref[:, hs] = _head_norm_gate(hh, onw_ref[:, hs], z_ref[:, hs]).astype(BF16)
    yield

    rowv = lax.broadcasted_iota(jnp.int32, (r, 1), 0)
    for h in heads:
        hs = hsl[h]
        kh = k_ref[:, hs] * QK_SCALE
        wv = w[h] * v_ref[:, hs]
        wk = w[h] * kh
        kh_t = kh.T.astype(BF16)
        for i in range(nseq):
            a_i = a_s[h][i * c:i * c + 1, :]
            if nseq == 1:
                upd = jnp.dot(kh_t, wv.astype(BF16), preferred_element_type=F32)
            else:
                seq_rows = (rowv >> _log2(c)) == i
                upd = jnp.dot(kh_t, jnp.where(seq_rows, wv, 0.0).astype(BF16),
                              preferred_element_type=F32)
            put_c(i, h, a_i * get_c(i, h) + upd)
            put_n(i, h, a_i * get_n(i, h) + jnp.sum(_rows(wk, i, c), axis=0, keepdims=True))


def _mlstm_prompt_kernel(q_ref, k_ref, v_ref, o_ref, z_ref, g_ref, gb_ref, onw_ref,
                         y_ref, c_out, n_out, m_out, c_scr, n_scr, m_scr):
    j = pl.program_id(0)
    bsz = c_scr.shape[0]

    @pl.when(j == 0)
    def _():
        c_scr[...] = jnp.zeros_like(c_scr)
        n_scr[...] = jnp.zeros_like(n_scr)
        m_scr[...] = jnp.zeros_like(m_scr)

    tok = lax.broadcasted_iota(jnp.int32, (TILE_R, LANES), 0) + j * TILE_R
    valid = tok >= TILE_R - N_META

    def tile(u):
        q, k, v, o, z, g, y = _tile_views((q_ref, k_ref, v_ref, o_ref, z_ref, g_ref, y_ref), u)
        m_exp = jnp.broadcast_to(m_scr[u, 0:1, :], (TILE_R, LANES))

        def put_c(i, h, val):
            c_scr[u, h] = val

        def put_n(i, h, val):
            n_scr[u, h:h + 1, :] = val

        def put_m(val):
            m_scr[u] = val[0:8, :]

        return _mlstm_tile(q, k, v, o, z, g, gb_ref, onw_ref, y, m_exp, valid,
                           lambda i, h: c_scr[u, h], lambda i, h: n_scr[u, h:h + 1, :],
                           put_c, put_n, put_m, 1, TILE_R)

    _interleave(tile(u) for u in range(bsz))

    @pl.when(j == pl.num_programs(0) - 1)
    def _():
        c_out[...] = c_scr[...]
        n_out[...] = n_scr[:, 0:H_A, :]
        m_out[...] = m_scr[:, 0:1, :]


def _mlstm_sample_kernel(q_ref, k_ref, v_ref, o_ref, z_ref, g_ref, gb_ref, onw_ref,
                         c_in, n_in, m_in, y_ref, c_out, n_out, m_out, *, c):
    nseq = TILE_R // c

    def put_c(i, h, val):
        c_out[i, h] = val

    def put_n(i, h, val):
        n_out[i, h:h + 1, :] = val

    def put_m(val):
        m_out[...] = val

    _interleave([_mlstm_tile(
        q_ref, k_ref, v_ref, o_ref, z_ref, g_ref, gb_ref, onw_ref, y_ref, m_in[...], None,
        lambda i, h: c_in[i, h], lambda i, h: n_in[i, h:h + 1, :], put_c, put_n, put_m, nseq, c)])


def _col_spec(width, off, rows, n_rows=TILE_R):
    assert off % width == 0
    blk = off // width
    return pl.BlockSpec((n_rows, width), lambda *ids: (rows(*ids), blk))


def _param_spec(width, off, layer):
    assert off % width == 0
    blk = off // width
    return pl.BlockSpec((None, 1, width), lambda *ids: (layer, 0, blk))


def _whole(*shape):
    return pl.BlockSpec(shape, lambda j: (0,) * len(shape))


def _mlstm_prompt(proj, gate_bias, onw3, layer, bsz, n_chunk):
    t = proj.shape[0]
    tm = bsz * TILE_R
    rows = lambda j: j
    in_specs = [_col_spec(W_A, off, rows, tm) for off in (OFF_AQ, OFF_AK, OFF_AV, OFF_AO, OFF_ZA)]
    in_specs += [_col_spec(LANES, OFF_G, rows, tm), _param_spec(LANES, 0, layer),
                 _param_spec(W_A, 0, layer)]
    return pl.pallas_call(
        _mlstm_prompt_kernel,
        grid=(n_chunk,),
        in_specs=in_specs,
        out_specs=[
            pl.BlockSpec((tm, W_A), lambda j: (j, 0)),
            _whole(bsz, H_A, HEAD_DIM, HEAD_DIM),
            _whole(bsz, H_A, HEAD_DIM),
            _whole(bsz, 1, LANES),
        ],
        out_shape=[
            jax.ShapeDtypeStruct((t, W_A), BF16),
            jax.ShapeDtypeStruct((bsz, H_A, HEAD_DIM, HEAD_DIM), F32),
            jax.ShapeDtypeStruct((bsz, H_A, HEAD_DIM), F32),
            jax.ShapeDtypeStruct((bsz, 1, LANES), F32),
        ],
        scratch_shapes=[pltpu.VMEM((bsz, H_A, HEAD_DIM, HEAD_DIM), F32),
                        pltpu.VMEM((bsz, 8, HEAD_DIM), F32), pltpu.VMEM((bsz, 8, LANES), F32)],
        compiler_params=pltpu.CompilerParams(
            dimension_semantics=("arbitrary",), vmem_limit_bytes=VMEM_LIMIT),
        name="mlstm_prompt",
    )(proj, proj, proj, proj, proj, proj, gate_bias, onw3)


def _in_place(operands, n_blocked, targets):
    specs, aliases = [], {}
    operands = list(operands)
    for out_idx, arr in targets.items():
        if arr is not None:
            aliases[len(operands)] = out_idx
            operands.append(arr)
            specs.append(pl.BlockSpec(memory_space=pl.ANY))
    n_extra = len(specs)

    def strip(kernel):
        def body(*refs):
            kernel(*refs[:n_blocked], *refs[n_blocked + n_extra:])
        return body

    return operands, specs, aliases, strip


def _mlstm_sample(proj, y_prev, c_stack, gate_bias, onw3, c_state, n_state, m_rows, layer,
                  dec_b, c, rows):
    nseq = TILE_R // c
    depth = c_state.shape[0]
    in_specs = [_col_spec(W_A, off, rows) for off in (OFF_AQ, OFF_AK, OFF_AV, OFF_AO, OFF_ZA)]
    in_specs += [_col_spec(LANES, OFF_G, rows), _param_spec(LANES, 0, layer),
                 _param_spec(W_A, 0, layer)]
    in_specs += [
        pl.BlockSpec((None, nseq, H_A, HEAD_DIM, HEAD_DIM), lambda i: (layer, i, 0, 0, 0)),
        pl.BlockSpec((None, nseq, H_A, HEAD_DIM), lambda i: (layer, i, 0, 0)),
        pl.BlockSpec((None, TILE_R, LANES), lambda i: (layer, i, 0)),
    ]
    operands = (proj, proj, proj, proj, proj, proj, gate_bias, onw3, c_state, n_state, m_rows)
    operands, extra, aliases, strip = _in_place(operands, len(in_specs), {0: y_prev, 1: c_stack})
    return pl.pallas_call(
        strip(functools.partial(_mlstm_sample_kernel, c=c)),
        grid=(dec_b // nseq,),
        in_specs=in_specs + extra,
        out_specs=[
            pl.BlockSpec((TILE_R, W_A), lambda i: (rows(i), 0)),
            pl.BlockSpec((None, nseq, H_A, HEAD_DIM, HEAD_DIM), lambda i: (layer, i, 0, 0, 0)),
            pl.BlockSpec((nseq, H_A, HEAD_DIM), lambda i: (i, 0, 0)),
            pl.BlockSpec((TILE_R, LANES), lambda i: (i, 0)),
        ],
        out_shape=[
            jax.ShapeDtypeStruct(y_prev.shape, BF16),
            jax.ShapeDtypeStruct((depth, dec_b, H_A, HEAD_DIM, HEAD_DIM), F32),
            jax.ShapeDtypeStruct((dec_b, H_A, HEAD_DIM), F32),
            jax.ShapeDtypeStruct((dec_b * c, LANES), F32),
        ],
        input_output_aliases=aliases,
        compiler_params=pltpu.CompilerParams(
            dimension_semantics=("parallel",), vmem_limit_bytes=VMEM_LIMIT),
        name="mlstm_sample",
    )(*operands)


def _l2norm(x):
    return x * lax.rsqrt(jnp.sum(x * x, axis=1, keepdims=True) + EPS)


INV_LEAF = 16


def _unit_lower_inverse(a_list, row, col, c):
    leaf = min(INV_LEAF, c)
    kl = _log2(leaf)
    leaf_mask = (row >> kl) == (col >> kl)
    eye = jnp.where(row == col, 1.0, 0.0)
    p = [jnp.where(leaf_mask, -a, 0.0) for a in a_list]
    t = [eye + n for n in p]
    for _ in range(kl - 1):
        p = [_dot(x, x) for x in p]
        yield
        t = [ti + _dot(pi, ti) for pi, ti in zip(p, t)]
        yield
    w = leaf
    while w < c:
        kw = _log2(w)
        lower_left = (((row >> (kw + 1)) == (col >> (kw + 1)))
                      & (((row >> kw) & 1) == 1) & (((col >> kw) & 1) == 0))
        left = [_dot(ti, jnp.where(lower_left, a, 0.0)) for ti, a in zip(t, a_list)]
        yield
        t = [ti - _dot(li, ti) for li, ti in zip(left, t)]
        yield
        w *= 2
    return t


def _gdn_tile(conv_win, cw_ref, z_ref, g_ref, gp_ref, onw_ref, y_ref, get_s, put_s, nseq, c):
    r = nseq * c
    row, col, same = _seq_masks(nseq, c)
    incl = same & (col <= row)
    strict = same & (col < row)
    gp = gp_ref[...]
    graw = g_ref[...]
    gdec = -jnp.exp(gp[1:2, :]) * _softplus(graw + gp[0:1, :])
    beta = _sigmoid(graw)
    gcum = _dot_exact(incl.astype(F32), gdec)
    if nseq == 1:
        glast = jnp.broadcast_to(gcum[r - 1:r, :], (r, LANES))
    else:
        glast = _dot_exact(same.astype(F32), gdec)
    yield
    gcum_t = gcum.T

    heads = range(H_B)
    hsl = [slice(h * HEAD_DIM, (h + 1) * HEAD_DIM) for h in heads]
    rowv = lax.broadcasted_iota(jnp.int32, (r, 1), 0)

    def conv(lo):
        acc = conv_win(0, slice(lo, lo + HEAD_DIM)) * cw_ref[0:1, lo:lo + HEAD_DIM]
        for jj in range(1, CONV_W):
            acc = acc + conv_win(jj, slice(lo, lo + HEAD_DIM)) * cw_ref[jj:jj + 1, lo:lo + HEAD_DIM]
        return _silu(acc)

    q = [_l2norm(conv(hs.start)) * QK_SCALE for hs in hsl]
    k = [_l2norm(conv(W_B + hs.start)) for hs in hsl]
    v = [conv(2 * W_B + hs.start) for hs in hsl]
    yield
    gc = [gcum[:, G_BA + h:G_BA + h + 1] for h in heads]
    gl = [glast[:, G_BA + h:G_BA + h + 1] for h in heads]
    bc = [beta[:, G_BB + h:G_BB + h + 1] for h in heads]
    decay = [jnp.exp(jnp.where(incl, gc[h] - gcum_t[G_BA + h:G_BA + h + 1, :], NEG_BIG))
             for h in heads]
    kk = [_dot_nt(k[h], k[h]) for h in heads]
    qk = [_dot_nt(q[h], k[h]) for h in heads]
    yield
    amat = [jnp.where(strict, bc[h] * decay[h] * kk[h], 0.0) for h in heads]
    tinv = yield from _unit_lower_inverse(amat, row, col, c)
    sol = [_dot(tinv[h], jnp.concatenate([bc[h] * v[h], (bc[h] * jnp.exp(gc[h])) * k[h]], axis=1))
           for h in heads]
    yield
    wks = [_cat_rows([_dot(_rows(sol[h][:, HEAD_DIM:], i, c), get_s(i, h)) for i in range(nseq)])
           for h in heads]
    qs = [_cat_rows([_dot(_rows(q[h], i, c), get_s(i, h)) for i in range(nseq)]) for h in heads]
    yield
    wmat = [sol[h][:, :HEAD_DIM] - wks[h] for h in heads]
    aw = [_dot(decay[h] * qk[h], wmat[h]) for h in heads]
    yield
    for h in heads:
        hs = hsl[h]
        oh = jnp.exp(gc[h]) * qs[h] + aw[h]
        y_ref[:, hs] = _head_norm_gate(oh, onw_ref[:, hs], z_ref[:, hs]).astype(BF16)
    yield
    for h in heads:
        kd_t = (k[h] * jnp.exp(gl[h] - gc[h])).T.astype(BF16)
        e_last = jnp.exp(gl[h])
        for i in range(nseq):
            if nseq == 1:
                upd = jnp.dot(kd_t, wmat[h].astype(BF16), preferred_element_type=F32)
            else:
                seq_rows = (rowv >> _log2(c)) == i
                upd = jnp.dot(kd_t, jnp.where(seq_rows, wmat[h], 0.0).astype(BF16),
                              preferred_element_type=F32)
            put_s(i, h, e_last[i * c:i * c + 1, :] * get_s(i, h) + upd)


def _gdn_prompt_kernel(q_ref, k_ref, v_ref, z_ref, g_ref, gp_ref, cw_ref, onw_ref,
                       y_ref, s_out, conv_out, s_scr, ext_scr):
    j = pl.program_id(0)
    bsz = s_scr.shape[0]

    @pl.when(j == 0)
    def _():
        s_scr[...] = jnp.zeros_like(s_scr)
        ext_scr[:, 0:CONV_PAD, :] = jnp.zeros((bsz, CONV_PAD, 3 * W_B), F32)

    def tile(u):
        q, k, v, z, g, y = _tile_views((q_ref, k_ref, v_ref, z_ref, g_ref, y_ref), u)
        ext_scr[u, CONV_PAD:, 0:W_B] = q[...]
        ext_scr[u, CONV_PAD:, W_B:2 * W_B] = k[...]
        ext_scr[u, CONV_PAD:, 2 * W_B:] = v[...]

        def conv_win(jj, cols):
            start = CONV_PAD - (CONV_W - 1) + jj
            return ext_scr[u, start:start + TILE_R, cols]

        def put_s(i, h, val):
            s_scr[u, h] = val

        return _gdn_tile(conv_win, cw_ref, z, g, gp_ref, onw_ref, y,
                         lambda i, h: s_scr[u, h], put_s, 1, TILE_R)

    _interleave(tile(u) for u in range(bsz))

    @pl.when(j == pl.num_programs(0) - 1)
    def _():
        s_out[...] = s_scr[...]
        conv_out[...] = ext_scr[:, TILE_R + CONV_PAD - (CONV_W - 1):TILE_R + CONV_PAD, :]

    ext_scr[:, 0:CONV_PAD, :] = ext_scr[:, TILE_R:TILE_R + CONV_PAD, :]


def _gdn_sample_kernel(q_ref, k_ref, v_ref, z_ref, g_ref, gp_ref, cw_ref, onw_ref,
                       s_in, conv_in, y_ref, s_out, conv_out, ext_scr, *, c):
    nseq = TILE_R // c
    hist = CONV_W - 1
    ext_scr[:, CONV_PAD - hist:CONV_PAD, :] = conv_in[...]
    ext_scr[:, CONV_PAD:, 0:W_B] = q_ref[...].reshape(nseq, c, W_B)
    ext_scr[:, CONV_PAD:, W_B:2 * W_B] = k_ref[...].reshape(nseq, c, W_B)
    ext_scr[:, CONV_PAD:, 2 * W_B:] = v_ref[...].reshape(nseq, c, W_B)

    def conv_win(jj, cols):
        start = CONV_PAD - hist + jj
        return ext_scr[:, start:start + c, cols].reshape(TILE_R, cols.stop - cols.start)

    def put_s(i, h, val):
        s_out[i, h] = val

    _interleave([_gdn_tile(conv_win, cw_ref, z_ref, g_ref, gp_ref, onw_ref, y_ref,
                           lambda i, h: s_in[i, h], put_s, nseq, c)])
    conv_out[...] = ext_scr[:, CONV_PAD + c - hist:CONV_PAD + c, :]


def _gdn_prompt(proj, gdn_par, conv_w, onw3, layer, bsz, n_chunk):
    t = proj.shape[0]
    tm = bsz * TILE_R
    rows = lambda j: j
    in_specs = [_col_spec(W_B, off, rows, tm) for off in (OFF_BQ, OFF_BK, OFF_BV, OFF_ZB)]
    in_specs += [
        _col_spec(LANES, OFF_G, rows, tm),
        pl.BlockSpec((None, 8, LANES), lambda j: (layer, 0, 0)),
        pl.BlockSpec((None, CONV_W, 3 * W_B), lambda j: (layer, 0, 0)),
        _param_spec(W_B, W_A, layer),
    ]
    return pl.pallas_call(
        _gdn_prompt_kernel,
        grid=(n_chunk,),
        in_specs=in_specs,
        out_specs=[
            pl.BlockSpec((tm, W_B), lambda j: (j, 0)),
            _whole(bsz, H_B, HEAD_DIM, HEAD_DIM),
            _whole(bsz, CONV_W - 1, 3 * W_B),
        ],
        out_shape=[
            jax.ShapeDtypeStruct((t, W_B), BF16),
            jax.ShapeDtypeStruct((bsz, H_B, HEAD_DIM, HEAD_DIM), F32),
            jax.ShapeDtypeStruct((bsz, CONV_W - 1, 3 * W_B), F32),
        ],
        scratch_shapes=[pltpu.VMEM((bsz, H_B, HEAD_DIM, HEAD_DIM), F32),
                        pltpu.VMEM((bsz, TILE_R + CONV_PAD, 3 * W_B), F32)],
        compiler_params=pltpu.CompilerParams(
            dimension_semantics=("arbitrary",), vmem_limit_bytes=VMEM_LIMIT),
        name="gdn_prompt",
    )(proj, proj, proj, proj, proj, gdn_par, conv_w, onw3)


def _gdn_sample(proj, y_prev, s_stack, gdn_par, conv_w, onw3, s_state, conv_state, layer,
                dec_b, c, rows):
    nseq = TILE_R // c
    depth = s_state.shape[0]
    in_specs = [_col_spec(W_B, off, rows) for off in (OFF_BQ, OFF_BK, OFF_BV, OFF_ZB)]
    in_specs += [
        _col_spec(LANES, OFF_G, rows),
        pl.BlockSpec((None, 8, LANES), lambda i: (layer, 0, 0)),
        pl.BlockSpec((None, CONV_W, 3 * W_B), lambda i: (layer, 0, 0)),
        _param_spec(W_B, W_A, layer),
        pl.BlockSpec((None, nseq, H_B, HEAD_DIM, HEAD_DIM), lambda i: (layer, i, 0, 0, 0)),
        pl.BlockSpec((None, nseq, CONV_W - 1, 3 * W_B), lambda i: (layer, i, 0, 0)),
    ]
    operands = (proj, proj, proj, proj, proj, gdn_par, conv_w, onw3, s_state, conv_state)
    operands, extra, aliases, strip = _in_place(operands, len(in_specs), {0: y_prev, 1: s_stack})
    return pl.pallas_call(
        strip(functools.partial(_gdn_sample_kernel, c=c)),
        grid=(dec_b // nseq,),
        in_specs=in_specs + extra,
        out_specs=[
            pl.BlockSpec((TILE_R, W_B), lambda i: (rows(i), 0)),
            pl.BlockSpec((None, nseq, H_B, HEAD_DIM, HEAD_DIM), lambda i: (layer, i, 0, 0, 0)),
            pl.BlockSpec((nseq, CONV_W - 1, 3 * W_B), lambda i: (i, 0, 0)),
        ],
        out_shape=[
            jax.ShapeDtypeStruct(y_prev.shape, BF16),
            jax.ShapeDtypeStruct((depth, dec_b, H_B, HEAD_DIM, HEAD_DIM), F32),
            jax.ShapeDtypeStruct((dec_b, CONV_W - 1, 3 * W_B), F32),
        ],
        scratch_shapes=[pltpu.VMEM((nseq, CONV_PAD + c, 3 * W_B), F32)],
        input_output_aliases=aliases,
        compiler_params=pltpu.CompilerParams(
            dimension_semantics=("parallel",), vmem_limit_bytes=VMEM_LIMIT),
        name="gdn_sample",
    )(*operands)


def _hgrn_tile(q_ref, f_ref, i_ref, z_ref, lbp_ref, onw_ref, y_ref, get_s, put_s,
               nseq, c, layer):
    r = nseq * c
    row, col, same = _seq_masks(nseq, c)
    incl = same & (col <= row)
    rowv = lax.broadcasted_iota(jnp.int32, (r, 1), 0)
    lbp = lbp_ref[...]
    e = jnp.exp(lbp - jnp.max(lbp, axis=0, keepdims=True))
    sm = e / jnp.sum(e, axis=0, keepdims=True)
    lb = jnp.zeros((1, W_C), F32)
    for l in range(1, layer + 1):
        lb = lb + sm[l:l + 1, :]

    cf = f_ref[...]
    logf = _log_sigmoid(cf) + jnp.log1p(lb * jnp.exp(jnp.minimum(-cf, EXP_CLIP)))
    kall = (1.0 - lb) * _sigmoid(-cf)
    bcum = _dot_exact(incl.astype(F32), logf)
    if nseq == 1:
        blast = jnp.broadcast_to(bcum[r - 1:r, :], (r, W_C))
    else:
        blast = _dot_exact(same.astype(F32), logf)

    levels = []
    w = c // 2
    while w >= 1:
        levels.append(w)
        w //= 2
    rowf = lax.broadcasted_iota(jnp.int32, (r, W_C), 0)
    refs = []
    for w in levels:
        if 2 * w >= 8:
            refs.append(_cat_rows([
                jnp.broadcast_to(bcum[g * 2 * w + w - 1:g * 2 * w + w, :], (2 * w, W_C))
                for g in range(r // (2 * w))]))
        else:
            pos = rowf & (2 * w - 1)
            ref = bcum
            for off in range(2 * w):
                if off != w - 1:
                    ref = jnp.where(pos == off, pltpu.roll(bcum, (off - (w - 1)) % r, axis=0), ref)
            refs.append(ref)

    heads = range(H_C)
    hsl = [slice(h * HEAD_DIM, (h + 1) * HEAD_DIM) for h in heads]
    q = [_silu(q_ref[:, hs]) for hs in hsl]
    k = [kall[:, hs] for hs in hsl]
    b = [bcum[:, hs] for hs in hsl]
    diag = row == col
    yield
    amat = [jnp.where(diag, _dot_nt(q[h], k[h]), 0.0) for h in heads]
    for w, ref_all in zip(levels, refs):
        yield
        lw = _log2(w)
        tgt = ((rowv >> lw) & 1) == 1
        pair = (row >> (lw + 1)) == (col >> (lw + 1))
        qp = [q[h] * jnp.exp(jnp.where(tgt, b[h] - ref_all[:, hsl[h]], NEG_BIG)) for h in heads]
        kp = [k[h] * jnp.exp(jnp.where(tgt, NEG_BIG, ref_all[:, hsl[h]] - b[h])) for h in heads]
        amat = [amat[h] + jnp.where(pair, _dot_nt(qp[h], kp[h]), 0.0) for h in heads]
    yield
    qs = [_cat_rows([_dot(_rows(q[h] * jnp.exp(b[h]), i, c), get_s(i, h)) for i in range(nseq)])
          for h in heads]
    av = [_dot(amat[h], i_ref[:, hsl[h]]) for h in heads]
    yield
    for h in heads:
        hs = hsl[h]
        y_ref[:, hs] = _head_norm_gate(qs[h] + av[h], onw_ref[:, hs], z_ref[:, hs]).astype(BF16)
    yield
    for h in heads:
        hs = hsl[h]
        vh = i_ref[:, hs]
        bl = blast[:, hs]
        kd_t = (k[h] * jnp.exp(bl - b[h])).T.astype(BF16)
        eb_t = jnp.exp(bl).T
        for i in range(nseq):
            if nseq == 1:
                upd = jnp.dot(kd_t, vh.astype(BF16), preferred_element_type=F32)
            else:
                seq_rows = (rowv >> _log2(c)) == i
                upd = jnp.dot(kd_t, jnp.where(seq_rows, vh, 0.0).astype(BF16),
                              preferred_element_type=F32)
            put_s(i, h, eb_t[:, i * c:i * c + 1] * get_s(i, h) + upd)


class _ColView:
    def __init__(self, parts):
        self.parts = []
        pos = 0
        for ref, lo, hi in parts:
            self.parts.append((pos, ref, lo, hi - lo))
            pos += hi - lo
        self.width = pos

    def __getitem__(self, idx):
        if idx is Ellipsis:
            rows, cols = slice(None), slice(0, self.width)
        else:
            rows, cols = idx
            cols = slice(cols.start or 0, self.width if cols.stop is None else cols.stop)
        out = []
        for pos, ref, lo, n in self.parts:
            a, b = max(cols.start, pos), min(cols.stop, pos + n)
            if a < b:
                out.append(ref[rows, lo + a - pos:lo + b - pos])
        return out[0] if len(out) == 1 else jnp.concatenate(out, axis=1)


def _hgrn_views(ca_ref, cb_ref, zc_ref):
    two = 2 * W_C - ca_ref.shape[1]
    q = _ColView([(ca_ref, 0, W_C)])
    f = _ColView([(ca_ref, W_C, ca_ref.shape[1]), (cb_ref, 0, two)])
    i = _ColView([(cb_ref, two, two + W_C)])
    z = _ColView([(zc_ref, 0, W_C)])
    return q, f, i, z


def _hgrn_prompt_kernel(ca_ref, cb_ref, zc_ref, lbp_ref, onw_ref, y_ref, s_out, s_scr,
                        *, layer):
    j = pl.program_id(0)
    bsz = s_scr.shape[0]

    @pl.when(j == 0)
    def _():
        s_scr[...] = jnp.zeros_like(s_scr)

    def tile(u):
        ca, cb, zc, y = _tile_views((ca_ref, cb_ref, zc_ref, y_ref), u)

        def put_s(i, h, val):
            s_scr[u, h] = val

        return _hgrn_tile(*_hgrn_views(ca, cb, zc), lbp_ref, onw_ref, y,
                          lambda i, h: s_scr[u, h], put_s, 1, TILE_R, layer)

    _interleave(tile(u) for u in range(bsz))

    @pl.when(j == pl.num_programs(0) - 1)
    def _():
        s_out[...] = s_scr[...]


def _hgrn_sample_kernel(ca_ref, cb_ref, zc_ref, lbp_ref, onw_ref, s_in, y_ref, s_out,
                        *, c, layer):
    def put_s(i, h, val):
        s_out[i, h] = val

    _interleave([_hgrn_tile(*_hgrn_views(ca_ref, cb_ref, zc_ref), lbp_ref, onw_ref, y_ref,
                            lambda i, h: s_in[i, h], put_s, TILE_R // c, c, layer)])


HGRN_BLK = W_A
assert OFF_C % HGRN_BLK == 0 and 3 * W_C == 2 * HGRN_BLK and OFF_ZC % HGRN_BLK == 0


def _hgrn_prompt(proj, lb_par, onw3, layer, bsz, n_chunk):
    t = proj.shape[0]
    tm = bsz * TILE_R
    depth = lb_par.shape[0]
    rows = lambda j: j
    in_specs = [_col_spec(HGRN_BLK, off, rows, tm) for off in (OFF_C, OFF_C + HGRN_BLK, OFF_ZC)]
    in_specs += [pl.BlockSpec((depth, W_C), lambda j: (0, 0)),
                 _param_spec(W_C, W_A + W_B, layer)]
    return pl.pallas_call(
        functools.partial(_hgrn_prompt_kernel, layer=layer),
        grid=(n_chunk,),
        in_specs=in_specs,
        out_specs=[
            pl.BlockSpec((tm, W_C), lambda j: (j, 0)),
            _whole(bsz, H_C, HEAD_DIM, HEAD_DIM),
        ],
        out_shape=[
            jax.ShapeDtypeStruct((t, W_C), BF16),
            jax.ShapeDtypeStruct((bsz, H_C, HEAD_DIM, HEAD_DIM), F32),
        ],
        scratch_shapes=[pltpu.VMEM((bsz, H_C, HEAD_DIM, HEAD_DIM), F32)],
        compiler_params=pltpu.CompilerParams(
            dimension_semantics=("arbitrary",), vmem_limit_bytes=VMEM_LIMIT),
        name="hgrn_prompt",
    )(proj, proj, proj, lb_par, onw3)


def _hgrn_sample(proj, y_prev, s_stack, lb_par, onw3, s_state, layer, dec_b, c, rows):
    nseq = TILE_R // c
    depth = lb_par.shape[0]
    in_specs = [_col_spec(HGRN_BLK, off, rows) for off in (OFF_C, OFF_C + HGRN_BLK, OFF_ZC)]
    in_specs += [
        pl.BlockSpec((depth, W_C), lambda i: (0, 0)),
        _param_spec(W_C, W_A + W_B, layer),
        pl.BlockSpec((None, nseq, H_C, HEAD_DIM, HEAD_DIM), lambda i: (layer, i, 0, 0, 0)),
    ]
    operands = (proj, proj, proj, lb_par, onw3, s_state)
    operands, extra, aliases, strip = _in_place(operands, len(in_specs), {0: y_prev, 1: s_stack})
    return pl.pallas_call(
        strip(functools.partial(_hgrn_sample_kernel, c=c, layer=layer)),
        grid=(dec_b // nseq,),
        in_specs=in_specs + extra,
        out_specs=[
            pl.BlockSpec((TILE_R, W_C), lambda i: (rows(i), 0)),
            pl.BlockSpec((None, nseq, H_C, HEAD_DIM, HEAD_DIM), lambda i: (layer, i, 0, 0, 0)),
        ],
        out_shape=[
            jax.ShapeDtypeStruct(y_prev.shape, BF16),
            jax.ShapeDtypeStruct((depth, dec_b, H_C, HEAD_DIM, HEAD_DIM), F32),
        ],
        input_output_aliases=aliases,
        compiler_params=pltpu.CompilerParams(
            dimension_semantics=("parallel",), vmem_limit_bytes=VMEM_LIMIT),
        name="hgrn_sample",
    )(*operands)


def _gate_weight(w_t):
    ai0 = 4 * W_A
    ba0 = ai0 + 2 * H_A + 3 * W_B
    n_gate = 2 * H_A + 2 * H_B
    pad = jnp.zeros((w_t.shape[0], LANES - n_gate, w_t.shape[2]), w_t.dtype)
    return jnp.concatenate([w_t[:, ai0:ai0 + 2 * H_A], w_t[:, ba0:ba0 + 2 * H_B], pad], axis=1)


def _lane_row(depth, pieces):
    row = jnp.zeros((depth, LANES), F32)
    for off, val in pieces:
        row = lax.dynamic_update_slice(row, val.astype(F32), (0, off))
    return row[:, None, :]


def kernel(x_prompt, x_sample, state_mlstm_C, state_mlstm_n, state_mlstm_m, state_gdn_S,
           state_gdn_conv, state_hgrn_S, meta_tokens, norm_w, w_in, mlstm_gate_b, gdn_A_log,
           gdn_dt_bias, gdn_conv_w, hgrn_lower_bounds, out_norm_w, w_out, final_norm_w):
    bsz, seq_len, d = x_prompt.shape
    dec_b, dec_seq, _ = x_sample.shape
    depth = w_in.shape[0]
    assert d == D_MODEL and seq_len % TILE_R == 0 and TILE_R % dec_seq == 0
    assert w_in.shape[2] == W_SEGMENTS[-1][1] + W_SEGMENTS[-1][2]
    nseq = TILE_R // dec_seq
    assert dec_b % nseq == 0

    chunks = seq_len // TILE_R
    n_chunk = chunks + 1
    tm = bsz * TILE_R
    n_dec = dec_b * dec_seq
    dec_blk0 = n_chunk * bsz

    def dec_rows(i):
        return dec_blk0 + i

    w_t = jnp.swapaxes(w_in.astype(F32), 1, 2)
    w_gate = _gate_weight(w_t)
    w_out_bf = w_out.astype(BF16)
    norm_w3 = norm_w.astype(F32)[:, None, :]
    onw3 = out_norm_w.astype(F32)[:, None, :]
    final_nw = final_norm_w.astype(F32)[None, :]
    gate_bias = _lane_row(depth, [(G_AI, mlstm_gate_b[:, 0]), (G_AF, mlstm_gate_b[:, 1])])
    gdn_par = jnp.concatenate([
        _lane_row(depth, [(G_BA, gdn_dt_bias)]), _lane_row(depth, [(G_BA, gdn_A_log)]),
        jnp.zeros((depth, 6, LANES), F32)], axis=1)
    conv_w = gdn_conv_w.astype(F32)
    lb_par = hgrn_lower_bounds.astype(F32)
    m_rows = jnp.repeat(
        jnp.pad(state_mlstm_m.astype(F32), ((0, 0), (0, 0), (0, LANES - H_A))), dec_seq, axis=1)

    x, h = _prep(x_prompt.astype(F32), x_sample.astype(F32), meta_tokens.astype(F32), norm_w3,
                 bsz, n_chunk)
    t_all = x.shape[0]
    tm_in = max(m for m in range(16, INPROJ_MAX_ROWS + 1, 16) if t_all % m == 0)

    new_p = [[] for _ in range(6)]
    new_s = [[] for _ in range(3)]
    c_s = s_s = h_s = None
    y_prompt = y_sample = None
    for l in range(depth):
        proj = _inproj(h, w_t, w_gate, l, tm_in)

        ya, c_p, n_p, m_p = _mlstm_prompt(proj, gate_bias, onw3, l, bsz, n_chunk)
        ya, c_s, n_s, m_s = _mlstm_sample(proj, ya, c_s, gate_bias, onw3, state_mlstm_C,
                                          state_mlstm_n, m_rows, l, dec_b, dec_seq, dec_rows)
        yb, s_p, cv_p = _gdn_prompt(proj, gdn_par, conv_w, onw3, l, bsz, n_chunk)
        yb, s_s, cv_s = _gdn_sample(proj, yb, s_s, gdn_par, conv_w, onw3, state_gdn_S,
                                    state_gdn_conv, l, dec_b, dec_seq, dec_rows)
        yc, h_p = _hgrn_prompt(proj, lb_par, onw3, l, bsz, n_chunk)
        yc, h_s = _hgrn_sample(proj, yc, h_s, lb_par, onw3, state_hgrn_S, l, dec_b, dec_seq,
                               dec_rows)

        for lst, val in zip(new_p, (c_p, n_p, m_p[:, 0, :H_A], s_p, cv_p, h_p)):
            lst.append(val)
        for lst, val in zip(new_s, (n_s, m_s[::dec_seq, :H_A], cv_s)):
            lst.append(val)

        if l + 1 < depth:
            x, h = _outproj_mid(ya, yb, yc, x, w_out_bf, l, norm_w3, tm)
        else:
            y_prompt = _outproj_final(
                ya, yb, yc, x, w_out_bf, l, final_nw, tm, 1, chunks, (bsz, chunks, TILE_R, d),
                pl.BlockSpec((bsz, None, TILE_R, d), lambda i: (0, i, 0, 0)))
            y_sample = _outproj_final(
                ya, yb, yc, x, w_out_bf, l, final_nw, tm, n_chunk, n_dec // tm, (n_dec, d),
                pl.BlockSpec((tm, d), lambda i: (i, 0)))

    outs_p = [jnp.stack(a, axis=0) for a in new_p]
    n_all, m_all, cv_all = [jnp.stack(a, axis=0) for a in new_s]
    return (y_prompt.reshape(bsz, seq_len, d), y_sample.reshape(dec_b, dec_seq, d),
            *outs_p, c_s, n_all, m_all, s_s, cv_all, h_s)
```

```python
import functools
import math

import jax
import jax.numpy as jnp
from jax import lax
from jax.experimental import pallas as pl
from jax.experimental.pallas import tpu as pltpu

F32 = jnp.float32
BF16 = jnp.bfloat16

D_MODEL = 2048
HEAD_DIM = 128
H_A, H_B, H_C = 6, 6, 4
W_A, W_B, W_C = H_A * HEAD_DIM, H_B * HEAD_DIM, H_C * HEAD_DIM
MIX = W_A + W_B + W_C
N_META = 16
CONV_W = 4
EPS = 1e-6
NEG_BIG = -1e30
EXP_CLIP = 60.0
QK_SCALE = HEAD_DIM ** -0.5

TILE_R = 128
LANES = 128
CONV_PAD = 8

OFF_AQ = 0
OFF_AK, OFF_AV, OFF_AO = W_A, 2 * W_A, 3 * W_A
OFF_BQ = 4 * W_A
OFF_BK, OFF_BV = OFF_BQ + W_B, OFF_BQ + 2 * W_B
OFF_C = OFF_BQ + 3 * W_B
OFF_ZA = OFF_C + 3 * W_C
OFF_ZB = OFF_ZA + W_A
OFF_ZC = OFF_ZB + W_B
OFF_G = OFF_ZC + W_C
N_PACK = OFF_G + 2 * LANES
G_AI, G_AF, G_BA, G_BB = 0, H_A, 2 * H_A, 2 * H_A + H_B
TN = 1024
INPROJ_MAX_ROWS = 1280
W_SEGMENTS = ((OFF_AQ, OFF_BQ, 0), (OFF_BQ, OFF_C, 2 * H_A), (OFF_C, OFF_G, 2 * H_A + 2 * H_B))
W_ROW_CHUNK = 256

VMEM_LIMIT = 56 * 1024 * 1024


def _sigmoid(x):
    return 1.0 / (1.0 + jnp.exp(-x))


def _silu(x):
    return x * _sigmoid(x)


def _softplus(x):
    return jnp.maximum(x, 0.0) + jnp.log1p(jnp.exp(-jnp.abs(x)))


def _log_sigmoid(x):
    return -_softplus(-x)


def _dot(a, b):
    return jnp.dot(a.astype(BF16), b.astype(BF16), preferred_element_type=F32)


def _dot_nt(a, b):
    return lax.dot_general(a.astype(BF16), b.astype(BF16), (((1,), (1,)), ((), ())),
                           preferred_element_type=F32)


def _dot_exact(sel, x):
    hi = x.astype(BF16)
    r1 = x - hi.astype(F32)
    mid = r1.astype(BF16)
    lo = (r1 - mid.astype(F32)).astype(BF16)
    s = sel.astype(BF16)
    return (jnp.dot(s, hi, preferred_element_type=F32)
            + jnp.dot(s, mid, preferred_element_type=F32)
            + jnp.dot(s, lo, preferred_element_type=F32))


def _log2(n):
    k = int(math.log2(n))
    assert (1 << k) == n
    return k


def _seq_masks(nseq, c):
    r = nseq * c
    row = lax.broadcasted_iota(jnp.int32, (r, r), 0)
    col = lax.broadcasted_iota(jnp.int32, (r, r), 1)
    if nseq == 1:
        same = row >= 0
    else:
        k = _log2(c)
        same = (row >> k) == (col >> k)
    return row, col, same


def _seq_outer(lhs_t, x, nseq, c):
    if nseq == 1:
        return [jnp.dot(lhs_t, x.astype(BF16), preferred_element_type=F32)]
    seq = lax.broadcasted_iota(jnp.int32, (nseq * c, 1), 0) >> _log2(c)
    wide = jnp.concatenate([jnp.where(seq == i, x, 0.0).astype(BF16) for i in range(nseq)], axis=1)
    out = jnp.dot(lhs_t, wide, preferred_element_type=F32)
    n = x.shape[1]
    return [out[:, i * n:(i + 1) * n] for i in range(nseq)]


def _seq_cummax(x, nseq, c):
    r = nseq * c
    pos = lax.broadcasted_iota(jnp.int32, x.shape, 0)
    if nseq > 1:
        pos = pos & (c - 1)
    k = 1
    while k < c:
        x = jnp.where(pos >= k, jnp.maximum(x, pltpu.roll(x, k, axis=0)), x)
        k *= 2
    assert x.shape[0] == r
    return x


def _head_norm_gate(h, onw, z):
    hn = h * lax.rsqrt(jnp.mean(h * h, axis=1, keepdims=True) + EPS)
    return hn * onw * _silu(z)


def _rows(x, i, c):
    return x[i * c:(i + 1) * c]


def _cat_rows(parts):
    return parts[0] if len(parts) == 1 else jnp.concatenate(parts, axis=0)


def _chunk_shift(jj, r0):
    p = jj * TN + r0
    for p0, p1, shift in W_SEGMENTS:
        if p0 <= p and p + W_ROW_CHUNK <= p1:
            return shift
    return None


def _fill_weight_tile(w0_ref, w1_ref, wg_ref, wb_scr, tail_scr, jj):
    for r0 in range(0, TN, W_ROW_CHUNK):
        shift = _chunk_shift(jj, r0)
        dst = slice(r0, r0 + W_ROW_CHUNK)
        if shift is None:
            g0 = OFF_G - jj * TN
            assert r0 <= g0 and g0 + LANES <= r0 + W_ROW_CHUNK
            wb_scr[dst, :] = jnp.zeros((W_ROW_CHUNK, D_MODEL), BF16)
            wb_scr[g0:g0 + LANES, :] = wg_ref[...].astype(BF16)
        elif r0 + shift + W_ROW_CHUNK <= TN:
            wb_scr[dst, :] = w0_ref[r0 + shift:r0 + shift + W_ROW_CHUNK, :].astype(BF16)
        else:
            tail_scr[0:W_ROW_CHUNK, :] = w0_ref[r0:TN, :]
            tail_scr[W_ROW_CHUNK:, :] = w1_ref[...]
            wb_scr[dst, :] = tail_scr[shift:shift + W_ROW_CHUNK, :].astype(BF16)


def _inproj_kernel(h_ref, w0_ref, w1_ref, wg_ref, o_ref, wb_scr, tail_scr):
    j = pl.program_id(0)

    @pl.when(pl.program_id(1) == 0)
    def _():
        for jj in range(N_PACK // TN):
            @pl.when(j == jj)
            def _(jj=jj):
                _fill_weight_tile(w0_ref, w1_ref, wg_ref, wb_scr, tail_scr, jj)

    o_ref[...] = lax.dot_general(h_ref[...], wb_scr[...], (((1,), (1,)), ((), ())),
                                 preferred_element_type=F32)


W_TAIL_ROWS = 32


def _inproj(h, w_t, w_gate_t, layer, tm):
    t = h.shape[0]
    n_tiles = N_PACK // TN
    assert n_tiles - 1 <= (w_t.shape[1] - 1) // TN and W_TAIL_ROWS >= W_SEGMENTS[-1][2]
    tail_per_tile = TN // W_TAIL_ROWS
    last_tail = (w_t.shape[1] - 1) // W_TAIL_ROWS
    return pl.pallas_call(
        _inproj_kernel,
        grid=(n_tiles, t // tm),
        in_specs=[
            pl.BlockSpec((tm, D_MODEL), lambda j, i: (i, 0)),
            pl.BlockSpec((None, TN, D_MODEL), lambda j, i: (layer, j, 0)),
            pl.BlockSpec((None, W_TAIL_ROWS, D_MODEL),
                         lambda j, i: (layer, jnp.minimum((j + 1) * tail_per_tile, last_tail), 0)),
            pl.BlockSpec((None, LANES, D_MODEL), lambda j, i: (layer, 0, 0)),
        ],
        out_specs=pl.BlockSpec((tm, TN), lambda j, i: (i, j)),
        out_shape=jax.ShapeDtypeStruct((t, N_PACK), F32),
        scratch_shapes=[pltpu.VMEM((TN, D_MODEL), BF16),
                        pltpu.VMEM((W_ROW_CHUNK + W_TAIL_ROWS, D_MODEL), F32)],
        compiler_params=pltpu.CompilerParams(
            dimension_semantics=("arbitrary", "arbitrary"), vmem_limit_bytes=VMEM_LIMIT),
        name="inproj",
    )(h, w_t, w_t, w_gate_t)


def _rmsnorm(x, w):
    ms = jnp.mean(x * x, axis=1, keepdims=True)
    return x * lax.rsqrt(ms + EPS) * w


def _prep_kernel(xp_ref, xs_ref, meta_ref, nw_ref, x_ref, h_ref, *, n_chunk, bsz):
    i = pl.program_id(0)
    d = x_ref.shape[1]

    def emit(x):
        x_ref[...] = x
        h_ref[...] = _rmsnorm(x, nw_ref[...]).astype(BF16)

    @pl.when(i == 0)
    def _():
        slot = jnp.concatenate([jnp.zeros((TILE_R - N_META, d), F32), meta_ref[...]], axis=0)
        emit(jnp.concatenate([slot] * bsz, axis=0))

    @pl.when((i > 0) & (i < n_chunk))
    def _():
        emit(xp_ref[...].reshape(bsz * TILE_R, d))

    @pl.when(i >= n_chunk)
    def _():
        emit(xs_ref[...])


def _prep(x_prompt, x_sample, meta, norm_w3, bsz, n_chunk):
    d = x_prompt.shape[-1]
    tm = bsz * TILE_R
    n_dec = x_sample.shape[0] * x_sample.shape[1]
    assert n_dec % tm == 0
    xp = x_prompt.reshape(bsz, n_chunk - 1, TILE_R, d)
    xs = x_sample.reshape(n_dec, d)
    n_blk = n_chunk + n_dec // tm
    t = n_blk * tm
    return pl.pallas_call(
        functools.partial(_prep_kernel, n_chunk=n_chunk, bsz=bsz),
        grid=(n_blk,),
        in_specs=[
            pl.BlockSpec((bsz, None, TILE_R, d), lambda i: (0, jnp.clip(i - 1, 0, n_chunk - 2), 0, 0)),
            pl.BlockSpec((tm, d), lambda i: (jnp.clip(i - n_chunk, 0, n_dec // tm - 1), 0)),
            pl.BlockSpec((N_META, d), lambda i: (0, 0)),
            pl.BlockSpec((None, 1, d), lambda i: (0, 0, 0)),
        ],
        out_specs=[pl.BlockSpec((tm, d), lambda i: (i, 0)),
                   pl.BlockSpec((tm, d), lambda i: (i, 0))],
        out_shape=[jax.ShapeDtypeStruct((t, d), F32), jax.ShapeDtypeStruct((t, d), BF16)],
        compiler_params=pltpu.CompilerParams(
            dimension_semantics=("arbitrary",), vmem_limit_bytes=VMEM_LIMIT),
        name="prep",
    )(xp, xs, meta, norm_w3)


def _outproj_residual(ya_ref, yb_ref, yc_ref, x_ref, w_ref):
    acc = jnp.dot(ya_ref[...], w_ref[0:W_A, :], preferred_element_type=F32)
    acc = acc + jnp.dot(yb_ref[...], w_ref[W_A:W_A + W_B, :], preferred_element_type=F32)
    acc = acc + jnp.dot(yc_ref[...], w_ref[W_A + W_B:MIX, :], preferred_element_type=F32)
    return x_ref[...] + acc


def _outproj_mid_kernel(ya_ref, yb_ref, yc_ref, x_ref, w_ref, nw_ref, x_out, h_out):
    xn = _outproj_residual(ya_ref, yb_ref, yc_ref, x_ref, w_ref)
    x_out[...] = xn
    h_out[...] = _rmsnorm(xn, nw_ref[...]).astype(BF16)


def _outproj_final_kernel(ya_ref, yb_ref, yc_ref, x_ref, w_ref, nw_ref, y_out):
    xn = _outproj_residual(ya_ref, yb_ref, yc_ref, x_ref, w_ref)
    y_out[...] = _rmsnorm(xn, nw_ref[...]).reshape(y_out.shape)


def _outproj_specs(tm, row_blk0, layer, nw_spec):
    rows = lambda i: (row_blk0 + i, 0)
    return [
        pl.BlockSpec((tm, W_A), rows),
        pl.BlockSpec((tm, W_B), rows),
        pl.BlockSpec((tm, W_C), rows),
        pl.BlockSpec((tm, D_MODEL), rows),
        pl.BlockSpec((None, MIX, D_MODEL), lambda i: (layer, 0, 0)),
        nw_spec,
    ]


def _outproj_mid(ya, yb, yc, x, w_out, layer, norm_w3, tm):
    t = x.shape[0]
    blk = pl.BlockSpec((tm, D_MODEL), lambda i: (i, 0))
    nw_spec = pl.BlockSpec((None, 1, D_MODEL), lambda i: (layer + 1, 0, 0))
    return pl.pallas_call(
        _outproj_mid_kernel,
        grid=(t // tm,),
        in_specs=_outproj_specs(tm, 0, layer, nw_spec),
        out_specs=[blk, blk],
        out_shape=[jax.ShapeDtypeStruct((t, D_MODEL), F32), jax.ShapeDtypeStruct((t, D_MODEL), BF16)],
        compiler_params=pltpu.CompilerParams(
            dimension_semantics=("parallel",), vmem_limit_bytes=VMEM_LIMIT),
        name="outproj",
    )(ya, yb, yc, x, w_out, norm_w3)


def _outproj_final(ya, yb, yc, x, w_out, layer, final_nw, tm, row_blk0, n_blk, out_shape, out_spec):
    nw_spec = pl.BlockSpec((1, D_MODEL), lambda i: (0, 0))
    return pl.pallas_call(
        _outproj_final_kernel,
        grid=(n_blk,),
        in_specs=_outproj_specs(tm, row_blk0, layer, nw_spec),
        out_specs=out_spec,
        out_shape=jax.ShapeDtypeStruct(out_shape, F32),
        compiler_params=pltpu.CompilerParams(
            dimension_semantics=("parallel",), vmem_limit_bytes=VMEM_LIMIT),
        name="outproj_final",
    )(ya, yb, yc, x, w_out, final_nw)


class _RowView:
    def __init__(self, ref, r0, n):
        self.ref, self.r0, self.n = ref, r0, n
        self.shape = (n, ref.shape[1])

    def _idx(self, idx):
        if idx is Ellipsis:
            return slice(self.r0, self.r0 + self.n), slice(None)
        rows, cols = idx
        start = self.r0 + (rows.start or 0)
        stop = self.r0 + (self.n if rows.stop is None else rows.stop)
        return slice(start, stop), cols

    def __getitem__(self, idx):
        return self.ref[self._idx(idx)]

    def __setitem__(self, idx, val):
        self.ref[self._idx(idx)] = val


def _tile_views(refs, u):
    return [_RowView(ref, u * TILE_R, TILE_R) for ref in refs]


def _interleave(tiles):
    tiles = list(tiles)
    while tiles:
        alive = []
        for t in tiles:
            try:
                next(t)
                alive.append(t)
            except StopIteration:
                pass
        tiles = alive


def _mlstm_tile(q_ref, k_ref, v_ref, o_ref, z_ref, g_ref, gb_ref, onw_ref, y_ref,
                m_exp, valid, get_c, get_n, put_c, put_n, put_m, nseq, c):
    r = nseq * c
    row, col, same = _seq_masks(nseq, c)
    incl = same & (col <= row)
    g = g_ref[...] + gb_ref[...]
    li = g
    lf = _log_sigmoid(g)
    if valid is not None:
        li = jnp.where(valid, li, NEG_BIG)
        lf = jnp.where(valid, lf, 0.0)
    fcum = _dot_exact(incl.astype(F32), lf)
    if nseq == 1:
        flast = jnp.broadcast_to(fcum[r - 1:r, :], (r, LANES))
    else:
        flast = _dot_exact(same.astype(F32), lf)
    yield
    heads = range(H_A)
    hsl = [slice(h * HEAD_DIM, (h + 1) * HEAD_DIM) for h in heads]

    li_a = pltpu.roll(li, G_AF - G_AI, axis=1)
    m_a = pltpu.roll(m_exp, G_AF, axis=1)
    x = li_a - fcum
    mx = jnp.maximum(m_a, _seq_cummax(x, nseq, c))
    mt_all = fcum + mx
    if nseq == 1:
        m_new = jnp.broadcast_to(mt_all[r - 1:r, :], (r, LANES))
    else:
        m_new = jnp.broadcast_to(mt_all.reshape(nseq, c, LANES)[:, c - 1:c, :],
                                 (nseq, c, LANES)).reshape(r, LANES)
    a_all = jnp.exp(m_a - mx)
    w_all = jnp.exp(flast - fcum + li_a - m_new)
    as_all = jnp.exp(flast + m_a - m_new)
    low_all = jnp.exp(jnp.minimum(-mt_all, EXP_CLIP))
    put_m(pltpu.roll(m_new, LANES - G_AF, axis=1))
    x_t = x.T
    col_of = lambda arr, h: arr[:, G_AF + h:G_AF + h + 1]
    a = [col_of(a_all, h) for h in heads]
    w = [col_of(w_all, h) for h in heads]
    a_s = [col_of(as_all, h) for h in heads]
    low = [col_of(low_all, h) for h in heads]
    pexp = [jnp.exp(jnp.where(incl, x_t[G_AF + h:G_AF + h + 1, :] - col_of(mx, h), NEG_BIG))
            for h in heads]
    yield

    s = [_dot_nt(q_ref[:, hs], k_ref[:, hs] * QK_SCALE) for hs in hsl]
    yield
    p = [pexp[h] * s[h] for h in heads]
    pv = [_dot(p[h], v_ref[:, hsl[h]]) for h in heads]
    yield
    qc = [_cat_rows([_dot(_rows(q_ref[:, hsl[h]], i, c), get_c(i, h)) for i in range(nseq)])
          for h in heads]
    yield
    for h in heads:
        hs = hsl[h]
        qh = q_ref[:, hs]
        psum = jnp.sum(p[h], axis=1, keepdims=True)
        qn = _cat_rows([jnp.sum(_rows(qh, i, c) * get_n(i, h), axis=1, keepdims=True)
                        for i in range(nseq)])
        num = a[h] * qc[h] + pv[h]
        den = a[h] * qn + psum
        hh = num / jnp.maximum(jnp.abs(den), low[h])
        hh = hh * _sigmoid(o_ref[:, hs])
        y_ref[:, hs] = _head_norm_gate(hh, onw_ref[:, hs], z_ref[:, hs]).astype(BF16)
    yield

    for h in heads:
        hs = hsl[h]
        kh = k_ref[:, hs] * QK_SCALE
        wk = w[h] * kh
        upd = _seq_outer(kh.T.astype(BF16), w[h] * v_ref[:, hs], nseq, c)
        for i in range(nseq):
            a_i = a_s[h][i * c:i * c + 1, :]
            put_c(i, h, a_i * get_c(i, h) + upd[i])
            put_n(i, h, a_i * get_n(i, h) + jnp.sum(_rows(wk, i, c), axis=0, keepdims=True))


def _mlstm_prompt_kernel(q_ref, k_ref, v_ref, o_ref, z_ref, g_ref, gb_ref, onw_ref,
                         y_ref, c_out, n_out, m_out, c_scr, n_scr, m_scr):
    j = pl.program_id(0)
    bsz = c_scr.shape[0]

    @pl.when(j == 0)
    def _():
        c_scr[...] = jnp.zeros_like(c_scr)
        n_scr[...] = jnp.zeros_like(n_scr)
        m_scr[...] = jnp.zeros_like(m_scr)

    tok = lax.broadcasted_iota(jnp.int32, (TILE_R, LANES), 0) + j * TILE_R
    valid = tok >= TILE_R - N_META

    def tile(u):
        q, k, v, o, z, g, y = _tile_views((q_ref, k_ref, v_ref, o_ref, z_ref, g_ref, y_ref), u)
        m_exp = jnp.broadcast_to(m_scr[u, 0:1, :], (TILE_R, LANES))

        def put_c(i, h, val):
            c_scr[u, h] = val

        def put_n(i, h, val):
            n_scr[u, h:h + 1, :] = val

        def put_m(val):
            m_scr[u] = val[0:8, :]

        return _mlstm_tile(q, k, v, o, z, g, gb_ref, onw_ref, y, m_exp, valid,
                           lambda i, h: c_scr[u, h], lambda i, h: n_scr[u, h:h + 1, :],
                           put_c, put_n, put_m, 1, TILE_R)

    _interleave(tile(u) for u in range(bsz))

    @pl.when(j == pl.num_programs(0) - 1)
    def _():
        c_out[...] = c_scr[...]
        n_out[...] = n_scr[:, 0:H_A, :]
        m_out[...] = m_scr[:, 0:1, :]


def _mlstm_sample_kernel(q_ref, k_ref, v_ref, o_ref, z_ref, g_ref, gb_ref, onw_ref,
                         c_in, n_in, m_in, y_ref, c_out, n_out, m_out, *, c):
    nseq = TILE_R // c

    def put_c(i, h, val):
        c_out[i, h] = val

    def put_n(i, h, val):
        n_out[i, h:h + 1, :] = val

    def put_m(val):
        m_out[...] = val

    _interleave([_mlstm_tile(
        q_ref, k_ref, v_ref, o_ref, z_ref, g_ref, gb_ref, onw_ref, y_ref, m_in[...], None,
        lambda i, h: c_in[i, h], lambda i, h: n_in[i, h:h + 1, :], put_c, put_n, put_m, nseq, c)])


def _col_spec(width, off, rows, n_rows=TILE_R):
    assert off % width == 0
    blk = off // width
    return pl.BlockSpec((n_rows, width), lambda *ids: (rows(*ids), blk))


def _param_spec(width, off, layer):
    assert off % width == 0
    blk = off // width
    return pl.BlockSpec((None, 1, width), lambda *ids: (layer, 0, blk))


def _whole(*shape):
    return pl.BlockSpec(shape, lambda j: (0,) * len(shape))


def _mlstm_prompt(proj, gate_bias, onw3, layer, bsz, n_chunk):
    t = proj.shape[0]
    tm = bsz * TILE_R
    rows = lambda j: j
    in_specs = [_col_spec(W_A, off, rows, tm) for off in (OFF_AQ, OFF_AK, OFF_AV, OFF_AO, OFF_ZA)]
    in_specs += [_col_spec(LANES, OFF_G, rows, tm), _param_spec(LANES, 0, layer),
                 _param_spec(W_A, 0, layer)]
    return pl.pallas_call(
        _mlstm_prompt_kernel,
        grid=(n_chunk,),
        in_specs=in_specs,
        out_specs=[
            pl.BlockSpec((tm, W_A), lambda j: (j, 0)),
            _whole(bsz, H_A, HEAD_DIM, HEAD_DIM),
            _whole(bsz, H_A, HEAD_DIM),
            _whole(bsz, 1, LANES),
        ],
        out_shape=[
            jax.ShapeDtypeStruct((t, W_A), BF16),
            jax.ShapeDtypeStruct((bsz, H_A, HEAD_DIM, HEAD_DIM), F32),
            jax.ShapeDtypeStruct((bsz, H_A, HEAD_DIM), F32),
            jax.ShapeDtypeStruct((bsz, 1, LANES), F32),
        ],
        scratch_shapes=[pltpu.VMEM((bsz, H_A, HEAD_DIM, HEAD_DIM), F32),
                        pltpu.VMEM((bsz, 8, HEAD_DIM), F32), pltpu.VMEM((bsz, 8, LANES), F32)],
        compiler_params=pltpu.CompilerParams(
            dimension_semantics=("arbitrary",), vmem_limit_bytes=VMEM_LIMIT),
        name="mlstm_prompt",
    )(proj, proj, proj, proj, proj, proj, gate_bias, onw3)


def _in_place(operands, n_blocked, targets):
    specs, aliases = [], {}
    operands = list(operands)
    for out_idx, arr in targets.items():
        if arr is not None:
            aliases[len(operands)] = out_idx
            operands.append(arr)
            specs.append(pl.BlockSpec(memory_space=pl.ANY))
    n_extra = len(specs)

    def strip(kernel):
        def body(*refs):
            kernel(*refs[:n_blocked], *refs[n_blocked + n_extra:])
        return body

    return operands, specs, aliases, strip


def _mlstm_sample(proj, y_prev, c_stack, gate_bias, onw3, c_state, n_state, m_rows, layer,
                  dec_b, c, rows):
    nseq = TILE_R // c
    depth = c_state.shape[0]
    in_specs = [_col_spec(W_A, off, rows) for off in (OFF_AQ, OFF_AK, OFF_AV, OFF_AO, OFF_ZA)]
    in_specs += [_col_spec(LANES, OFF_G, rows), _param_spec(LANES, 0, layer),
                 _param_spec(W_A, 0, layer)]
    in_specs += [
        pl.BlockSpec((None, nseq, H_A, HEAD_DIM, HEAD_DIM), lambda i: (layer, i, 0, 0, 0)),
        pl.BlockSpec((None, nseq, H_A, HEAD_DIM), lambda i: (layer, i, 0, 0)),
        pl.BlockSpec((None, TILE_R, LANES), lambda i: (layer, i, 0)),
    ]
    operands = (proj, proj, proj, proj, proj, proj, gate_bias, onw3, c_state, n_state, m_rows)
    operands, extra, aliases, strip = _in_place(operands, len(in_specs), {0: y_prev, 1: c_stack})
    return pl.pallas_call(
        strip(functools.partial(_mlstm_sample_kernel, c=c)),
        grid=(dec_b // nseq,),
        in_specs=in_specs + extra,
        out_specs=[
            pl.BlockSpec((TILE_R, W_A), lambda i: (rows(i), 0)),
            pl.BlockSpec((None, nseq, H_A, HEAD_DIM, HEAD_DIM), lambda i: (layer, i, 0, 0, 0)),
            pl.BlockSpec((nseq, H_A, HEAD_DIM), lambda i: (i, 0, 0)),
            pl.BlockSpec((TILE_R, LANES), lambda i: (i, 0)),
        ],
        out_shape=[
            jax.ShapeDtypeStruct(y_prev.shape, BF16),
            jax.ShapeDtypeStruct((depth, dec_b, H_A, HEAD_DIM, HEAD_DIM), F32),
            jax.ShapeDtypeStruct((dec_b, H_A, HEAD_DIM), F32),
            jax.ShapeDtypeStruct((dec_b * c, LANES), F32),
        ],
        input_output_aliases=aliases,
        compiler_params=pltpu.CompilerParams(
            dimension_semantics=("parallel",), vmem_limit_bytes=VMEM_LIMIT),
        name="mlstm_sample",
    )(*operands)


def _l2norm(x):
    return x * lax.rsqrt(jnp.sum(x * x, axis=1, keepdims=True) + EPS)


INV_LEAF = 16


def _unit_lower_inverse(a_list, row, col, c):
    leaf = min(INV_LEAF, c)
    kl = _log2(leaf)
    leaf_mask = (row >> kl) == (col >> kl)
    eye = jnp.where(row == col, 1.0, 0.0)
    p = [jnp.where(leaf_mask, -a, 0.0) for a in a_list]
    t = [eye + n for n in p]
    for _ in range(kl - 1):
        p = [_dot(x, x) for x in p]
        yield
        t = [ti + _dot(pi, ti) for pi, ti in zip(p, t)]
        yield
    w = leaf
    while w < c:
        kw = _log2(w)
        lower_left = (((row >> (kw + 1)) == (col >> (kw + 1)))
                      & (((row >> kw) & 1) == 1) & (((col >> kw) & 1) == 0))
        left = [_dot(ti, jnp.where(lower_left, a, 0.0)) for ti, a in zip(t, a_list)]
        yield
        t = [ti - _dot(li, ti) for li, ti in zip(left, t)]
        yield
        w *= 2
    return t


def _gdn_tile(conv_win, cw_ref, z_ref, g_ref, gp_ref, onw_ref, y_ref, get_s, put_s, nseq, c):
    r = nseq * c
    row, col, same = _seq_masks(nseq, c)
    incl = same & (col <= row)
    strict = same & (col < row)
    gp = gp_ref[...]
    graw = g_ref[...]
    gdec = -jnp.exp(gp[1:2, :]) * _softplus(graw + gp[0:1, :])
    beta = _sigmoid(graw)
    gcum = _dot_exact(incl.astype(F32), gdec)
    if nseq == 1:
        glast = jnp.broadcast_to(gcum[r - 1:r, :], (r, LANES))
    else:
        glast = _dot_exact(same.astype(F32), gdec)
    yield
    gcum_t = gcum.T

    heads = range(H_B)
    hsl = [slice(h * HEAD_DIM, (h + 1) * HEAD_DIM) for h in heads]

    def conv(lo):
        acc = conv_win(0, slice(lo, lo + HEAD_DIM)) * cw_ref[0:1, lo:lo + HEAD_DIM]
        for jj in range(1, CONV_W):
            acc = acc + conv_win(jj, slice(lo, lo + HEAD_DIM)) * cw_ref[jj:jj + 1, lo:lo + HEAD_DIM]
        return _silu(acc)

    q = [_l2norm(conv(hs.start)) * QK_SCALE for hs in hsl]
    k = [_l2norm(conv(W_B + hs.start)) for hs in hsl]
    v = [conv(2 * W_B + hs.start) for hs in hsl]
    yield
    gc = [gcum[:, G_BA + h:G_BA + h + 1] for h in heads]
    gl = [glast[:, G_BA + h:G_BA + h + 1] for h in heads]
    bc = [beta[:, G_BB + h:G_BB + h + 1] for h in heads]
    decay = [jnp.exp(jnp.where(incl, gc[h] - gcum_t[G_BA + h:G_BA + h + 1, :], NEG_BIG))
             for h in heads]
    kk = [_dot_nt(k[h], k[h]) for h in heads]
    qk = [_dot_nt(q[h], k[h]) for h in heads]
    yield
    amat = [jnp.where(strict, bc[h] * decay[h] * kk[h], 0.0) for h in heads]
    tinv = yield from _unit_lower_inverse(amat, row, col, c)
    sol = [_dot(tinv[h], jnp.concatenate([bc[h] * v[h], (bc[h] * jnp.exp(gc[h])) * k[h]], axis=1))
           for h in heads]
    yield
    both = [[_dot(jnp.concatenate([_rows(sol[h][:, HEAD_DIM:], i, c), _rows(q[h], i, c)], axis=0),
                  get_s(i, h)) for i in range(nseq)] for h in heads]
    wks = [_cat_rows([both[h][i][:c] for i in range(nseq)]) for h in heads]
    qs = [_cat_rows([both[h][i][c:] for i in range(nseq)]) for h in heads]
    yield
    wmat = [sol[h][:, :HEAD_DIM] - wks[h] for h in heads]
    aw = [_dot(decay[h] * qk[h], wmat[h]) for h in heads]
    yield
    for h in heads:
        hs = hsl[h]
        oh = jnp.exp(gc[h]) * qs[h] + aw[h]
        y_ref[:, hs] = _head_norm_gate(oh, onw_ref[:, hs], z_ref[:, hs]).astype(BF16)
    yield
    for h in heads:
        kd_t = (k[h] * jnp.exp(gl[h] - gc[h])).T.astype(BF16)
        e_last = jnp.exp(gl[h])
        upd = _seq_outer(kd_t, wmat[h], nseq, c)
        for i in range(nseq):
            put_s(i, h, e_last[i * c:i * c + 1, :] * get_s(i, h) + upd[i])


def _gdn_prompt_kernel(q_ref, k_ref, v_ref, z_ref, g_ref, gp_ref, cw_ref, onw_ref,
                       y_ref, s_out, conv_out, s_scr, ext_scr):
    j = pl.program_id(0)
    bsz = s_scr.shape[0]

    @pl.when(j == 0)
    def _():
        s_scr[...] = jnp.zeros_like(s_scr)
        ext_scr[:, 0:CONV_PAD, :] = jnp.zeros((bsz, CONV_PAD, 3 * W_B), F32)

    def tile(u):
        q, k, v, z, g, y = _tile_views((q_ref, k_ref, v_ref, z_ref, g_ref, y_ref), u)
        ext_scr[u, CONV_PAD:, 0:W_B] = q[...]
        ext_scr[u, CONV_PAD:, W_B:2 * W_B] = k[...]
        ext_scr[u, CONV_PAD:, 2 * W_B:] = v[...]

        def conv_win(jj, cols):
            start = CONV_PAD - (CONV_W - 1) + jj
            return ext_scr[u, start:start + TILE_R, cols]

        def put_s(i, h, val):
            s_scr[u, h] = val

        return _gdn_tile(conv_win, cw_ref, z, g, gp_ref, onw_ref, y,
                         lambda i, h: s_scr[u, h], put_s, 1, TILE_R)

    _interleave(tile(u) for u in range(bsz))

    @pl.when(j == pl.num_programs(0) - 1)
    def _():
        s_out[...] = s_scr[...]
        conv_out[...] = ext_scr[:, TILE_R + CONV_PAD - (CONV_W - 1):TILE_R + CONV_PAD, :]

    ext_scr[:, 0:CONV_PAD, :] = ext_scr[:, TILE_R:TILE_R + CONV_PAD, :]


def _gdn_sample_kernel(q_ref, k_ref, v_ref, z_ref, g_ref, gp_ref, cw_ref, onw_ref,
                       s_in, conv_in, y_ref, s_out, conv_out, ext_scr, *, c):
    nseq = TILE_R // c
    hist = CONV_W - 1
    ext_scr[:, CONV_PAD - hist:CONV_PAD, :] = conv_in[...]
    ext_scr[:, CONV_PAD:, 0:W_B] = q_ref[...].reshape(nseq, c, W_B)
    ext_scr[:, CONV_PAD:, W_B:2 * W_B] = k_ref[...].reshape(nseq, c, W_B)
    ext_scr[:, CONV_PAD:, 2 * W_B:] = v_ref[...].reshape(nseq, c, W_B)

    def conv_win(jj, cols):
        start = CONV_PAD - hist + jj
        return ext_scr[:, start:start + c, cols].reshape(TILE_R, cols.stop - cols.start)

    def put_s(i, h, val):
        s_out[i, h] = val

    _interleave([_gdn_tile(conv_win, cw_ref, z_ref, g_ref, gp_ref, onw_ref, y_ref,
                           lambda i, h: s_in[i, h], put_s, nseq, c)])
    conv_out[...] = ext_scr[:, CONV_PAD + c - hist:CONV_PAD + c, :]


def _gdn_prompt(proj, gdn_par, conv_w, onw3, layer, bsz, n_chunk):
    t = proj.shape[0]
    tm = bsz * TILE_R
    rows = lambda j: j
    in_specs = [_col_spec(W_B, off, rows, tm) for off in (OFF_BQ, OFF_BK, OFF_BV, OFF_ZB)]
    in_specs += [
        _col_spec(LANES, OFF_G, rows, tm),
        pl.BlockSpec((None, 8, LANES), lambda j: (layer, 0, 0)),
        pl.BlockSpec((None, CONV_W, 3 * W_B), lambda j: (layer, 0, 0)),
        _param_spec(W_B, W_A, layer),
    ]
    return pl.pallas_call(
        _gdn_prompt_kernel,
        grid=(n_chunk,),
        in_specs=in_specs,
        out_specs=[
            pl.BlockSpec((tm, W_B), lambda j: (j, 0)),
            _whole(bsz, H_B, HEAD_DIM, HEAD_DIM),
            _whole(bsz, CONV_W - 1, 3 * W_B),
        ],
        out_shape=[
            jax.ShapeDtypeStruct((t, W_B), BF16),
            jax.ShapeDtypeStruct((bsz, H_B, HEAD_DIM, HEAD_DIM), F32),
            jax.ShapeDtypeStruct((bsz, CONV_W - 1, 3 * W_B), F32),
        ],
        scratch_shapes=[pltpu.VMEM((bsz, H_B, HEAD_DIM, HEAD_DIM), F32),
                        pltpu.VMEM((bsz, TILE_R + CONV_PAD, 3 * W_B), F32)],
        compiler_params=pltpu.CompilerParams(
            dimension_semantics=("arbitrary",), vmem_limit_bytes=VMEM_LIMIT),
        name="gdn_prompt",
    )(proj, proj, proj, proj, proj, gdn_par, conv_w, onw3)


def _gdn_sample(proj, y_prev, s_stack, gdn_par, conv_w, onw3, s_state, conv_state, layer,
                dec_b, c, rows):
    nseq = TILE_R // c
    depth = s_state.shape[0]
    in_specs = [_col_spec(W_B, off, rows) for off in (OFF_BQ, OFF_BK, OFF_BV, OFF_ZB)]
    in_specs += [
        _col_spec(LANES, OFF_G, rows),
        pl.BlockSpec((None, 8, LANES), lambda i: (layer, 0, 0)),
        pl.BlockSpec((None, CONV_W, 3 * W_B), lambda i: (layer, 0, 0)),
        _param_spec(W_B, W_A, layer),
        pl.BlockSpec((None, nseq, H_B, HEAD_DIM, HEAD_DIM), lambda i: (layer, i, 0, 0, 0)),
        pl.BlockSpec((None, nseq, CONV_W - 1, 3 * W_B), lambda i: (layer, i, 0, 0)),
    ]
    operands = (proj, proj, proj, proj, proj, gdn_par, conv_w, onw3, s_state, conv_state)
    operands, extra, aliases, strip = _in_place(operands, len(in_specs), {0: y_prev, 1: s_stack})
    return pl.pallas_call(
        strip(functools.partial(_gdn_sample_kernel, c=c)),
        grid=(dec_b // nseq,),
        in_specs=in_specs + extra,
        out_specs=[
            pl.BlockSpec((TILE_R, W_B), lambda i: (rows(i), 0)),
            pl.BlockSpec((None, nseq, H_B, HEAD_DIM, HEAD_DIM), lambda i: (layer, i, 0, 0, 0)),
            pl.BlockSpec((nseq, CONV_W - 1, 3 * W_B), lambda i: (i, 0, 0)),
        ],
        out_shape=[
            jax.ShapeDtypeStruct(y_prev.shape, BF16),
            jax.ShapeDtypeStruct((depth, dec_b, H_B, HEAD_DIM, HEAD_DIM), F32),
            jax.ShapeDtypeStruct((dec_b, CONV_W - 1, 3 * W_B), F32),
        ],
        scratch_shapes=[pltpu.VMEM((nseq, CONV_PAD + c, 3 * W_B), F32)],
        input_output_aliases=aliases,
        compiler_params=pltpu.CompilerParams(
            dimension_semantics=("parallel",), vmem_limit_bytes=VMEM_LIMIT),
        name="gdn_sample",
    )(*operands)


def _hgrn_tile(q_ref, f_ref, i_ref, z_ref, lbp_ref, onw_ref, y_ref, get_s, put_s,
               nseq, c, layer):
    r = nseq * c
    row, col, same = _seq_masks(nseq, c)
    incl = same & (col <= row)
    rowv = lax.broadcasted_iota(jnp.int32, (r, 1), 0)
    lbp = lbp_ref[...]
    e = jnp.exp(lbp - jnp.max(lbp, axis=0, keepdims=True))
    sm = e / jnp.sum(e, axis=0, keepdims=True)
    lb = jnp.zeros((1, W_C), F32)
    for l in range(1, layer + 1):
        lb = lb + sm[l:l + 1, :]

    cf = f_ref[...]
    logf = _log_sigmoid(cf) + jnp.log1p(lb * jnp.exp(jnp.minimum(-cf, EXP_CLIP)))
    kall = (1.0 - lb) * _sigmoid(-cf)
    bcum = _dot_exact(incl.astype(F32), logf)
    if nseq == 1:
        blast = jnp.broadcast_to(bcum[r - 1:r, :], (r, W_C))
    else:
        blast = _dot_exact(same.astype(F32), logf)

    levels = []
    w = c // 2
    while w >= 1:
        levels.append(w)
        w //= 2
    rowf = lax.broadcasted_iota(jnp.int32, (r, W_C), 0)
    refs = []
    for w in levels:
        if 2 * w >= 8:
            refs.append(_cat_rows([
                jnp.broadcast_to(bcum[g * 2 * w + w - 1:g * 2 * w + w, :], (2 * w, W_C))
                for g in range(r // (2 * w))]))
        else:
            pos = rowf & (2 * w - 1)
            ref = bcum
            for off in range(2 * w):
                if off != w - 1:
                    ref = jnp.where(pos == off, pltpu.roll(bcum, (off - (w - 1)) % r, axis=0), ref)
            refs.append(ref)

    heads = range(H_C)
    hsl = [slice(h * HEAD_DIM, (h + 1) * HEAD_DIM) for h in heads]
    q = [_silu(q_ref[:, hs]) for hs in hsl]
    k = [kall[:, hs] for hs in hsl]
    b = [bcum[:, hs] for hs in hsl]
    diag = row == col
    yield
    amat = [jnp.where(diag, _dot_nt(q[h], k[h]), 0.0) for h in heads]
    for w, ref_all in zip(levels, refs):
        yield
        lw = _log2(w)
        tgt = ((rowv >> lw) & 1) == 1
        pair = (row >> (lw + 1)) == (col >> (lw + 1))
        qp = [q[h] * jnp.exp(jnp.where(tgt, b[h] - ref_all[:, hsl[h]], NEG_BIG)) for h in heads]
        kp = [k[h] * jnp.exp(jnp.where(tgt, NEG_BIG, ref_all[:, hsl[h]] - b[h])) for h in heads]
        amat = [amat[h] + jnp.where(pair, _dot_nt(qp[h], kp[h]), 0.0) for h in heads]
    yield
    qs = [_cat_rows([_dot(_rows(q[h] * jnp.exp(b[h]), i, c), get_s(i, h)) for i in range(nseq)])
          for h in heads]
    av = [_dot(amat[h], i_ref[:, hsl[h]]) for h in heads]
    yield
    for h in heads:
        hs = hsl[h]
        y_ref[:, hs] = _head_norm_gate(qs[h] + av[h], onw_ref[:, hs], z_ref[:, hs]).astype(BF16)
    yield
    for h in heads:
        hs = hsl[h]
        vh = i_ref[:, hs]
        bl = blast[:, hs]
        kd_t = (k[h] * jnp.exp(bl - b[h])).T.astype(BF16)
        eb_t = jnp.exp(bl).T
        upd = _seq_outer(kd_t, vh, nseq, c)
        for i in range(nseq):
            put_s(i, h, eb_t[:, i * c:i * c + 1] * get_s(i, h) + upd[i])


class _ColView:
    def __init__(self, parts):
        self.parts = []
        pos = 0
        for ref, lo, hi in parts:
            self.parts.append((pos, ref, lo, hi - lo))
            pos += hi - lo
        self.width = pos

    def __getitem__(self, idx):
        if idx is Ellipsis:
            rows, cols = slice(None), slice(0, self.width)
        else:
            rows, cols = idx
            cols = slice(cols.start or 0, self.width if cols.stop is None else cols.stop)
        out = []
        for pos, ref, lo, n in self.parts:
            a, b = max(cols.start, pos), min(cols.stop, pos + n)
            if a < b:
                out.append(ref[rows, lo + a - pos:lo + b - pos])
        return out[0] if len(out) == 1 else jnp.concatenate(out, axis=1)


def _hgrn_views(ca_ref, cb_ref, zc_ref):
    two = 2 * W_C - ca_ref.shape[1]
    q = _ColView([(ca_ref, 0, W_C)])
    f = _ColView([(ca_ref, W_C, ca_ref.shape[1]), (cb_ref, 0, two)])
    i = _ColView([(cb_ref, two, two + W_C)])
    z = _ColView([(zc_ref, 0, W_C)])
    return q, f, i, z


def _hgrn_prompt_kernel(ca_ref, cb_ref, zc_ref, lbp_ref, onw_ref, y_ref, s_out, s_scr,
                        *, layer):
    j = pl.program_id(0)
    bsz = s_scr.shape[0]

    @pl.when(j == 0)
    def _():
        s_scr[...] = jnp.zeros_like(s_scr)

    def tile(u):
        ca, cb, zc, y = _tile_views((ca_ref, cb_ref, zc_ref, y_ref), u)

        def put_s(i, h, val):
            s_scr[u, h] = val

        return _hgrn_tile(*_hgrn_views(ca, cb, zc), lbp_ref, onw_ref, y,
                          lambda i, h: s_scr[u, h], put_s, 1, TILE_R, layer)

    _interleave(tile(u) for u in range(bsz))

    @pl.when(j == pl.num_programs(0) - 1)
    def _():
        s_out[...] = s_scr[...]


def _hgrn_sample_kernel(ca_ref, cb_ref, zc_ref, lbp_ref, onw_ref, s_in, y_ref, s_out,
                        *, c, layer):
    def put_s(i, h, val):
        s_out[i, h] = val

    _interleave([_hgrn_tile(*_hgrn_views(ca_ref, cb_ref, zc_ref), lbp_ref, onw_ref, y_ref,
                            lambda i, h: s_in[i, h], put_s, TILE_R // c, c, layer)])


HGRN_BLK = W_A
assert OFF_C % HGRN_BLK == 0 and 3 * W_C == 2 * HGRN_BLK and OFF_ZC % HGRN_BLK == 0


def _hgrn_prompt(proj, lb_par, onw3, layer, bsz, n_chunk):
    t = proj.shape[0]
    tm = bsz * TILE_R
    depth = lb_par.shape[0]
    rows = lambda j: j
    in_specs = [_col_spec(HGRN_BLK, off, rows, tm) for off in (OFF_C, OFF_C + HGRN_BLK, OFF_ZC)]
    in_specs += [pl.BlockSpec((depth, W_C), lambda j: (0, 0)),
                 _param_spec(W_C, W_A + W_B, layer)]
    return pl.pallas_call(
        functools.partial(_hgrn_prompt_kernel, layer=layer),
        grid=(n_chunk,),
        in_specs=in_specs,
        out_specs=[
            pl.BlockSpec((tm, W_C), lambda j: (j, 0)),
            _whole(bsz, H_C, HEAD_DIM, HEAD_DIM),
        ],
        out_shape=[
            jax.ShapeDtypeStruct((t, W_C), BF16),
            jax.ShapeDtypeStruct((bsz, H_C, HEAD_DIM, HEAD_DIM), F32),
        ],
        scratch_shapes=[pltpu.VMEM((bsz, H_C, HEAD_DIM, HEAD_DIM), F32)],
        compiler_params=pltpu.CompilerParams(
            dimension_semantics=("arbitrary",), vmem_limit_bytes=VMEM_LIMIT),
        name="hgrn_prompt",
    )(proj, proj, proj, lb_par, onw3)


def _hgrn_sample(proj, y_prev, s_stack, lb_par, onw3, s_state, layer, dec_b, c, rows):
    nseq = TILE_R // c
    depth = lb_par.shape[0]
    in_specs = [_col_spec(HGRN_BLK, off, rows) for off in (OFF_C, OFF_C + HGRN_BLK, OFF_ZC)]
    in_specs += [
        pl.BlockSpec((depth, W_C), lambda i: (0, 0)),
        _param_spec(W_C, W_A + W_B, layer),
        pl.BlockSpec((None, nseq, H_C, HEAD_DIM, HEAD_DIM), lambda i: (layer, i, 0, 0, 0)),
    ]
    operands = (proj, proj, proj, lb_par, onw3, s_state)
    operands, extra, aliases, strip = _in_place(operands, len(in_specs), {0: y_prev, 1: s_stack})
    return pl.pallas_call(
        strip(functools.partial(_hgrn_sample_kernel, c=c, layer=layer)),
        grid=(dec_b // nseq,),
        in_specs=in_specs + extra,
        out_specs=[
            pl.BlockSpec((TILE_R, W_C), lambda i: (rows(i), 0)),
            pl.BlockSpec((None, nseq, H_C, HEAD_DIM, HEAD_DIM), lambda i: (layer, i, 0, 0, 0)),
        ],
        out_shape=[
            jax.ShapeDtypeStruct(y_prev.shape, BF16),
            jax.ShapeDtypeStruct((depth, dec_b, H_C, HEAD_DIM, HEAD_DIM), F32),
        ],
        input_output_aliases=aliases,
        compiler_params=pltpu.CompilerParams(
            dimension_semantics=("parallel",), vmem_limit_bytes=VMEM_LIMIT),
        name="hgrn_sample",
    )(*operands)


def _gate_weight(w_t):
    ai0 = 4 * W_A
    ba0 = ai0 + 2 * H_A + 3 * W_B
    n_gate = 2 * H_A + 2 * H_B
    pad = jnp.zeros((w_t.shape[0], LANES - n_gate, w_t.shape[2]), w_t.dtype)
    return jnp.concatenate([w_t[:, ai0:ai0 + 2 * H_A], w_t[:, ba0:ba0 + 2 * H_B], pad], axis=1)


def _lane_row(depth, pieces):
    row = jnp.zeros((depth, LANES), F32)
    for off, val in pieces:
        row = lax.dynamic_update_slice(row, val.astype(F32), (0, off))
    return row[:, None, :]


def kernel(x_prompt, x_sample, state_mlstm_C, state_mlstm_n, state_mlstm_m, state_gdn_S,
           state_gdn_conv, state_hgrn_S, meta_tokens, norm_w, w_in, mlstm_gate_b, gdn_A_log,
           gdn_dt_bias, gdn_conv_w, hgrn_lower_bounds, out_norm_w, w_out, final_norm_w):
    bsz, seq_len, d = x_prompt.shape
    dec_b, dec_seq, _ = x_sample.shape
    depth = w_in.shape[0]
    assert d == D_MODEL and seq_len % TILE_R == 0 and TILE_R % dec_seq == 0
    assert w_in.shape[2] == W_SEGMENTS[-1][1] + W_SEGMENTS[-1][2]
    nseq = TILE_R // dec_seq
    assert dec_b % nseq == 0

    chunks = seq_len // TILE_R
    n_chunk = chunks + 1
    tm = bsz * TILE_R
    n_dec = dec_b * dec_seq
    dec_blk0 = n_chunk * bsz

    def dec_rows(i):
        return dec_blk0 + i

    w_t = jnp.swapaxes(w_in.astype(F32), 1, 2)
    w_gate = _gate_weight(w_t)
    w_out_bf = w_out.astype(BF16)
    norm_w3 = norm_w.astype(F32)[:, None, :]
    onw3 = out_norm_w.astype(F32)[:, None, :]
    final_nw = final_norm_w.astype(F32)[None, :]
    gate_bias = _lane_row(depth, [(G_AI, mlstm_gate_b[:, 0]), (G_AF, mlstm_gate_b[:, 1])])
    gdn_par = jnp.concatenate([
        _lane_row(depth, [(G_BA, gdn_dt_bias)]), _lane_row(depth, [(G_BA, gdn_A_log)]),
        jnp.zeros((depth, 6, LANES), F32)], axis=1)
    conv_w = gdn_conv_w.astype(F32)
    lb_par = hgrn_lower_bounds.astype(F32)
    m_rows = jnp.repeat(
        jnp.pad(state_mlstm_m.astype(F32), ((0, 0), (0, 0), (0, LANES - H_A))), dec_seq, axis=1)

    x, h = _prep(x_prompt.astype(F32), x_sample.astype(F32), meta_tokens.astype(F32), norm_w3,
                 bsz, n_chunk)
    t_all = x.shape[0]
    tm_in = max(m for m in range(16, INPROJ_MAX_ROWS + 1, 16) if t_all % m == 0)

    new_p = [[] for _ in range(6)]
    new_s = [[] for _ in range(3)]
    c_s = s_s = h_s = None
    y_prompt = y_sample = None
    for l in range(depth):
        proj = _inproj(h, w_t, w_gate, l, tm_in)

        ya, c_p, n_p, m_p = _mlstm_prompt(proj, gate_bias, onw3, l, bsz, n_chunk)
        ya, c_s, n_s, m_s = _mlstm_sample(proj, ya, c_s, gate_bias, onw3, state_mlstm_C,
                                          state_mlstm_n, m_rows, l, dec_b, dec_seq, dec_rows)
        yb, s_p, cv_p = _gdn_prompt(proj, gdn_par, conv_w, onw3, l, bsz, n_chunk)
        yb, s_s, cv_s = _gdn_sample(proj, yb, s_s, gdn_par, conv_w, onw3, state_gdn_S,
                                    state_gdn_conv, l, dec_b, dec_seq, dec_rows)
        yc, h_p = _hgrn_prompt(proj, lb_par, onw3, l, bsz, n_chunk)
        yc, h_s = _hgrn_sample(proj, yc, h_s, lb_par, onw3, state_hgrn_S, l, dec_b, dec_seq,
                               dec_rows)

        for lst, val in zip(new_p, (c_p, n_p, m_p[:, 0, :H_A], s_p, cv_p, h_p)):
            lst.append(val)
        for lst, val in zip(new_s, (n_s, m_s[::dec_seq, :H_A], cv_s)):
            lst.append(val)

        if l + 1 < depth:
            x, h = _outproj_mid(ya, yb, yc, x, w_out_bf, l, norm_w3, tm)
        else:
            y_prompt = _outproj_final(
                ya, yb, yc, x, w_out_bf, l, final_nw, tm, 1, chunks, (bsz, chunks, TILE_R, d),
                pl.BlockSpec((bsz, None, TILE_R, d), lambda i: (0, i, 0, 0)))
            y_sample = _outproj_final(
                ya, yb, yc, x, w_out_bf, l, final_nw, tm, n_chunk, n_dec // tm, (n_dec, d),
                pl.BlockSpec((tm, d), lambda i: (i, 0)))

    outs_p = [jnp.stack(a, axis=0) for a in new_p]
    n_all, m_all, cv_all = [jnp.stack(a, axis=0) for a in new_s]
    return (y_prompt.reshape(bsz, seq_len, d), y_sample.reshape(dec_b, dec_seq, d),
            *outs_p, c_s, n_all, m_all, s_s, cv_all, h_s)
```

```python
import functools
import math

import jax
import jax.numpy as jnp
from jax import lax
from jax.experimental import pallas as pl
from jax.experimental.pallas import tpu as pltpu

F32 = jnp.float32
BF16 = jnp.bfloat16

D_MODEL = 2048
HEAD_DIM = 128
H_A, H_B, H_C = 6, 6, 4
W_A, W_B, W_C = H_A * HEAD_DIM, H_B * HEAD_DIM, H_C * HEAD_DIM
MIX = W_A + W_B + W_C
N_META = 16
CONV_W = 4
EPS = 1e-6
NEG_BIG = -1e30
EXP_CLIP = 60.0
QK_SCALE = HEAD_DIM ** -0.5

TILE_R = 128
LANES = 128
CONV_PAD = 8

OFF_AQ = 0
OFF_AK, OFF_AV, OFF_AO = W_A, 2 * W_A, 3 * W_A
OFF_BQ = 4 * W_A
OFF_BK, OFF_BV = OFF_BQ + W_B, OFF_BQ + 2 * W_B
OFF_C = OFF_BQ + 3 * W_B
OFF_ZA = OFF_C + 3 * W_C
OFF_ZB = OFF_ZA + W_A
OFF_ZC = OFF_ZB + W_B
OFF_G = OFF_ZC + W_C
N_PACK = OFF_G + 2 * LANES
G_AI, G_AF, G_BA, G_BB = 0, H_A, 2 * H_A, 2 * H_A + H_B
TN = 1024
INPROJ_MAX_ROWS = 1280
W_SEGMENTS = ((OFF_AQ, OFF_BQ, 0), (OFF_BQ, OFF_C, 2 * H_A), (OFF_C, OFF_G, 2 * H_A + 2 * H_B))
W_ROW_CHUNK = 256

VMEM_LIMIT = 56 * 1024 * 1024


def _sigmoid(x):
    return 1.0 / (1.0 + jnp.exp(-x))


def _silu(x):
    return x * _sigmoid(x)


def _log1p(u):
    return jnp.log(1.0 + u)


def _softplus(x):
    return jnp.maximum(x, 0.0) + _log1p(jnp.exp(-jnp.abs(x)))


def _log_sigmoid(x):
    return -_softplus(-x)


def _dot(a, b):
    return jnp.dot(a.astype(BF16), b.astype(BF16), preferred_element_type=F32)


def _dot_nt(a, b):
    return lax.dot_general(a.astype(BF16), b.astype(BF16), (((1,), (1,)), ((), ())),
                           preferred_element_type=F32)


def _dot_exact(sel, x):
    hi = x.astype(BF16)
    r1 = x - hi.astype(F32)
    mid = r1.astype(BF16)
    lo = (r1 - mid.astype(F32)).astype(BF16)
    s = sel.astype(BF16)
    return (jnp.dot(s, hi, preferred_element_type=F32)
            + jnp.dot(s, mid, preferred_element_type=F32)
            + jnp.dot(s, lo, preferred_element_type=F32))


def _log2(n):
    k = int(math.log2(n))
    assert (1 << k) == n
    return k


def _seq_masks(nseq, c):
    r = nseq * c
    row = lax.broadcasted_iota(jnp.int32, (r, r), 0)
    col = lax.broadcasted_iota(jnp.int32, (r, r), 1)
    if nseq == 1:
        same = row >= 0
    else:
        k = _log2(c)
        same = (row >> k) == (col >> k)
    return row, col, same


def _seq_outer(lhs_t, x, nseq, c):
    if nseq == 1:
        return [jnp.dot(lhs_t, x.astype(BF16), preferred_element_type=F32)]
    seq = lax.broadcasted_iota(jnp.int32, (nseq * c, 1), 0) >> _log2(c)
    wide = jnp.concatenate([jnp.where(seq == i, x, 0.0).astype(BF16) for i in range(nseq)], axis=1)
    out = jnp.dot(lhs_t, wide, preferred_element_type=F32)
    n = x.shape[1]
    return [out[:, i * n:(i + 1) * n] for i in range(nseq)]


def _seq_cummax(x, nseq, c):
    r = nseq * c
    pos = lax.broadcasted_iota(jnp.int32, x.shape, 0)
    if nseq > 1:
        pos = pos & (c - 1)
    k = 1
    while k < c:
        x = jnp.where(pos >= k, jnp.maximum(x, pltpu.roll(x, k, axis=0)), x)
        k *= 2
    assert x.shape[0] == r
    return x


def _head_norm_gate(h, onw, z):
    hn = h * lax.rsqrt(jnp.mean(h * h, axis=1, keepdims=True) + EPS)
    return hn * onw * _silu(z)


def _rows(x, i, c):
    return x[i * c:(i + 1) * c]


def _cat_rows(parts):
    return parts[0] if len(parts) == 1 else jnp.concatenate(parts, axis=0)


def _chunk_shift(jj, r0):
    p = jj * TN + r0
    for p0, p1, shift in W_SEGMENTS:
        if p0 <= p and p + W_ROW_CHUNK <= p1:
            return shift
    return None


def _fill_weight_tile(w0_ref, w1_ref, wg_ref, wb_scr, tail_scr, jj):
    for r0 in range(0, TN, W_ROW_CHUNK):
        shift = _chunk_shift(jj, r0)
        dst = slice(r0, r0 + W_ROW_CHUNK)
        if shift is None:
            g0 = OFF_G - jj * TN
            assert r0 <= g0 and g0 + LANES <= r0 + W_ROW_CHUNK
            wb_scr[dst, :] = jnp.zeros((W_ROW_CHUNK, D_MODEL), BF16)
            wb_scr[g0:g0 + LANES, :] = wg_ref[...].astype(BF16)
        elif r0 + shift + W_ROW_CHUNK <= TN:
            wb_scr[dst, :] = w0_ref[r0 + shift:r0 + shift + W_ROW_CHUNK, :].astype(BF16)
        else:
            tail_scr[0:W_ROW_CHUNK, :] = w0_ref[r0:TN, :]
            tail_scr[W_ROW_CHUNK:, :] = w1_ref[...]
            wb_scr[dst, :] = tail_scr[shift:shift + W_ROW_CHUNK, :].astype(BF16)


def _inproj_kernel(h_ref, w0_ref, w1_ref, wg_ref, o_ref, wb_scr, tail_scr):
    j = pl.program_id(0)

    @pl.when(pl.program_id(1) == 0)
    def _():
        for jj in range(N_PACK // TN):
            @pl.when(j == jj)
            def _(jj=jj):
                _fill_weight_tile(w0_ref, w1_ref, wg_ref, wb_scr, tail_scr, jj)

    o_ref[...] = lax.dot_general(h_ref[...], wb_scr[...], (((1,), (1,)), ((), ())),
                                 preferred_element_type=F32)


W_TAIL_ROWS = 32


def _inproj(h, w_t, w_gate_t, layer, tm):
    t = h.shape[0]
    n_tiles = N_PACK // TN
    assert n_tiles - 1 <= (w_t.shape[1] - 1) // TN and W_TAIL_ROWS >= W_SEGMENTS[-1][2]
    tail_per_tile = TN // W_TAIL_ROWS
    last_tail = (w_t.shape[1] - 1) // W_TAIL_ROWS
    return pl.pallas_call(
        _inproj_kernel,
        grid=(n_tiles, t // tm),
        in_specs=[
            pl.BlockSpec((tm, D_MODEL), lambda j, i: (i, 0)),
            pl.BlockSpec((None, TN, D_MODEL), lambda j, i: (layer, j, 0)),
            pl.BlockSpec((None, W_TAIL_ROWS, D_MODEL),
                         lambda j, i: (layer, jnp.minimum((j + 1) * tail_per_tile, last_tail), 0)),
            pl.BlockSpec((None, LANES, D_MODEL), lambda j, i: (layer, 0, 0)),
        ],
        out_specs=pl.BlockSpec((tm, TN), lambda j, i: (i, j)),
        out_shape=jax.ShapeDtypeStruct((t, N_PACK), F32),
        scratch_shapes=[pltpu.VMEM((TN, D_MODEL), BF16),
                        pltpu.VMEM((W_ROW_CHUNK + W_TAIL_ROWS, D_MODEL), F32)],
        compiler_params=pltpu.CompilerParams(
            dimension_semantics=("arbitrary", "arbitrary"), vmem_limit_bytes=VMEM_LIMIT),
        name="inproj",
    )(h, w_t, w_t, w_gate_t)


def _rmsnorm(x, w):
    ms = jnp.mean(x * x, axis=1, keepdims=True)
    return x * lax.rsqrt(ms + EPS) * w


def _prep_kernel(xp_ref, xs_ref, meta_ref, nw_ref, x_ref, h_ref, *, n_chunk, bsz):
    i = pl.program_id(0)
    d = x_ref.shape[1]

    def emit(x):
        x_ref[...] = x
        h_ref[...] = _rmsnorm(x, nw_ref[...]).astype(BF16)

    @pl.when(i == 0)
    def _():
        slot = jnp.concatenate([jnp.zeros((TILE_R - N_META, d), F32), meta_ref[...]], axis=0)
        emit(jnp.concatenate([slot] * bsz, axis=0))

    @pl.when((i > 0) & (i < n_chunk))
    def _():
        emit(xp_ref[...].reshape(bsz * TILE_R, d))

    @pl.when(i >= n_chunk)
    def _():
        emit(xs_ref[...])


def _prep(x_prompt, x_sample, meta, norm_w3, bsz, n_chunk):
    d = x_prompt.shape[-1]
    tm = bsz * TILE_R
    n_dec = x_sample.shape[0] * x_sample.shape[1]
    assert n_dec % tm == 0
    xp = x_prompt.reshape(bsz, n_chunk - 1, TILE_R, d)
    xs = x_sample.reshape(n_dec, d)
    n_blk = n_chunk + n_dec // tm
    t = n_blk * tm
    return pl.pallas_call(
        functools.partial(_prep_kernel, n_chunk=n_chunk, bsz=bsz),
        grid=(n_blk,),
        in_specs=[
            pl.BlockSpec((bsz, None, TILE_R, d), lambda i: (0, jnp.clip(i - 1, 0, n_chunk - 2), 0, 0)),
            pl.BlockSpec((tm, d), lambda i: (jnp.clip(i - n_chunk, 0, n_dec // tm - 1), 0)),
            pl.BlockSpec((N_META, d), lambda i: (0, 0)),
            pl.BlockSpec((None, 1, d), lambda i: (0, 0, 0)),
        ],
        out_specs=[pl.BlockSpec((tm, d), lambda i: (i, 0)),
                   pl.BlockSpec((tm, d), lambda i: (i, 0))],
        out_shape=[jax.ShapeDtypeStruct((t, d), F32), jax.ShapeDtypeStruct((t, d), BF16)],
        compiler_params=pltpu.CompilerParams(
            dimension_semantics=("arbitrary",), vmem_limit_bytes=VMEM_LIMIT),
        name="prep",
    )(xp, xs, meta, norm_w3)


def _outproj_residual(ya_ref, yb_ref, yc_ref, x_ref, w_ref):
    acc = jnp.dot(ya_ref[...], w_ref[0:W_A, :], preferred_element_type=F32)
    acc = acc + jnp.dot(yb_ref[...], w_ref[W_A:W_A + W_B, :], preferred_element_type=F32)
    acc = acc + jnp.dot(yc_ref[...], w_ref[W_A + W_B:MIX, :], preferred_element_type=F32)
    return x_ref[...] + acc


def _outproj_mid_kernel(ya_ref, yb_ref, yc_ref, x_ref, w_ref, nw_ref, x_out, h_out):
    xn = _outproj_residual(ya_ref, yb_ref, yc_ref, x_ref, w_ref)
    x_out[...] = xn
    h_out[...] = _rmsnorm(xn, nw_ref[...]).astype(BF16)


def _outproj_final_kernel(ya_ref, yb_ref, yc_ref, x_ref, w_ref, nw_ref, y_out):
    xn = _outproj_residual(ya_ref, yb_ref, yc_ref, x_ref, w_ref)
    y_out[...] = _rmsnorm(xn, nw_ref[...]).reshape(y_out.shape)


def _outproj_specs(tm, row_blk0, layer, nw_spec):
    rows = lambda i: (row_blk0 + i, 0)
    return [
        pl.BlockSpec((tm, W_A), rows),
        pl.BlockSpec((tm, W_B), rows),
        pl.BlockSpec((tm, W_C), rows),
        pl.BlockSpec((tm, D_MODEL), rows),
        pl.BlockSpec((None, MIX, D_MODEL), lambda i: (layer, 0, 0)),
        nw_spec,
    ]


def _outproj_mid(ya, yb, yc, x, w_out, layer, norm_w3, tm):
    t = x.shape[0]
    blk = pl.BlockSpec((tm, D_MODEL), lambda i: (i, 0))
    nw_spec = pl.BlockSpec((None, 1, D_MODEL), lambda i: (layer + 1, 0, 0))
    return pl.pallas_call(
        _outproj_mid_kernel,
        grid=(t // tm,),
        in_specs=_outproj_specs(tm, 0, layer, nw_spec),
        out_specs=[blk, blk],
        out_shape=[jax.ShapeDtypeStruct((t, D_MODEL), F32), jax.ShapeDtypeStruct((t, D_MODEL), BF16)],
        compiler_params=pltpu.CompilerParams(
            dimension_semantics=("parallel",), vmem_limit_bytes=VMEM_LIMIT),
        name="outproj",
    )(ya, yb, yc, x, w_out, norm_w3)


def _outproj_final(ya, yb, yc, x, w_out, layer, final_nw, tm, row_blk0, n_blk, out_shape, out_spec):
    nw_spec = pl.BlockSpec((1, D_MODEL), lambda i: (0, 0))
    return pl.pallas_call(
        _outproj_final_kernel,
        grid=(n_blk,),
        in_specs=_outproj_specs(tm, row_blk0, layer, nw_spec),
        out_specs=out_spec,
        out_shape=jax.ShapeDtypeStruct(out_shape, F32),
        compiler_params=pltpu.CompilerParams(
            dimension_semantics=("parallel",), vmem_limit_bytes=VMEM_LIMIT),
        name="outproj_final",
    )(ya, yb, yc, x, w_out, final_nw)


class _RowView:
    def __init__(self, ref, r0, n):
        self.ref, self.r0, self.n = ref, r0, n
        self.shape = (n, ref.shape[1])

    def _idx(self, idx):
        if idx is Ellipsis:
            return slice(self.r0, self.r0 + self.n), slice(None)
        rows, cols = idx
        start = self.r0 + (rows.start or 0)
        stop = self.r0 + (self.n if rows.stop is None else rows.stop)
        return slice(start, stop), cols

    def __getitem__(self, idx):
        return self.ref[self._idx(idx)]

    def __setitem__(self, idx, val):
        self.ref[self._idx(idx)] = val


def _tile_views(refs, u):
    return [_RowView(ref, u * TILE_R, TILE_R) for ref in refs]


def _interleave(tiles):
    tiles = list(tiles)
    while tiles:
        alive = []
        for t in tiles:
            try:
                next(t)
                alive.append(t)
            except StopIteration:
                pass
        tiles = alive


def _mlstm_tile(q_ref, k_ref, v_ref, o_ref, z_ref, g_ref, gb_ref, onw_ref, y_ref,
                m_exp, valid, get_c, get_n, put_c, put_n, put_m, nseq, c):
    r = nseq * c
    row, col, same = _seq_masks(nseq, c)
    incl = same & (col <= row)
    g = g_ref[...] + gb_ref[...]
    li = g
    lf = _log_sigmoid(g)
    if valid is not None:
        li = jnp.where(valid, li, NEG_BIG)
        lf = jnp.where(valid, lf, 0.0)
    fcum = _dot_exact(incl.astype(F32), lf)
    if nseq == 1:
        flast = jnp.broadcast_to(fcum[r - 1:r, :], (r, LANES))
    else:
        flast = _dot_exact(same.astype(F32), lf)
    yield
    heads = range(H_A)
    hsl = [slice(h * HEAD_DIM, (h + 1) * HEAD_DIM) for h in heads]

    li_a = pltpu.roll(li, G_AF - G_AI, axis=1)
    m_a = pltpu.roll(m_exp, G_AF, axis=1)
    x = li_a - fcum
    mx = jnp.maximum(m_a, _seq_cummax(x, nseq, c))
    mt_all = fcum + mx
    if nseq == 1:
        m_new = jnp.broadcast_to(mt_all[r - 1:r, :], (r, LANES))
    else:
        m_new = jnp.broadcast_to(mt_all.reshape(nseq, c, LANES)[:, c - 1:c, :],
                                 (nseq, c, LANES)).reshape(r, LANES)
    a_all = jnp.exp(m_a - mx)
    w_all = jnp.exp(flast - fcum + li_a - m_new)
    as_all = jnp.exp(flast + m_a - m_new)
    low_all = jnp.exp(jnp.minimum(-mt_all, EXP_CLIP))
    put_m(pltpu.roll(m_new, LANES - G_AF, axis=1))
    x_t = x.T
    col_of = lambda arr, h: arr[:, G_AF + h:G_AF + h + 1]
    a = [col_of(a_all, h) for h in heads]
    w = [col_of(w_all, h) for h in heads]
    a_s = [col_of(as_all, h) for h in heads]
    low = [col_of(low_all, h) for h in heads]
    pexp = [jnp.exp(jnp.where(incl, x_t[G_AF + h:G_AF + h + 1, :] - col_of(mx, h), NEG_BIG))
            for h in heads]
    yield

    s = [_dot_nt(q_ref[:, hs], k_ref[:, hs] * QK_SCALE) for hs in hsl]
    yield
    p = [pexp[h] * s[h] for h in heads]
    pv = [_dot(p[h], v_ref[:, hsl[h]]) for h in heads]
    yield
    qc = [_cat_rows([_dot(_rows(q_ref[:, hsl[h]], i, c), get_c(i, h)) for i in range(nseq)])
          for h in heads]
    yield
    for h in heads:
        hs = hsl[h]
        qh = q_ref[:, hs]
        psum = jnp.sum(p[h], axis=1, keepdims=True)
        qn = _cat_rows([jnp.sum(_rows(qh, i, c) * get_n(i, h), axis=1, keepdims=True)
                        for i in range(nseq)])
        num = a[h] * qc[h] + pv[h]
        den = a[h] * qn + psum
        hh = num / jnp.maximum(jnp.abs(den), low[h])
        hh = hh * _sigmoid(o_ref[:, hs])
        y_ref[:, hs] = _head_norm_gate(hh, onw_ref[:, hs], z_ref[:, hs]).astype(BF16)
    yield

    for h in heads:
        hs = hsl[h]
        kh = k_ref[:, hs] * QK_SCALE
        wk = w[h] * kh
        upd = _seq_outer(kh.T.astype(BF16), w[h] * v_ref[:, hs], nseq, c)
        for i in range(nseq):
            a_i = a_s[h][i * c:i * c + 1, :]
            put_c(i, h, a_i * get_c(i, h) + upd[i])
            put_n(i, h, a_i * get_n(i, h) + jnp.sum(_rows(wk, i, c), axis=0, keepdims=True))


def _mlstm_prompt_kernel(q_ref, k_ref, v_ref, o_ref, z_ref, g_ref, gb_ref, onw_ref,
                         y_ref, c_out, n_out, m_out, c_scr, n_scr, m_scr):
    j = pl.program_id(0)
    bsz = c_scr.shape[0]

    @pl.when(j == 0)
    def _():
        c_scr[...] = jnp.zeros_like(c_scr)
        n_scr[...] = jnp.zeros_like(n_scr)
        m_scr[...] = jnp.zeros_like(m_scr)

    tok = lax.broadcasted_iota(jnp.int32, (TILE_R, LANES), 0) + j * TILE_R
    valid = tok >= TILE_R - N_META

    def tile(u):
        q, k, v, o, z, g, y = _tile_views((q_ref, k_ref, v_ref, o_ref, z_ref, g_ref, y_ref), u)
        m_exp = jnp.broadcast_to(m_scr[u, 0:1, :], (TILE_R, LANES))

        def put_c(i, h, val):
            c_scr[u, h] = val

        def put_n(i, h, val):
            n_scr[u, h:h + 1, :] = val

        def put_m(val):
            m_scr[u] = val[0:8, :]

        return _mlstm_tile(q, k, v, o, z, g, gb_ref, onw_ref, y, m_exp, valid,
                           lambda i, h: c_scr[u, h], lambda i, h: n_scr[u, h:h + 1, :],
                           put_c, put_n, put_m, 1, TILE_R)

    _interleave(tile(u) for u in range(bsz))

    @pl.when(j == pl.num_programs(0) - 1)
    def _():
        c_out[...] = c_scr[...]
        n_out[...] = n_scr[:, 0:H_A, :]
        m_out[...] = m_scr[:, 0:1, :]


def _mlstm_sample_kernel(q_ref, k_ref, v_ref, o_ref, z_ref, g_ref, gb_ref, onw_ref,
                         c_in, n_in, m_in, y_ref, c_out, n_out, m_out, *, c):
    nseq = TILE_R // c

    def put_c(i, h, val):
        c_out[i, h] = val

    def put_n(i, h, val):
        n_out[i, h:h + 1, :] = val

    def put_m(val):
        m_out[...] = val

    _interleave([_mlstm_tile(
        q_ref, k_ref, v_ref, o_ref, z_ref, g_ref, gb_ref, onw_ref, y_ref, m_in[...], None,
        lambda i, h: c_in[i, h], lambda i, h: n_in[i, h:h + 1, :], put_c, put_n, put_m, nseq, c)])


def _col_spec(width, off, rows, n_rows=TILE_R):
    assert off % width == 0
    blk = off // width
    return pl.BlockSpec((n_rows, width), lambda *ids: (rows(*ids), blk))


def _param_spec(width, off, layer):
    assert off % width == 0
    blk = off // width
    return pl.BlockSpec((None, 1, width), lambda *ids: (layer, 0, blk))


def _whole(*shape):
    return pl.BlockSpec(shape, lambda j: (0,) * len(shape))


def _mlstm_prompt(proj, gate_bias, onw3, layer, bsz, n_chunk):
    t = proj.shape[0]
    tm = bsz * TILE_R
    rows = lambda j: j
    in_specs = [_col_spec(W_A, off, rows, tm) for off in (OFF_AQ, OFF_AK, OFF_AV, OFF_AO, OFF_ZA)]
    in_specs += [_col_spec(LANES, OFF_G, rows, tm), _param_spec(LANES, 0, layer),
                 _param_spec(W_A, 0, layer)]
    return pl.pallas_call(
        _mlstm_prompt_kernel,
        grid=(n_chunk,),
        in_specs=in_specs,
        out_specs=[
            pl.BlockSpec((tm, W_A), lambda j: (j, 0)),
            _whole(bsz, H_A, HEAD_DIM, HEAD_DIM),
            _whole(bsz, H_A, HEAD_DIM),
            _whole(bsz, 1, LANES),
        ],
        out_shape=[
            jax.ShapeDtypeStruct((t, W_A), BF16),
            jax.ShapeDtypeStruct((bsz, H_A, HEAD_DIM, HEAD_DIM), F32),
            jax.ShapeDtypeStruct((bsz, H_A, HEAD_DIM), F32),
            jax.ShapeDtypeStruct((bsz, 1, LANES), F32),
        ],
        scratch_shapes=[pltpu.VMEM((bsz, H_A, HEAD_DIM, HEAD_DIM), F32),
                        pltpu.VMEM((bsz, 8, HEAD_DIM), F32), pltpu.VMEM((bsz, 8, LANES), F32)],
        compiler_params=pltpu.CompilerParams(
            dimension_semantics=("arbitrary",), vmem_limit_bytes=VMEM_LIMIT),
        name="mlstm_prompt",
    )(proj, proj, proj, proj, proj, proj, gate_bias, onw3)


def _in_place(operands, n_blocked, targets):
    specs, aliases = [], {}
    operands = list(operands)
    for out_idx, arr in targets.items():
        if arr is not None:
            aliases[len(operands)] = out_idx
            operands.append(arr)
            specs.append(pl.BlockSpec(memory_space=pl.ANY))
    n_extra = len(specs)

    def strip(kernel):
        def body(*refs):
            kernel(*refs[:n_blocked], *refs[n_blocked + n_extra:])
        return body

    return operands, specs, aliases, strip


def _mlstm_sample(proj, y_prev, c_stack, gate_bias, onw3, c_state, n_state, m_rows, layer,
                  dec_b, c, rows):
    nseq = TILE_R // c
    depth = c_state.shape[0]
    in_specs = [_col_spec(W_A, off, rows) for off in (OFF_AQ, OFF_AK, OFF_AV, OFF_AO, OFF_ZA)]
    in_specs += [_col_spec(LANES, OFF_G, rows), _param_spec(LANES, 0, layer),
                 _param_spec(W_A, 0, layer)]
    in_specs += [
        pl.BlockSpec((None, nseq, H_A, HEAD_DIM, HEAD_DIM), lambda i: (layer, i, 0, 0, 0)),
        pl.BlockSpec((None, nseq, H_A, HEAD_DIM), lambda i: (layer, i, 0, 0)),
        pl.BlockSpec((None, TILE_R, LANES), lambda i: (layer, i, 0)),
    ]
    operands = (proj, proj, proj, proj, proj, proj, gate_bias, onw3, c_state, n_state, m_rows)
    operands, extra, aliases, strip = _in_place(operands, len(in_specs), {0: y_prev, 1: c_stack})
    return pl.pallas_call(
        strip(functools.partial(_mlstm_sample_kernel, c=c)),
        grid=(dec_b // nseq,),
        in_specs=in_specs + extra,
        out_specs=[
            pl.BlockSpec((TILE_R, W_A), lambda i: (rows(i), 0)),
            pl.BlockSpec((None, nseq, H_A, HEAD_DIM, HEAD_DIM), lambda i: (layer, i, 0, 0, 0)),
            pl.BlockSpec((nseq, H_A, HEAD_DIM), lambda i: (i, 0, 0)),
            pl.BlockSpec((TILE_R, LANES), lambda i: (i, 0)),
        ],
        out_shape=[
            jax.ShapeDtypeStruct(y_prev.shape, BF16),
            jax.ShapeDtypeStruct((depth, dec_b, H_A, HEAD_DIM, HEAD_DIM), F32),
            jax.ShapeDtypeStruct((dec_b, H_A, HEAD_DIM), F32),
            jax.ShapeDtypeStruct((dec_b * c, LANES), F32),
        ],
        input_output_aliases=aliases,
        compiler_params=pltpu.CompilerParams(
            dimension_semantics=("parallel",), vmem_limit_bytes=VMEM_LIMIT),
        name="mlstm_sample",
    )(*operands)


def _l2norm(x):
    return x * lax.rsqrt(jnp.sum(x * x, axis=1, keepdims=True) + EPS)


INV_LEAF = 16


def _unit_lower_inverse(a_list, row, col, c):
    leaf = min(INV_LEAF, c)
    kl = _log2(leaf)
    leaf_mask = (row >> kl) == (col >> kl)
    eye = jnp.where(row == col, 1.0, 0.0)
    p = [jnp.where(leaf_mask, -a, 0.0) for a in a_list]
    t = [eye + n for n in p]
    for _ in range(kl - 1):
        p = [_dot(x, x) for x in p]
        yield
        t = [ti + _dot(pi, ti) for pi, ti in zip(p, t)]
        yield
    w = leaf
    while w < c:
        kw = _log2(w)
        lower_left = (((row >> (kw + 1)) == (col >> (kw + 1)))
                      & (((row >> kw) & 1) == 1) & (((col >> kw) & 1) == 0))
        left = [_dot(ti, jnp.where(lower_left, a, 0.0)) for ti, a in zip(t, a_list)]
        yield
        t = [ti - _dot(li, ti) for li, ti in zip(left, t)]
        yield
        w *= 2
    return t


def _gdn_tile(u_cols, prev_cols, cw_ref, z_ref, g_ref, gp_ref, onw_ref, y_ref, get_s, put_s,
              nseq, c):
    r = nseq * c
    row, col, same = _seq_masks(nseq, c)
    incl = same & (col <= row)
    strict = same & (col < row)
    gp = gp_ref[...]
    graw = g_ref[...]
    gdec = -jnp.exp(gp[1:2, :]) * _softplus(graw + gp[0:1, :])
    beta = _sigmoid(graw)
    gcum = _dot_exact(incl.astype(F32), gdec)
    if nseq == 1:
        glast = jnp.broadcast_to(gcum[r - 1:r, :], (r, LANES))
    else:
        glast = _dot_exact(same.astype(F32), gdec)
    yield
    gcum_t = gcum.T

    heads = range(H_B)
    hsl = [slice(h * HEAD_DIM, (h + 1) * HEAD_DIM) for h in heads]

    pos = lax.broadcasted_iota(jnp.int32, (r, HEAD_DIM), 0)
    if nseq > 1:
        pos = pos & (c - 1)

    def conv(lo):
        cols = slice(lo, lo + HEAD_DIM)
        x = u_cols(cols)
        prev = prev_cols(cols)
        acc = x * cw_ref[CONV_W - 1:CONV_W, cols]
        for back in range(1, CONV_W):
            shifted = jnp.where(pos < back, pltpu.roll(prev, (back - CONV_PAD) % r, axis=0),
                                pltpu.roll(x, back, axis=0))
            acc = acc + shifted * cw_ref[CONV_W - 1 - back:CONV_W - back, cols]
        return _silu(acc)

    q = [_l2norm(conv(hs.start)) * QK_SCALE for hs in hsl]
    k = [_l2norm(conv(W_B + hs.start)) for hs in hsl]
    v = [conv(2 * W_B + hs.start) for hs in hsl]
    yield
    gc = [gcum[:, G_BA + h:G_BA + h + 1] for h in heads]
    gl = [glast[:, G_BA + h:G_BA + h + 1] for h in heads]
    bc = [beta[:, G_BB + h:G_BB + h + 1] for h in heads]
    decay = [jnp.exp(jnp.where(incl, gc[h] - gcum_t[G_BA + h:G_BA + h + 1, :], NEG_BIG))
             for h in heads]
    kk = [_dot_nt(k[h], k[h]) for h in heads]
    qk = [_dot_nt(q[h], k[h]) for h in heads]
    yield
    amat = [jnp.where(strict, bc[h] * decay[h] * kk[h], 0.0) for h in heads]
    tinv = yield from _unit_lower_inverse(amat, row, col, c)
    sol = [_dot(tinv[h], jnp.concatenate([bc[h] * v[h], (bc[h] * jnp.exp(gc[h])) * k[h]], axis=1))
           for h in heads]
    yield
    both = [[_dot(jnp.concatenate([_rows(sol[h][:, HEAD_DIM:], i, c), _rows(q[h], i, c)], axis=0),
                  get_s(i, h)) for i in range(nseq)] for h in heads]
    wks = [_cat_rows([both[h][i][:c] for i in range(nseq)]) for h in heads]
    qs = [_cat_rows([both[h][i][c:] for i in range(nseq)]) for h in heads]
    yield
    wmat = [sol[h][:, :HEAD_DIM] - wks[h] for h in heads]
    aw = [_dot(decay[h] * qk[h], wmat[h]) for h in heads]
    yield
    for h in heads:
        hs = hsl[h]
        oh = jnp.exp(gc[h]) * qs[h] + aw[h]
        y_ref[:, hs] = _head_norm_gate(oh, onw_ref[:, hs], z_ref[:, hs]).astype(BF16)
    yield
    for h in heads:
        kd_t = (k[h] * jnp.exp(gl[h] - gc[h])).T.astype(BF16)
        e_last = jnp.exp(gl[h])
        upd = _seq_outer(kd_t, wmat[h], nseq, c)
        for i in range(nseq):
            put_s(i, h, e_last[i * c:i * c + 1, :] * get_s(i, h) + upd[i])


def _gdn_prompt_kernel(q_ref, k_ref, v_ref, z_ref, g_ref, gp_ref, cw_ref, onw_ref,
                       y_ref, s_out, conv_out, s_scr, hist_scr):
    j = pl.program_id(0)
    bsz = s_scr.shape[0]
    pieces = ((q_ref, 0), (k_ref, W_B), (v_ref, 2 * W_B))

    @pl.when(j == 0)
    def _():
        s_scr[...] = jnp.zeros_like(s_scr)
        hist_scr[...] = jnp.zeros_like(hist_scr)

    def tile(u):
        z, g, y = _tile_views((z_ref, g_ref, y_ref), u)

        def u_cols(cols):
            ref, off = pieces[cols.start // W_B]
            return ref[u * TILE_R:(u + 1) * TILE_R, cols.start - off:cols.stop - off]

        def prev_cols(cols):
            return jnp.concatenate([hist_scr[u, :, cols], u_cols(cols)[CONV_PAD:]], axis=0)

        def put_s(i, h, val):
            s_scr[u, h] = val

        return _gdn_tile(u_cols, prev_cols, cw_ref, z, g, gp_ref, onw_ref, y,
                         lambda i, h: s_scr[u, h], put_s, 1, TILE_R)

    _interleave(tile(u) for u in range(bsz))

    for u in range(bsz):
        for ref, off in pieces:
            hist_scr[u, :, off:off + W_B] = ref[(u + 1) * TILE_R - CONV_PAD:(u + 1) * TILE_R, :]

    @pl.when(j == pl.num_programs(0) - 1)
    def _():
        s_out[...] = s_scr[...]
        conv_out[...] = hist_scr[:, CONV_PAD - (CONV_W - 1):, :]


def _gdn_sample_kernel(q_ref, k_ref, v_ref, z_ref, g_ref, gp_ref, cw_ref, onw_ref,
                       s_in, conv_in, y_ref, s_out, conv_out, hist_scr, *, c):
    nseq = TILE_R // c
    hist = CONV_W - 1
    assert c == CONV_PAD
    pieces = ((q_ref, 0), (k_ref, W_B), (v_ref, 2 * W_B))
    hist_scr[:, 0:CONV_PAD - hist, :] = jnp.zeros((nseq, CONV_PAD - hist, 3 * W_B), F32)
    hist_scr[:, CONV_PAD - hist:, :] = conv_in[...]

    def u_cols(cols):
        ref, off = pieces[cols.start // W_B]
        return ref[:, cols.start - off:cols.stop - off]

    def prev_cols(cols):
        return hist_scr[:, :, cols].reshape(TILE_R, cols.stop - cols.start)

    def put_s(i, h, val):
        s_out[i, h] = val

    _interleave([_gdn_tile(u_cols, prev_cols, cw_ref, z_ref, g_ref, gp_ref, onw_ref, y_ref,
                           lambda i, h: s_in[i, h], put_s, nseq, c)])
    for ref, off in pieces:
        conv_out[:, :, off:off + W_B] = ref[...].reshape(nseq, c, W_B)[:, c - hist:, :]


def _gdn_prompt(proj, gdn_par, conv_w, onw3, layer, bsz, n_chunk):
    t = proj.shape[0]
    tm = bsz * TILE_R
    rows = lambda j: j
    in_specs = [_col_spec(W_B, off, rows, tm) for off in (OFF_BQ, OFF_BK, OFF_BV, OFF_ZB)]
    in_specs += [
        _col_spec(LANES, OFF_G, rows, tm),
        pl.BlockSpec((None, 8, LANES), lambda j: (layer, 0, 0)),
        pl.BlockSpec((None, CONV_W, 3 * W_B), lambda j: (layer, 0, 0)),
        _param_spec(W_B, W_A, layer),
    ]
    return pl.pallas_call(
        _gdn_prompt_kernel,
        grid=(n_chunk,),
        in_specs=in_specs,
        out_specs=[
            pl.BlockSpec((tm, W_B), lambda j: (j, 0)),
            _whole(bsz, H_B, HEAD_DIM, HEAD_DIM),
            _whole(bsz, CONV_W - 1, 3 * W_B),
        ],
        out_shape=[
            jax.ShapeDtypeStruct((t, W_B), BF16),
            jax.ShapeDtypeStruct((bsz, H_B, HEAD_DIM, HEAD_DIM), F32),
            jax.ShapeDtypeStruct((bsz, CONV_W - 1, 3 * W_B), F32),
        ],
        scratch_shapes=[pltpu.VMEM((bsz, H_B, HEAD_DIM, HEAD_DIM), F32),
                        pltpu.VMEM((bsz, CONV_PAD, 3 * W_B), F32)],
        compiler_params=pltpu.CompilerParams(
            dimension_semantics=("arbitrary",), vmem_limit_bytes=VMEM_LIMIT),
        name="gdn_prompt",
    )(proj, proj, proj, proj, proj, gdn_par, conv_w, onw3)


def _gdn_sample(proj, y_prev, s_stack, gdn_par, conv_w, onw3, s_state, conv_state, layer,
                dec_b, c, rows):
    nseq = TILE_R // c
    depth = s_state.shape[0]
    in_specs = [_col_spec(W_B, off, rows) for off in (OFF_BQ, OFF_BK, OFF_BV, OFF_ZB)]
    in_specs += [
        _col_spec(LANES, OFF_G, rows),
        pl.BlockSpec((None, 8, LANES), lambda i: (layer, 0, 0)),
        pl.BlockSpec((None, CONV_W, 3 * W_B), lambda i: (layer, 0, 0)),
        _param_spec(W_B, W_A, layer),
        pl.BlockSpec((None, nseq, H_B, HEAD_DIM, HEAD_DIM), lambda i: (layer, i, 0, 0, 0)),
        pl.BlockSpec((None, nseq, CONV_W - 1, 3 * W_B), lambda i: (layer, i, 0, 0)),
    ]
    operands = (proj, proj, proj, proj, proj, gdn_par, conv_w, onw3, s_state, conv_state)
    operands, extra, aliases, strip = _in_place(operands, len(in_specs), {0: y_prev, 1: s_stack})
    return pl.pallas_call(
        strip(functools.partial(_gdn_sample_kernel, c=c)),
        grid=(dec_b // nseq,),
        in_specs=in_specs + extra,
        out_specs=[
            pl.BlockSpec((TILE_R, W_B), lambda i: (rows(i), 0)),
            pl.BlockSpec((None, nseq, H_B, HEAD_DIM, HEAD_DIM), lambda i: (layer, i, 0, 0, 0)),
            pl.BlockSpec((nseq, CONV_W - 1, 3 * W_B), lambda i: (i, 0, 0)),
        ],
        out_shape=[
            jax.ShapeDtypeStruct(y_prev.shape, BF16),
            jax.ShapeDtypeStruct((depth, dec_b, H_B, HEAD_DIM, HEAD_DIM), F32),
            jax.ShapeDtypeStruct((dec_b, CONV_W - 1, 3 * W_B), F32),
        ],
        scratch_shapes=[pltpu.VMEM((nseq, CONV_PAD, 3 * W_B), F32)],
        input_output_aliases=aliases,
        compiler_params=pltpu.CompilerParams(
            dimension_semantics=("parallel",), vmem_limit_bytes=VMEM_LIMIT),
        name="gdn_sample",
    )(*operands)


def _hgrn_tile(q_ref, f_ref, i_ref, z_ref, lbp_ref, onw_ref, y_ref, get_s, put_s,
               nseq, c, layer):
    r = nseq * c
    row, col, same = _seq_masks(nseq, c)
    incl = same & (col <= row)
    rowv = lax.broadcasted_iota(jnp.int32, (r, 1), 0)
    lbp = lbp_ref[...]
    e = jnp.exp(lbp - jnp.max(lbp, axis=0, keepdims=True))
    sm = e / jnp.sum(e, axis=0, keepdims=True)
    lb = jnp.zeros((1, W_C), F32)
    for l in range(1, layer + 1):
        lb = lb + sm[l:l + 1, :]

    cf = f_ref[...]
    logf = _log_sigmoid(cf)
    if layer > 0:
        logf = logf + _log1p(lb * jnp.exp(jnp.minimum(-cf, EXP_CLIP)))
    kall = (1.0 - lb) * _sigmoid(-cf)
    bcum = _dot_exact(incl.astype(F32), logf)
    if nseq == 1:
        blast = jnp.broadcast_to(bcum[r - 1:r, :], (r, W_C))
    else:
        blast = _dot_exact(same.astype(F32), logf)

    levels = []
    w = c // 2
    while w >= 1:
        levels.append(w)
        w //= 2
    rowf = lax.broadcasted_iota(jnp.int32, (r, W_C), 0)
    refs = []
    for w in levels:
        if 2 * w >= 8:
            refs.append(_cat_rows([
                jnp.broadcast_to(bcum[g * 2 * w + w - 1:g * 2 * w + w, :], (2 * w, W_C))
                for g in range(r // (2 * w))]))
        else:
            pos = rowf & (2 * w - 1)
            ref = bcum
            for off in range(2 * w):
                if off != w - 1:
                    ref = jnp.where(pos == off, pltpu.roll(bcum, (off - (w - 1)) % r, axis=0), ref)
            refs.append(ref)

    heads = range(H_C)
    hsl = [slice(h * HEAD_DIM, (h + 1) * HEAD_DIM) for h in heads]
    q = [_silu(q_ref[:, hs]) for hs in hsl]
    k = [kall[:, hs] for hs in hsl]
    b = [bcum[:, hs] for hs in hsl]
    diag = row == col
    yield
    amat = [jnp.where(diag, _dot_nt(q[h], k[h]), 0.0) for h in heads]
    for w, ref_all in zip(levels, refs):
        yield
        lw = _log2(w)
        level = (((row >> (lw + 1)) == (col >> (lw + 1)))
                 & (((row >> lw) & 1) == 1) & (((col >> lw) & 1) == 0))
        e = [jnp.exp(-jnp.abs(b[h] - ref_all[:, hsl[h]])) for h in heads]
        amat = [jnp.where(level, _dot_nt(q[h] * e[h], k[h] * e[h]), amat[h]) for h in heads]
    yield
    qs = [_cat_rows([_dot(_rows(q[h] * jnp.exp(b[h]), i, c), get_s(i, h)) for i in range(nseq)])
          for h in heads]
    av = [_dot(amat[h], i_ref[:, hsl[h]]) for h in heads]
    yield
    for h in heads:
        hs = hsl[h]
        y_ref[:, hs] = _head_norm_gate(qs[h] + av[h], onw_ref[:, hs], z_ref[:, hs]).astype(BF16)
    yield
    for h in heads:
        hs = hsl[h]
        vh = i_ref[:, hs]
        bl = blast[:, hs]
        kd_t = (k[h] * jnp.exp(bl - b[h])).T.astype(BF16)
        eb_t = jnp.exp(bl).T
        upd = _seq_outer(kd_t, vh, nseq, c)
        for i in range(nseq):
            put_s(i, h, eb_t[:, i * c:i * c + 1] * get_s(i, h) + upd[i])


class _ColView:
    def __init__(self, parts):
        self.parts = []
        pos = 0
        for ref, lo, hi in parts:
            self.parts.append((pos, ref, lo, hi - lo))
            pos += hi - lo
        self.width = pos

    def __getitem__(self, idx):
        if idx is Ellipsis:
            rows, cols = slice(None), slice(0, self.width)
        else:
            rows, cols = idx
            cols = slice(cols.start or 0, self.width if cols.stop is None else cols.stop)
        out = []
        for pos, ref, lo, n in self.parts:
            a, b = max(cols.start, pos), min(cols.stop, pos + n)
            if a < b:
                out.append(ref[rows, lo + a - pos:lo + b - pos])
        return out[0] if len(out) == 1 else jnp.concatenate(out, axis=1)


def _hgrn_views(ca_ref, cb_ref, zc_ref):
    two = 2 * W_C - ca_ref.shape[1]
    q = _ColView([(ca_ref, 0, W_C)])
    f = _ColView([(ca_ref, W_C, ca_ref.shape[1]), (cb_ref, 0, two)])
    i = _ColView([(cb_ref, two, two + W_C)])
    z = _ColView([(zc_ref, 0, W_C)])
    return q, f, i, z


def _hgrn_prompt_kernel(ca_ref, cb_ref, zc_ref, lbp_ref, onw_ref, y_ref, s_out, s_scr,
                        *, layer):
    j = pl.program_id(0)
    bsz = s_scr.shape[0]

    @pl.when(j == 0)
    def _():
        s_scr[...] = jnp.zeros_like(s_scr)

    def tile(u):
        ca, cb, zc, y = _tile_views((ca_ref, cb_ref, zc_ref, y_ref), u)

        def put_s(i, h, val):
            s_scr[u, h] = val

        return _hgrn_tile(*_hgrn_views(ca, cb, zc), lbp_ref, onw_ref, y,
                          lambda i, h: s_scr[u, h], put_s, 1, TILE_R, layer)

    _interleave(tile(u) for u in range(bsz))

    @pl.when(j == pl.num_programs(0) - 1)
    def _():
        s_out[...] = s_scr[...]


def _hgrn_sample_kernel(ca_ref, cb_ref, zc_ref, lbp_ref, onw_ref, s_in, y_ref, s_out,
                        *, c, layer):
    def put_s(i, h, val):
        s_out[i, h] = val

    _interleave([_hgrn_tile(*_hgrn_views(ca_ref, cb_ref, zc_ref), lbp_ref, onw_ref, y_ref,
                            lambda i, h: s_in[i, h], put_s, TILE_R // c, c, layer)])


HGRN_BLK = W_A
assert OFF_C % HGRN_BLK == 0 and 3 * W_C == 2 * HGRN_BLK and OFF_ZC % HGRN_BLK == 0


def _hgrn_prompt(proj, lb_par, onw3, layer, bsz, n_chunk):
    t = proj.shape[0]
    tm = bsz * TILE_R
    depth = lb_par.shape[0]
    rows = lambda j: j
    in_specs = [_col_spec(HGRN_BLK, off, rows, tm) for off in (OFF_C, OFF_C + HGRN_BLK, OFF_ZC)]
    in_specs += [pl.BlockSpec((depth, W_C), lambda j: (0, 0)),
                 _param_spec(W_C, W_A + W_B, layer)]
    return pl.pallas_call(
        functools.partial(_hgrn_prompt_kernel, layer=layer),
        grid=(n_chunk,),
        in_specs=in_specs,
        out_specs=[
            pl.BlockSpec((tm, W_C), lambda j: (j, 0)),
            _whole(bsz, H_C, HEAD_DIM, HEAD_DIM),
        ],
        out_shape=[
            jax.ShapeDtypeStruct((t, W_C), BF16),
            jax.ShapeDtypeStruct((bsz, H_C, HEAD_DIM, HEAD_DIM), F32),
        ],
        scratch_shapes=[pltpu.VMEM((bsz, H_C, HEAD_DIM, HEAD_DIM), F32)],
        compiler_params=pltpu.CompilerParams(
            dimension_semantics=("arbitrary",), vmem_limit_bytes=VMEM_LIMIT),
        name="hgrn_prompt",
    )(proj, proj, proj, lb_par, onw3)


def _hgrn_sample(proj, y_prev, s_stack, lb_par, onw3, s_state, layer, dec_b, c, rows):
    nseq = TILE_R // c
    depth = lb_par.shape[0]
    in_specs = [_col_spec(HGRN_BLK, off, rows) for off in (OFF_C, OFF_C + HGRN_BLK, OFF_ZC)]
    in_specs += [
        pl.BlockSpec((depth, W_C), lambda i: (0, 0)),
        _param_spec(W_C, W_A + W_B, layer),
        pl.BlockSpec((None, nseq, H_C, HEAD_DIM, HEAD_DIM), lambda i: (layer, i, 0, 0, 0)),
    ]
    operands = (proj, proj, proj, lb_par, onw3, s_state)
    operands, extra, aliases, strip = _in_place(operands, len(in_specs), {0: y_prev, 1: s_stack})
    return pl.pallas_call(
        strip(functools.partial(_hgrn_sample_kernel, c=c, layer=layer)),
        grid=(dec_b // nseq,),
        in_specs=in_specs + extra,
        out_specs=[
            pl.BlockSpec((TILE_R, W_C), lambda i: (rows(i), 0)),
            pl.BlockSpec((None, nseq, H_C, HEAD_DIM, HEAD_DIM), lambda i: (layer, i, 0, 0, 0)),
        ],
        out_shape=[
            jax.ShapeDtypeStruct(y_prev.shape, BF16),
            jax.ShapeDtypeStruct((depth, dec_b, H_C, HEAD_DIM, HEAD_DIM), F32),
        ],
        input_output_aliases=aliases,
        compiler_params=pltpu.CompilerParams(
            dimension_semantics=("parallel",), vmem_limit_bytes=VMEM_LIMIT),
        name="hgrn_sample",
    )(*operands)


def _gate_weight(w_t):
    ai0 = 4 * W_A
    ba0 = ai0 + 2 * H_A + 3 * W_B
    n_gate = 2 * H_A + 2 * H_B
    pad = jnp.zeros((w_t.shape[0], LANES - n_gate, w_t.shape[2]), w_t.dtype)
    return jnp.concatenate([w_t[:, ai0:ai0 + 2 * H_A], w_t[:, ba0:ba0 + 2 * H_B], pad], axis=1)


def _lane_row(depth, pieces):
    row = jnp.zeros((depth, LANES), F32)
    for off, val in pieces:
        row = lax.dynamic_update_slice(row, val.astype(F32), (0, off))
    return row[:, None, :]


def kernel(x_prompt, x_sample, state_mlstm_C, state_mlstm_n, state_mlstm_m, state_gdn_S,
           state_gdn_conv, state_hgrn_S, meta_tokens, norm_w, w_in, mlstm_gate_b, gdn_A_log,
           gdn_dt_bias, gdn_conv_w, hgrn_lower_bounds, out_norm_w, w_out, final_norm_w):
    bsz, seq_len, d = x_prompt.shape
    dec_b, dec_seq, _ = x_sample.shape
    depth = w_in.shape[0]
    assert d == D_MODEL and seq_len % TILE_R == 0 and TILE_R % dec_seq == 0
    assert w_in.shape[2] == W_SEGMENTS[-1][1] + W_SEGMENTS[-1][2]
    nseq = TILE_R // dec_seq
    assert dec_b % nseq == 0

    chunks = seq_len // TILE_R
    n_chunk = chunks + 1
    tm = bsz * TILE_R
    n_dec = dec_b * dec_seq
    dec_blk0 = n_chunk * bsz

    def dec_rows(i):
        return dec_blk0 + i

    w_t = jnp.swapaxes(w_in.astype(F32), 1, 2)
    w_gate = _gate_weight(w_t)
    w_out_bf = w_out.astype(BF16)
    norm_w3 = norm_w.astype(F32)[:, None, :]
    onw3 = out_norm_w.astype(F32)[:, None, :]
    final_nw = final_norm_w.astype(F32)[None, :]
    gate_bias = _lane_row(depth, [(G_AI, mlstm_gate_b[:, 0]), (G_AF, mlstm_gate_b[:, 1])])
    gdn_par = jnp.concatenate([
        _lane_row(depth, [(G_BA, gdn_dt_bias)]), _lane_row(depth, [(G_BA, gdn_A_log)]),
        jnp.zeros((depth, 6, LANES), F32)], axis=1)
    conv_w = gdn_conv_w.astype(F32)
    lb_par = hgrn_lower_bounds.astype(F32)
    m_rows = jnp.repeat(
        jnp.pad(state_mlstm_m.astype(F32), ((0, 0), (0, 0), (0, LANES - H_A))), dec_seq, axis=1)

    x, h = _prep(x_prompt.astype(F32), x_sample.astype(F32), meta_tokens.astype(F32), norm_w3,
                 bsz, n_chunk)
    t_all = x.shape[0]
    tm_in = max(m for m in range(16, INPROJ_MAX_ROWS + 1, 16) if t_all % m == 0)

    new_p = [[] for _ in range(6)]
    new_s = [[] for _ in range(3)]
    c_s = s_s = h_s = None
    y_prompt = y_sample = None
    for l in range(depth):
        proj = _inproj(h, w_t, w_gate, l, tm_in)

        ya, c_p, n_p, m_p = _mlstm_prompt(proj, gate_bias, onw3, l, bsz, n_chunk)
        ya, c_s, n_s, m_s = _mlstm_sample(proj, ya, c_s, gate_bias, onw3, state_mlstm_C,
                                          state_mlstm_n, m_rows, l, dec_b, dec_seq, dec_rows)
        yb, s_p, cv_p = _gdn_prompt(proj, gdn_par, conv_w, onw3, l, bsz, n_chunk)
        yb, s_s, cv_s = _gdn_sample(proj, yb, s_s, gdn_par, conv_w, onw3, state_gdn_S,
                                    state_gdn_conv, l, dec_b, dec_seq, dec_rows)
        yc, h_p = _hgrn_prompt(proj, lb_par, onw3, l, bsz, n_chunk)
        yc, h_s = _hgrn_sample(proj, yc, h_s, lb_par, onw3, state_hgrn_S, l, dec_b, dec_seq,
                               dec_rows)

        for lst, val in zip(new_p, (c_p, n_p, m_p[:, 0, :H_A], s_p, cv_p, h_p)):
            lst.append(val)
        for lst, val in zip(new_s, (n_s, m_s[::dec_seq, :H_A], cv_s)):
            lst.append(val)

        if l + 1 < depth:
            x, h = _outproj_mid(ya, yb, yc, x, w_out_bf, l, norm_w3, tm)
        else:
            y_prompt = _outproj_final(
                ya, yb, yc, x, w_out_bf, l, final_nw, tm, 1, chunks, (bsz, chunks, TILE_R, d),
                pl.BlockSpec((bsz, None, TILE_R, d), lambda i: (0, i, 0, 0)))
            y_sample = _outproj_final(
                ya, yb, yc, x, w_out_bf, l, final_nw, tm, n_chunk, n_dec // tm, (n_dec, d),
                pl.BlockSpec((tm, d), lambda i: (i, 0)))

    outs_p = [jnp.stack(a, axis=0) for a in new_p]
    n_all, m_all, cv_all = [jnp.stack(a, axis=0) for a in new_s]
    return (y_prompt.reshape(bsz, seq_len, d), y_sample.reshape(dec_b, dec_seq, d),
            *outs_p, c_s, n_all, m_all, s_s, cv_all, h_s)
```

```python
import functools
import math

import jax
import jax.numpy as jnp
from jax import lax
from jax.experimental import pallas as pl
from jax.experimental.pallas import tpu as pltpu

F32 = jnp.float32
BF16 = jnp.bfloat16

D_MODEL = 2048
HEAD_DIM = 128
H_A, H_B, H_C = 6, 6, 4
W_A, W_B, W_C = H_A * HEAD_DIM, H_B * HEAD_DIM, H_C * HEAD_DIM
MIX = W_A + W_B + W_C
N_META = 16
CONV_W = 4
EPS = 1e-6
NEG_BIG = -1e30
EXP_CLIP = 60.0
QK_SCALE = HEAD_DIM ** -0.5

TILE_R = 128
LANES = 128
CONV_PAD = 8

OFF_AQ = 0
OFF_AK, OFF_AV, OFF_AO = W_A, 2 * W_A, 3 * W_A
OFF_BQ = 4 * W_A
OFF_BK, OFF_BV = OFF_BQ + W_B, OFF_BQ + 2 * W_B
OFF_C = OFF_BQ + 3 * W_B
OFF_ZA = OFF_C + 3 * W_C
OFF_ZB = OFF_ZA + W_A
OFF_ZC = OFF_ZB + W_B
OFF_G = OFF_ZC + W_C
N_PACK = OFF_G + 2 * LANES
G_AI, G_AF, G_BA, G_BB = 0, H_A, 2 * H_A, 2 * H_A + H_B
TN = 1024
INPROJ_MAX_ROWS = 1280
W_SEGMENTS = ((OFF_AQ, OFF_BQ, 0), (OFF_BQ, OFF_C, 2 * H_A), (OFF_C, OFF_G, 2 * H_A + 2 * H_B))
W_ROW_CHUNK = 256

VMEM_LIMIT = 56 * 1024 * 1024


def _sigmoid(x):
    return 1.0 / (1.0 + jnp.exp(-x))


def _silu(x):
    return x * _sigmoid(x)


def _log1p(u):
    return jnp.log(1.0 + u)


def _softplus(x):
    return jnp.maximum(x, 0.0) + _log1p(jnp.exp(-jnp.abs(x)))


def _log_sigmoid(x):
    return -_softplus(-x)


def _dot(a, b):
    return jnp.dot(a.astype(BF16), b.astype(BF16), preferred_element_type=F32)


def _dot_nt(a, b):
    return lax.dot_general(a.astype(BF16), b.astype(BF16), (((1,), (1,)), ((), ())),
                           preferred_element_type=F32)


def _dot_exact(sel, x):
    hi = x.astype(BF16)
    r1 = x - hi.astype(F32)
    mid = r1.astype(BF16)
    lo = (r1 - mid.astype(F32)).astype(BF16)
    s = sel.astype(BF16)
    return (jnp.dot(s, hi, preferred_element_type=F32)
            + jnp.dot(s, mid, preferred_element_type=F32)
            + jnp.dot(s, lo, preferred_element_type=F32))


def _log2(n):
    k = int(math.log2(n))
    assert (1 << k) == n
    return k


_TRACE_MEMO = {}


def _memo(key, fn):
    if key not in _TRACE_MEMO:
        _TRACE_MEMO[key] = fn()
    return _TRACE_MEMO[key]


def _seq_masks(nseq, c):
    return _memo(("seq", nseq, c), lambda: _seq_masks_uncached(nseq, c))


def _seq_masks_uncached(nseq, c):
    r = nseq * c
    row = lax.broadcasted_iota(jnp.int32, (r, r), 0)
    col = lax.broadcasted_iota(jnp.int32, (r, r), 1)
    if nseq == 1:
        same = row >= 0
    else:
        k = _log2(c)
        same = (row >> k) == (col >> k)
    return row, col, same


def _seq_outer(lhs_t, x, nseq, c):
    if nseq == 1:
        return [jnp.dot(lhs_t, x.astype(BF16), preferred_element_type=F32)]
    seq = lax.broadcasted_iota(jnp.int32, (nseq * c, 1), 0) >> _log2(c)
    wide = jnp.concatenate([jnp.where(seq == i, x, 0.0).astype(BF16) for i in range(nseq)], axis=1)
    out = jnp.dot(lhs_t, wide, preferred_element_type=F32)
    n = x.shape[1]
    return [out[:, i * n:(i + 1) * n] for i in range(nseq)]


def _seq_cummax(x, nseq, c):
    r = nseq * c
    pos = lax.broadcasted_iota(jnp.int32, x.shape, 0)
    if nseq > 1:
        pos = pos & (c - 1)
    k = 1
    while k < c:
        x = jnp.where(pos >= k, jnp.maximum(x, pltpu.roll(x, k, axis=0)), x)
        k *= 2
    assert x.shape[0] == r
    return x


def _head_norm_gate(h, onw, z):
    hn = h * lax.rsqrt(jnp.mean(h * h, axis=1, keepdims=True) + EPS)
    return hn * onw * _silu(z)


def _rows(x, i, c):
    return x[i * c:(i + 1) * c]


def _cat_rows(parts):
    return parts[0] if len(parts) == 1 else jnp.concatenate(parts, axis=0)


def _chunk_shift(jj, r0):
    p = jj * TN + r0
    for p0, p1, shift in W_SEGMENTS:
        if p0 <= p and p + W_ROW_CHUNK <= p1:
            return shift
    return None


def _fill_weight_tile(w0_ref, w1_ref, wg_ref, wb_scr, tail_scr, jj):
    for r0 in range(0, TN, W_ROW_CHUNK):
        shift = _chunk_shift(jj, r0)
        dst = slice(r0, r0 + W_ROW_CHUNK)
        if shift is None:
            g0 = OFF_G - jj * TN
            assert r0 <= g0 and g0 + LANES <= r0 + W_ROW_CHUNK
            wb_scr[dst, :] = jnp.zeros((W_ROW_CHUNK, D_MODEL), BF16)
            wb_scr[g0:g0 + LANES, :] = wg_ref[...].astype(BF16)
        elif r0 + shift + W_ROW_CHUNK <= TN:
            wb_scr[dst, :] = w0_ref[r0 + shift:r0 + shift + W_ROW_CHUNK, :].astype(BF16)
        else:
            tail_scr[0:W_ROW_CHUNK, :] = w0_ref[r0:TN, :]
            tail_scr[W_ROW_CHUNK:, :] = w1_ref[...]
            wb_scr[dst, :] = tail_scr[shift:shift + W_ROW_CHUNK, :].astype(BF16)


def _inproj_kernel(h_ref, w0_ref, w1_ref, wg_ref, o_ref, wb_scr, tail_scr):
    j = pl.program_id(0)

    @pl.when(pl.program_id(1) == 0)
    def _():
        for jj in range(N_PACK // TN):
            @pl.when(j == jj)
            def _(jj=jj):
                _fill_weight_tile(w0_ref, w1_ref, wg_ref, wb_scr, tail_scr, jj)

    o_ref[...] = lax.dot_general(h_ref[...], wb_scr[...], (((1,), (1,)), ((), ())),
                                 preferred_element_type=F32)


W_TAIL_ROWS = 32


def _inproj(h, w_t, w_gate_t, layer, tm):
    t = h.shape[0]
    n_tiles = N_PACK // TN
    assert n_tiles - 1 <= (w_t.shape[1] - 1) // TN and W_TAIL_ROWS >= W_SEGMENTS[-1][2]
    tail_per_tile = TN // W_TAIL_ROWS
    last_tail = (w_t.shape[1] - 1) // W_TAIL_ROWS
    return pl.pallas_call(
        _inproj_kernel,
        grid=(n_tiles, t // tm),
        in_specs=[
            pl.BlockSpec((tm, D_MODEL), lambda j, i: (i, 0)),
            pl.BlockSpec((None, TN, D_MODEL), lambda j, i: (layer, j, 0)),
            pl.BlockSpec((None, W_TAIL_ROWS, D_MODEL),
                         lambda j, i: (layer, jnp.minimum((j + 1) * tail_per_tile, last_tail), 0)),
            pl.BlockSpec((None, LANES, D_MODEL), lambda j, i: (layer, 0, 0)),
        ],
        out_specs=pl.BlockSpec((tm, TN), lambda j, i: (i, j)),
        out_shape=jax.ShapeDtypeStruct((t, N_PACK), F32),
        scratch_shapes=[pltpu.VMEM((TN, D_MODEL), BF16),
                        pltpu.VMEM((W_ROW_CHUNK + W_TAIL_ROWS, D_MODEL), F32)],
        compiler_params=pltpu.CompilerParams(
            dimension_semantics=("arbitrary", "arbitrary"), vmem_limit_bytes=VMEM_LIMIT),
        name="inproj",
    )(h, w_t, w_t, w_gate_t)


def _rmsnorm(x, w):
    ms = jnp.mean(x * x, axis=1, keepdims=True)
    return x * lax.rsqrt(ms + EPS) * w


def _prep_kernel(xp_ref, xs_ref, meta_ref, nw_ref, x_ref, h_ref, *, n_chunk, bsz):
    i = pl.program_id(0)
    d = x_ref.shape[1]

    def emit(x):
        x_ref[...] = x
        h_ref[...] = _rmsnorm(x, nw_ref[...]).astype(BF16)

    @pl.when(i == 0)
    def _():
        slot = jnp.concatenate([jnp.zeros((TILE_R - N_META, d), F32), meta_ref[...]], axis=0)
        emit(jnp.concatenate([slot] * bsz, axis=0))

    @pl.when((i > 0) & (i < n_chunk))
    def _():
        emit(xp_ref[...].reshape(bsz * TILE_R, d))

    @pl.when(i >= n_chunk)
    def _():
        emit(xs_ref[...])


def _prep(x_prompt, x_sample, meta, norm_w3, bsz, n_chunk):
    d = x_prompt.shape[-1]
    tm = bsz * TILE_R
    n_dec = x_sample.shape[0] * x_sample.shape[1]
    assert n_dec % tm == 0
    xp = x_prompt.reshape(bsz, n_chunk - 1, TILE_R, d)
    xs = x_sample.reshape(n_dec, d)
    n_blk = n_chunk + n_dec // tm
    t = n_blk * tm
    return pl.pallas_call(
        functools.partial(_prep_kernel, n_chunk=n_chunk, bsz=bsz),
        grid=(n_blk,),
        in_specs=[
            pl.BlockSpec((bsz, None, TILE_R, d), lambda i: (0, jnp.clip(i - 1, 0, n_chunk - 2), 0, 0)),
            pl.BlockSpec((tm, d), lambda i: (jnp.clip(i - n_chunk, 0, n_dec // tm - 1), 0)),
            pl.BlockSpec((N_META, d), lambda i: (0, 0)),
            pl.BlockSpec((None, 1, d), lambda i: (0, 0, 0)),
        ],
        out_specs=[pl.BlockSpec((tm, d), lambda i: (i, 0)),
                   pl.BlockSpec((tm, d), lambda i: (i, 0))],
        out_shape=[jax.ShapeDtypeStruct((t, d), F32), jax.ShapeDtypeStruct((t, d), BF16)],
        compiler_params=pltpu.CompilerParams(
            dimension_semantics=("arbitrary",), vmem_limit_bytes=VMEM_LIMIT),
        name="prep",
    )(xp, xs, meta, norm_w3)


def _outproj_residual(ya_ref, yb_ref, yc_ref, x_ref, w_ref):
    acc = jnp.dot(ya_ref[...], w_ref[0:W_A, :], preferred_element_type=F32)
    acc = acc + jnp.dot(yb_ref[...], w_ref[W_A:W_A + W_B, :], preferred_element_type=F32)
    acc = acc + jnp.dot(yc_ref[...], w_ref[W_A + W_B:MIX, :], preferred_element_type=F32)
    return x_ref[...] + acc


def _outproj_mid_kernel(ya_ref, yb_ref, yc_ref, x_ref, w_ref, nw_ref, x_out, h_out):
    xn = _outproj_residual(ya_ref, yb_ref, yc_ref, x_ref, w_ref)
    x_out[...] = xn
    h_out[...] = _rmsnorm(xn, nw_ref[...]).astype(BF16)


def _outproj_final_kernel(ya_ref, yb_ref, yc_ref, x_ref, w_ref, nw_ref, y_out):
    xn = _outproj_residual(ya_ref, yb_ref, yc_ref, x_ref, w_ref)
    y_out[...] = _rmsnorm(xn, nw_ref[...]).reshape(y_out.shape)


def _outproj_specs(tm, row_blk0, layer, nw_spec):
    rows = lambda i: (row_blk0 + i, 0)
    return [
        pl.BlockSpec((tm, W_A), rows),
        pl.BlockSpec((tm, W_B), rows),
        pl.BlockSpec((tm, W_C), rows),
        pl.BlockSpec((tm, D_MODEL), rows),
        pl.BlockSpec((None, MIX, D_MODEL), lambda i: (layer, 0, 0)),
        nw_spec,
    ]


def _outproj_mid(ya, yb, yc, x, w_out, layer, norm_w3, tm):
    t = x.shape[0]
    blk = pl.BlockSpec((tm, D_MODEL), lambda i: (i, 0))
    nw_spec = pl.BlockSpec((None, 1, D_MODEL), lambda i: (layer + 1, 0, 0))
    return pl.pallas_call(
        _outproj_mid_kernel,
        grid=(t // tm,),
        in_specs=_outproj_specs(tm, 0, layer, nw_spec),
        out_specs=[blk, blk],
        out_shape=[jax.ShapeDtypeStruct((t, D_MODEL), F32), jax.ShapeDtypeStruct((t, D_MODEL), BF16)],
        compiler_params=pltpu.CompilerParams(
            dimension_semantics=("parallel",), vmem_limit_bytes=VMEM_LIMIT),
        name="outproj",
    )(ya, yb, yc, x, w_out, norm_w3)


def _outproj_final(ya, yb, yc, x, w_out, layer, final_nw, tm, row_blk0, n_blk, out_shape, out_spec):
    nw_spec = pl.BlockSpec((1, D_MODEL), lambda i: (0, 0))
    return pl.pallas_call(
        _outproj_final_kernel,
        grid=(n_blk,),
        in_specs=_outproj_specs(tm, row_blk0, layer, nw_spec),
        out_specs=out_spec,
        out_shape=jax.ShapeDtypeStruct(out_shape, F32),
        compiler_params=pltpu.CompilerParams(
            dimension_semantics=("parallel",), vmem_limit_bytes=VMEM_LIMIT),
        name="outproj_final",
    )(ya, yb, yc, x, w_out, final_nw)


class _RowView:
    def __init__(self, ref, r0, n):
        self.ref, self.r0, self.n = ref, r0, n
        self.shape = (n, ref.shape[1])

    def _idx(self, idx):
        if idx is Ellipsis:
            return slice(self.r0, self.r0 + self.n), slice(None)
        rows, cols = idx
        start = self.r0 + (rows.start or 0)
        stop = self.r0 + (self.n if rows.stop is None else rows.stop)
        return slice(start, stop), cols

    def __getitem__(self, idx):
        return self.ref[self._idx(idx)]

    def __setitem__(self, idx, val):
        self.ref[self._idx(idx)] = val


def _tile_views(refs, u):
    return [_RowView(ref, u * TILE_R, TILE_R) for ref in refs]


def _interleave(tiles):
    tiles = list(tiles)
    while tiles:
        alive = []
        for t in tiles:
            try:
                next(t)
                alive.append(t)
            except StopIteration:
                pass
        tiles = alive


def _mlstm_tile(q_ref, k_ref, v_ref, o_ref, z_ref, g_ref, gb_ref, onw_ref, y_ref,
                m_exp, valid, get_c, get_n, put_c, put_n, put_m, nseq, c):
    r = nseq * c
    row, col, same = _seq_masks(nseq, c)
    incl = _memo(("incl", nseq, c), lambda: same & (col <= row))
    g = g_ref[...] + gb_ref[...]
    li = g
    lf = _log_sigmoid(g)
    if valid is not None:
        li = jnp.where(valid, li, NEG_BIG)
        lf = jnp.where(valid, lf, 0.0)
    fcum = _dot_exact(incl.astype(F32), lf)
    if nseq == 1:
        flast = jnp.broadcast_to(fcum[r - 1:r, :], (r, LANES))
    else:
        flast = _dot_exact(same.astype(F32), lf)
    yield
    heads = range(H_A)
    hsl = [slice(h * HEAD_DIM, (h + 1) * HEAD_DIM) for h in heads]

    li_a = pltpu.roll(li, G_AF - G_AI, axis=1)
    m_a = pltpu.roll(m_exp, G_AF, axis=1)
    x = li_a - fcum
    mx = jnp.maximum(m_a, _seq_cummax(x, nseq, c))
    mt_all = fcum + mx
    if nseq == 1:
        m_new = jnp.broadcast_to(mt_all[r - 1:r, :], (r, LANES))
    else:
        m_new = jnp.broadcast_to(mt_all.reshape(nseq, c, LANES)[:, c - 1:c, :],
                                 (nseq, c, LANES)).reshape(r, LANES)
    a_all = jnp.exp(m_a - mx)
    w_all = jnp.exp(flast - fcum + li_a - m_new)
    as_all = jnp.exp(flast + m_a - m_new)
    low_all = jnp.exp(jnp.minimum(-mt_all, EXP_CLIP))
    put_m(pltpu.roll(m_new, LANES - G_AF, axis=1))
    x_t = x.T
    col_of = lambda arr, h: arr[:, G_AF + h:G_AF + h + 1]
    a = [col_of(a_all, h) for h in heads]
    w = [col_of(w_all, h) for h in heads]
    a_s = [col_of(as_all, h) for h in heads]
    low = [col_of(low_all, h) for h in heads]
    pexp = [jnp.exp(jnp.where(incl, x_t[G_AF + h:G_AF + h + 1, :] - col_of(mx, h), NEG_BIG))
            for h in heads]
    yield

    s = [_dot_nt(q_ref[:, hs], k_ref[:, hs] * QK_SCALE) for hs in hsl]
    yield
    p = [pexp[h] * s[h] for h in heads]
    pv = [_dot(p[h], v_ref[:, hsl[h]]) for h in heads]
    yield
    qc = [_cat_rows([_dot(_rows(q_ref[:, hsl[h]], i, c), get_c(i, h)) for i in range(nseq)])
          for h in heads]
    yield
    psum = [jnp.sum(p[h], axis=1, keepdims=True) for h in heads]
    qn = [_cat_rows([jnp.sum(_rows(q_ref[:, hsl[h]], i, c) * get_n(i, h), axis=1, keepdims=True)
                     for i in range(nseq)]) for h in heads]
    yield
    hh = [(a[h] * qc[h] + pv[h]) / jnp.maximum(jnp.abs(a[h] * qn[h] + psum[h]), low[h])
          * _sigmoid(o_ref[:, hsl[h]]) for h in heads]
    ms = [jnp.mean(hh[h] * hh[h], axis=1, keepdims=True) for h in heads]
    yield
    for h in heads:
        hs = hsl[h]
        y_ref[:, hs] = (hh[h] * lax.rsqrt(ms[h] + EPS) * onw_ref[:, hs]
                        * _silu(z_ref[:, hs])).astype(BF16)
    yield

    for h in heads:
        hs = hsl[h]
        kh = k_ref[:, hs] * QK_SCALE
        wk = w[h] * kh
        upd = _seq_outer(kh.T.astype(BF16), w[h] * v_ref[:, hs], nseq, c)
        for i in range(nseq):
            a_i = a_s[h][i * c:i * c + 1, :]
            put_c(i, h, a_i * get_c(i, h) + upd[i])
            put_n(i, h, a_i * get_n(i, h) + jnp.sum(_rows(wk, i, c), axis=0, keepdims=True))


def _mlstm_prompt_kernel(q_ref, k_ref, v_ref, o_ref, z_ref, g_ref, gb_ref, onw_ref,
                         y_ref, c_out, n_out, m_out, c_scr, n_scr, m_scr):
    _TRACE_MEMO.clear()
    j = pl.program_id(0)
    bsz = c_scr.shape[0]

    @pl.when(j == 0)
    def _():
        c_scr[...] = jnp.zeros_like(c_scr)
        n_scr[...] = jnp.zeros_like(n_scr)
        m_scr[...] = jnp.zeros_like(m_scr)

    tok = lax.broadcasted_iota(jnp.int32, (TILE_R, LANES), 0) + j * TILE_R
    valid = tok >= TILE_R - N_META

    def tile(u):
        q, k, v, o, z, g, y = _tile_views((q_ref, k_ref, v_ref, o_ref, z_ref, g_ref, y_ref), u)
        m_exp = jnp.broadcast_to(m_scr[u, 0:1, :], (TILE_R, LANES))

        def put_c(i, h, val):
            c_scr[u, h] = val

        def put_n(i, h, val):
            n_scr[u, h:h + 1, :] = val

        def put_m(val):
            m_scr[u] = val[0:8, :]

        return _mlstm_tile(q, k, v, o, z, g, gb_ref, onw_ref, y, m_exp, valid,
                           lambda i, h: c_scr[u, h], lambda i, h: n_scr[u, h:h + 1, :],
                           put_c, put_n, put_m, 1, TILE_R)

    _interleave(tile(u) for u in range(bsz))

    @pl.when(j == pl.num_programs(0) - 1)
    def _():
        c_out[...] = c_scr[...]
        n_out[...] = n_scr[:, 0:H_A, :]
        m_out[...] = m_scr[:, 0:1, :]


def _mlstm_sample_kernel(q_ref, k_ref, v_ref, o_ref, z_ref, g_ref, gb_ref, onw_ref,
                         c_in, n_in, m_in, y_ref, c_out, n_out, m_out, *, c):
    _TRACE_MEMO.clear()
    nseq = TILE_R // c

    def put_c(i, h, val):
        c_out[i, h] = val

    def put_n(i, h, val):
        n_out[i, h:h + 1, :] = val

    def put_m(val):
        m_out[...] = val

    _interleave([_mlstm_tile(
        q_ref, k_ref, v_ref, o_ref, z_ref, g_ref, gb_ref, onw_ref, y_ref, m_in[...], None,
        lambda i, h: c_in[i, h], lambda i, h: n_in[i, h:h + 1, :], put_c, put_n, put_m, nseq, c)])


def _col_spec(width, off, rows, n_rows=TILE_R):
    assert off % width == 0
    blk = off // width
    return pl.BlockSpec((n_rows, width), lambda *ids: (rows(*ids), blk))


def _param_spec(width, off, layer):
    assert off % width == 0
    blk = off // width
    return pl.BlockSpec((None, 1, width), lambda *ids: (layer, 0, blk))


def _whole(*shape):
    return pl.BlockSpec(shape, lambda j: (0,) * len(shape))


def _mlstm_prompt(proj, gate_bias, onw3, layer, bsz, n_chunk):
    t = proj.shape[0]
    tm = bsz * TILE_R
    rows = lambda j: j
    in_specs = [_col_spec(W_A, off, rows, tm) for off in (OFF_AQ, OFF_AK, OFF_AV, OFF_AO, OFF_ZA)]
    in_specs += [_col_spec(LANES, OFF_G, rows, tm), _param_spec(LANES, 0, layer),
                 _param_spec(W_A, 0, layer)]
    return pl.pallas_call(
        _mlstm_prompt_kernel,
        grid=(n_chunk,),
        in_specs=in_specs,
        out_specs=[
            pl.BlockSpec((tm, W_A), lambda j: (j, 0)),
            _whole(bsz, H_A, HEAD_DIM, HEAD_DIM),
            _whole(bsz, H_A, HEAD_DIM),
            _whole(bsz, 1, LANES),
        ],
        out_shape=[
            jax.ShapeDtypeStruct((t, W_A), BF16),
            jax.ShapeDtypeStruct((bsz, H_A, HEAD_DIM, HEAD_DIM), F32),
            jax.ShapeDtypeStruct((bsz, H_A, HEAD_DIM), F32),
            jax.ShapeDtypeStruct((bsz, 1, LANES), F32),
        ],
        scratch_shapes=[pltpu.VMEM((bsz, H_A, HEAD_DIM, HEAD_DIM), F32),
                        pltpu.VMEM((bsz, 8, HEAD_DIM), F32), pltpu.VMEM((bsz, 8, LANES), F32)],
        compiler_params=pltpu.CompilerParams(
            dimension_semantics=("arbitrary",), vmem_limit_bytes=VMEM_LIMIT),
        name="mlstm_prompt",
    )(proj, proj, proj, proj, proj, proj, gate_bias, onw3)


def _in_place(operands, n_blocked, targets):
    specs, aliases = [], {}
    operands = list(operands)
    for out_idx, arr in targets.items():
        if arr is not None:
            aliases[len(operands)] = out_idx
            operands.append(arr)
            specs.append(pl.BlockSpec(memory_space=pl.ANY))
    n_extra = len(specs)

    def strip(kernel):
        def body(*refs):
            kernel(*refs[:n_blocked], *refs[n_blocked + n_extra:])
        return body

    return operands, specs, aliases, strip


def _mlstm_sample(proj, y_prev, c_stack, gate_bias, onw3, c_state, n_state, m_rows, layer,
                  dec_b, c, rows):
    nseq = TILE_R // c
    depth = c_state.shape[0]
    in_specs = [_col_spec(W_A, off, rows) for off in (OFF_AQ, OFF_AK, OFF_AV, OFF_AO, OFF_ZA)]
    in_specs += [_col_spec(LANES, OFF_G, rows), _param_spec(LANES, 0, layer),
                 _param_spec(W_A, 0, layer)]
    in_specs += [
        pl.BlockSpec((None, nseq, H_A, HEAD_DIM, HEAD_DIM), lambda i: (layer, i, 0, 0, 0)),
        pl.BlockSpec((None, nseq, H_A, HEAD_DIM), lambda i: (layer, i, 0, 0)),
        pl.BlockSpec((None, TILE_R, LANES), lambda i: (layer, i, 0)),
    ]
    operands = (proj, proj, proj, proj, proj, proj, gate_bias, onw3, c_state, n_state, m_rows)
    operands, extra, aliases, strip = _in_place(operands, len(in_specs), {0: y_prev, 1: c_stack})
    return pl.pallas_call(
        strip(functools.partial(_mlstm_sample_kernel, c=c)),
        grid=(dec_b // nseq,),
        in_specs=in_specs + extra,
        out_specs=[
            pl.BlockSpec((TILE_R, W_A), lambda i: (rows(i), 0)),
            pl.BlockSpec((None, nseq, H_A, HEAD_DIM, HEAD_DIM), lambda i: (layer, i, 0, 0, 0)),
            pl.BlockSpec((nseq, H_A, HEAD_DIM), lambda i: (i, 0, 0)),
            pl.BlockSpec((TILE_R, LANES), lambda i: (i, 0)),
        ],
        out_shape=[
            jax.ShapeDtypeStruct(y_prev.shape, BF16),
            jax.ShapeDtypeStruct((depth, dec_b, H_A, HEAD_DIM, HEAD_DIM), F32),
            jax.ShapeDtypeStruct((dec_b, H_A, HEAD_DIM), F32),
            jax.ShapeDtypeStruct((dec_b * c, LANES), F32),
        ],
        input_output_aliases=aliases,
        compiler_params=pltpu.CompilerParams(
            dimension_semantics=("parallel",), vmem_limit_bytes=VMEM_LIMIT),
        name="mlstm_sample",
    )(*operands)


def _l2norm(x):
    return x * lax.rsqrt(jnp.sum(x * x, axis=1, keepdims=True) + EPS)


INV_LEAF = 16


def _half_block_mask(row, col, k):
    return (((row >> (k + 1)) == (col >> (k + 1)))
            & (((row >> k) & 1) == 1) & (((col >> k) & 1) == 0))


def _unit_lower_inverse(a_list, row, col, c):
    leaf = min(INV_LEAF, c)
    kl = _log2(leaf)
    leaf_mask = _memo(("leaf", c), lambda: (row >> kl) == (col >> kl))
    eye = _memo(("eye", c), lambda: jnp.where(row == col, 1.0, 0.0))
    p = [jnp.where(leaf_mask, -a, 0.0) for a in a_list]
    t = [eye + n for n in p]
    for _ in range(kl - 1):
        p = [_dot(x, x) for x in p]
        yield
        t = [ti + _dot(pi, ti) for pi, ti in zip(p, t)]
        yield
    w = leaf
    while w < c:
        kw = _log2(w)
        lower_left = _memo(("lower_left", c, w), lambda: _half_block_mask(row, col, kw))
        left = [_dot(ti, jnp.where(lower_left, a, 0.0)) for ti, a in zip(t, a_list)]
        yield
        t = [ti - _dot(li, ti) for li, ti in zip(left, t)]
        yield
        w *= 2
    return t


def _gdn_tile(u_cols, prev_cols, cw_ref, z_ref, g_ref, gp_ref, onw_ref, y_ref, get_s, put_s,
              nseq, c):
    r = nseq * c
    row, col, same = _seq_masks(nseq, c)
    incl = _memo(("incl", nseq, c), lambda: same & (col <= row))
    strict = _memo(("strict", nseq, c), lambda: same & (col < row))
    gp = gp_ref[...]
    graw = g_ref[...]
    gdec = -jnp.exp(gp[1:2, :]) * _softplus(graw + gp[0:1, :])
    beta = _sigmoid(graw)
    gcum = _dot_exact(incl.astype(F32), gdec)
    if nseq == 1:
        glast = jnp.broadcast_to(gcum[r - 1:r, :], (r, LANES))
    else:
        glast = _dot_exact(same.astype(F32), gdec)
    yield
    gcum_t = gcum.T

    heads = range(H_B)
    hsl = [slice(h * HEAD_DIM, (h + 1) * HEAD_DIM) for h in heads]

    pos = lax.broadcasted_iota(jnp.int32, (r, HEAD_DIM), 0)
    if nseq > 1:
        pos = pos & (c - 1)

    def conv(lo):
        cols = slice(lo, lo + HEAD_DIM)
        x = u_cols(cols)
        prev = prev_cols(cols)
        acc = x * cw_ref[CONV_W - 1:CONV_W, cols]
        for back in range(1, CONV_W):
            shifted = jnp.where(pos < back, pltpu.roll(prev, (back - CONV_PAD) % r, axis=0),
                                pltpu.roll(x, back, axis=0))
            acc = acc + shifted * cw_ref[CONV_W - 1 - back:CONV_W - back, cols]
        return _silu(acc)

    q = [_l2norm(conv(hs.start)) * QK_SCALE for hs in hsl]
    k = [_l2norm(conv(W_B + hs.start)) for hs in hsl]
    v = [conv(2 * W_B + hs.start) for hs in hsl]
    yield
    gc = [gcum[:, G_BA + h:G_BA + h + 1] for h in heads]
    gl = [glast[:, G_BA + h:G_BA + h + 1] for h in heads]
    bc = [beta[:, G_BB + h:G_BB + h + 1] for h in heads]
    decay = [jnp.exp(jnp.where(incl, gc[h] - gcum_t[G_BA + h:G_BA + h + 1, :], NEG_BIG))
             for h in heads]
    kk = [_dot_nt(k[h], k[h]) for h in heads]
    qk = [_dot_nt(q[h], k[h]) for h in heads]
    yield
    amat = [jnp.where(strict, bc[h] * decay[h] * kk[h], 0.0) for h in heads]
    tinv = yield from _unit_lower_inverse(amat, row, col, c)
    sol = [_dot(tinv[h], jnp.concatenate([bc[h] * v[h], (bc[h] * jnp.exp(gc[h])) * k[h]], axis=1))
           for h in heads]
    yield
    both = [[_dot(jnp.concatenate([_rows(sol[h][:, HEAD_DIM:], i, c), _rows(q[h], i, c)], axis=0),
                  get_s(i, h)) for i in range(nseq)] for h in heads]
    wks = [_cat_rows([both[h][i][:c] for i in range(nseq)]) for h in heads]
    qs = [_cat_rows([both[h][i][c:] for i in range(nseq)]) for h in heads]
    yield
    wmat = [sol[h][:, :HEAD_DIM] - wks[h] for h in heads]
    aw = [_dot(decay[h] * qk[h], wmat[h]) for h in heads]
    yield
    for h in heads:
        hs = hsl[h]
        oh = jnp.exp(gc[h]) * qs[h] + aw[h]
        y_ref[:, hs] = _head_norm_gate(oh, onw_ref[:, hs], z_ref[:, hs]).astype(BF16)
    yield
    for h in heads:
        kd_t = (k[h] * jnp.exp(gl[h] - gc[h])).T.astype(BF16)
        e_last = jnp.exp(gl[h])
        upd = _seq_outer(kd_t, wmat[h], nseq, c)
        for i in range(nseq):
            put_s(i, h, e_last[i * c:i * c + 1, :] * get_s(i, h) + upd[i])


def _gdn_prompt_kernel(q_ref, k_ref, v_ref, z_ref, g_ref, gp_ref, cw_ref, onw_ref,
                       y_ref, s_out, conv_out, s_scr, hist_scr):
    _TRACE_MEMO.clear()
    j = pl.program_id(0)
    bsz = s_scr.shape[0]
    pieces = ((q_ref, 0), (k_ref, W_B), (v_ref, 2 * W_B))

    @pl.when(j == 0)
    def _():
        s_scr[...] = jnp.zeros_like(s_scr)
        hist_scr[...] = jnp.zeros_like(hist_scr)

    def tile(u):
        z, g, y = _tile_views((z_ref, g_ref, y_ref), u)

        def u_cols(cols):
            ref, off = pieces[cols.start // W_B]
            return ref[u * TILE_R:(u + 1) * TILE_R, cols.start - off:cols.stop - off]

        def prev_cols(cols):
            return jnp.concatenate([hist_scr[u, :, cols], u_cols(cols)[CONV_PAD:]], axis=0)

        def put_s(i, h, val):
            s_scr[u, h] = val

        return _gdn_tile(u_cols, prev_cols, cw_ref, z, g, gp_ref, onw_ref, y,
                         lambda i, h: s_scr[u, h], put_s, 1, TILE_R)

    _interleave(tile(u) for u in range(bsz))

    for u in range(bsz):
        for ref, off in pieces:
            hist_scr[u, :, off:off + W_B] = ref[(u + 1) * TILE_R - CONV_PAD:(u + 1) * TILE_R, :]

    @pl.when(j == pl.num_programs(0) - 1)
    def _():
        s_out[...] = s_scr[...]
        conv_out[...] = hist_scr[:, CONV_PAD - (CONV_W - 1):, :]


def _gdn_sample_kernel(q_ref, k_ref, v_ref, z_ref, g_ref, gp_ref, cw_ref, onw_ref,
                       s_in, conv_in, y_ref, s_out, conv_out, hist_scr, *, c):
    _TRACE_MEMO.clear()
    nseq = TILE_R // c
    hist = CONV_W - 1
    assert c == CONV_PAD
    pieces = ((q_ref, 0), (k_ref, W_B), (v_ref, 2 * W_B))
    hist_scr[:, 0:CONV_PAD - hist, :] = jnp.zeros((nseq, CONV_PAD - hist, 3 * W_B), F32)
    hist_scr[:, CONV_PAD - hist:, :] = conv_in[...]

    def u_cols(cols):
        ref, off = pieces[cols.start // W_B]
        return ref[:, cols.start - off:cols.stop - off]

    def prev_cols(cols):
        return hist_scr[:, :, cols].reshape(TILE_R, cols.stop - cols.start)

    def put_s(i, h, val):
        s_out[i, h] = val

    _interleave([_gdn_tile(u_cols, prev_cols, cw_ref, z_ref, g_ref, gp_ref, onw_ref, y_ref,
                           lambda i, h: s_in[i, h], put_s, nseq, c)])
    for ref, off in pieces:
        conv_out[:, :, off:off + W_B] = ref[...].reshape(nseq, c, W_B)[:, c - hist:, :]


def _gdn_prompt(proj, gdn_par, conv_w, onw3, layer, bsz, n_chunk):
    t = proj.shape[0]
    tm = bsz * TILE_R
    rows = lambda j: j
    in_specs = [_col_spec(W_B, off, rows, tm) for off in (OFF_BQ, OFF_BK, OFF_BV, OFF_ZB)]
    in_specs += [
        _col_spec(LANES, OFF_G, rows, tm),
        pl.BlockSpec((None, 8, LANES), lambda j: (layer, 0, 0)),
        pl.BlockSpec((None, CONV_W, 3 * W_B), lambda j: (layer, 0, 0)),
        _param_spec(W_B, W_A, layer),
    ]
    return pl.pallas_call(
        _gdn_prompt_kernel,
        grid=(n_chunk,),
        in_specs=in_specs,
        out_specs=[
            pl.BlockSpec((tm, W_B), lambda j: (j, 0)),
            _whole(bsz, H_B, HEAD_DIM, HEAD_DIM),
            _whole(bsz, CONV_W - 1, 3 * W_B),
        ],
        out_shape=[
            jax.ShapeDtypeStruct((t, W_B), BF16),
            jax.ShapeDtypeStruct((bsz, H_B, HEAD_DIM, HEAD_DIM), F32),
            jax.ShapeDtypeStruct((bsz, CONV_W - 1, 3 * W_B), F32),
        ],
        scratch_shapes=[pltpu.VMEM((bsz, H_B, HEAD_DIM, HEAD_DIM), F32),
                        pltpu.VMEM((bsz, CONV_PAD, 3 * W_B), F32)],
        compiler_params=pltpu.CompilerParams(
            dimension_semantics=("arbitrary",), vmem_limit_bytes=VMEM_LIMIT),
        name="gdn_prompt",
    )(proj, proj, proj, proj, proj, gdn_par, conv_w, onw3)


def _gdn_sample(proj, y_prev, s_stack, gdn_par, conv_w, onw3, s_state, conv_state, layer,
                dec_b, c, rows):
    nseq = TILE_R // c
    depth = s_state.shape[0]
    in_specs = [_col_spec(W_B, off, rows) for off in (OFF_BQ, OFF_BK, OFF_BV, OFF_ZB)]
    in_specs += [
        _col_spec(LANES, OFF_G, rows),
        pl.BlockSpec((None, 8, LANES), lambda i: (layer, 0, 0)),
        pl.BlockSpec((None, CONV_W, 3 * W_B), lambda i: (layer, 0, 0)),
        _param_spec(W_B, W_A, layer),
        pl.BlockSpec((None, nseq, H_B, HEAD_DIM, HEAD_DIM), lambda i: (layer, i, 0, 0, 0)),
        pl.BlockSpec((None, nseq, CONV_W - 1, 3 * W_B), lambda i: (layer, i, 0, 0)),
    ]
    operands = (proj, proj, proj, proj, proj, gdn_par, conv_w, onw3, s_state, conv_state)
    operands, extra, aliases, strip = _in_place(operands, len(in_specs), {0: y_prev, 1: s_stack})
    return pl.pallas_call(
        strip(functools.partial(_gdn_sample_kernel, c=c)),
        grid=(dec_b // nseq,),
        in_specs=in_specs + extra,
        out_specs=[
            pl.BlockSpec((TILE_R, W_B), lambda i: (rows(i), 0)),
            pl.BlockSpec((None, nseq, H_B, HEAD_DIM, HEAD_DIM), lambda i: (layer, i, 0, 0, 0)),
            pl.BlockSpec((nseq, CONV_W - 1, 3 * W_B), lambda i: (i, 0, 0)),
        ],
        out_shape=[
            jax.ShapeDtypeStruct(y_prev.shape, BF16),
            jax.ShapeDtypeStruct((depth, dec_b, H_B, HEAD_DIM, HEAD_DIM), F32),
            jax.ShapeDtypeStruct((dec_b, CONV_W - 1, 3 * W_B), F32),
        ],
        scratch_shapes=[pltpu.VMEM((nseq, CONV_PAD, 3 * W_B), F32)],
        input_output_aliases=aliases,
        compiler_params=pltpu.CompilerParams(
            dimension_semantics=("parallel",), vmem_limit_bytes=VMEM_LIMIT),
        name="gdn_sample",
    )(*operands)


def _hgrn_tile(q_ref, f_ref, i_ref, z_ref, lbp_ref, onw_ref, y_ref, get_s, put_s,
               nseq, c, layer):
    r = nseq * c
    row, col, same = _seq_masks(nseq, c)
    incl = _memo(("incl", nseq, c), lambda: same & (col <= row))
    rowv = lax.broadcasted_iota(jnp.int32, (r, 1), 0)
    lbp = lbp_ref[...]
    e = jnp.exp(lbp - jnp.max(lbp, axis=0, keepdims=True))
    sm = e / jnp.sum(e, axis=0, keepdims=True)
    lb = jnp.zeros((1, W_C), F32)
    for l in range(1, layer + 1):
        lb = lb + sm[l:l + 1, :]

    cf = f_ref[...]
    logf = _log_sigmoid(cf)
    if layer > 0:
        logf = logf + _log1p(lb * jnp.exp(jnp.minimum(-cf, EXP_CLIP)))
    kall = (1.0 - lb) * _sigmoid(-cf)
    bcum = _dot_exact(incl.astype(F32), logf)
    if nseq == 1:
        blast = jnp.broadcast_to(bcum[r - 1:r, :], (r, W_C))
    else:
        blast = _dot_exact(same.astype(F32), logf)

    levels = []
    w = c // 2
    while w >= 1:
        levels.append(w)
        w //= 2
    rowf = lax.broadcasted_iota(jnp.int32, (r, W_C), 0)
    refs = []
    for w in levels:
        if 2 * w >= 8:
            refs.append(_cat_rows([
                jnp.broadcast_to(bcum[g * 2 * w + w - 1:g * 2 * w + w, :], (2 * w, W_C))
                for g in range(r // (2 * w))]))
        else:
            pos = rowf & (2 * w - 1)
            ref = bcum
            for off in range(2 * w):
                if off != w - 1:
                    ref = jnp.where(pos == off, pltpu.roll(bcum, (off - (w - 1)) % r, axis=0), ref)
            refs.append(ref)

    heads = range(H_C)
    hsl = [slice(h * HEAD_DIM, (h + 1) * HEAD_DIM) for h in heads]
    q = [_silu(q_ref[:, hs]) for hs in hsl]
    k = [kall[:, hs] for hs in hsl]
    b = [bcum[:, hs] for hs in hsl]
    diag = _memo(("diag", nseq, c), lambda: row == col)
    yield
    amat = [jnp.where(diag, _dot_nt(q[h], k[h]), 0.0) for h in heads]
    for w, ref_all in zip(levels, refs):
        yield
        lw = _log2(w)
        level = _memo(("lower_left", c, w), lambda: _half_block_mask(row, col, lw))
        e = [jnp.exp(-jnp.abs(b[h] - ref_all[:, hsl[h]])) for h in heads]
        amat = [jnp.where(level, _dot_nt(q[h] * e[h], k[h] * e[h]), amat[h]) for h in heads]
    yield
    qs = [_cat_rows([_dot(_rows(q[h] * jnp.exp(b[h]), i, c), get_s(i, h)) for i in range(nseq)])
          for h in heads]
    av = [_dot(amat[h], i_ref[:, hsl[h]]) for h in heads]
    yield
    for h in heads:
        hs = hsl[h]
        y_ref[:, hs] = _head_norm_gate(qs[h] + av[h], onw_ref[:, hs], z_ref[:, hs]).astype(BF16)
    yield
    for h in heads:
        hs = hsl[h]
        vh = i_ref[:, hs]
        bl = blast[:, hs]
        kd_t = (k[h] * jnp.exp(bl - b[h])).T.astype(BF16)
        eb_t = jnp.exp(bl).T
        upd = _seq_outer(kd_t, vh, nseq, c)
        for i in range(nseq):
            put_s(i, h, eb_t[:, i * c:i * c + 1] * get_s(i, h) + upd[i])


class _ColView:
    def __init__(self, parts):
        self.parts = []
        pos = 0
        for ref, lo, hi in parts:
            self.parts.append((pos, ref, lo, hi - lo))
            pos += hi - lo
        self.width = pos

    def __getitem__(self, idx):
        if idx is Ellipsis:
            rows, cols = slice(None), slice(0, self.width)
        else:
            rows, cols = idx
            cols = slice(cols.start or 0, self.width if cols.stop is None else cols.stop)
        out = []
        for pos, ref, lo, n in self.parts:
            a, b = max(cols.start, pos), min(cols.stop, pos + n)
            if a < b:
                out.append(ref[rows, lo + a - pos:lo + b - pos])
        return out[0] if len(out) == 1 else jnp.concatenate(out, axis=1)


def _hgrn_views(ca_ref, cb_ref, zc_ref):
    two = 2 * W_C - ca_ref.shape[1]
    q = _ColView([(ca_ref, 0, W_C)])
    f = _ColView([(ca_ref, W_C, ca_ref.shape[1]), (cb_ref, 0, two)])
    i = _ColView([(cb_ref, two, two + W_C)])
    z = _ColView([(zc_ref, 0, W_C)])
    return q, f, i, z


def _hgrn_prompt_kernel(ca_ref, cb_ref, zc_ref, lbp_ref, onw_ref, y_ref, s_out, s_scr,
                        *, layer):
    _TRACE_MEMO.clear()
    j = pl.program_id(0)
    bsz = s_scr.shape[0]

    @pl.when(j == 0)
    def _():
        s_scr[...] = jnp.zeros_like(s_scr)

    def tile(u):
        ca, cb, zc, y = _tile_views((ca_ref, cb_ref, zc_ref, y_ref), u)

        def put_s(i, h, val):
            s_scr[u, h] = val

        return _hgrn_tile(*_hgrn_views(ca, cb, zc), lbp_ref, onw_ref, y,
                          lambda i, h: s_scr[u, h], put_s, 1, TILE_R, layer)

    _interleave(tile(u) for u in range(bsz))

    @pl.when(j == pl.num_programs(0) - 1)
    def _():
        s_out[...] = s_scr[...]


def _hgrn_sample_kernel(ca_ref, cb_ref, zc_ref, lbp_ref, onw_ref, s_in, y_ref, s_out,
                        *, c, layer):
    _TRACE_MEMO.clear()

    def put_s(i, h, val):
        s_out[i, h] = val

    _interleave([_hgrn_tile(*_hgrn_views(ca_ref, cb_ref, zc_ref), lbp_ref, onw_ref, y_ref,
                            lambda i, h: s_in[i, h], put_s, TILE_R // c, c, layer)])


HGRN_BLK = W_A
assert OFF_C % HGRN_BLK == 0 and 3 * W_C == 2 * HGRN_BLK and OFF_ZC % HGRN_BLK == 0


def _hgrn_prompt(proj, lb_par, onw3, layer, bsz, n_chunk):
    t = proj.shape[0]
    tm = bsz * TILE_R
    depth = lb_par.shape[0]
    rows = lambda j: j
    in_specs = [_col_spec(HGRN_BLK, off, rows, tm) for off in (OFF_C, OFF_C + HGRN_BLK, OFF_ZC)]
    in_specs += [pl.BlockSpec((depth, W_C), lambda j: (0, 0)),
                 _param_spec(W_C, W_A + W_B, layer)]
    return pl.pallas_call(
        functools.partial(_hgrn_prompt_kernel, layer=layer),
        grid=(n_chunk,),
        in_specs=in_specs,
        out_specs=[
            pl.BlockSpec((tm, W_C), lambda j: (j, 0)),
            _whole(bsz, H_C, HEAD_DIM, HEAD_DIM),
        ],
        out_shape=[
            jax.ShapeDtypeStruct((t, W_C), BF16),
            jax.ShapeDtypeStruct((bsz, H_C, HEAD_DIM, HEAD_DIM), F32),
        ],
        scratch_shapes=[pltpu.VMEM((bsz, H_C, HEAD_DIM, HEAD_DIM), F32)],
        compiler_params=pltpu.CompilerParams(
            dimension_semantics=("arbitrary",), vmem_limit_bytes=VMEM_LIMIT),
        name="hgrn_prompt",
    )(proj, proj, proj, lb_par, onw3)


def _hgrn_sample(proj, y_prev, s_stack, lb_par, onw3, s_state, layer, dec_b, c, rows):
    nseq = TILE_R // c
    depth = lb_par.shape[0]
    in_specs = [_col_spec(HGRN_BLK, off, rows) for off in (OFF_C, OFF_C + HGRN_BLK, OFF_ZC)]
    in_specs += [
        pl.BlockSpec((depth, W_C), lambda i: (0, 0)),
        _param_spec(W_C, W_A + W_B, layer),
        pl.BlockSpec((None, nseq, H_C, HEAD_DIM, HEAD_DIM), lambda i: (layer, i, 0, 0, 0)),
    ]
    operands = (proj, proj, proj, lb_par, onw3, s_state)
    operands, extra, aliases, strip = _in_place(operands, len(in_specs), {0: y_prev, 1: s_stack})
    return pl.pallas_call(
        strip(functools.partial(_hgrn_sample_kernel, c=c, layer=layer)),
        grid=(dec_b // nseq,),
        in_specs=in_specs + extra,
        out_specs=[
            pl.BlockSpec((TILE_R, W_C), lambda i: (rows(i), 0)),
            pl.BlockSpec((None, nseq, H_C, HEAD_DIM, HEAD_DIM), lambda i: (layer, i, 0, 0, 0)),
        ],
        out_shape=[
            jax.ShapeDtypeStruct(y_prev.shape, BF16),
            jax.ShapeDtypeStruct((depth, dec_b, H_C, HEAD_DIM, HEAD_DIM), F32),
        ],
        input_output_aliases=aliases,
        compiler_params=pltpu.CompilerParams(
            dimension_semantics=("parallel",), vmem_limit_bytes=VMEM_LIMIT),
        name="hgrn_sample",
    )(*operands)


def _gate_weight(w_t):
    ai0 = 4 * W_A
    ba0 = ai0 + 2 * H_A + 3 * W_B
    n_gate = 2 * H_A + 2 * H_B
    pad = jnp.zeros((w_t.shape[0], LANES - n_gate, w_t.shape[2]), w_t.dtype)
    return jnp.concatenate([w_t[:, ai0:ai0 + 2 * H_A], w_t[:, ba0:ba0 + 2 * H_B], pad], axis=1)


def _lane_row(depth, pieces):
    row = jnp.zeros((depth, LANES), F32)
    for off, val in pieces:
        row = lax.dynamic_update_slice(row, val.astype(F32), (0, off))
    return row[:, None, :]


def kernel(x_prompt, x_sample, state_mlstm_C, state_mlstm_n, state_mlstm_m, state_gdn_S,
           state_gdn_conv, state_hgrn_S, meta_tokens, norm_w, w_in, mlstm_gate_b, gdn_A_log,
           gdn_dt_bias, gdn_conv_w, hgrn_lower_bounds, out_norm_w, w_out, final_norm_w):
    bsz, seq_len, d = x_prompt.shape
    dec_b, dec_seq, _ = x_sample.shape
    depth = w_in.shape[0]
    assert d == D_MODEL and seq_len % TILE_R == 0 and TILE_R % dec_seq == 0
    assert w_in.shape[2] == W_SEGMENTS[-1][1] + W_SEGMENTS[-1][2]
    nseq = TILE_R // dec_seq
    assert dec_b % nseq == 0

    chunks = seq_len // TILE_R
    n_chunk = chunks + 1
    tm = bsz * TILE_R
    n_dec = dec_b * dec_seq
    dec_blk0 = n_chunk * bsz

    def dec_rows(i):
        return dec_blk0 + i

    w_t = jnp.swapaxes(w_in.astype(F32), 1, 2)
    w_gate = _gate_weight(w_t)
    w_out_bf = w_out.astype(BF16)
    norm_w3 = norm_w.astype(F32)[:, None, :]
    onw3 = out_norm_w.astype(F32)[:, None, :]
    final_nw = final_norm_w.astype(F32)[None, :]
    gate_bias = _lane_row(depth, [(G_AI, mlstm_gate_b[:, 0]), (G_AF, mlstm_gate_b[:, 1])])
    gdn_par = jnp.concatenate([
        _lane_row(depth, [(G_BA, gdn_dt_bias)]), _lane_row(depth, [(G_BA, gdn_A_log)]),
        jnp.zeros((depth, 6, LANES), F32)], axis=1)
    conv_w = gdn_conv_w.astype(F32)
    lb_par = hgrn_lower_bounds.astype(F32)
    m_rows = jnp.repeat(
        jnp.pad(state_mlstm_m.astype(F32), ((0, 0), (0, 0), (0, LANES - H_A))), dec_seq, axis=1)

    x, h = _prep(x_prompt.astype(F32), x_sample.astype(F32), meta_tokens.astype(F32), norm_w3,
                 bsz, n_chunk)
    t_all = x.shape[0]
    tm_in = max(m for m in range(16, INPROJ_MAX_ROWS + 1, 16) if t_all % m == 0)

    new_p = [[] for _ in range(6)]
    new_s = [[] for _ in range(3)]
    c_s = s_s = h_s = None
    y_prompt = y_sample = None
    for l in range(depth):
        proj = _inproj(h, w_t, w_gate, l, tm_in)

        ya, c_p, n_p, m_p = _mlstm_prompt(proj, gate_bias, onw3, l, bsz, n_chunk)
        ya, c_s, n_s, m_s = _mlstm_sample(proj, ya, c_s, gate_bias, onw3, state_mlstm_C,
                                          state_mlstm_n, m_rows, l, dec_b, dec_seq, dec_rows)
        yb, s_p, cv_p = _gdn_prompt(proj, gdn_par, conv_w, onw3, l, bsz, n_chunk)
        yb, s_s, cv_s = _gdn_sample(proj, yb, s_s, gdn_par, conv_w, onw3, state_gdn_S,
                                    state_gdn_conv, l, dec_b, dec_seq, dec_rows)
        yc, h_p = _hgrn_prompt(proj, lb_par, onw3, l, bsz, n_chunk)
        yc, h_s = _hgrn_sample(proj, yc, h_s, lb_par, onw3, state_hgrn_S, l, dec_b, dec_seq,
                               dec_rows)

        for lst, val in zip(new_p, (c_p, n_p, m_p[:, 0, :H_A], s_p, cv_p, h_p)):
            lst.append(val)
        for lst, val in zip(new_s, (n_s, m_s[::dec_seq, :H_A], cv_s)):
            lst.append(val)

        if l + 1 < depth:
            x, h = _outproj_mid(ya, yb, yc, x, w_out_bf, l, norm_w3, tm)
        else:
            y_prompt = _outproj_final(
                ya, yb, yc, x, w_out_bf, l, final_nw, tm, 1, chunks, (bsz, chunks, TILE_R, d),
                pl.BlockSpec((bsz, None, TILE_R, d), lambda i: (0, i, 0, 0)))
            y_sample = _outproj_final(
                ya, yb, yc, x, w_out_bf, l, final_nw, tm, n_chunk, n_dec // tm, (n_dec, d),
                pl.BlockSpec((tm, d), lambda i: (i, 0)))

    outs_p = [jnp.stack(a, axis=0) for a in new_p]
    n_all, m_all, cv_all = [jnp.stack(a, axis=0) for a in new_s]
    return (y_prompt.reshape(bsz, seq_len, d), y_sample.reshape(dec_b, dec_seq, d),
            *outs_p, c_s, n_all, m_all, s_s, cv_all, h_s)
```

```python
import functools
import math

import jax
import jax.numpy as jnp
from jax import lax
from jax.experimental import pallas as pl
from jax.experimental.pallas import tpu as pltpu

F32 = jnp.float32
BF16 = jnp.bfloat16

D_MODEL = 2048
HEAD_DIM = 128
H_A, H_B, H_C = 6, 6, 4
W_A, W_B, W_C = H_A * HEAD_DIM, H_B * HEAD_DIM, H_C * HEAD_DIM
MIX = W_A + W_B + W_C
N_META = 16
CONV_W = 4
EPS = 1e-6
NEG_BIG = -1e30
EXP_CLIP = 60.0
QK_SCALE = HEAD_DIM ** -0.5

TILE_R = 128
LANES = 128
CONV_PAD = 8

OFF_AQ = 0
OFF_AK, OFF_AV, OFF_AO = W_A, 2 * W_A, 3 * W_A
OFF_BQ = 4 * W_A
OFF_BK, OFF_BV = OFF_BQ + W_B, OFF_BQ + 2 * W_B
OFF_C = OFF_BQ + 3 * W_B
OFF_ZA = OFF_C + 3 * W_C
OFF_ZB = OFF_ZA + W_A
OFF_ZC = OFF_ZB + W_B
OFF_G = OFF_ZC + W_C
N_PACK = OFF_G + 2 * LANES
G_AI, G_AF, G_BA, G_BB = 0, H_A, 2 * H_A, 2 * H_A + H_B
TN = 1024
INPROJ_MAX_ROWS = 1280
W_SEGMENTS = ((OFF_AQ, OFF_BQ, 0), (OFF_BQ, OFF_C, 2 * H_A), (OFF_C, OFF_G, 2 * H_A + 2 * H_B))
W_ROW_CHUNK = 256

VMEM_LIMIT = 56 * 1024 * 1024


def _sigmoid(x):
    return 1.0 / (1.0 + jnp.exp(-x))


def _silu(x):
    return x * _sigmoid(x)


def _log1p(u):
    return jnp.log(1.0 + u)


def _softplus(x):
    return jnp.maximum(x, 0.0) + _log1p(jnp.exp(-jnp.abs(x)))


def _log_sigmoid(x):
    return -_softplus(-x)


def _dot(a, b):
    return jnp.dot(a.astype(BF16), b.astype(BF16), preferred_element_type=F32)


def _dot_nt(a, b):
    return lax.dot_general(a.astype(BF16), b.astype(BF16), (((1,), (1,)), ((), ())),
                           preferred_element_type=F32)


def _dot_exact(sel, x):
    hi = x.astype(BF16)
    r1 = x - hi.astype(F32)
    mid = r1.astype(BF16)
    lo = (r1 - mid.astype(F32)).astype(BF16)
    s = sel.astype(BF16)
    return (jnp.dot(s, hi, preferred_element_type=F32)
            + jnp.dot(s, mid, preferred_element_type=F32)
            + jnp.dot(s, lo, preferred_element_type=F32))


def _log2(n):
    k = int(math.log2(n))
    assert (1 << k) == n
    return k


_TRACE_MEMO = {}


def _memo(key, fn):
    if key not in _TRACE_MEMO:
        _TRACE_MEMO[key] = fn()
    return _TRACE_MEMO[key]


def _seq_masks(nseq, c):
    return _memo(("seq", nseq, c), lambda: _seq_masks_uncached(nseq, c))


def _seq_masks_uncached(nseq, c):
    r = nseq * c
    row = lax.broadcasted_iota(jnp.int32, (r, r), 0)
    col = lax.broadcasted_iota(jnp.int32, (r, r), 1)
    if nseq == 1:
        same = row >= 0
    else:
        k = _log2(c)
        same = (row >> k) == (col >> k)
    return row, col, same


def _seq_outer(lhs_t, x, nseq, c):
    if nseq == 1:
        return [jnp.dot(lhs_t, x.astype(BF16), preferred_element_type=F32)]
    seq = lax.broadcasted_iota(jnp.int32, (nseq * c, 1), 0) >> _log2(c)
    wide = jnp.concatenate([jnp.where(seq == i, x, 0.0).astype(BF16) for i in range(nseq)], axis=1)
    out = jnp.dot(lhs_t, wide, preferred_element_type=F32)
    n = x.shape[1]
    return [out[:, i * n:(i + 1) * n] for i in range(nseq)]


def _seq_cummax(x, nseq, c):
    r = nseq * c
    pos = lax.broadcasted_iota(jnp.int32, x.shape, 0)
    if nseq > 1:
        pos = pos & (c - 1)
    k = 1
    while k < c:
        x = jnp.where(pos >= k, jnp.maximum(x, pltpu.roll(x, k, axis=0)), x)
        k *= 2
    assert x.shape[0] == r
    return x


def _head_norm_gate(h, onw, z):
    hn = h * lax.rsqrt(jnp.mean(h * h, axis=1, keepdims=True) + EPS)
    return hn * onw * _silu(z)


def _rows(x, i, c):
    return x[i * c:(i + 1) * c]


def _cat_rows(parts):
    return parts[0] if len(parts) == 1 else jnp.concatenate(parts, axis=0)


def _chunk_shift(jj, r0):
    p = jj * TN + r0
    for p0, p1, shift in W_SEGMENTS:
        if p0 <= p and p + W_ROW_CHUNK <= p1:
            return shift
    return None


def _fill_weight_tile(w0_ref, w1_ref, wg_ref, wb_scr, tail_scr, jj):
    for r0 in range(0, TN, W_ROW_CHUNK):
        shift = _chunk_shift(jj, r0)
        dst = slice(r0, r0 + W_ROW_CHUNK)
        if shift is None:
            g0 = OFF_G - jj * TN
            assert r0 <= g0 and g0 + LANES <= r0 + W_ROW_CHUNK
            wb_scr[dst, :] = jnp.zeros((W_ROW_CHUNK, D_MODEL), BF16)
            wb_scr[g0:g0 + LANES, :] = wg_ref[...].astype(BF16)
        elif r0 + shift + W_ROW_CHUNK <= TN:
            wb_scr[dst, :] = w0_ref[r0 + shift:r0 + shift + W_ROW_CHUNK, :].astype(BF16)
        else:
            tail_scr[0:W_ROW_CHUNK, :] = w0_ref[r0:TN, :]
            tail_scr[W_ROW_CHUNK:, :] = w1_ref[...]
            wb_scr[dst, :] = tail_scr[shift:shift + W_ROW_CHUNK, :].astype(BF16)


def _inproj_kernel(h_ref, w0_ref, w1_ref, wg_ref, o_ref, wb_scr, tail_scr):
    j = pl.program_id(0)

    @pl.when(pl.program_id(1) == 0)
    def _():
        for jj in range(N_PACK // TN):
            @pl.when(j == jj)
            def _(jj=jj):
                _fill_weight_tile(w0_ref, w1_ref, wg_ref, wb_scr, tail_scr, jj)

    o_ref[...] = lax.dot_general(h_ref[...], wb_scr[...], (((1,), (1,)), ((), ())),
                                 preferred_element_type=F32)


W_TAIL_ROWS = 32


def _inproj(h, w_t, w_gate_t, layer, tm):
    t = h.shape[0]
    n_tiles = N_PACK // TN
    assert n_tiles - 1 <= (w_t.shape[1] - 1) // TN and W_TAIL_ROWS >= W_SEGMENTS[-1][2]
    tail_per_tile = TN // W_TAIL_ROWS
    last_tail = (w_t.shape[1] - 1) // W_TAIL_ROWS
    return pl.pallas_call(
        _inproj_kernel,
        grid=(n_tiles, t // tm),
        in_specs=[
            pl.BlockSpec((tm, D_MODEL), lambda j, i: (i, 0)),
            pl.BlockSpec((None, TN, D_MODEL), lambda j, i: (layer, j, 0)),
            pl.BlockSpec((None, W_TAIL_ROWS, D_MODEL),
                         lambda j, i: (layer, jnp.minimum((j + 1) * tail_per_tile, last_tail), 0)),
            pl.BlockSpec((None, LANES, D_MODEL), lambda j, i: (layer, 0, 0)),
        ],
        out_specs=pl.BlockSpec((tm, TN), lambda j, i: (i, j)),
        out_shape=jax.ShapeDtypeStruct((t, N_PACK), F32),
        scratch_shapes=[pltpu.VMEM((TN, D_MODEL), BF16),
                        pltpu.VMEM((W_ROW_CHUNK + W_TAIL_ROWS, D_MODEL), F32)],
        compiler_params=pltpu.CompilerParams(
            dimension_semantics=("arbitrary", "arbitrary"), vmem_limit_bytes=VMEM_LIMIT),
        name="inproj",
    )(h, w_t, w_t, w_gate_t)


def _rmsnorm(x, w):
    ms = jnp.mean(x * x, axis=1, keepdims=True)
    return x * lax.rsqrt(ms + EPS) * w


def _prep_kernel(xp_ref, xs_ref, meta_ref, nw_ref, x_ref, h_ref, *, n_chunk, bsz):
    i = pl.program_id(0)
    d = x_ref.shape[1]

    def emit(x):
        x_ref[...] = x
        h_ref[...] = _rmsnorm(x, nw_ref[...]).astype(BF16)

    @pl.when(i == 0)
    def _():
        slot = jnp.concatenate([jnp.zeros((TILE_R - N_META, d), F32), meta_ref[...]], axis=0)
        emit(jnp.concatenate([slot] * bsz, axis=0))

    @pl.when((i > 0) & (i < n_chunk))
    def _():
        emit(xp_ref[...].reshape(bsz * TILE_R, d))

    @pl.when(i >= n_chunk)
    def _():
        emit(xs_ref[...])


def _prep(x_prompt, x_sample, meta, norm_w3, bsz, n_chunk):
    d = x_prompt.shape[-1]
    tm = bsz * TILE_R
    n_dec = x_sample.shape[0] * x_sample.shape[1]
    assert n_dec % tm == 0
    xp = x_prompt.reshape(bsz, n_chunk - 1, TILE_R, d)
    xs = x_sample.reshape(n_dec, d)
    n_blk = n_chunk + n_dec // tm
    t = n_blk * tm
    return pl.pallas_call(
        functools.partial(_prep_kernel, n_chunk=n_chunk, bsz=bsz),
        grid=(n_blk,),
        in_specs=[
            pl.BlockSpec((bsz, None, TILE_R, d), lambda i: (0, jnp.clip(i - 1, 0, n_chunk - 2), 0, 0)),
            pl.BlockSpec((tm, d), lambda i: (jnp.clip(i - n_chunk, 0, n_dec // tm - 1), 0)),
            pl.BlockSpec((N_META, d), lambda i: (0, 0)),
            pl.BlockSpec((None, 1, d), lambda i: (0, 0, 0)),
        ],
        out_specs=[pl.BlockSpec((tm, d), lambda i: (i, 0)),
                   pl.BlockSpec((tm, d), lambda i: (i, 0))],
        out_shape=[jax.ShapeDtypeStruct((t, d), F32), jax.ShapeDtypeStruct((t, d), BF16)],
        compiler_params=pltpu.CompilerParams(
            dimension_semantics=("arbitrary",), vmem_limit_bytes=VMEM_LIMIT),
        name="prep",
    )(xp, xs, meta, norm_w3)


def _outproj_residual(ya_ref, yb_ref, yc_ref, x_ref, w_ref):
    acc = jnp.dot(ya_ref[...], w_ref[0:W_A, :], preferred_element_type=F32)
    acc = acc + jnp.dot(yb_ref[...], w_ref[W_A:W_A + W_B, :], preferred_element_type=F32)
    acc = acc + jnp.dot(yc_ref[...], w_ref[W_A + W_B:MIX, :], preferred_element_type=F32)
    return x_ref[...] + acc


def _outproj_mid_kernel(ya_ref, yb_ref, yc_ref, x_ref, w_ref, nw_ref, x_out, h_out):
    xn = _outproj_residual(ya_ref, yb_ref, yc_ref, x_ref, w_ref)
    x_out[...] = xn
    h_out[...] = _rmsnorm(xn, nw_ref[...]).astype(BF16)


def _outproj_final_kernel(ya_ref, yb_ref, yc_ref, x_ref, w_ref, nw_ref, y_out):
    xn = _outproj_residual(ya_ref, yb_ref, yc_ref, x_ref, w_ref)
    y_out[...] = _rmsnorm(xn, nw_ref[...]).reshape(y_out.shape)


def _outproj_specs(tm, row_blk0, layer, nw_spec):
    rows = lambda i: (row_blk0 + i, 0)
    return [
        pl.BlockSpec((tm, W_A), rows),
        pl.BlockSpec((tm, W_B), rows),
        pl.BlockSpec((tm, W_C), rows),
        pl.BlockSpec((tm, D_MODEL), rows),
        pl.BlockSpec((None, MIX, D_MODEL), lambda i: (layer, 0, 0)),
        nw_spec,
    ]


def _outproj_mid(ya, yb, yc, x, w_out, layer, norm_w3, tm):
    t = x.shape[0]
    blk = pl.BlockSpec((tm, D_MODEL), lambda i: (i, 0))
    nw_spec = pl.BlockSpec((None, 1, D_MODEL), lambda i: (layer + 1, 0, 0))
    return pl.pallas_call(
        _outproj_mid_kernel,
        grid=(t // tm,),
        in_specs=_outproj_specs(tm, 0, layer, nw_spec),
        out_specs=[blk, blk],
        out_shape=[jax.ShapeDtypeStruct((t, D_MODEL), F32), jax.ShapeDtypeStruct((t, D_MODEL), BF16)],
        compiler_params=pltpu.CompilerParams(
            dimension_semantics=("parallel",), vmem_limit_bytes=VMEM_LIMIT),
        name="outproj",
    )(ya, yb, yc, x, w_out, norm_w3)


def _outproj_final(ya, yb, yc, x, w_out, layer, final_nw, tm, row_blk0, n_blk, out_shape, out_spec):
    nw_spec = pl.BlockSpec((1, D_MODEL), lambda i: (0, 0))
    return pl.pallas_call(
        _outproj_final_kernel,
        grid=(n_blk,),
        in_specs=_outproj_specs(tm, row_blk0, layer, nw_spec),
        out_specs=out_spec,
        out_shape=jax.ShapeDtypeStruct(out_shape, F32),
        compiler_params=pltpu.CompilerParams(
            dimension_semantics=("parallel",), vmem_limit_bytes=VMEM_LIMIT),
        name="outproj_final",
    )(ya, yb, yc, x, w_out, final_nw)


class _RowView:
    def __init__(self, ref, r0, n):
        self.ref, self.r0, self.n = ref, r0, n
        self.shape = (n, ref.shape[1])

    def _idx(self, idx):
        if idx is Ellipsis:
            return slice(self.r0, self.r0 + self.n), slice(None)
        rows, cols = idx
        start = self.r0 + (rows.start or 0)
        stop = self.r0 + (self.n if rows.stop is None else rows.stop)
        return slice(start, stop), cols

    def __getitem__(self, idx):
        return self.ref[self._idx(idx)]

    def __setitem__(self, idx, val):
        self.ref[self._idx(idx)] = val


def _tile_views(refs, u):
    return [_RowView(ref, u * TILE_R, TILE_R) for ref in refs]


def _interleave(tiles):
    tiles = list(tiles)
    while tiles:
        alive = []
        for t in tiles:
            try:
                next(t)
                alive.append(t)
            except StopIteration:
                pass
        tiles = alive


def _mlstm_tile(q_ref, k_ref, v_ref, o_ref, z_ref, g_ref, gb_ref, onw_ref, y_ref,
                m_exp, valid, get_c, get_n, put_c, put_n, put_m, nseq, c):
    r = nseq * c
    row, col, same = _seq_masks(nseq, c)
    incl = _memo(("incl", nseq, c), lambda: same & (col <= row))
    g = g_ref[...] + gb_ref[...]
    li = g
    lf = _log_sigmoid(g)
    if valid is not None:
        li = jnp.where(valid, li, NEG_BIG)
        lf = jnp.where(valid, lf, 0.0)
    fcum = _dot_exact(incl.astype(F32), lf)
    if nseq == 1:
        flast = jnp.broadcast_to(fcum[r - 1:r, :], (r, LANES))
    else:
        flast = _dot_exact(same.astype(F32), lf)
    yield
    heads = range(H_A)
    hsl = [slice(h * HEAD_DIM, (h + 1) * HEAD_DIM) for h in heads]

    li_a = pltpu.roll(li, G_AF - G_AI, axis=1)
    m_a = pltpu.roll(m_exp, G_AF, axis=1)
    x = li_a - fcum
    mx = jnp.maximum(m_a, _seq_cummax(x, nseq, c))
    mt_all = fcum + mx
    if nseq == 1:
        m_new = jnp.broadcast_to(mt_all[r - 1:r, :], (r, LANES))
    else:
        m_new = jnp.broadcast_to(mt_all.reshape(nseq, c, LANES)[:, c - 1:c, :],
                                 (nseq, c, LANES)).reshape(r, LANES)
    a_all = jnp.exp(m_a - mx)
    w_all = jnp.exp(flast - fcum + li_a - m_new)
    as_all = jnp.exp(flast + m_a - m_new)
    low_all = jnp.exp(jnp.minimum(-mt_all, EXP_CLIP))
    put_m(pltpu.roll(m_new, LANES - G_AF, axis=1))
    x_t = x.T
    col_of = lambda arr, h: arr[:, G_AF + h:G_AF + h + 1]
    a = [col_of(a_all, h) for h in heads]
    w = [col_of(w_all, h) for h in heads]
    a_s = [col_of(as_all, h) for h in heads]
    low = [col_of(low_all, h) for h in heads]
    pexp = [jnp.exp(jnp.where(incl, x_t[G_AF + h:G_AF + h + 1, :] - col_of(mx, h), NEG_BIG))
            for h in heads]
    yield

    s = [_dot_nt(q_ref[:, hs], k_ref[:, hs] * QK_SCALE) for hs in hsl]
    yield
    p = [pexp[h] * s[h] for h in heads]
    pv = [_dot(p[h], v_ref[:, hsl[h]]) for h in heads]
    yield
    qc = [_cat_rows([_dot(_rows(q_ref[:, hsl[h]], i, c), get_c(i, h)) for i in range(nseq)])
          for h in heads]
    yield
    psum = [jnp.sum(p[h], axis=1, keepdims=True) for h in heads]
    qn = [_cat_rows([jnp.sum(_rows(q_ref[:, hsl[h]], i, c) * get_n(i, h), axis=1, keepdims=True)
                     for i in range(nseq)]) for h in heads]
    yield
    hh = [(a[h] * qc[h] + pv[h]) / jnp.maximum(jnp.abs(a[h] * qn[h] + psum[h]), low[h])
          * _sigmoid(o_ref[:, hsl[h]]) for h in heads]
    ms = [jnp.mean(hh[h] * hh[h], axis=1, keepdims=True) for h in heads]
    yield
    for h in heads:
        hs = hsl[h]
        y_ref[:, hs] = (hh[h] * lax.rsqrt(ms[h] + EPS) * onw_ref[:, hs]
                        * _silu(z_ref[:, hs])).astype(BF16)
    yield

    for h in heads:
        hs = hsl[h]
        kh = k_ref[:, hs] * QK_SCALE
        wk = w[h] * kh
        upd = _seq_outer(kh.T.astype(BF16), w[h] * v_ref[:, hs], nseq, c)
        for i in range(nseq):
            a_i = a_s[h][i * c:i * c + 1, :]
            put_c(i, h, a_i * get_c(i, h) + upd[i])
            put_n(i, h, a_i * get_n(i, h) + jnp.sum(_rows(wk, i, c), axis=0, keepdims=True))


def _mlstm_prompt_kernel(q_ref, k_ref, v_ref, o_ref, z_ref, g_ref, gb_ref, onw_ref,
                         y_ref, c_out, n_out, m_out, c_scr, n_scr, m_scr):
    _TRACE_MEMO.clear()
    j = pl.program_id(0)
    bsz = c_scr.shape[0]

    @pl.when(j == 0)
    def _():
        c_scr[...] = jnp.zeros_like(c_scr)
        n_scr[...] = jnp.zeros_like(n_scr)
        m_scr[...] = jnp.zeros_like(m_scr)

    tok = lax.broadcasted_iota(jnp.int32, (TILE_R, LANES), 0) + j * TILE_R
    valid = tok >= TILE_R - N_META

    def tile(u):
        q, k, v, o, z, g, y = _tile_views((q_ref, k_ref, v_ref, o_ref, z_ref, g_ref, y_ref), u)
        m_exp = jnp.broadcast_to(m_scr[u, 0:1, :], (TILE_R, LANES))

        def put_c(i, h, val):
            c_scr[u, h] = val

        def put_n(i, h, val):
            n_scr[u, h:h + 1, :] = val

        def put_m(val):
            m_scr[u] = val[0:8, :]

        return _mlstm_tile(q, k, v, o, z, g, gb_ref, onw_ref, y, m_exp, valid,
                           lambda i, h: c_scr[u, h], lambda i, h: n_scr[u, h:h + 1, :],
                           put_c, put_n, put_m, 1, TILE_R)

    _interleave(tile(u) for u in range(bsz))

    @pl.when(j == pl.num_programs(0) - 1)
    def _():
        c_out[...] = c_scr[...]
        n_out[...] = n_scr[:, 0:H_A, :]
        m_out[...] = m_scr[:, 0:1, :]


def _mlstm_sample_kernel(q_ref, k_ref, v_ref, o_ref, z_ref, g_ref, gb_ref, onw_ref,
                         c_in, n_in, m_in, y_ref, c_out, n_out, m_out, *, c):
    _TRACE_MEMO.clear()
    nseq = TILE_R // c

    def put_c(i, h, val):
        c_out[i, h] = val

    def put_n(i, h, val):
        n_out[i, h:h + 1, :] = val

    def put_m(val):
        m_out[...] = val

    _interleave([_mlstm_tile(
        q_ref, k_ref, v_ref, o_ref, z_ref, g_ref, gb_ref, onw_ref, y_ref, m_in[...], None,
        lambda i, h: c_in[i, h], lambda i, h: n_in[i, h:h + 1, :], put_c, put_n, put_m, nseq, c)])


def _col_spec(width, off, rows, n_rows=TILE_R):
    assert off % width == 0
    blk = off // width
    return pl.BlockSpec((n_rows, width), lambda *ids: (rows(*ids), blk))


def _param_spec(width, off, layer):
    assert off % width == 0
    blk = off // width
    return pl.BlockSpec((None, 1, width), lambda *ids: (layer, 0, blk))


def _whole(*shape):
    return pl.BlockSpec(shape, lambda j: (0,) * len(shape))


def _mlstm_prompt(proj, gate_bias, onw3, layer, bsz, n_chunk):
    t = proj.shape[0]
    tm = bsz * TILE_R
    rows = lambda j: j
    in_specs = [_col_spec(W_A, off, rows, tm) for off in (OFF_AQ, OFF_AK, OFF_AV, OFF_AO, OFF_ZA)]
    in_specs += [_col_spec(LANES, OFF_G, rows, tm), _param_spec(LANES, 0, layer),
                 _param_spec(W_A, 0, layer)]
    return pl.pallas_call(
        _mlstm_prompt_kernel,
        grid=(n_chunk,),
        in_specs=in_specs,
        out_specs=[
            pl.BlockSpec((tm, W_A), lambda j: (j, 0)),
            _whole(bsz, H_A, HEAD_DIM, HEAD_DIM),
            _whole(bsz, H_A, HEAD_DIM),
            _whole(bsz, 1, LANES),
        ],
        out_shape=[
            jax.ShapeDtypeStruct((t, W_A), BF16),
            jax.ShapeDtypeStruct((bsz, H_A, HEAD_DIM, HEAD_DIM), F32),
            jax.ShapeDtypeStruct((bsz, H_A, HEAD_DIM), F32),
            jax.ShapeDtypeStruct((bsz, 1, LANES), F32),
        ],
        scratch_shapes=[pltpu.VMEM((bsz, H_A, HEAD_DIM, HEAD_DIM), F32),
                        pltpu.VMEM((bsz, 8, HEAD_DIM), F32), pltpu.VMEM((bsz, 8, LANES), F32)],
        compiler_params=pltpu.CompilerParams(
            dimension_semantics=("arbitrary",), vmem_limit_bytes=VMEM_LIMIT),
        name="mlstm_prompt",
    )(proj, proj, proj, proj, proj, proj, gate_bias, onw3)


def _in_place(operands, n_blocked, targets):
    specs, aliases = [], {}
    operands = list(operands)
    for out_idx, arr in targets.items():
        if arr is not None:
            aliases[len(operands)] = out_idx
            operands.append(arr)
            specs.append(pl.BlockSpec(memory_space=pl.ANY))
    n_extra = len(specs)

    def strip(kernel):
        def body(*refs):
            kernel(*refs[:n_blocked], *refs[n_blocked + n_extra:])
        return body

    return operands, specs, aliases, strip


def _mlstm_sample(proj, y_prev, c_stack, gate_bias, onw3, c_state, n_state, m_rows, layer,
                  dec_b, c, rows):
    nseq = TILE_R // c
    depth = c_state.shape[0]
    in_specs = [_col_spec(W_A, off, rows) for off in (OFF_AQ, OFF_AK, OFF_AV, OFF_AO, OFF_ZA)]
    in_specs += [_col_spec(LANES, OFF_G, rows), _param_spec(LANES, 0, layer),
                 _param_spec(W_A, 0, layer)]
    in_specs += [
        pl.BlockSpec((None, nseq, H_A, HEAD_DIM, HEAD_DIM), lambda i: (layer, i, 0, 0, 0)),
        pl.BlockSpec((None, nseq, H_A, HEAD_DIM), lambda i: (layer, i, 0, 0)),
        pl.BlockSpec((None, TILE_R, LANES), lambda i: (layer, i, 0)),
    ]
    operands = (proj, proj, proj, proj, proj, proj, gate_bias, onw3, c_state, n_state, m_rows)
    operands, extra, aliases, strip = _in_place(operands, len(in_specs), {0: y_prev, 1: c_stack})
    return pl.pallas_call(
        strip(functools.partial(_mlstm_sample_kernel, c=c)),
        grid=(dec_b // nseq,),
        in_specs=in_specs + extra,
        out_specs=[
            pl.BlockSpec((TILE_R, W_A), lambda i: (rows(i), 0)),
            pl.BlockSpec((None, nseq, H_A, HEAD_DIM, HEAD_DIM), lambda i: (layer, i, 0, 0, 0)),
            pl.BlockSpec((nseq, H_A, HEAD_DIM), lambda i: (i, 0, 0)),
            pl.BlockSpec((TILE_R, LANES), lambda i: (i, 0)),
        ],
        out_shape=[
            jax.ShapeDtypeStruct(y_prev.shape, BF16),
            jax.ShapeDtypeStruct((depth, dec_b, H_A, HEAD_DIM, HEAD_DIM), F32),
            jax.ShapeDtypeStruct((dec_b, H_A, HEAD_DIM), F32),
            jax.ShapeDtypeStruct((dec_b * c, LANES), F32),
        ],
        input_output_aliases=aliases,
        compiler_params=pltpu.CompilerParams(
            dimension_semantics=("parallel",), vmem_limit_bytes=VMEM_LIMIT),
        name="mlstm_sample",
    )(*operands)


def _l2norm(x):
    return x * lax.rsqrt(jnp.sum(x * x, axis=1, keepdims=True) + EPS)


INV_LEAF = 16


def _half_block_mask(row, col, k):
    return (((row >> (k + 1)) == (col >> (k + 1)))
            & (((row >> k) & 1) == 1) & (((col >> k) & 1) == 0))


def _unit_lower_inverse(a_list, row, col, c):
    leaf = min(INV_LEAF, c)
    kl = _log2(leaf)
    leaf_mask = _memo(("leaf", c), lambda: (row >> kl) == (col >> kl))
    eye = _memo(("eye", c), lambda: jnp.where(row == col, 1.0, 0.0))
    p = [jnp.where(leaf_mask, -a, 0.0) for a in a_list]
    t = [eye + n for n in p]
    for _ in range(kl - 1):
        p = [_dot(x, x) for x in p]
        yield
        t = [ti + _dot(pi, ti) for pi, ti in zip(p, t)]
        yield
    w = leaf
    while w < c:
        kw = _log2(w)
        lower_left = _memo(("lower_left", c, w), lambda: _half_block_mask(row, col, kw))
        n_pair = row.shape[0] // (2 * w)
        second = lambda x: _cat_rows([x[(2 * m + 1) * w:(2 * m + 2) * w] for m in range(n_pair)])
        t_sec = [second(ti) for ti in t]
        left = [_dot(ts, jnp.where(lower_left, a, 0.0)) for ts, a in zip(t_sec, a_list)]
        yield
        new_sec = [ts - _dot(li, ti) for ts, li, ti in zip(t_sec, left, t)]
        t = [_cat_rows([part for m in range(n_pair)
                        for part in (ti[2 * m * w:(2 * m + 1) * w], ns[m * w:(m + 1) * w])])
             for ti, ns in zip(t, new_sec)]
        yield
        w *= 2
    return t


def _gdn_tile(u_cols, prev_cols, cw_ref, z_ref, g_ref, gp_ref, onw_ref, y_ref, get_s, put_s,
              nseq, c):
    r = nseq * c
    row, col, same = _seq_masks(nseq, c)
    incl = _memo(("incl", nseq, c), lambda: same & (col <= row))
    strict = _memo(("strict", nseq, c), lambda: same & (col < row))
    gp = gp_ref[...]
    graw = g_ref[...]
    gdec = -jnp.exp(gp[1:2, :]) * _softplus(graw + gp[0:1, :])
    beta = _sigmoid(graw)
    gcum = _dot_exact(incl.astype(F32), gdec)
    if nseq == 1:
        glast = jnp.broadcast_to(gcum[r - 1:r, :], (r, LANES))
    else:
        glast = _dot_exact(same.astype(F32), gdec)
    yield
    gcum_t = gcum.T

    heads = range(H_B)
    hsl = [slice(h * HEAD_DIM, (h + 1) * HEAD_DIM) for h in heads]

    pos = lax.broadcasted_iota(jnp.int32, (CONV_PAD if nseq == 1 else r, HEAD_DIM), 0)
    if nseq > 1:
        pos = pos & (c - 1)

    def conv(lo):
        cols = slice(lo, lo + HEAD_DIM)
        x = u_cols(cols)
        prev = prev_cols(cols)
        acc = x * cw_ref[CONV_W - 1:CONV_W, cols]
        for back in range(1, CONV_W):
            shifted = pltpu.roll(x, back, axis=0)
            if nseq == 1:
                first = jnp.where(pos < back, pltpu.roll(prev, back, axis=0), shifted[0:CONV_PAD])
                shifted = jnp.concatenate([first, shifted[CONV_PAD:]], axis=0)
            else:
                shifted = jnp.where(pos < back, pltpu.roll(prev, (back - CONV_PAD) % r, axis=0),
                                    shifted)
            acc = acc + shifted * cw_ref[CONV_W - 1 - back:CONV_W - back, cols]
        return _silu(acc)

    q = [_l2norm(conv(hs.start)) * QK_SCALE for hs in hsl]
    k = [_l2norm(conv(W_B + hs.start)) for hs in hsl]
    v = [conv(2 * W_B + hs.start) for hs in hsl]
    yield
    gc = [gcum[:, G_BA + h:G_BA + h + 1] for h in heads]
    gl = [glast[:, G_BA + h:G_BA + h + 1] for h in heads]
    bc = [beta[:, G_BB + h:G_BB + h + 1] for h in heads]
    decay = [jnp.exp(jnp.where(incl, gc[h] - gcum_t[G_BA + h:G_BA + h + 1, :], NEG_BIG))
             for h in heads]
    kk = [_dot_nt(k[h], k[h]) for h in heads]
    qk = [_dot_nt(q[h], k[h]) for h in heads]
    yield
    amat = [jnp.where(strict, bc[h] * decay[h] * kk[h], 0.0) for h in heads]
    tinv = yield from _unit_lower_inverse(amat, row, col, c)
    sol = [_dot(tinv[h], jnp.concatenate([bc[h] * v[h], (bc[h] * jnp.exp(gc[h])) * k[h]], axis=1))
           for h in heads]
    yield
    both = [[_dot(jnp.concatenate([_rows(sol[h][:, HEAD_DIM:], i, c), _rows(q[h], i, c)], axis=0),
                  get_s(i, h)) for i in range(nseq)] for h in heads]
    wks = [_cat_rows([both[h][i][:c] for i in range(nseq)]) for h in heads]
    qs = [_cat_rows([both[h][i][c:] for i in range(nseq)]) for h in heads]
    yield
    wmat = [sol[h][:, :HEAD_DIM] - wks[h] for h in heads]
    aw = [_dot(decay[h] * qk[h], wmat[h]) for h in heads]
    yield
    for h in heads:
        hs = hsl[h]
        oh = jnp.exp(gc[h]) * qs[h] + aw[h]
        y_ref[:, hs] = _head_norm_gate(oh, onw_ref[:, hs], z_ref[:, hs]).astype(BF16)
    yield
    for h in heads:
        kd_t = (k[h] * jnp.exp(gl[h] - gc[h])).T.astype(BF16)
        e_last = jnp.exp(gl[h])
        upd = _seq_outer(kd_t, wmat[h], nseq, c)
        for i in range(nseq):
            put_s(i, h, e_last[i * c:i * c + 1, :] * get_s(i, h) + upd[i])


def _gdn_prompt_kernel(q_ref, k_ref, v_ref, z_ref, g_ref, gp_ref, cw_ref, onw_ref,
                       y_ref, s_out, conv_out, s_scr, hist_scr):
    _TRACE_MEMO.clear()
    j = pl.program_id(0)
    bsz = s_scr.shape[0]
    pieces = ((q_ref, 0), (k_ref, W_B), (v_ref, 2 * W_B))

    @pl.when(j == 0)
    def _():
        s_scr[...] = jnp.zeros_like(s_scr)
        hist_scr[...] = jnp.zeros_like(hist_scr)

    def tile(u):
        z, g, y = _tile_views((z_ref, g_ref, y_ref), u)

        def u_cols(cols):
            ref, off = pieces[cols.start // W_B]
            return ref[u * TILE_R:(u + 1) * TILE_R, cols.start - off:cols.stop - off]

        def prev_cols(cols):
            return hist_scr[u, :, cols]

        def put_s(i, h, val):
            s_scr[u, h] = val

        return _gdn_tile(u_cols, prev_cols, cw_ref, z, g, gp_ref, onw_ref, y,
                         lambda i, h: s_scr[u, h], put_s, 1, TILE_R)

    _interleave(tile(u) for u in range(bsz))

    for u in range(bsz):
        for ref, off in pieces:
            hist_scr[u, :, off:off + W_B] = ref[(u + 1) * TILE_R - CONV_PAD:(u + 1) * TILE_R, :]

    @pl.when(j == pl.num_programs(0) - 1)
    def _():
        s_out[...] = s_scr[...]
        conv_out[...] = hist_scr[:, CONV_PAD - (CONV_W - 1):, :]


def _gdn_sample_kernel(q_ref, k_ref, v_ref, z_ref, g_ref, gp_ref, cw_ref, onw_ref,
                       s_in, conv_in, y_ref, s_out, conv_out, hist_scr, *, c):
    _TRACE_MEMO.clear()
    nseq = TILE_R // c
    hist = CONV_W - 1
    assert c == CONV_PAD
    pieces = ((q_ref, 0), (k_ref, W_B), (v_ref, 2 * W_B))
    hist_scr[:, 0:CONV_PAD - hist, :] = jnp.zeros((nseq, CONV_PAD - hist, 3 * W_B), F32)
    hist_scr[:, CONV_PAD - hist:, :] = conv_in[...]

    def u_cols(cols):
        ref, off = pieces[cols.start // W_B]
        return ref[:, cols.start - off:cols.stop - off]

    def prev_cols(cols):
        return hist_scr[:, :, cols].reshape(TILE_R, cols.stop - cols.start)

    def put_s(i, h, val):
        s_out[i, h] = val

    _interleave([_gdn_tile(u_cols, prev_cols, cw_ref, z_ref, g_ref, gp_ref, onw_ref, y_ref,
                           lambda i, h: s_in[i, h], put_s, nseq, c)])
    for ref, off in pieces:
        conv_out[:, :, off:off + W_B] = ref[...].reshape(nseq, c, W_B)[:, c - hist:, :]


def _gdn_prompt(proj, gdn_par, conv_w, onw3, layer, bsz, n_chunk):
    t = proj.shape[0]
    tm = bsz * TILE_R
    rows = lambda j: j
    in_specs = [_col_spec(W_B, off, rows, tm) for off in (OFF_BQ, OFF_BK, OFF_BV, OFF_ZB)]
    in_specs += [
        _col_spec(LANES, OFF_G, rows, tm),
        pl.BlockSpec((None, 8, LANES), lambda j: (layer, 0, 0)),
        pl.BlockSpec((None, CONV_W, 3 * W_B), lambda j: (layer, 0, 0)),
        _param_spec(W_B, W_A, layer),
    ]
    return pl.pallas_call(
        _gdn_prompt_kernel,
        grid=(n_chunk,),
        in_specs=in_specs,
        out_specs=[
            pl.BlockSpec((tm, W_B), lambda j: (j, 0)),
            _whole(bsz, H_B, HEAD_DIM, HEAD_DIM),
            _whole(bsz, CONV_W - 1, 3 * W_B),
        ],
        out_shape=[
            jax.ShapeDtypeStruct((t, W_B), BF16),
            jax.ShapeDtypeStruct((bsz, H_B, HEAD_DIM, HEAD_DIM), F32),
            jax.ShapeDtypeStruct((bsz, CONV_W - 1, 3 * W_B), F32),
        ],
        scratch_shapes=[pltpu.VMEM((bsz, H_B, HEAD_DIM, HEAD_DIM), F32),
                        pltpu.VMEM((bsz, CONV_PAD, 3 * W_B), F32)],
        compiler_params=pltpu.CompilerParams(
            dimension_semantics=("arbitrary",), vmem_limit_bytes=VMEM_LIMIT),
        name="gdn_prompt",
    )(proj, proj, proj, proj, proj, gdn_par, conv_w, onw3)


def _gdn_sample(proj, y_prev, s_stack, gdn_par, conv_w, onw3, s_state, conv_state, layer,
                dec_b, c, rows):
    nseq = TILE_R // c
    depth = s_state.shape[0]
    in_specs = [_col_spec(W_B, off, rows) for off in (OFF_BQ, OFF_BK, OFF_BV, OFF_ZB)]
    in_specs += [
        _col_spec(LANES, OFF_G, rows),
        pl.BlockSpec((None, 8, LANES), lambda i: (layer, 0, 0)),
        pl.BlockSpec((None, CONV_W, 3 * W_B), lambda i: (layer, 0, 0)),
        _param_spec(W_B, W_A, layer),
        pl.BlockSpec((None, nseq, H_B, HEAD_DIM, HEAD_DIM), lambda i: (layer, i, 0, 0, 0)),
        pl.BlockSpec((None, nseq, CONV_W - 1, 3 * W_B), lambda i: (layer, i, 0, 0)),
    ]
    operands = (proj, proj, proj, proj, proj, gdn_par, conv_w, onw3, s_state, conv_state)
    operands, extra, aliases, strip = _in_place(operands, len(in_specs), {0: y_prev, 1: s_stack})
    return pl.pallas_call(
        strip(functools.partial(_gdn_sample_kernel, c=c)),
        grid=(dec_b // nseq,),
        in_specs=in_specs + extra,
        out_specs=[
            pl.BlockSpec((TILE_R, W_B), lambda i: (rows(i), 0)),
            pl.BlockSpec((None, nseq, H_B, HEAD_DIM, HEAD_DIM), lambda i: (layer, i, 0, 0, 0)),
            pl.BlockSpec((nseq, CONV_W - 1, 3 * W_B), lambda i: (i, 0, 0)),
        ],
        out_shape=[
            jax.ShapeDtypeStruct(y_prev.shape, BF16),
            jax.ShapeDtypeStruct((depth, dec_b, H_B, HEAD_DIM, HEAD_DIM), F32),
            jax.ShapeDtypeStruct((dec_b, CONV_W - 1, 3 * W_B), F32),
        ],
        scratch_shapes=[pltpu.VMEM((nseq, CONV_PAD, 3 * W_B), F32)],
        input_output_aliases=aliases,
        compiler_params=pltpu.CompilerParams(
            dimension_semantics=("parallel",), vmem_limit_bytes=VMEM_LIMIT),
        name="gdn_sample",
    )(*operands)


def _hgrn_tile(q_ref, f_ref, i_ref, z_ref, lbp_ref, onw_ref, y_ref, get_s, put_s,
               nseq, c, layer):
    r = nseq * c
    row, col, same = _seq_masks(nseq, c)
    incl = _memo(("incl", nseq, c), lambda: same & (col <= row))
    rowv = lax.broadcasted_iota(jnp.int32, (r, 1), 0)
    lbp = lbp_ref[...]
    e = jnp.exp(lbp - jnp.max(lbp, axis=0, keepdims=True))
    sm = e / jnp.sum(e, axis=0, keepdims=True)
    lb = jnp.zeros((1, W_C), F32)
    for l in range(1, layer + 1):
        lb = lb + sm[l:l + 1, :]

    cf = f_ref[...]
    logf = _log_sigmoid(cf)
    if layer > 0:
        logf = logf + _log1p(lb * jnp.exp(jnp.minimum(-cf, EXP_CLIP)))
    kall = (1.0 - lb) * _sigmoid(-cf)
    bcum = _dot_exact(incl.astype(F32), logf)
    if nseq == 1:
        blast = jnp.broadcast_to(bcum[r - 1:r, :], (r, W_C))
    else:
        blast = _dot_exact(same.astype(F32), logf)

    levels = []
    w = c // 2
    while w >= 1:
        levels.append(w)
        w //= 2
    rowf = lax.broadcasted_iota(jnp.int32, (r, W_C), 0)
    refs = []
    for w in levels:
        if 2 * w >= 8:
            refs.append(_cat_rows([
                jnp.broadcast_to(bcum[g * 2 * w + w - 1:g * 2 * w + w, :], (2 * w, W_C))
                for g in range(r // (2 * w))]))
        else:
            pos = rowf & (2 * w - 1)
            ref = bcum
            for off in range(2 * w):
                if off != w - 1:
                    ref = jnp.where(pos == off, pltpu.roll(bcum, (off - (w - 1)) % r, axis=0), ref)
            refs.append(ref)

    heads = range(H_C)
    hsl = [slice(h * HEAD_DIM, (h + 1) * HEAD_DIM) for h in heads]
    q = [_silu(q_ref[:, hs]) for hs in hsl]
    k = [kall[:, hs] for hs in hsl]
    b = [bcum[:, hs] for hs in hsl]
    diag = _memo(("diag", nseq, c), lambda: row == col)
    yield
    amat = [jnp.where(diag, _dot_nt(q[h], k[h]), 0.0) for h in heads]
    for w, ref_all in zip(levels, refs):
        yield
        lw = _log2(w)
        level = _memo(("lower_left", c, w), lambda: _half_block_mask(row, col, lw))
        e = [jnp.exp(-jnp.abs(b[h] - ref_all[:, hsl[h]])) for h in heads]
        amat = [jnp.where(level, _dot_nt(q[h] * e[h], k[h] * e[h]), amat[h]) for h in heads]
    yield
    qs = [_cat_rows([_dot(_rows(q[h] * jnp.exp(b[h]), i, c), get_s(i, h)) for i in range(nseq)])
          for h in heads]
    av = [_dot(amat[h], i_ref[:, hsl[h]]) for h in heads]
    yield
    for h in heads:
        hs = hsl[h]
        y_ref[:, hs] = _head_norm_gate(qs[h] + av[h], onw_ref[:, hs], z_ref[:, hs]).astype(BF16)
    yield
    for h in heads:
        hs = hsl[h]
        vh = i_ref[:, hs]
        bl = blast[:, hs]
        kd_t = (k[h] * jnp.exp(bl - b[h])).T.astype(BF16)
        eb_t = jnp.exp(bl).T
        upd = _seq_outer(kd_t, vh, nseq, c)
        for i in range(nseq):
            put_s(i, h, eb_t[:, i * c:i * c + 1] * get_s(i, h) + upd[i])


class _ColView:
    def __init__(self, parts):
        self.parts = []
        pos = 0
        for ref, lo, hi in parts:
            self.parts.append((pos, ref, lo, hi - lo))
            pos += hi - lo
        self.width = pos

    def __getitem__(self, idx):
        if idx is Ellipsis:
            rows, cols = slice(None), slice(0, self.width)
        else:
            rows, cols = idx
            cols = slice(cols.start or 0, self.width if cols.stop is None else cols.stop)
        out = []
        for pos, ref, lo, n in self.parts:
            a, b = max(cols.start, pos), min(cols.stop, pos + n)
            if a < b:
                out.append(ref[rows, lo + a - pos:lo + b - pos])
        return out[0] if len(out) == 1 else jnp.concatenate(out, axis=1)


def _hgrn_views(ca_ref, cb_ref, zc_ref):
    two = 2 * W_C - ca_ref.shape[1]
    q = _ColView([(ca_ref, 0, W_C)])
    f = _ColView([(ca_ref, W_C, ca_ref.shape[1]), (cb_ref, 0, two)])
    i = _ColView([(cb_ref, two, two + W_C)])
    z = _ColView([(zc_ref, 0, W_C)])
    return q, f, i, z


def _hgrn_prompt_kernel(ca_ref, cb_ref, zc_ref, lbp_ref, onw_ref, y_ref, s_out, s_scr,
                        *, layer):
    _TRACE_MEMO.clear()
    j = pl.program_id(0)
    bsz = s_scr.shape[0]

    @pl.when(j == 0)
    def _():
        s_scr[...] = jnp.zeros_like(s_scr)

    def tile(u):
        ca, cb, zc, y = _tile_views((ca_ref, cb_ref, zc_ref, y_ref), u)

        def put_s(i, h, val):
            s_scr[u, h] = val

        return _hgrn_tile(*_hgrn_views(ca, cb, zc), lbp_ref, onw_ref, y,
                          lambda i, h: s_scr[u, h], put_s, 1, TILE_R, layer)

    _interleave(tile(u) for u in range(bsz))

    @pl.when(j == pl.num_programs(0) - 1)
    def _():
        s_out[...] = s_scr[...]


def _hgrn_sample_kernel(ca_ref, cb_ref, zc_ref, lbp_ref, onw_ref, s_in, y_ref, s_out,
                        *, c, layer):
    _TRACE_MEMO.clear()

    def put_s(i, h, val):
        s_out[i, h] = val

    _interleave([_hgrn_tile(*_hgrn_views(ca_ref, cb_ref, zc_ref), lbp_ref, onw_ref, y_ref,
                            lambda i, h: s_in[i, h], put_s, TILE_R // c, c, layer)])


HGRN_BLK = W_A
assert OFF_C % HGRN_BLK == 0 and 3 * W_C == 2 * HGRN_BLK and OFF_ZC % HGRN_BLK == 0


def _hgrn_prompt(proj, lb_par, onw3, layer, bsz, n_chunk):
    t = proj.shape[0]
    tm = bsz * TILE_R
    depth = lb_par.shape[0]
    rows = lambda j: j
    in_specs = [_col_spec(HGRN_BLK, off, rows, tm) for off in (OFF_C, OFF_C + HGRN_BLK, OFF_ZC)]
    in_specs += [pl.BlockSpec((depth, W_C), lambda j: (0, 0)),
                 _param_spec(W_C, W_A + W_B, layer)]
    return pl.pallas_call(
        functools.partial(_hgrn_prompt_kernel, layer=layer),
        grid=(n_chunk,),
        in_specs=in_specs,
        out_specs=[
            pl.BlockSpec((tm, W_C), lambda j: (j, 0)),
            _whole(bsz, H_C, HEAD_DIM, HEAD_DIM),
        ],
        out_shape=[
            jax.ShapeDtypeStruct((t, W_C), BF16),
            jax.ShapeDtypeStruct((bsz, H_C, HEAD_DIM, HEAD_DIM), F32),
        ],
        scratch_shapes=[pltpu.VMEM((bsz, H_C, HEAD_DIM, HEAD_DIM), F32)],
        compiler_params=pltpu.CompilerParams(
            dimension_semantics=("arbitrary",), vmem_limit_bytes=VMEM_LIMIT),
        name="hgrn_prompt",
    )(proj, proj, proj, lb_par, onw3)


def _hgrn_sample(proj, y_prev, s_stack, lb_par, onw3, s_state, layer, dec_b, c, rows):
    nseq = TILE_R // c
    depth = lb_par.shape[0]
    in_specs = [_col_spec(HGRN_BLK, off, rows) for off in (OFF_C, OFF_C + HGRN_BLK, OFF_ZC)]
    in_specs += [
        pl.BlockSpec((depth, W_C), lambda i: (0, 0)),
        _param_spec(W_C, W_A + W_B, layer),
        pl.BlockSpec((None, nseq, H_C, HEAD_DIM, HEAD_DIM), lambda i: (layer, i, 0, 0, 0)),
    ]
    operands = (proj, proj, proj, lb_par, onw3, s_state)
    operands, extra, aliases, strip = _in_place(operands, len(in_specs), {0: y_prev, 1: s_stack})
    return pl.pallas_call(
        strip(functools.partial(_hgrn_sample_kernel, c=c, layer=layer)),
        grid=(dec_b // nseq,),
        in_specs=in_specs + extra,
        out_specs=[
            pl.BlockSpec((TILE_R, W_C), lambda i: (rows(i), 0)),
            pl.BlockSpec((None, nseq, H_C, HEAD_DIM, HEAD_DIM), lambda i: (layer, i, 0, 0, 0)),
        ],
        out_shape=[
            jax.ShapeDtypeStruct(y_prev.shape, BF16),
            jax.ShapeDtypeStruct((depth, dec_b, H_C, HEAD_DIM, HEAD_DIM), F32),
        ],
        input_output_aliases=aliases,
        compiler_params=pltpu.CompilerParams(
            dimension_semantics=("parallel",), vmem_limit_bytes=VMEM_LIMIT),
        name="hgrn_sample",
    )(*operands)


def _gate_weight(w_t):
    ai0 = 4 * W_A
    ba0 = ai0 + 2 * H_A + 3 * W_B
    n_gate = 2 * H_A + 2 * H_B
    pad = jnp.zeros((w_t.shape[0], LANES - n_gate, w_t.shape[2]), w_t.dtype)
    return jnp.concatenate([w_t[:, ai0:ai0 + 2 * H_A], w_t[:, ba0:ba0 + 2 * H_B], pad], axis=1)


def _lane_row(depth, pieces):
    row = jnp.zeros((depth, LANES), F32)
    for off, val in pieces:
        row = lax.dynamic_update_slice(row, val.astype(F32), (0, off))
    return row[:, None, :]


def kernel(x_prompt, x_sample, state_mlstm_C, state_mlstm_n, state_mlstm_m, state_gdn_S,
           state_gdn_conv, state_hgrn_S, meta_tokens, norm_w, w_in, mlstm_gate_b, gdn_A_log,
           gdn_dt_bias, gdn_conv_w, hgrn_lower_bounds, out_norm_w, w_out, final_norm_w):
    bsz, seq_len, d = x_prompt.shape
    dec_b, dec_seq, _ = x_sample.shape
    depth = w_in.shape[0]
    assert d == D_MODEL and seq_len % TILE_R == 0 and TILE_R % dec_seq == 0
    assert w_in.shape[2] == W_SEGMENTS[-1][1] + W_SEGMENTS[-1][2]
    nseq = TILE_R // dec_seq
    assert dec_b % nseq == 0

    chunks = seq_len // TILE_R
    n_chunk = chunks + 1
    tm = bsz * TILE_R
    n_dec = dec_b * dec_seq
    dec_blk0 = n_chunk * bsz

    def dec_rows(i):
        return dec_blk0 + i

    w_t = jnp.swapaxes(w_in.astype(F32), 1, 2)
    w_gate = _gate_weight(w_t)
    w_out_bf = w_out.astype(BF16)
    norm_w3 = norm_w.astype(F32)[:, None, :]
    onw3 = out_norm_w.astype(F32)[:, None, :]
    final_nw = final_norm_w.astype(F32)[None, :]
    gate_bias = _lane_row(depth, [(G_AI, mlstm_gate_b[:, 0]), (G_AF, mlstm_gate_b[:, 1])])
    gdn_par = jnp.concatenate([
        _lane_row(depth, [(G_BA, gdn_dt_bias)]), _lane_row(depth, [(G_BA, gdn_A_log)]),
        jnp.zeros((depth, 6, LANES), F32)], axis=1)
    conv_w = gdn_conv_w.astype(F32)
    lb_par = hgrn_lower_bounds.astype(F32)
    m_rows = jnp.repeat(
        jnp.pad(state_mlstm_m.astype(F32), ((0, 0), (0, 0), (0, LANES - H_A))), dec_seq, axis=1)

    x, h = _prep(x_prompt.astype(F32), x_sample.astype(F32), meta_tokens.astype(F32), norm_w3,
                 bsz, n_chunk)
    t_all = x.shape[0]
    tm_in = max(m for m in range(16, INPROJ_MAX_ROWS + 1, 16) if t_all % m == 0)

    new_p = [[] for _ in range(6)]
    new_s = [[] for _ in range(3)]
    c_s = s_s = h_s = None
    y_prompt = y_sample = None
    for l in range(depth):
        proj = _inproj(h, w_t, w_gate, l, tm_in)

        ya, c_p, n_p, m_p = _mlstm_prompt(proj, gate_bias, onw3, l, bsz, n_chunk)
        ya, c_s, n_s, m_s = _mlstm_sample(proj, ya, c_s, gate_bias, onw3, state_mlstm_C,
                                          state_mlstm_n, m_rows, l, dec_b, dec_seq, dec_rows)
        yb, s_p, cv_p = _gdn_prompt(proj, gdn_par, conv_w, onw3, l, bsz, n_chunk)
        yb, s_s, cv_s = _gdn_sample(proj, yb, s_s, gdn_par, conv_w, onw3, state_gdn_S,
                                    state_gdn_conv, l, dec_b, dec_seq, dec_rows)
        yc, h_p = _hgrn_prompt(proj, lb_par, onw3, l, bsz, n_chunk)
        yc, h_s = _hgrn_sample(proj, yc, h_s, lb_par, onw3, state_hgrn_S, l, dec_b, dec_seq,
                               dec_rows)

        for lst, val in zip(new_p, (c_p, n_p, m_p[:, 0, :H_A], s_p, cv_p, h_p)):
            lst.append(val)
        for lst, val in zip(new_s, (n_s, m_s[::dec_seq, :H_A], cv_s)):
            lst.append(val)

        if l + 1 < depth:
            x, h = _outproj_mid(ya, yb, yc, x, w_out_bf, l, norm_w3, tm)
        else:
            y_prompt = _outproj_final(
                ya, yb, yc, x, w_out_bf, l, final_nw, tm, 1, chunks, (bsz, chunks, TILE_R, d),
                pl.BlockSpec((bsz, None, TILE_R, d), lambda i: (0, i, 0, 0)))
            y_sample = _outproj_final(
                ya, yb, yc, x, w_out_bf, l, final_nw, tm, n_chunk, n_dec // tm, (n_dec, d),
                pl.BlockSpec((tm, d), lambda i: (i, 0)))

    outs_p = [jnp.stack(a, axis=0) for a in new_p]
    n_all, m_all, cv_all = [jnp.stack(a, axis=0) for a in new_s]
    return (y_prompt.reshape(bsz, seq_len, d), y_sample.reshape(dec_b, dec_seq, d),
            *outs_p, c_s, n_all, m_all, s_s, cv_all, h_s)
```

```python
import functools
import math

import jax
import jax.numpy as jnp
from jax import lax
from jax.experimental import pallas as pl
from jax.experimental.pallas import tpu as pltpu

F32 = jnp.float32
BF16 = jnp.bfloat16

D_MODEL = 2048
HEAD_DIM = 128
H_A, H_B, H_C = 6, 6, 4
W_A, W_B, W_C = H_A * HEAD_DIM, H_B * HEAD_DIM, H_C * HEAD_DIM
MIX = W_A + W_B + W_C
N_META = 16
CONV_W = 4
EPS = 1e-6
NEG_BIG = -1e30
EXP_CLIP = 60.0
QK_SCALE = HEAD_DIM ** -0.5

TILE_R = 128
LANES = 128
CONV_PAD = 8

OFF_AQ = 0
OFF_AK, OFF_AV, OFF_AO = W_A, 2 * W_A, 3 * W_A
OFF_BQ = 4 * W_A
OFF_BK, OFF_BV = OFF_BQ + W_B, OFF_BQ + 2 * W_B
OFF_C = OFF_BQ + 3 * W_B
OFF_ZA = OFF_C + 3 * W_C
OFF_ZB = OFF_ZA + W_A
OFF_ZC = OFF_ZB + W_B
OFF_G = OFF_ZC + W_C
N_PACK = OFF_G + 2 * LANES
G_AI, G_AF, G_BA, G_BB = 0, H_A, 2 * H_A, 2 * H_A + H_B
TN = 1024
INPROJ_MAX_ROWS = 1280
W_SEGMENTS = ((OFF_AQ, OFF_BQ, 0), (OFF_BQ, OFF_C, 2 * H_A), (OFF_C, OFF_G, 2 * H_A + 2 * H_B))
W_ROW_CHUNK = 256

VMEM_LIMIT = 56 * 1024 * 1024


def _sigmoid(x):
    return 1.0 / (1.0 + jnp.exp(-x))


def _silu(x):
    return x * _sigmoid(x)


def _log1p(u):
    return jnp.log(1.0 + u)


def _softplus(x):
    return jnp.maximum(x, 0.0) + _log1p(jnp.exp(-jnp.abs(x)))


def _log_sigmoid(x):
    return -_softplus(-x)


def _dot(a, b):
    return jnp.dot(a.astype(BF16), b.astype(BF16), preferred_element_type=F32)


def _dot_nt(a, b):
    return lax.dot_general(a.astype(BF16), b.astype(BF16), (((1,), (1,)), ((), ())),
                           preferred_element_type=F32)


def _dot_exact(sel, x):
    hi = x.astype(BF16)
    r1 = x - hi.astype(F32)
    mid = r1.astype(BF16)
    lo = (r1 - mid.astype(F32)).astype(BF16)
    s = sel.astype(BF16)
    return (jnp.dot(s, hi, preferred_element_type=F32)
            + jnp.dot(s, mid, preferred_element_type=F32)
            + jnp.dot(s, lo, preferred_element_type=F32))


def _log2(n):
    k = int(math.log2(n))
    assert (1 << k) == n
    return k


_TRACE_MEMO = {}


def _memo(key, fn):
    if key not in _TRACE_MEMO:
        _TRACE_MEMO[key] = fn()
    return _TRACE_MEMO[key]


def _seq_masks(nseq, c):
    return _memo(("seq", nseq, c), lambda: _seq_masks_uncached(nseq, c))


def _seq_masks_uncached(nseq, c):
    r = nseq * c
    row = lax.broadcasted_iota(jnp.int32, (r, r), 0)
    col = lax.broadcasted_iota(jnp.int32, (r, r), 1)
    if nseq == 1:
        same = row >= 0
    else:
        k = _log2(c)
        same = (row >> k) == (col >> k)
    return row, col, same


def _seq_outer(lhs_t, x, nseq, c):
    if nseq == 1:
        return [jnp.dot(lhs_t, x.astype(BF16), preferred_element_type=F32)]
    seq = lax.broadcasted_iota(jnp.int32, (nseq * c, 1), 0) >> _log2(c)
    wide = jnp.concatenate([jnp.where(seq == i, x, 0.0).astype(BF16) for i in range(nseq)], axis=1)
    out = jnp.dot(lhs_t, wide, preferred_element_type=F32)
    n = x.shape[1]
    return [out[:, i * n:(i + 1) * n] for i in range(nseq)]


def _seq_cummax(x, nseq, c):
    r = nseq * c
    pos = lax.broadcasted_iota(jnp.int32, x.shape, 0)
    if nseq > 1:
        pos = pos & (c - 1)
    k = 1
    while k < c:
        x = jnp.where(pos >= k, jnp.maximum(x, pltpu.roll(x, k, axis=0)), x)
        k *= 2
    assert x.shape[0] == r
    return x


def _head_norm_gate(h, onw, z):
    hn = h * lax.rsqrt(jnp.mean(h * h, axis=1, keepdims=True) + EPS)
    return hn * onw * _silu(z)


def _rows(x, i, c):
    return x[i * c:(i + 1) * c]


def _cat_rows(parts):
    return parts[0] if len(parts) == 1 else jnp.concatenate(parts, axis=0)


def _chunk_shift(jj, r0):
    p = jj * TN + r0
    for p0, p1, shift in W_SEGMENTS:
        if p0 <= p and p + W_ROW_CHUNK <= p1:
            return shift
    return None


def _fill_weight_tile(w0_ref, w1_ref, wg_ref, wb_scr, tail_scr, jj):
    for r0 in range(0, TN, W_ROW_CHUNK):
        shift = _chunk_shift(jj, r0)
        dst = slice(r0, r0 + W_ROW_CHUNK)
        if shift is None:
            g0 = OFF_G - jj * TN
            assert r0 <= g0 and g0 + LANES <= r0 + W_ROW_CHUNK
            wb_scr[dst, :] = jnp.zeros((W_ROW_CHUNK, D_MODEL), BF16)
            wb_scr[g0:g0 + LANES, :] = wg_ref[...].astype(BF16)
        elif r0 + shift + W_ROW_CHUNK <= TN:
            wb_scr[dst, :] = w0_ref[r0 + shift:r0 + shift + W_ROW_CHUNK, :].astype(BF16)
        else:
            tail_scr[0:W_ROW_CHUNK, :] = w0_ref[r0:TN, :]
            tail_scr[W_ROW_CHUNK:, :] = w1_ref[...]
            wb_scr[dst, :] = tail_scr[shift:shift + W_ROW_CHUNK, :].astype(BF16)


def _inproj_kernel(h_ref, w0_ref, w1_ref, wg_ref, o_ref, wb_scr, tail_scr):
    j = pl.program_id(0)

    @pl.when(pl.program_id(1) == 0)
    def _():
        for jj in range(N_PACK // TN):
            @pl.when(j == jj)
            def _(jj=jj):
                _fill_weight_tile(w0_ref, w1_ref, wg_ref, wb_scr, tail_scr, jj)

    o_ref[...] = lax.dot_general(h_ref[...], wb_scr[...], (((1,), (1,)), ((), ())),
                                 preferred_element_type=F32)


W_TAIL_ROWS = 32


def _inproj(h, w_t, w_gate_t, layer, tm):
    t = h.shape[0]
    n_tiles = N_PACK // TN
    assert n_tiles - 1 <= (w_t.shape[1] - 1) // TN and W_TAIL_ROWS >= W_SEGMENTS[-1][2]
    tail_per_tile = TN // W_TAIL_ROWS
    last_tail = (w_t.shape[1] - 1) // W_TAIL_ROWS
    return pl.pallas_call(
        _inproj_kernel,
        grid=(n_tiles, t // tm),
        in_specs=[
            pl.BlockSpec((tm, D_MODEL), lambda j, i: (i, 0)),
            pl.BlockSpec((None, TN, D_MODEL), lambda j, i: (layer, j, 0)),
            pl.BlockSpec((None, W_TAIL_ROWS, D_MODEL),
                         lambda j, i: (layer, jnp.minimum((j + 1) * tail_per_tile, last_tail), 0)),
            pl.BlockSpec((None, LANES, D_MODEL), lambda j, i: (layer, 0, 0)),
        ],
        out_specs=pl.BlockSpec((tm, TN), lambda j, i: (i, j)),
        out_shape=jax.ShapeDtypeStruct((t, N_PACK), F32),
        scratch_shapes=[pltpu.VMEM((TN, D_MODEL), BF16),
                        pltpu.VMEM((W_ROW_CHUNK + W_TAIL_ROWS, D_MODEL), F32)],
        compiler_params=pltpu.CompilerParams(
            dimension_semantics=("arbitrary", "arbitrary"), vmem_limit_bytes=VMEM_LIMIT),
        name="inproj",
    )(h, w_t, w_t, w_gate_t)


def _rmsnorm(x, w):
    ms = jnp.mean(x * x, axis=1, keepdims=True)
    return x * lax.rsqrt(ms + EPS) * w


def _prep_kernel(xp_ref, xs_ref, meta_ref, nw_ref, x_ref, h_ref, *, n_chunk, bsz):
    i = pl.program_id(0)
    d = x_ref.shape[1]

    def emit(x):
        x_ref[...] = x
        h_ref[...] = _rmsnorm(x, nw_ref[...]).astype(BF16)

    @pl.when(i == 0)
    def _():
        slot = jnp.concatenate([jnp.zeros((TILE_R - N_META, d), F32), meta_ref[...]], axis=0)
        emit(jnp.concatenate([slot] * bsz, axis=0))

    @pl.when((i > 0) & (i < n_chunk))
    def _():
        emit(xp_ref[...].reshape(bsz * TILE_R, d))

    @pl.when(i >= n_chunk)
    def _():
        emit(xs_ref[...])


def _prep(x_prompt, x_sample, meta, norm_w3, bsz, n_chunk):
    d = x_prompt.shape[-1]
    tm = bsz * TILE_R
    n_dec = x_sample.shape[0] * x_sample.shape[1]
    assert n_dec % tm == 0
    xp = x_prompt.reshape(bsz, n_chunk - 1, TILE_R, d)
    xs = x_sample.reshape(n_dec, d)
    n_blk = n_chunk + n_dec // tm
    t = n_blk * tm
    return pl.pallas_call(
        functools.partial(_prep_kernel, n_chunk=n_chunk, bsz=bsz),
        grid=(n_blk,),
        in_specs=[
            pl.BlockSpec((bsz, None, TILE_R, d), lambda i: (0, jnp.clip(i - 1, 0, n_chunk - 2), 0, 0)),
            pl.BlockSpec((tm, d), lambda i: (jnp.clip(i - n_chunk, 0, n_dec // tm - 1), 0)),
            pl.BlockSpec((N_META, d), lambda i: (0, 0)),
            pl.BlockSpec((None, 1, d), lambda i: (0, 0, 0)),
        ],
        out_specs=[pl.BlockSpec((tm, d), lambda i: (i, 0)),
                   pl.BlockSpec((tm, d), lambda i: (i, 0))],
        out_shape=[jax.ShapeDtypeStruct((t, d), F32), jax.ShapeDtypeStruct((t, d), BF16)],
        compiler_params=pltpu.CompilerParams(
            dimension_semantics=("arbitrary",), vmem_limit_bytes=VMEM_LIMIT),
        name="prep",
    )(xp, xs, meta, norm_w3)


def _outproj_residual(ya_ref, yb_ref, yc_ref, x_ref, w_ref):
    acc = jnp.dot(ya_ref[...], w_ref[0:W_A, :], preferred_element_type=F32)
    acc = acc + jnp.dot(yb_ref[...], w_ref[W_A:W_A + W_B, :], preferred_element_type=F32)
    acc = acc + jnp.dot(yc_ref[...], w_ref[W_A + W_B:MIX, :], preferred_element_type=F32)
    return x_ref[...] + acc


def _outproj_mid_kernel(ya_ref, yb_ref, yc_ref, x_ref, w_ref, nw_ref, x_out, h_out):
    xn = _outproj_residual(ya_ref, yb_ref, yc_ref, x_ref, w_ref)
    x_out[...] = xn
    h_out[...] = _rmsnorm(xn, nw_ref[...]).astype(BF16)


def _outproj_final_kernel(ya_ref, yb_ref, yc_ref, x_ref, w_ref, nw_ref, y_out):
    xn = _outproj_residual(ya_ref, yb_ref, yc_ref, x_ref, w_ref)
    y_out[...] = _rmsnorm(xn, nw_ref[...]).reshape(y_out.shape)


def _outproj_specs(tm, row_blk0, layer, nw_spec):
    rows = lambda i: (row_blk0 + i, 0)
    return [
        pl.BlockSpec((tm, W_A), rows),
        pl.BlockSpec((tm, W_B), rows),
        pl.BlockSpec((tm, W_C), rows),
        pl.BlockSpec((tm, D_MODEL), rows),
        pl.BlockSpec((None, MIX, D_MODEL), lambda i: (layer, 0, 0)),
        nw_spec,
    ]


def _outproj_mid(ya, yb, yc, x, w_out, layer, norm_w3, tm):
    t = x.shape[0]
    blk = pl.BlockSpec((tm, D_MODEL), lambda i: (i, 0))
    nw_spec = pl.BlockSpec((None, 1, D_MODEL), lambda i: (layer + 1, 0, 0))
    return pl.pallas_call(
        _outproj_mid_kernel,
        grid=(t // tm,),
        in_specs=_outproj_specs(tm, 0, layer, nw_spec),
        out_specs=[blk, blk],
        out_shape=[jax.ShapeDtypeStruct((t, D_MODEL), F32), jax.ShapeDtypeStruct((t, D_MODEL), BF16)],
        compiler_params=pltpu.CompilerParams(
            dimension_semantics=("parallel",), vmem_limit_bytes=VMEM_LIMIT),
        name="outproj",
    )(ya, yb, yc, x, w_out, norm_w3)


def _outproj_final(ya, yb, yc, x, w_out, layer, final_nw, tm, row_blk0, n_blk, out_shape, out_spec):
    nw_spec = pl.BlockSpec((1, D_MODEL), lambda i: (0, 0))
    return pl.pallas_call(
        _outproj_final_kernel,
        grid=(n_blk,),
        in_specs=_outproj_specs(tm, row_blk0, layer, nw_spec),
        out_specs=out_spec,
        out_shape=jax.ShapeDtypeStruct(out_shape, F32),
        compiler_params=pltpu.CompilerParams(
            dimension_semantics=("parallel",), vmem_limit_bytes=VMEM_LIMIT),
        name="outproj_final",
    )(ya, yb, yc, x, w_out, final_nw)


class _RowView:
    def __init__(self, ref, r0, n):
        self.ref, self.r0, self.n = ref, r0, n
        self.shape = (n, ref.shape[1])

    def _idx(self, idx):
        if idx is Ellipsis:
            return slice(self.r0, self.r0 + self.n), slice(None)
        rows, cols = idx
        start = self.r0 + (rows.start or 0)
        stop = self.r0 + (self.n if rows.stop is None else rows.stop)
        return slice(start, stop), cols

    def __getitem__(self, idx):
        return self.ref[self._idx(idx)]

    def __setitem__(self, idx, val):
        self.ref[self._idx(idx)] = val


def _tile_views(refs, u):
    return [_RowView(ref, u * TILE_R, TILE_R) for ref in refs]


def _interleave(tiles):
    tiles = list(tiles)
    while tiles:
        alive = []
        for t in tiles:
            try:
                next(t)
                alive.append(t)
            except StopIteration:
                pass
        tiles = alive


def _mlstm_tile(q_ref, k_ref, v_ref, o_ref, z_ref, g_ref, gb_ref, onw_ref, y_ref,
                m_exp, valid, get_c, get_n, put_c, put_n, put_m, nseq, c):
    r = nseq * c
    row, col, same = _seq_masks(nseq, c)
    incl = _memo(("incl", nseq, c), lambda: same & (col <= row))
    g = g_ref[...] + gb_ref[...]
    li = g
    lf = _log_sigmoid(g)
    if valid is not None:
        li = jnp.where(valid, li, NEG_BIG)
        lf = jnp.where(valid, lf, 0.0)
    fcum = _dot_exact(incl.astype(F32), lf)
    if nseq == 1:
        flast = jnp.broadcast_to(fcum[r - 1:r, :], (r, LANES))
    else:
        flast = _dot_exact(same.astype(F32), lf)
    yield
    heads = range(H_A)
    hsl = [slice(h * HEAD_DIM, (h + 1) * HEAD_DIM) for h in heads]

    li_a = pltpu.roll(li, G_AF - G_AI, axis=1)
    m_a = pltpu.roll(m_exp, G_AF, axis=1)
    x = li_a - fcum
    mx = jnp.maximum(m_a, _seq_cummax(x, nseq, c))
    mt_all = fcum + mx
    if nseq == 1:
        m_new = jnp.broadcast_to(mt_all[r - 1:r, :], (r, LANES))
    else:
        m_new = jnp.broadcast_to(mt_all.reshape(nseq, c, LANES)[:, c - 1:c, :],
                                 (nseq, c, LANES)).reshape(r, LANES)
    a_all = jnp.exp(m_a - mx)
    w_all = jnp.exp(flast - fcum + li_a - m_new)
    as_all = jnp.exp(flast + m_a - m_new)
    low_all = jnp.exp(jnp.minimum(-mt_all, EXP_CLIP))
    put_m(pltpu.roll(m_new, LANES - G_AF, axis=1))
    x_t = x.T
    col_of = lambda arr, h: arr[:, G_AF + h:G_AF + h + 1]
    a = [col_of(a_all, h) for h in heads]
    w = [col_of(w_all, h) for h in heads]
    a_s = [col_of(as_all, h) for h in heads]
    low = [col_of(low_all, h) for h in heads]
    pexp = [jnp.exp(jnp.where(incl, x_t[G_AF + h:G_AF + h + 1, :] - col_of(mx, h), NEG_BIG))
            for h in heads]
    yield

    s = [_dot_nt(q_ref[:, hs], k_ref[:, hs] * QK_SCALE) for hs in hsl]
    yield
    p = [pexp[h] * s[h] for h in heads]
    ones = jnp.ones((r, HEAD_DIM), BF16)
    pvs = [jnp.dot(p[h].astype(BF16),
                   jnp.concatenate([v_ref[:, hsl[h]].astype(BF16), ones], axis=1),
                   preferred_element_type=F32) for h in heads]
    pv = [x[:, :HEAD_DIM] for x in pvs]
    psum = [x[:, HEAD_DIM:] for x in pvs]
    yield
    qc = [_cat_rows([_dot(_rows(q_ref[:, hsl[h]], i, c), get_c(i, h)) for i in range(nseq)])
          for h in heads]
    yield
    if nseq == 1:
        qn = [_dot_nt(q_ref[:, hsl[h]], jnp.broadcast_to(get_n(0, h), (HEAD_DIM, HEAD_DIM)))
              for h in heads]
    else:
        qn = [_cat_rows([jnp.sum(_rows(q_ref[:, hsl[h]], i, c) * get_n(i, h), axis=1,
                                 keepdims=True) for i in range(nseq)]) for h in heads]
    yield
    hh = [(a[h] * qc[h] + pv[h]) / jnp.maximum(jnp.abs(a[h] * qn[h] + psum[h]), low[h])
          * _sigmoid(o_ref[:, hsl[h]]) for h in heads]
    ms = [_dot(hh[h] * hh[h], ones) * (1.0 / HEAD_DIM) for h in heads]
    yield
    for h in heads:
        hs = hsl[h]
        y_ref[:, hs] = (hh[h] * lax.rsqrt(ms[h] + EPS) * onw_ref[:, hs]
                        * _silu(z_ref[:, hs])).astype(BF16)
    yield

    for h in heads:
        hs = hsl[h]
        kh = k_ref[:, hs] * QK_SCALE
        wk = w[h] * kh
        upd = _seq_outer(kh.T.astype(BF16), w[h] * v_ref[:, hs], nseq, c)
        for i in range(nseq):
            a_i = a_s[h][i * c:i * c + 1, :]
            put_c(i, h, a_i * get_c(i, h) + upd[i])
            put_n(i, h, a_i * get_n(i, h) + jnp.sum(_rows(wk, i, c), axis=0, keepdims=True))


def _mlstm_prompt_kernel(q_ref, k_ref, v_ref, o_ref, z_ref, g_ref, gb_ref, onw_ref,
                         y_ref, c_out, n_out, m_out, c_scr, n_scr, m_scr):
    _TRACE_MEMO.clear()
    j = pl.program_id(0)
    bsz = c_scr.shape[0]

    @pl.when(j == 0)
    def _():
        c_scr[...] = jnp.zeros_like(c_scr)
        n_scr[...] = jnp.zeros_like(n_scr)
        m_scr[...] = jnp.zeros_like(m_scr)

    tok = lax.broadcasted_iota(jnp.int32, (TILE_R, LANES), 0) + j * TILE_R
    valid = tok >= TILE_R - N_META

    def tile(u):
        q, k, v, o, z, g, y = _tile_views((q_ref, k_ref, v_ref, o_ref, z_ref, g_ref, y_ref), u)
        m_exp = jnp.broadcast_to(m_scr[u, 0:1, :], (TILE_R, LANES))

        def put_c(i, h, val):
            c_scr[u, h] = val

        def put_n(i, h, val):
            n_scr[u, h:h + 1, :] = val

        def put_m(val):
            m_scr[u] = val[0:8, :]

        return _mlstm_tile(q, k, v, o, z, g, gb_ref, onw_ref, y, m_exp, valid,
                           lambda i, h: c_scr[u, h], lambda i, h: n_scr[u, h:h + 1, :],
                           put_c, put_n, put_m, 1, TILE_R)

    _interleave(tile(u) for u in range(bsz))

    @pl.when(j == pl.num_programs(0) - 1)
    def _():
        c_out[...] = c_scr[...]
        n_out[...] = n_scr[:, 0:H_A, :]
        m_out[...] = m_scr[:, 0:1, :]


def _mlstm_sample_kernel(q_ref, k_ref, v_ref, o_ref, z_ref, g_ref, gb_ref, onw_ref,
                         c_in, n_in, m_in, y_ref, c_out, n_out, m_out, *, c):
    _TRACE_MEMO.clear()
    nseq = TILE_R // c

    def put_c(i, h, val):
        c_out[i, h] = val

    def put_n(i, h, val):
        n_out[i, h:h + 1, :] = val

    def put_m(val):
        m_out[...] = val

    _interleave([_mlstm_tile(
        q_ref, k_ref, v_ref, o_ref, z_ref, g_ref, gb_ref, onw_ref, y_ref, m_in[...], None,
        lambda i, h: c_in[i, h], lambda i, h: n_in[i, h:h + 1, :], put_c, put_n, put_m, nseq, c)])


def _col_spec(width, off, rows, n_rows=TILE_R):
    assert off % width == 0
    blk = off // width
    return pl.BlockSpec((n_rows, width), lambda *ids: (rows(*ids), blk))


def _param_spec(width, off, layer):
    assert off % width == 0
    blk = off // width
    return pl.BlockSpec((None, 1, width), lambda *ids: (layer, 0, blk))


def _whole(*shape):
    return pl.BlockSpec(shape, lambda j: (0,) * len(shape))


def _mlstm_prompt(proj, gate_bias, onw3, layer, bsz, n_chunk):
    t = proj.shape[0]
    tm = bsz * TILE_R
    rows = lambda j: j
    in_specs = [_col_spec(W_A, off, rows, tm) for off in (OFF_AQ, OFF_AK, OFF_AV, OFF_AO, OFF_ZA)]
    in_specs += [_col_spec(LANES, OFF_G, rows, tm), _param_spec(LANES, 0, layer),
                 _param_spec(W_A, 0, layer)]
    return pl.pallas_call(
        _mlstm_prompt_kernel,
        grid=(n_chunk,),
        in_specs=in_specs,
        out_specs=[
            pl.BlockSpec((tm, W_A), lambda j: (j, 0)),
            _whole(bsz, H_A, HEAD_DIM, HEAD_DIM),
            _whole(bsz, H_A, HEAD_DIM),
            _whole(bsz, 1, LANES),
        ],
        out_shape=[
            jax.ShapeDtypeStruct((t, W_A), BF16),
            jax.ShapeDtypeStruct((bsz, H_A, HEAD_DIM, HEAD_DIM), F32),
            jax.ShapeDtypeStruct((bsz, H_A, HEAD_DIM), F32),
            jax.ShapeDtypeStruct((bsz, 1, LANES), F32),
        ],
        scratch_shapes=[pltpu.VMEM((bsz, H_A, HEAD_DIM, HEAD_DIM), F32),
                        pltpu.VMEM((bsz, 8, HEAD_DIM), F32), pltpu.VMEM((bsz, 8, LANES), F32)],
        compiler_params=pltpu.CompilerParams(
            dimension_semantics=("arbitrary",), vmem_limit_bytes=VMEM_LIMIT),
        name="mlstm_prompt",
    )(proj, proj, proj, proj, proj, proj, gate_bias, onw3)


def _in_place(operands, n_blocked, targets):
    specs, aliases = [], {}
    operands = list(operands)
    for out_idx, arr in targets.items():
        if arr is not None:
            aliases[len(operands)] = out_idx
            operands.append(arr)
            specs.append(pl.BlockSpec(memory_space=pl.ANY))
    n_extra = len(specs)

    def strip(kernel):
        def body(*refs):
            kernel(*refs[:n_blocked], *refs[n_blocked + n_extra:])
        return body

    return operands, specs, aliases, strip


def _mlstm_sample(proj, y_prev, c_stack, gate_bias, onw3, c_state, n_state, m_rows, layer,
                  dec_b, c, rows):
    nseq = TILE_R // c
    depth = c_state.shape[0]
    in_specs = [_col_spec(W_A, off, rows) for off in (OFF_AQ, OFF_AK, OFF_AV, OFF_AO, OFF_ZA)]
    in_specs += [_col_spec(LANES, OFF_G, rows), _param_spec(LANES, 0, layer),
                 _param_spec(W_A, 0, layer)]
    in_specs += [
        pl.BlockSpec((None, nseq, H_A, HEAD_DIM, HEAD_DIM), lambda i: (layer, i, 0, 0, 0)),
        pl.BlockSpec((None, nseq, H_A, HEAD_DIM), lambda i: (layer, i, 0, 0)),
        pl.BlockSpec((None, TILE_R, LANES), lambda i: (layer, i, 0)),
    ]
    operands = (proj, proj, proj, proj, proj, proj, gate_bias, onw3, c_state, n_state, m_rows)
    operands, extra, aliases, strip = _in_place(operands, len(in_specs), {0: y_prev, 1: c_stack})
    return pl.pallas_call(
        strip(functools.partial(_mlstm_sample_kernel, c=c)),
        grid=(dec_b // nseq,),
        in_specs=in_specs + extra,
        out_specs=[
            pl.BlockSpec((TILE_R, W_A), lambda i: (rows(i), 0)),
            pl.BlockSpec((None, nseq, H_A, HEAD_DIM, HEAD_DIM), lambda i: (layer, i, 0, 0, 0)),
            pl.BlockSpec((nseq, H_A, HEAD_DIM), lambda i: (i, 0, 0)),
            pl.BlockSpec((TILE_R, LANES), lambda i: (i, 0)),
        ],
        out_shape=[
            jax.ShapeDtypeStruct(y_prev.shape, BF16),
            jax.ShapeDtypeStruct((depth, dec_b, H_A, HEAD_DIM, HEAD_DIM), F32),
            jax.ShapeDtypeStruct((dec_b, H_A, HEAD_DIM), F32),
            jax.ShapeDtypeStruct((dec_b * c, LANES), F32),
        ],
        input_output_aliases=aliases,
        compiler_params=pltpu.CompilerParams(
            dimension_semantics=("parallel",), vmem_limit_bytes=VMEM_LIMIT),
        name="mlstm_sample",
    )(*operands)


def _l2norm(x):
    return x * lax.rsqrt(jnp.sum(x * x, axis=1, keepdims=True) + EPS)


INV_LEAF = 16


def _half_block_mask(row, col, k):
    return (((row >> (k + 1)) == (col >> (k + 1)))
            & (((row >> k) & 1) == 1) & (((col >> k) & 1) == 0))


def _unit_lower_inverse(a_list, row, col, c):
    leaf = min(INV_LEAF, c)
    kl = _log2(leaf)
    leaf_mask = _memo(("leaf", c), lambda: (row >> kl) == (col >> kl))
    eye = _memo(("eye", c), lambda: jnp.where(row == col, 1.0, 0.0))
    p = [jnp.where(leaf_mask, -a, 0.0) for a in a_list]
    t = [eye + n for n in p]
    for _ in range(kl - 1):
        p = [_dot(x, x) for x in p]
        yield
        t = [ti + _dot(pi, ti) for pi, ti in zip(p, t)]
        yield
    w = leaf
    while w < c:
        kw = _log2(w)
        lower_left = _memo(("lower_left", c, w), lambda: _half_block_mask(row, col, kw))
        n_pair = row.shape[0] // (2 * w)
        second = lambda x: _cat_rows([x[(2 * m + 1) * w:(2 * m + 2) * w] for m in range(n_pair)])
        t_sec = [second(ti) for ti in t]
        left = [_dot(ts, jnp.where(lower_left, a, 0.0)) for ts, a in zip(t_sec, a_list)]
        yield
        new_sec = [ts - _dot(li, ti) for ts, li, ti in zip(t_sec, left, t)]
        t = [_cat_rows([part for m in range(n_pair)
                        for part in (ti[2 * m * w:(2 * m + 1) * w], ns[m * w:(m + 1) * w])])
             for ti, ns in zip(t, new_sec)]
        yield
        w *= 2
    return t


def _gdn_tile(u_cols, prev_cols, cw_ref, z_ref, g_ref, gp_ref, onw_ref, y_ref, get_s, put_s,
              nseq, c):
    r = nseq * c
    row, col, same = _seq_masks(nseq, c)
    incl = _memo(("incl", nseq, c), lambda: same & (col <= row))
    strict = _memo(("strict", nseq, c), lambda: same & (col < row))
    gp = gp_ref[...]
    graw = g_ref[...]
    gdec = -jnp.exp(gp[1:2, :]) * _softplus(graw + gp[0:1, :])
    beta = _sigmoid(graw)
    gcum = _dot_exact(incl.astype(F32), gdec)
    if nseq == 1:
        glast = jnp.broadcast_to(gcum[r - 1:r, :], (r, LANES))
    else:
        glast = _dot_exact(same.astype(F32), gdec)
    yield
    gcum_t = gcum.T

    heads = range(H_B)
    hsl = [slice(h * HEAD_DIM, (h + 1) * HEAD_DIM) for h in heads]

    pos = lax.broadcasted_iota(jnp.int32, (CONV_PAD if nseq == 1 else r, HEAD_DIM), 0)
    if nseq > 1:
        pos = pos & (c - 1)

    def conv(lo):
        cols = slice(lo, lo + HEAD_DIM)
        x = u_cols(cols)
        prev = prev_cols(cols)
        acc = x * cw_ref[CONV_W - 1:CONV_W, cols]
        for back in range(1, CONV_W):
            shifted = pltpu.roll(x, back, axis=0)
            if nseq == 1:
                first = jnp.where(pos < back, pltpu.roll(prev, back, axis=0), shifted[0:CONV_PAD])
                shifted = jnp.concatenate([first, shifted[CONV_PAD:]], axis=0)
            else:
                shifted = jnp.where(pos < back, pltpu.roll(prev, (back - CONV_PAD) % r, axis=0),
                                    shifted)
            acc = acc + shifted * cw_ref[CONV_W - 1 - back:CONV_W - back, cols]
        return _silu(acc)

    q = [_l2norm(conv(hs.start)) * QK_SCALE for hs in hsl]
    k = [_l2norm(conv(W_B + hs.start)) for hs in hsl]
    v = [conv(2 * W_B + hs.start) for hs in hsl]
    yield
    gc = [gcum[:, G_BA + h:G_BA + h + 1] for h in heads]
    gl = [glast[:, G_BA + h:G_BA + h + 1] for h in heads]
    bc = [beta[:, G_BB + h:G_BB + h + 1] for h in heads]
    decay = [jnp.exp(jnp.where(incl, gc[h] - gcum_t[G_BA + h:G_BA + h + 1, :], NEG_BIG))
             for h in heads]
    kk = [_dot_nt(k[h], k[h]) for h in heads]
    qk = [_dot_nt(q[h], k[h]) for h in heads]
    yield
    amat = [jnp.where(strict, bc[h] * decay[h] * kk[h], 0.0) for h in heads]
    tinv = yield from _unit_lower_inverse(amat, row, col, c)
    sol = [_dot(tinv[h], jnp.concatenate([bc[h] * v[h], (bc[h] * jnp.exp(gc[h])) * k[h]], axis=1))
           for h in heads]
    yield
    both = [[_dot(jnp.concatenate([_rows(sol[h][:, HEAD_DIM:], i, c), _rows(q[h], i, c)], axis=0),
                  get_s(i, h)) for i in range(nseq)] for h in heads]
    wks = [_cat_rows([both[h][i][:c] for i in range(nseq)]) for h in heads]
    qs = [_cat_rows([both[h][i][c:] for i in range(nseq)]) for h in heads]
    yield
    wmat = [sol[h][:, :HEAD_DIM] - wks[h] for h in heads]
    aw = [_dot(decay[h] * qk[h], wmat[h]) for h in heads]
    yield
    for h in heads:
        hs = hsl[h]
        oh = jnp.exp(gc[h]) * qs[h] + aw[h]
        y_ref[:, hs] = _head_norm_gate(oh, onw_ref[:, hs], z_ref[:, hs]).astype(BF16)
    yield
    for h in heads:
        kd_t = (k[h] * jnp.exp(gl[h] - gc[h])).T.astype(BF16)
        e_last = jnp.exp(gl[h])
        upd = _seq_outer(kd_t, wmat[h], nseq, c)
        for i in range(nseq):
            put_s(i, h, e_last[i * c:i * c + 1, :] * get_s(i, h) + upd[i])


def _gdn_prompt_kernel(q_ref, k_ref, v_ref, z_ref, g_ref, gp_ref, cw_ref, onw_ref,
                       y_ref, s_out, conv_out, s_scr, hist_scr):
    _TRACE_MEMO.clear()
    j = pl.program_id(0)
    bsz = s_scr.shape[0]
    pieces = ((q_ref, 0), (k_ref, W_B), (v_ref, 2 * W_B))

    @pl.when(j == 0)
    def _():
        s_scr[...] = jnp.zeros_like(s_scr)
        hist_scr[...] = jnp.zeros_like(hist_scr)

    def tile(u):
        z, g, y = _tile_views((z_ref, g_ref, y_ref), u)

        def u_cols(cols):
            ref, off = pieces[cols.start // W_B]
            return ref[u * TILE_R:(u + 1) * TILE_R, cols.start - off:cols.stop - off]

        def prev_cols(cols):
            return hist_scr[u, :, cols]

        def put_s(i, h, val):
            s_scr[u, h] = val

        return _gdn_tile(u_cols, prev_cols, cw_ref, z, g, gp_ref, onw_ref, y,
                         lambda i, h: s_scr[u, h], put_s, 1, TILE_R)

    _interleave(tile(u) for u in range(bsz))

    for u in range(bsz):
        for ref, off in pieces:
            hist_scr[u, :, off:off + W_B] = ref[(u + 1) * TILE_R - CONV_PAD:(u + 1) * TILE_R, :]

    @pl.when(j == pl.num_programs(0) - 1)
    def _():
        s_out[...] = s_scr[...]
        conv_out[...] = hist_scr[:, CONV_PAD - (CONV_W - 1):, :]


def _gdn_sample_kernel(q_ref, k_ref, v_ref, z_ref, g_ref, gp_ref, cw_ref, onw_ref,
                       s_in, conv_in, y_ref, s_out, conv_out, hist_scr, *, c):
    _TRACE_MEMO.clear()
    nseq = TILE_R // c
    hist = CONV_W - 1
    assert c == CONV_PAD
    pieces = ((q_ref, 0), (k_ref, W_B), (v_ref, 2 * W_B))
    hist_scr[:, 0:CONV_PAD - hist, :] = jnp.zeros((nseq, CONV_PAD - hist, 3 * W_B), F32)
    hist_scr[:, CONV_PAD - hist:, :] = conv_in[...]

    def u_cols(cols):
        ref, off = pieces[cols.start // W_B]
        return ref[:, cols.start - off:cols.stop - off]

    def prev_cols(cols):
        return hist_scr[:, :, cols].reshape(TILE_R, cols.stop - cols.start)

    def put_s(i, h, val):
        s_out[i, h] = val

    _interleave([_gdn_tile(u_cols, prev_cols, cw_ref, z_ref, g_ref, gp_ref, onw_ref, y_ref,
                           lambda i, h: s_in[i, h], put_s, nseq, c)])
    for ref, off in pieces:
        conv_out[:, :, off:off + W_B] = ref[...].reshape(nseq, c, W_B)[:, c - hist:, :]


def _gdn_prompt(proj, gdn_par, conv_w, onw3, layer, bsz, n_chunk):
    t = proj.shape[0]
    tm = bsz * TILE_R
    rows = lambda j: j
    in_specs = [_col_spec(W_B, off, rows, tm) for off in (OFF_BQ, OFF_BK, OFF_BV, OFF_ZB)]
    in_specs += [
        _col_spec(LANES, OFF_G, rows, tm),
        pl.BlockSpec((None, 8, LANES), lambda j: (layer, 0, 0)),
        pl.BlockSpec((None, CONV_W, 3 * W_B), lambda j: (layer, 0, 0)),
        _param_spec(W_B, W_A, layer),
    ]
    return pl.pallas_call(
        _gdn_prompt_kernel,
        grid=(n_chunk,),
        in_specs=in_specs,
        out_specs=[
            pl.BlockSpec((tm, W_B), lambda j: (j, 0)),
            _whole(bsz, H_B, HEAD_DIM, HEAD_DIM),
            _whole(bsz, CONV_W - 1, 3 * W_B),
        ],
        out_shape=[
            jax.ShapeDtypeStruct((t, W_B), BF16),
            jax.ShapeDtypeStruct((bsz, H_B, HEAD_DIM, HEAD_DIM), F32),
            jax.ShapeDtypeStruct((bsz, CONV_W - 1, 3 * W_B), F32),
        ],
        scratch_shapes=[pltpu.VMEM((bsz, H_B, HEAD_DIM, HEAD_DIM), F32),
                        pltpu.VMEM((bsz, CONV_PAD, 3 * W_B), F32)],
        compiler_params=pltpu.CompilerParams(
            dimension_semantics=("arbitrary",), vmem_limit_bytes=VMEM_LIMIT),
        name="gdn_prompt",
    )(proj, proj, proj, proj, proj, gdn_par, conv_w, onw3)


def _gdn_sample(proj, y_prev, s_stack, gdn_par, conv_w, onw3, s_state, conv_state, layer,
                dec_b, c, rows):
    nseq = TILE_R // c
    depth = s_state.shape[0]
    in_specs = [_col_spec(W_B, off, rows) for off in (OFF_BQ, OFF_BK, OFF_BV, OFF_ZB)]
    in_specs += [
        _col_spec(LANES, OFF_G, rows),
        pl.BlockSpec((None, 8, LANES), lambda i: (layer, 0, 0)),
        pl.BlockSpec((None, CONV_W, 3 * W_B), lambda i: (layer, 0, 0)),
        _param_spec(W_B, W_A, layer),
        pl.BlockSpec((None, nseq, H_B, HEAD_DIM, HEAD_DIM), lambda i: (layer, i, 0, 0, 0)),
        pl.BlockSpec((None, nseq, CONV_W - 1, 3 * W_B), lambda i: (layer, i, 0, 0)),
    ]
    operands = (proj, proj, proj, proj, proj, gdn_par, conv_w, onw3, s_state, conv_state)
    operands, extra, aliases, strip = _in_place(operands, len(in_specs), {0: y_prev, 1: s_stack})
    return pl.pallas_call(
        strip(functools.partial(_gdn_sample_kernel, c=c)),
        grid=(dec_b // nseq,),
        in_specs=in_specs + extra,
        out_specs=[
            pl.BlockSpec((TILE_R, W_B), lambda i: (rows(i), 0)),
            pl.BlockSpec((None, nseq, H_B, HEAD_DIM, HEAD_DIM), lambda i: (layer, i, 0, 0, 0)),
            pl.BlockSpec((nseq, CONV_W - 1, 3 * W_B), lambda i: (i, 0, 0)),
        ],
        out_shape=[
            jax.ShapeDtypeStruct(y_prev.shape, BF16),
            jax.ShapeDtypeStruct((depth, dec_b, H_B, HEAD_DIM, HEAD_DIM), F32),
            jax.ShapeDtypeStruct((dec_b, CONV_W - 1, 3 * W_B), F32),
        ],
        scratch_shapes=[pltpu.VMEM((nseq, CONV_PAD, 3 * W_B), F32)],
        input_output_aliases=aliases,
        compiler_params=pltpu.CompilerParams(
            dimension_semantics=("parallel",), vmem_limit_bytes=VMEM_LIMIT),
        name="gdn_sample",
    )(*operands)


def _hgrn_tile(q_ref, f_ref, i_ref, z_ref, lbp_ref, onw_ref, y_ref, get_s, put_s,
               nseq, c, layer):
    r = nseq * c
    row, col, same = _seq_masks(nseq, c)
    incl = _memo(("incl", nseq, c), lambda: same & (col <= row))
    rowv = lax.broadcasted_iota(jnp.int32, (r, 1), 0)
    lbp = lbp_ref[...]
    e = jnp.exp(lbp - jnp.max(lbp, axis=0, keepdims=True))
    sm = e / jnp.sum(e, axis=0, keepdims=True)
    lb = jnp.zeros((1, W_C), F32)
    for l in range(1, layer + 1):
        lb = lb + sm[l:l + 1, :]

    cf = f_ref[...]
    logf = _log_sigmoid(cf)
    if layer > 0:
        logf = logf + _log1p(lb * jnp.exp(jnp.minimum(-cf, EXP_CLIP)))
    kall = (1.0 - lb) * _sigmoid(-cf)
    bcum = _dot_exact(incl.astype(F32), logf)
    if nseq == 1:
        blast = jnp.broadcast_to(bcum[r - 1:r, :], (r, W_C))
    else:
        blast = _dot_exact(same.astype(F32), logf)

    levels = []
    w = c // 2
    while w >= 1:
        levels.append(w)
        w //= 2
    rowf = lax.broadcasted_iota(jnp.int32, (r, W_C), 0)
    refs = []
    for w in levels:
        if 2 * w >= 8:
            refs.append(_cat_rows([
                jnp.broadcast_to(bcum[g * 2 * w + w - 1:g * 2 * w + w, :], (2 * w, W_C))
                for g in range(r // (2 * w))]))
        else:
            pos = rowf & (2 * w - 1)
            ref = bcum
            for off in range(2 * w):
                if off != w - 1:
                    ref = jnp.where(pos == off, pltpu.roll(bcum, (off - (w - 1)) % r, axis=0), ref)
            refs.append(ref)

    heads = range(H_C)
    hsl = [slice(h * HEAD_DIM, (h + 1) * HEAD_DIM) for h in heads]
    q = [_silu(q_ref[:, hs]) for hs in hsl]
    k = [kall[:, hs] for hs in hsl]
    b = [bcum[:, hs] for hs in hsl]
    diag = _memo(("diag", nseq, c), lambda: row == col)
    yield
    amat = [jnp.where(diag, _dot_nt(q[h], k[h]), 0.0) for h in heads]
    for w, ref_all in zip(levels, refs):
        yield
        lw = _log2(w)
        level = _memo(("lower_left", c, w), lambda: _half_block_mask(row, col, lw))
        e = [jnp.exp(-jnp.abs(b[h] - ref_all[:, hsl[h]])) for h in heads]
        amat = [jnp.where(level, _dot_nt(q[h] * e[h], k[h] * e[h]), amat[h]) for h in heads]
    yield
    qs = [_cat_rows([_dot(_rows(q[h] * jnp.exp(b[h]), i, c), get_s(i, h)) for i in range(nseq)])
          for h in heads]
    av = [_dot(amat[h], i_ref[:, hsl[h]]) for h in heads]
    yield
    for h in heads:
        hs = hsl[h]
        y_ref[:, hs] = _head_norm_gate(qs[h] + av[h], onw_ref[:, hs], z_ref[:, hs]).astype(BF16)
    yield
    for h in heads:
        hs = hsl[h]
        vh = i_ref[:, hs]
        bl = blast[:, hs]
        kd_t = (k[h] * jnp.exp(bl - b[h])).T.astype(BF16)
        eb_t = jnp.exp(bl).T
        upd = _seq_outer(kd_t, vh, nseq, c)
        for i in range(nseq):
            put_s(i, h, eb_t[:, i * c:i * c + 1] * get_s(i, h) + upd[i])


class _ColView:
    def __init__(self, parts):
        self.parts = []
        pos = 0
        for ref, lo, hi in parts:
            self.parts.append((pos, ref, lo, hi - lo))
            pos += hi - lo
        self.width = pos

    def __getitem__(self, idx):
        if idx is Ellipsis:
            rows, cols = slice(None), slice(0, self.width)
        else:
            rows, cols = idx
            cols = slice(cols.start or 0, self.width if cols.stop is None else cols.stop)
        out = []
        for pos, ref, lo, n in self.parts:
            a, b = max(cols.start, pos), min(cols.stop, pos + n)
            if a < b:
                out.append(ref[rows, lo + a - pos:lo + b - pos])
        return out[0] if len(out) == 1 else jnp.concatenate(out, axis=1)


def _hgrn_views(ca_ref, cb_ref, zc_ref):
    two = 2 * W_C - ca_ref.shape[1]
    q = _ColView([(ca_ref, 0, W_C)])
    f = _ColView([(ca_ref, W_C, ca_ref.shape[1]), (cb_ref, 0, two)])
    i = _ColView([(cb_ref, two, two + W_C)])
    z = _ColView([(zc_ref, 0, W_C)])
    return q, f, i, z


def _hgrn_prompt_kernel(ca_ref, cb_ref, zc_ref, lbp_ref, onw_ref, y_ref, s_out, s_scr,
                        *, layer):
    _TRACE_MEMO.clear()
    j = pl.program_id(0)
    bsz = s_scr.shape[0]

    @pl.when(j == 0)
    def _():
        s_scr[...] = jnp.zeros_like(s_scr)

    def tile(u):
        ca, cb, zc, y = _tile_views((ca_ref, cb_ref, zc_ref, y_ref), u)

        def put_s(i, h, val):
            s_scr[u, h] = val

        return _hgrn_tile(*_hgrn_views(ca, cb, zc), lbp_ref, onw_ref, y,
                          lambda i, h: s_scr[u, h], put_s, 1, TILE_R, layer)

    _interleave(tile(u) for u in range(bsz))

    @pl.when(j == pl.num_programs(0) - 1)
    def _():
        s_out[...] = s_scr[...]


def _hgrn_sample_kernel(ca_ref, cb_ref, zc_ref, lbp_ref, onw_ref, s_in, y_ref, s_out,
                        *, c, layer):
    _TRACE_MEMO.clear()

    def put_s(i, h, val):
        s_out[i, h] = val

    _interleave([_hgrn_tile(*_hgrn_views(ca_ref, cb_ref, zc_ref), lbp_ref, onw_ref, y_ref,
                            lambda i, h: s_in[i, h], put_s, TILE_R // c, c, layer)])


HGRN_BLK = W_A
assert OFF_C % HGRN_BLK == 0 and 3 * W_C == 2 * HGRN_BLK and OFF_ZC % HGRN_BLK == 0


def _hgrn_prompt(proj, lb_par, onw3, layer, bsz, n_chunk):
    t = proj.shape[0]
    tm = bsz * TILE_R
    depth = lb_par.shape[0]
    rows = lambda j: j
    in_specs = [_col_spec(HGRN_BLK, off, rows, tm) for off in (OFF_C, OFF_C + HGRN_BLK, OFF_ZC)]
    in_specs += [pl.BlockSpec((depth, W_C), lambda j: (0, 0)),
                 _param_spec(W_C, W_A + W_B, layer)]
    return pl.pallas_call(
        functools.partial(_hgrn_prompt_kernel, layer=layer),
        grid=(n_chunk,),
        in_specs=in_specs,
        out_specs=[
            pl.BlockSpec((tm, W_C), lambda j: (j, 0)),
            _whole(bsz, H_C, HEAD_DIM, HEAD_DIM),
        ],
        out_shape=[
            jax.ShapeDtypeStruct((t, W_C), BF16),
            jax.ShapeDtypeStruct((bsz, H_C, HEAD_DIM, HEAD_DIM), F32),
        ],
        scratch_shapes=[pltpu.VMEM((bsz, H_C, HEAD_DIM, HEAD_DIM), F32)],
        compiler_params=pltpu.CompilerParams(
            dimension_semantics=("arbitrary",), vmem_limit_bytes=VMEM_LIMIT),
        name="hgrn_prompt",
    )(proj, proj, proj, lb_par, onw3)


def _hgrn_sample(proj, y_prev, s_stack, lb_par, onw3, s_state, layer, dec_b, c, rows):
    nseq = TILE_R // c
    depth = lb_par.shape[0]
    in_specs = [_col_spec(HGRN_BLK, off, rows) for off in (OFF_C, OFF_C + HGRN_BLK, OFF_ZC)]
    in_specs += [
        pl.BlockSpec((depth, W_C), lambda i: (0, 0)),
        _param_spec(W_C, W_A + W_B, layer),
        pl.BlockSpec((None, nseq, H_C, HEAD_DIM, HEAD_DIM), lambda i: (layer, i, 0, 0, 0)),
    ]
    operands = (proj, proj, proj, lb_par, onw3, s_state)
    operands, extra, aliases, strip = _in_place(operands, len(in_specs), {0: y_prev, 1: s_stack})
    return pl.pallas_call(
        strip(functools.partial(_hgrn_sample_kernel, c=c, layer=layer)),
        grid=(dec_b // nseq,),
        in_specs=in_specs + extra,
        out_specs=[
            pl.BlockSpec((TILE_R, W_C), lambda i: (rows(i), 0)),
            pl.BlockSpec((None, nseq, H_C, HEAD_DIM, HEAD_DIM), lambda i: (layer, i, 0, 0, 0)),
        ],
        out_shape=[
            jax.ShapeDtypeStruct(y_prev.shape, BF16),
            jax.ShapeDtypeStruct((depth, dec_b, H_C, HEAD_DIM, HEAD_DIM), F32),
        ],
        input_output_aliases=aliases,
        compiler_params=pltpu.CompilerParams(
            dimension_semantics=("parallel",), vmem_limit_bytes=VMEM_LIMIT),
        name="hgrn_sample",
    )(*operands)


def _gate_weight(w_t):
    ai0 = 4 * W_A
    ba0 = ai0 + 2 * H_A + 3 * W_B
    n_gate = 2 * H_A + 2 * H_B
    pad = jnp.zeros((w_t.shape[0], LANES - n_gate, w_t.shape[2]), w_t.dtype)
    return jnp.concatenate([w_t[:, ai0:ai0 + 2 * H_A], w_t[:, ba0:ba0 + 2 * H_B], pad], axis=1)


def _lane_row(depth, pieces):
    row = jnp.zeros((depth, LANES), F32)
    for off, val in pieces:
        row = lax.dynamic_update_slice(row, val.astype(F32), (0, off))
    return row[:, None, :]


def kernel(x_prompt, x_sample, state_mlstm_C, state_mlstm_n, state_mlstm_m, state_gdn_S,
           state_gdn_conv, state_hgrn_S, meta_tokens, norm_w, w_in, mlstm_gate_b, gdn_A_log,
           gdn_dt_bias, gdn_conv_w, hgrn_lower_bounds, out_norm_w, w_out, final_norm_w):
    bsz, seq_len, d = x_prompt.shape
    dec_b, dec_seq, _ = x_sample.shape
    depth = w_in.shape[0]
    assert d == D_MODEL and seq_len % TILE_R == 0 and TILE_R % dec_seq == 0
    assert w_in.shape[2] == W_SEGMENTS[-1][1] + W_SEGMENTS[-1][2]
    nseq = TILE_R // dec_seq
    assert dec_b % nseq == 0

    chunks = seq_len // TILE_R
    n_chunk = chunks + 1
    tm = bsz * TILE_R
    n_dec = dec_b * dec_seq
    dec_blk0 = n_chunk * bsz

    def dec_rows(i):
        return dec_blk0 + i

    w_t = jnp.swapaxes(w_in.astype(F32), 1, 2)
    w_gate = _gate_weight(w_t)
    w_out_bf = w_out.astype(BF16)
    norm_w3 = norm_w.astype(F32)[:, None, :]
    onw3 = out_norm_w.astype(F32)[:, None, :]
    final_nw = final_norm_w.astype(F32)[None, :]
    gate_bias = _lane_row(depth, [(G_AI, mlstm_gate_b[:, 0]), (G_AF, mlstm_gate_b[:, 1])])
    gdn_par = jnp.concatenate([
        _lane_row(depth, [(G_BA, gdn_dt_bias)]), _lane_row(depth, [(G_BA, gdn_A_log)]),
        jnp.zeros((depth, 6, LANES), F32)], axis=1)
    conv_w = gdn_conv_w.astype(F32)
    lb_par = hgrn_lower_bounds.astype(F32)
    m_rows = jnp.repeat(
        jnp.pad(state_mlstm_m.astype(F32), ((0, 0), (0, 0), (0, LANES - H_A))), dec_seq, axis=1)

    x, h = _prep(x_prompt.astype(F32), x_sample.astype(F32), meta_tokens.astype(F32), norm_w3,
                 bsz, n_chunk)
    t_all = x.shape[0]
    tm_in = max(m for m in range(16, INPROJ_MAX_ROWS + 1, 16) if t_all % m == 0)

    new_p = [[] for _ in range(6)]
    new_s = [[] for _ in range(3)]
    c_s = s_s = h_s = None
    y_prompt = y_sample = None
    for l in range(depth):
        proj = _inproj(h, w_t, w_gate, l, tm_in)

        ya, c_p, n_p, m_p = _mlstm_prompt(proj, gate_bias, onw3, l, bsz, n_chunk)
        ya, c_s, n_s, m_s = _mlstm_sample(proj, ya, c_s, gate_bias, onw3, state_mlstm_C,
                                          state_mlstm_n, m_rows, l, dec_b, dec_seq, dec_rows)
        yb, s_p, cv_p = _gdn_prompt(proj, gdn_par, conv_w, onw3, l, bsz, n_chunk)
        yb, s_s, cv_s = _gdn_sample(proj, yb, s_s, gdn_par, conv_w, onw3, state_gdn_S,
                                    state_gdn_conv, l, dec_b, dec_seq, dec_rows)
        yc, h_p = _hgrn_prompt(proj, lb_par, onw3, l, bsz, n_chunk)
        yc, h_s = _hgrn_sample(proj, yc, h_s, lb_par, onw3, state_hgrn_S, l, dec_b, dec_seq,
                               dec_rows)

        for lst, val in zip(new_p, (c_p, n_p, m_p[:, 0, :H_A], s_p, cv_p, h_p)):
            lst.append(val)
        for lst, val in zip(new_s, (n_s, m_s[::dec_seq, :H_A], cv_s)):
            lst.append(val)

        if l + 1 < depth:
            x, h = _outproj_mid(ya, yb, yc, x, w_out_bf, l, norm_w3, tm)
        else:
            y_prompt = _outproj_final(
                ya, yb, yc, x, w_out_bf, l, final_nw, tm, 1, chunks, (bsz, chunks, TILE_R, d),
                pl.BlockSpec((bsz, None, TILE_R, d), lambda i: (0, i, 0, 0)))
            y_sample = _outproj_final(
                ya, yb, yc, x, w_out_bf, l, final_nw, tm, n_chunk, n_dec // tm, (n_dec, d),
                pl.BlockSpec((tm, d), lambda i: (i, 0)))

    outs_p = [jnp.stack(a, axis=0) for a in new_p]
    n_all, m_all, cv_all = [jnp.stack(a, axis=0) for a in new_s]
    return (y_prompt.reshape(bsz, seq_len, d), y_sample.reshape(dec_b, dec_seq, d),
            *outs_p, c_s, n_all, m_all, s_s, cv_all, h_s)
```

```python
import functools
import math

import jax
import jax.numpy as jnp
from jax import lax
from jax.experimental import pallas as pl
from jax.experimental.pallas import tpu as pltpu

F32 = jnp.float32
BF16 = jnp.bfloat16

D_MODEL = 2048
HEAD_DIM = 128
H_A, H_B, H_C = 6, 6, 4
W_A, W_B, W_C = H_A * HEAD_DIM, H_B * HEAD_DIM, H_C * HEAD_DIM
MIX = W_A + W_B + W_C
N_META = 16
CONV_W = 4
EPS = 1e-6
NEG_BIG = -1e30
EXP_CLIP = 60.0
QK_SCALE = HEAD_DIM ** -0.5

TILE_R = 128
LANES = 128
CONV_PAD = 8

OFF_AQ = 0
OFF_AK, OFF_AV, OFF_AO = W_A, 2 * W_A, 3 * W_A
OFF_BQ = 4 * W_A
OFF_BK, OFF_BV = OFF_BQ + W_B, OFF_BQ + 2 * W_B
OFF_C = OFF_BQ + 3 * W_B
OFF_ZA = OFF_C + 3 * W_C
OFF_ZB = OFF_ZA + W_A
OFF_ZC = OFF_ZB + W_B
OFF_G = OFF_ZC + W_C
N_PACK = OFF_G + 2 * LANES
G_AI, G_AF, G_BA, G_BB = 0, H_A, 2 * H_A, 2 * H_A + H_B
TN = 1024
INPROJ_MAX_ROWS = 1280
W_SEGMENTS = ((OFF_AQ, OFF_BQ, 0), (OFF_BQ, OFF_C, 2 * H_A), (OFF_C, OFF_G, 2 * H_A + 2 * H_B))
W_ROW_CHUNK = 256

VMEM_LIMIT = 56 * 1024 * 1024


def _sigmoid(x):
    return 1.0 / (1.0 + jnp.exp(-x))


def _silu(x):
    return x * _sigmoid(x)


def _log1p(u):
    return jnp.log(1.0 + u)


def _softplus(x):
    return jnp.maximum(x, 0.0) + _log1p(jnp.exp(-jnp.abs(x)))


def _log_sigmoid(x):
    return -_softplus(-x)


def _dot(a, b):
    return jnp.dot(a.astype(BF16), b.astype(BF16), preferred_element_type=F32)


def _dot_nt(a, b):
    return lax.dot_general(a.astype(BF16), b.astype(BF16), (((1,), (1,)), ((), ())),
                           preferred_element_type=F32)


def _dot_exact(sel, x):
    hi = x.astype(BF16)
    r1 = x - hi.astype(F32)
    mid = r1.astype(BF16)
    lo = (r1 - mid.astype(F32)).astype(BF16)
    s = sel.astype(BF16)
    return (jnp.dot(s, hi, preferred_element_type=F32)
            + jnp.dot(s, mid, preferred_element_type=F32)
            + jnp.dot(s, lo, preferred_element_type=F32))


def _log2(n):
    k = int(math.log2(n))
    assert (1 << k) == n
    return k


_TRACE_MEMO = {}


def _memo(key, fn):
    if key not in _TRACE_MEMO:
        _TRACE_MEMO[key] = fn()
    return _TRACE_MEMO[key]


def _seq_masks(nseq, c):
    return _memo(("seq", nseq, c), lambda: _seq_masks_uncached(nseq, c))


def _seq_masks_uncached(nseq, c):
    r = nseq * c
    row = lax.broadcasted_iota(jnp.int32, (r, r), 0)
    col = lax.broadcasted_iota(jnp.int32, (r, r), 1)
    if nseq == 1:
        same = row >= 0
    else:
        k = _log2(c)
        same = (row >> k) == (col >> k)
    return row, col, same


def _seq_outer(lhs_t, x, nseq, c):
    if nseq == 1:
        return [jnp.dot(lhs_t, x.astype(BF16), preferred_element_type=F32)]
    seq = lax.broadcasted_iota(jnp.int32, (nseq * c, 1), 0) >> _log2(c)
    wide = jnp.concatenate([jnp.where(seq == i, x, 0.0).astype(BF16) for i in range(nseq)], axis=1)
    out = jnp.dot(lhs_t, wide, preferred_element_type=F32)
    n = x.shape[1]
    return [out[:, i * n:(i + 1) * n] for i in range(nseq)]


def _seq_cummax(x, nseq, c):
    r = nseq * c
    pos = lax.broadcasted_iota(jnp.int32, x.shape, 0)
    if nseq > 1:
        pos = pos & (c - 1)
    k = 1
    while k < c:
        x = jnp.where(pos >= k, jnp.maximum(x, pltpu.roll(x, k, axis=0)), x)
        k *= 2
    assert x.shape[0] == r
    return x


def _head_norm_gate(h, onw, z):
    hn = h * lax.rsqrt(jnp.mean(h * h, axis=1, keepdims=True) + EPS)
    return hn * onw * _silu(z)


def _rows(x, i, c):
    return x[i * c:(i + 1) * c]


def _cat_rows(parts):
    return parts[0] if len(parts) == 1 else jnp.concatenate(parts, axis=0)


def _chunk_shift(jj, r0):
    p = jj * TN + r0
    for p0, p1, shift in W_SEGMENTS:
        if p0 <= p and p + W_ROW_CHUNK <= p1:
            return shift
    return None


def _fill_weight_tile(w0_ref, w1_ref, wg_ref, wb_scr, tail_scr, jj):
    for r0 in range(0, TN, W_ROW_CHUNK):
        shift = _chunk_shift(jj, r0)
        dst = slice(r0, r0 + W_ROW_CHUNK)
        if shift is None:
            g0 = OFF_G - jj * TN
            assert r0 <= g0 and g0 + LANES <= r0 + W_ROW_CHUNK
            wb_scr[dst, :] = jnp.zeros((W_ROW_CHUNK, D_MODEL), BF16)
            wb_scr[g0:g0 + LANES, :] = wg_ref[...].astype(BF16)
        elif r0 + shift + W_ROW_CHUNK <= TN:
            wb_scr[dst, :] = w0_ref[r0 + shift:r0 + shift + W_ROW_CHUNK, :].astype(BF16)
        else:
            tail_scr[0:W_ROW_CHUNK, :] = w0_ref[r0:TN, :]
            tail_scr[W_ROW_CHUNK:, :] = w1_ref[...]
            wb_scr[dst, :] = tail_scr[shift:shift + W_ROW_CHUNK, :].astype(BF16)


def _inproj_kernel(h_ref, w0_ref, w1_ref, wg_ref, o_ref, wb_scr, tail_scr):
    j = pl.program_id(0)

    @pl.when(pl.program_id(1) == 0)
    def _():
        for jj in range(N_PACK // TN):
            @pl.when(j == jj)
            def _(jj=jj):
                _fill_weight_tile(w0_ref, w1_ref, wg_ref, wb_scr, tail_scr, jj)

    o_ref[...] = lax.dot_general(h_ref[...], wb_scr[...], (((1,), (1,)), ((), ())),
                                 preferred_element_type=F32)


W_TAIL_ROWS = 32


def _inproj(h, w_t, w_gate_t, layer, tm):
    t = h.shape[0]
    n_tiles = N_PACK // TN
    assert n_tiles - 1 <= (w_t.shape[1] - 1) // TN and W_TAIL_ROWS >= W_SEGMENTS[-1][2]
    tail_per_tile = TN // W_TAIL_ROWS
    last_tail = (w_t.shape[1] - 1) // W_TAIL_ROWS
    return pl.pallas_call(
        _inproj_kernel,
        grid=(n_tiles, t // tm),
        in_specs=[
            pl.BlockSpec((tm, D_MODEL), lambda j, i: (i, 0)),
            pl.BlockSpec((None, TN, D_MODEL), lambda j, i: (layer, j, 0)),
            pl.BlockSpec((None, W_TAIL_ROWS, D_MODEL),
                         lambda j, i: (layer, jnp.minimum((j + 1) * tail_per_tile, last_tail), 0)),
            pl.BlockSpec((None, LANES, D_MODEL), lambda j, i: (layer, 0, 0)),
        ],
        out_specs=pl.BlockSpec((tm, TN), lambda j, i: (i, j)),
        out_shape=jax.ShapeDtypeStruct((t, N_PACK), F32),
        scratch_shapes=[pltpu.VMEM((TN, D_MODEL), BF16),
                        pltpu.VMEM((W_ROW_CHUNK + W_TAIL_ROWS, D_MODEL), F32)],
        compiler_params=pltpu.CompilerParams(
            dimension_semantics=("arbitrary", "arbitrary"), vmem_limit_bytes=VMEM_LIMIT),
        name="inproj",
    )(h, w_t, w_t, w_gate_t)


def _rmsnorm(x, w):
    ms = jnp.mean(x * x, axis=1, keepdims=True)
    return x * lax.rsqrt(ms + EPS) * w


def _prep_kernel(xp_ref, xs_ref, meta_ref, nw_ref, x_ref, h_ref, *, n_chunk, bsz):
    i = pl.program_id(0)
    d = x_ref.shape[1]

    def emit(x):
        x_ref[...] = x
        h_ref[...] = _rmsnorm(x, nw_ref[...]).astype(BF16)

    @pl.when(i == 0)
    def _():
        slot = jnp.concatenate([jnp.zeros((TILE_R - N_META, d), F32), meta_ref[...]], axis=0)
        emit(jnp.concatenate([slot] * bsz, axis=0))

    @pl.when((i > 0) & (i < n_chunk))
    def _():
        emit(xp_ref[...].reshape(bsz * TILE_R, d))

    @pl.when(i >= n_chunk)
    def _():
        emit(xs_ref[...])


def _prep(x_prompt, x_sample, meta, norm_w3, bsz, n_chunk):
    d = x_prompt.shape[-1]
    tm = bsz * TILE_R
    n_dec = x_sample.shape[0] * x_sample.shape[1]
    assert n_dec % tm == 0
    xp = x_prompt.reshape(bsz, n_chunk - 1, TILE_R, d)
    xs = x_sample.reshape(n_dec, d)
    n_blk = n_chunk + n_dec // tm
    t = n_blk * tm
    return pl.pallas_call(
        functools.partial(_prep_kernel, n_chunk=n_chunk, bsz=bsz),
        grid=(n_blk,),
        in_specs=[
            pl.BlockSpec((bsz, None, TILE_R, d), lambda i: (0, jnp.clip(i - 1, 0, n_chunk - 2), 0, 0)),
            pl.BlockSpec((tm, d), lambda i: (jnp.clip(i - n_chunk, 0, n_dec // tm - 1), 0)),
            pl.BlockSpec((N_META, d), lambda i: (0, 0)),
            pl.BlockSpec((None, 1, d), lambda i: (0, 0, 0)),
        ],
        out_specs=[pl.BlockSpec((tm, d), lambda i: (i, 0)),
                   pl.BlockSpec((tm, d), lambda i: (i, 0))],
        out_shape=[jax.ShapeDtypeStruct((t, d), F32), jax.ShapeDtypeStruct((t, d), BF16)],
        compiler_params=pltpu.CompilerParams(
            dimension_semantics=("arbitrary",), vmem_limit_bytes=VMEM_LIMIT),
        name="prep",
    )(xp, xs, meta, norm_w3)


def _outproj_residual(ya_ref, yb_ref, yc_ref, x_ref, w_ref):
    acc = jnp.dot(ya_ref[...], w_ref[0:W_A, :], preferred_element_type=F32)
    acc = acc + jnp.dot(yb_ref[...], w_ref[W_A:W_A + W_B, :], preferred_element_type=F32)
    acc = acc + jnp.dot(yc_ref[...], w_ref[W_A + W_B:MIX, :], preferred_element_type=F32)
    return x_ref[...] + acc


def _outproj_mid_kernel(ya_ref, yb_ref, yc_ref, x_ref, w_ref, nw_ref, x_out, h_out):
    xn = _outproj_residual(ya_ref, yb_ref, yc_ref, x_ref, w_ref)
    x_out[...] = xn
    h_out[...] = _rmsnorm(xn, nw_ref[...]).astype(BF16)


def _outproj_final_kernel(ya_ref, yb_ref, yc_ref, x_ref, w_ref, nw_ref, y_out):
    xn = _outproj_residual(ya_ref, yb_ref, yc_ref, x_ref, w_ref)
    y_out[...] = _rmsnorm(xn, nw_ref[...]).reshape(y_out.shape)


def _outproj_specs(tm, row_blk0, layer, nw_spec):
    rows = lambda i: (row_blk0 + i, 0)
    return [
        pl.BlockSpec((tm, W_A), rows),
        pl.BlockSpec((tm, W_B), rows),
        pl.BlockSpec((tm, W_C), rows),
        pl.BlockSpec((tm, D_MODEL), rows),
        pl.BlockSpec((None, MIX, D_MODEL), lambda i: (layer, 0, 0)),
        nw_spec,
    ]


def _outproj_mid(ya, yb, yc, x, w_out, layer, norm_w3, tm):
    t = x.shape[0]
    blk = pl.BlockSpec((tm, D_MODEL), lambda i: (i, 0))
    nw_spec = pl.BlockSpec((None, 1, D_MODEL), lambda i: (layer + 1, 0, 0))
    return pl.pallas_call(
        _outproj_mid_kernel,
        grid=(t // tm,),
        in_specs=_outproj_specs(tm, 0, layer, nw_spec),
        out_specs=[blk, blk],
        out_shape=[jax.ShapeDtypeStruct((t, D_MODEL), F32), jax.ShapeDtypeStruct((t, D_MODEL), BF16)],
        compiler_params=pltpu.CompilerParams(
            dimension_semantics=("parallel",), vmem_limit_bytes=VMEM_LIMIT),
        name="outproj",
    )(ya, yb, yc, x, w_out, norm_w3)


def _outproj_final(ya, yb, yc, x, w_out, layer, final_nw, tm, row_blk0, n_blk, out_shape, out_spec):
    nw_spec = pl.BlockSpec((1, D_MODEL), lambda i: (0, 0))
    return pl.pallas_call(
        _outproj_final_kernel,
        grid=(n_blk,),
        in_specs=_outproj_specs(tm, row_blk0, layer, nw_spec),
        out_specs=out_spec,
        out_shape=jax.ShapeDtypeStruct(out_shape, F32),
        compiler_params=pltpu.CompilerParams(
            dimension_semantics=("parallel",), vmem_limit_bytes=VMEM_LIMIT),
        name="outproj_final",
    )(ya, yb, yc, x, w_out, final_nw)


class _RowView:
    def __init__(self, ref, r0, n):
        self.ref, self.r0, self.n = ref, r0, n
        self.shape = (n, ref.shape[1])

    def _idx(self, idx):
        if idx is Ellipsis:
            return slice(self.r0, self.r0 + self.n), slice(None)
        rows, cols = idx
        start = self.r0 + (rows.start or 0)
        stop = self.r0 + (self.n if rows.stop is None else rows.stop)
        return slice(start, stop), cols

    def __getitem__(self, idx):
        return self.ref[self._idx(idx)]

    def __setitem__(self, idx, val):
        self.ref[self._idx(idx)] = val


def _tile_views(refs, u):
    return [_RowView(ref, u * TILE_R, TILE_R) for ref in refs]


def _interleave(tiles):
    tiles = list(tiles)
    while tiles:
        alive = []
        for t in tiles:
            try:
                next(t)
                alive.append(t)
            except StopIteration:
                pass
        tiles = alive


def _prompt_step(j, bsz, tile, share):
    @pl.when(j == 0)
    def _():
        _TRACE_MEMO.clear()
        _interleave([tile(0)])
        for u in range(1, bsz):
            share(u)

    @pl.when(j > 0)
    def _():
        _TRACE_MEMO.clear()
        _interleave(tile(u) for u in range(bsz))

    _TRACE_MEMO.clear()


def _mlstm_tile(q_ref, k_ref, v_ref, o_ref, z_ref, g_ref, gb_ref, onw_ref, y_ref,
                m_exp, valid, get_c, get_n, put_c, put_n, put_m, nseq, c):
    r = nseq * c
    row, col, same = _seq_masks(nseq, c)
    incl = _memo(("incl", nseq, c), lambda: same & (col <= row))
    g = g_ref[...] + gb_ref[...]
    li = g
    lf = _log_sigmoid(g)
    if valid is not None:
        li = jnp.where(valid, li, NEG_BIG)
        lf = jnp.where(valid, lf, 0.0)
    fcum = _dot_exact(incl.astype(F32), lf)
    if nseq == 1:
        flast = jnp.broadcast_to(fcum[r - 1:r, :], (r, LANES))
    else:
        flast = _dot_exact(same.astype(F32), lf)
    yield
    heads = range(H_A)
    hsl = [slice(h * HEAD_DIM, (h + 1) * HEAD_DIM) for h in heads]

    li_a = pltpu.roll(li, G_AF - G_AI, axis=1)
    m_a = pltpu.roll(m_exp, G_AF, axis=1)
    x = li_a - fcum
    mx = jnp.maximum(m_a, _seq_cummax(x, nseq, c))
    mt_all = fcum + mx
    if nseq == 1:
        m_new = jnp.broadcast_to(mt_all[r - 1:r, :], (r, LANES))
    else:
        m_new = jnp.broadcast_to(mt_all.reshape(nseq, c, LANES)[:, c - 1:c, :],
                                 (nseq, c, LANES)).reshape(r, LANES)
    a_all = jnp.exp(m_a - mx)
    w_all = jnp.exp(flast - fcum + li_a - m_new)
    as_all = jnp.exp(flast + m_a - m_new)
    low_all = jnp.exp(jnp.minimum(-mt_all, EXP_CLIP))
    put_m(pltpu.roll(m_new, LANES - G_AF, axis=1))
    x_t = x.T
    col_of = lambda arr, h: arr[:, G_AF + h:G_AF + h + 1]
    a = [col_of(a_all, h) for h in heads]
    w = [col_of(w_all, h) for h in heads]
    a_s = [col_of(as_all, h) for h in heads]
    low = [col_of(low_all, h) for h in heads]
    pexp = [jnp.exp(jnp.where(incl, x_t[G_AF + h:G_AF + h + 1, :] - col_of(mx, h), NEG_BIG))
            for h in heads]
    yield

    s = [_dot_nt(q_ref[:, hs], k_ref[:, hs] * QK_SCALE) for hs in hsl]
    yield
    p = [pexp[h] * s[h] for h in heads]
    ones = jnp.ones((r, HEAD_DIM), BF16)
    pvs = [jnp.dot(p[h].astype(BF16),
                   jnp.concatenate([v_ref[:, hsl[h]].astype(BF16), ones], axis=1),
                   preferred_element_type=F32) for h in heads]
    pv = [x[:, :HEAD_DIM] for x in pvs]
    psum = [x[:, HEAD_DIM:] for x in pvs]
    yield
    qc = [_cat_rows([_dot(_rows(q_ref[:, hsl[h]], i, c), get_c(i, h)) for i in range(nseq)])
          for h in heads]
    yield
    if nseq == 1:
        qn = [_dot_nt(q_ref[:, hsl[h]], jnp.broadcast_to(get_n(0, h), (HEAD_DIM, HEAD_DIM)))
              for h in heads]
    else:
        qn = [_cat_rows([jnp.sum(_rows(q_ref[:, hsl[h]], i, c) * get_n(i, h), axis=1,
                                 keepdims=True) for i in range(nseq)]) for h in heads]
    yield
    hh = [(a[h] * qc[h] + pv[h]) / jnp.maximum(jnp.abs(a[h] * qn[h] + psum[h]), low[h])
          * _sigmoid(o_ref[:, hsl[h]]) for h in heads]
    ms = [_dot(hh[h] * hh[h], ones) * (1.0 / HEAD_DIM) for h in heads]
    yield
    for h in heads:
        hs = hsl[h]
        y_ref[:, hs] = (hh[h] * lax.rsqrt(ms[h] + EPS) * onw_ref[:, hs]
                        * _silu(z_ref[:, hs])).astype(BF16)
    yield

    for h in heads:
        hs = hsl[h]
        kh = k_ref[:, hs] * QK_SCALE
        wk = w[h] * kh
        upd = _seq_outer(kh.T.astype(BF16), w[h] * v_ref[:, hs], nseq, c)
        for i in range(nseq):
            a_i = a_s[h][i * c:i * c + 1, :]
            put_c(i, h, a_i * get_c(i, h) + upd[i])
            put_n(i, h, a_i * get_n(i, h) + jnp.sum(_rows(wk, i, c), axis=0, keepdims=True))


def _mlstm_prompt_kernel(q_ref, k_ref, v_ref, o_ref, z_ref, g_ref, gb_ref, onw_ref,
                         y_ref, c_out, n_out, m_out, c_scr, n_scr, m_scr):
    _TRACE_MEMO.clear()
    j = pl.program_id(0)
    bsz = c_scr.shape[0]

    @pl.when(j == 0)
    def _():
        c_scr[...] = jnp.zeros_like(c_scr)
        n_scr[...] = jnp.zeros_like(n_scr)
        m_scr[...] = jnp.zeros_like(m_scr)

    tok = lax.broadcasted_iota(jnp.int32, (TILE_R, LANES), 0) + j * TILE_R
    valid = tok >= TILE_R - N_META

    def tile(u):
        q, k, v, o, z, g, y = _tile_views((q_ref, k_ref, v_ref, o_ref, z_ref, g_ref, y_ref), u)
        m_exp = jnp.broadcast_to(m_scr[u, 0:1, :], (TILE_R, LANES))

        def put_c(i, h, val):
            c_scr[u, h] = val

        def put_n(i, h, val):
            n_scr[u, h:h + 1, :] = val

        def put_m(val):
            m_scr[u] = val[0:8, :]

        return _mlstm_tile(q, k, v, o, z, g, gb_ref, onw_ref, y, m_exp, valid,
                           lambda i, h: c_scr[u, h], lambda i, h: n_scr[u, h:h + 1, :],
                           put_c, put_n, put_m, 1, TILE_R)

    def share(u):
        y_ref[u * TILE_R:(u + 1) * TILE_R, :] = y_ref[0:TILE_R, :]
        c_scr[u] = c_scr[0]
        n_scr[u] = n_scr[0]
        m_scr[u] = m_scr[0]

    _prompt_step(j, bsz, tile, share)

    @pl.when(j == pl.num_programs(0) - 1)
    def _():
        c_out[...] = c_scr[...]
        n_out[...] = n_scr[:, 0:H_A, :]
        m_out[...] = m_scr[:, 0:1, :]


def _mlstm_sample_kernel(q_ref, k_ref, v_ref, o_ref, z_ref, g_ref, gb_ref, onw_ref,
                         c_in, n_in, m_in, y_ref, c_out, n_out, m_out, *, c):
    _TRACE_MEMO.clear()
    nseq = TILE_R // c

    def put_c(i, h, val):
        c_out[i, h] = val

    def put_n(i, h, val):
        n_out[i, h:h + 1, :] = val

    def put_m(val):
        m_out[...] = val

    _interleave([_mlstm_tile(
        q_ref, k_ref, v_ref, o_ref, z_ref, g_ref, gb_ref, onw_ref, y_ref, m_in[...], None,
        lambda i, h: c_in[i, h], lambda i, h: n_in[i, h:h + 1, :], put_c, put_n, put_m, nseq, c)])


def _col_spec(width, off, rows, n_rows=TILE_R):
    assert off % width == 0
    blk = off // width
    return pl.BlockSpec((n_rows, width), lambda *ids: (rows(*ids), blk))


def _param_spec(width, off, layer):
    assert off % width == 0
    blk = off // width
    return pl.BlockSpec((None, 1, width), lambda *ids: (layer, 0, blk))


def _whole(*shape):
    return pl.BlockSpec(shape, lambda j: (0,) * len(shape))


def _mlstm_prompt(proj, gate_bias, onw3, layer, bsz, n_chunk):
    t = proj.shape[0]
    tm = bsz * TILE_R
    rows = lambda j: j
    in_specs = [_col_spec(W_A, off, rows, tm) for off in (OFF_AQ, OFF_AK, OFF_AV, OFF_AO, OFF_ZA)]
    in_specs += [_col_spec(LANES, OFF_G, rows, tm), _param_spec(LANES, 0, layer),
                 _param_spec(W_A, 0, layer)]
    return pl.pallas_call(
        _mlstm_prompt_kernel,
        grid=(n_chunk,),
        in_specs=in_specs,
        out_specs=[
            pl.BlockSpec((tm, W_A), lambda j: (j, 0)),
            _whole(bsz, H_A, HEAD_DIM, HEAD_DIM),
            _whole(bsz, H_A, HEAD_DIM),
            _whole(bsz, 1, LANES),
        ],
        out_shape=[
            jax.ShapeDtypeStruct((t, W_A), BF16),
            jax.ShapeDtypeStruct((bsz, H_A, HEAD_DIM, HEAD_DIM), F32),
            jax.ShapeDtypeStruct((bsz, H_A, HEAD_DIM), F32),
            jax.ShapeDtypeStruct((bsz, 1, LANES), F32),
        ],
        scratch_shapes=[pltpu.VMEM((bsz, H_A, HEAD_DIM, HEAD_DIM), F32),
                        pltpu.VMEM((bsz, 8, HEAD_DIM), F32), pltpu.VMEM((bsz, 8, LANES), F32)],
        compiler_params=pltpu.CompilerParams(
            dimension_semantics=("arbitrary",), vmem_limit_bytes=VMEM_LIMIT),
        name="mlstm_prompt",
    )(proj, proj, proj, proj, proj, proj, gate_bias, onw3)


def _in_place(operands, n_blocked, targets):
    specs, aliases = [], {}
    operands = list(operands)
    for out_idx, arr in targets.items():
        if arr is not None:
            aliases[len(operands)] = out_idx
            operands.append(arr)
            specs.append(pl.BlockSpec(memory_space=pl.ANY))
    n_extra = len(specs)

    def strip(kernel):
        def body(*refs):
            kernel(*refs[:n_blocked], *refs[n_blocked + n_extra:])
        return body

    return operands, specs, aliases, strip


def _mlstm_sample(proj, y_prev, c_stack, gate_bias, onw3, c_state, n_state, m_rows, layer,
                  dec_b, c, rows):
    nseq = TILE_R // c
    depth = c_state.shape[0]
    in_specs = [_col_spec(W_A, off, rows) for off in (OFF_AQ, OFF_AK, OFF_AV, OFF_AO, OFF_ZA)]
    in_specs += [_col_spec(LANES, OFF_G, rows), _param_spec(LANES, 0, layer),
                 _param_spec(W_A, 0, layer)]
    in_specs += [
        pl.BlockSpec((None, nseq, H_A, HEAD_DIM, HEAD_DIM), lambda i: (layer, i, 0, 0, 0)),
        pl.BlockSpec((None, nseq, H_A, HEAD_DIM), lambda i: (layer, i, 0, 0)),
        pl.BlockSpec((None, TILE_R, LANES), lambda i: (layer, i, 0)),
    ]
    operands = (proj, proj, proj, proj, proj, proj, gate_bias, onw3, c_state, n_state, m_rows)
    operands, extra, aliases, strip = _in_place(operands, len(in_specs), {0: y_prev, 1: c_stack})
    return pl.pallas_call(
        strip(functools.partial(_mlstm_sample_kernel, c=c)),
        grid=(dec_b // nseq,),
        in_specs=in_specs + extra,
        out_specs=[
            pl.BlockSpec((TILE_R, W_A), lambda i: (rows(i), 0)),
            pl.BlockSpec((None, nseq, H_A, HEAD_DIM, HEAD_DIM), lambda i: (layer, i, 0, 0, 0)),
            pl.BlockSpec((nseq, H_A, HEAD_DIM), lambda i: (i, 0, 0)),
            pl.BlockSpec((TILE_R, LANES), lambda i: (i, 0)),
        ],
        out_shape=[
            jax.ShapeDtypeStruct(y_prev.shape, BF16),
            jax.ShapeDtypeStruct((depth, dec_b, H_A, HEAD_DIM, HEAD_DIM), F32),
            jax.ShapeDtypeStruct((dec_b, H_A, HEAD_DIM), F32),
            jax.ShapeDtypeStruct((dec_b * c, LANES), F32),
        ],
        input_output_aliases=aliases,
        compiler_params=pltpu.CompilerParams(
            dimension_semantics=("parallel",), vmem_limit_bytes=VMEM_LIMIT),
        name="mlstm_sample",
    )(*operands)


def _l2norm(x):
    return x * lax.rsqrt(jnp.sum(x * x, axis=1, keepdims=True) + EPS)


INV_LEAF = 16


def _half_block_mask(row, col, k):
    return (((row >> (k + 1)) == (col >> (k + 1)))
            & (((row >> k) & 1) == 1) & (((col >> k) & 1) == 0))


def _unit_lower_inverse(a_list, row, col, c):
    leaf = min(INV_LEAF, c)
    kl = _log2(leaf)
    leaf_mask = _memo(("leaf", c), lambda: (row >> kl) == (col >> kl))
    eye = _memo(("eye", c), lambda: jnp.where(row == col, 1.0, 0.0))
    p = [jnp.where(leaf_mask, -a, 0.0) for a in a_list]
    t = [eye + n for n in p]
    for _ in range(kl - 1):
        p = [_dot(x, x) for x in p]
        yield
        t = [ti + _dot(pi, ti) for pi, ti in zip(p, t)]
        yield
    w = leaf
    while w < c:
        kw = _log2(w)
        lower_left = _memo(("lower_left", c, w), lambda: _half_block_mask(row, col, kw))
        n_pair = row.shape[0] // (2 * w)
        second = lambda x: _cat_rows([x[(2 * m + 1) * w:(2 * m + 2) * w] for m in range(n_pair)])
        t_sec = [second(ti) for ti in t]
        left = [_dot(ts, jnp.where(lower_left, a, 0.0)) for ts, a in zip(t_sec, a_list)]
        yield
        new_sec = [ts - _dot(li, ti) for ts, li, ti in zip(t_sec, left, t)]
        t = [_cat_rows([part for m in range(n_pair)
                        for part in (ti[2 * m * w:(2 * m + 1) * w], ns[m * w:(m + 1) * w])])
             for ti, ns in zip(t, new_sec)]
        yield
        w *= 2
    return t


def _gdn_tile(u_cols, prev_cols, cw_ref, z_ref, g_ref, gp_ref, onw_ref, y_ref, get_s, put_s,
              nseq, c):
    r = nseq * c
    row, col, same = _seq_masks(nseq, c)
    incl = _memo(("incl", nseq, c), lambda: same & (col <= row))
    strict = _memo(("strict", nseq, c), lambda: same & (col < row))
    gp = gp_ref[...]
    graw = g_ref[...]
    gdec = -jnp.exp(gp[1:2, :]) * _softplus(graw + gp[0:1, :])
    beta = _sigmoid(graw)
    gcum = _dot_exact(incl.astype(F32), gdec)
    if nseq == 1:
        glast = jnp.broadcast_to(gcum[r - 1:r, :], (r, LANES))
    else:
        glast = _dot_exact(same.astype(F32), gdec)
    yield
    gcum_t = gcum.T

    heads = range(H_B)
    hsl = [slice(h * HEAD_DIM, (h + 1) * HEAD_DIM) for h in heads]

    pos = lax.broadcasted_iota(jnp.int32, (CONV_PAD if nseq == 1 else r, HEAD_DIM), 0)
    if nseq > 1:
        pos = pos & (c - 1)

    def conv(lo):
        cols = slice(lo, lo + HEAD_DIM)
        x = u_cols(cols)
        prev = prev_cols(cols)
        acc = x * cw_ref[CONV_W - 1:CONV_W, cols]
        for back in range(1, CONV_W):
            shifted = pltpu.roll(x, back, axis=0)
            if nseq == 1:
                first = jnp.where(pos < back, pltpu.roll(prev, back, axis=0), shifted[0:CONV_PAD])
                shifted = jnp.concatenate([first, shifted[CONV_PAD:]], axis=0)
            else:
                shifted = jnp.where(pos < back, pltpu.roll(prev, (back - CONV_PAD) % r, axis=0),
                                    shifted)
            acc = acc + shifted * cw_ref[CONV_W - 1 - back:CONV_W - back, cols]
        return _silu(acc)

    q = [_l2norm(conv(hs.start)) * QK_SCALE for hs in hsl]
    k = [_l2norm(conv(W_B + hs.start)) for hs in hsl]
    v = [conv(2 * W_B + hs.start) for hs in hsl]
    yield
    gc = [gcum[:, G_BA + h:G_BA + h + 1] for h in heads]
    gl = [glast[:, G_BA + h:G_BA + h + 1] for h in heads]
    bc = [beta[:, G_BB + h:G_BB + h + 1] for h in heads]
    decay = [jnp.exp(jnp.where(incl, gc[h] - gcum_t[G_BA + h:G_BA + h + 1, :], NEG_BIG))
             for h in heads]
    kk = [_dot_nt(k[h], k[h]) for h in heads]
    qk = [_dot_nt(q[h], k[h]) for h in heads]
    yield
    amat = [jnp.where(strict, bc[h] * decay[h] * kk[h], 0.0) for h in heads]
    tinv = yield from _unit_lower_inverse(amat, row, col, c)
    sol = [_dot(tinv[h], jnp.concatenate([bc[h] * v[h], (bc[h] * jnp.exp(gc[h])) * k[h]], axis=1))
           for h in heads]
    yield
    both = [[_dot(jnp.concatenate([_rows(sol[h][:, HEAD_DIM:], i, c), _rows(q[h], i, c)], axis=0),
                  get_s(i, h)) for i in range(nseq)] for h in heads]
    wks = [_cat_rows([both[h][i][:c] for i in range(nseq)]) for h in heads]
    qs = [_cat_rows([both[h][i][c:] for i in range(nseq)]) for h in heads]
    yield
    wmat = [sol[h][:, :HEAD_DIM] - wks[h] for h in heads]
    aw = [_dot(decay[h] * qk[h], wmat[h]) for h in heads]
    yield
    for h in heads:
        hs = hsl[h]
        oh = jnp.exp(gc[h]) * qs[h] + aw[h]
        y_ref[:, hs] = _head_norm_gate(oh, onw_ref[:, hs], z_ref[:, hs]).astype(BF16)
    yield
    for h in heads:
        kd_t = (k[h] * jnp.exp(gl[h] - gc[h])).T.astype(BF16)
        e_last = jnp.exp(gl[h])
        upd = _seq_outer(kd_t, wmat[h], nseq, c)
        for i in range(nseq):
            put_s(i, h, e_last[i * c:i * c + 1, :] * get_s(i, h) + upd[i])


def _gdn_prompt_kernel(q_ref, k_ref, v_ref, z_ref, g_ref, gp_ref, cw_ref, onw_ref,
                       y_ref, s_out, conv_out, s_scr, hist_scr):
    _TRACE_MEMO.clear()
    j = pl.program_id(0)
    bsz = s_scr.shape[0]
    pieces = ((q_ref, 0), (k_ref, W_B), (v_ref, 2 * W_B))

    @pl.when(j == 0)
    def _():
        s_scr[...] = jnp.zeros_like(s_scr)
        hist_scr[...] = jnp.zeros_like(hist_scr)

    def tile(u):
        z, g, y = _tile_views((z_ref, g_ref, y_ref), u)

        def u_cols(cols):
            ref, off = pieces[cols.start // W_B]
            return ref[u * TILE_R:(u + 1) * TILE_R, cols.start - off:cols.stop - off]

        def prev_cols(cols):
            return hist_scr[u, :, cols]

        def put_s(i, h, val):
            s_scr[u, h] = val

        return _gdn_tile(u_cols, prev_cols, cw_ref, z, g, gp_ref, onw_ref, y,
                         lambda i, h: s_scr[u, h], put_s, 1, TILE_R)

    def share(u):
        y_ref[u * TILE_R:(u + 1) * TILE_R, :] = y_ref[0:TILE_R, :]
        s_scr[u] = s_scr[0]

    _prompt_step(j, bsz, tile, share)

    for u in range(bsz):
        for ref, off in pieces:
            hist_scr[u, :, off:off + W_B] = ref[(u + 1) * TILE_R - CONV_PAD:(u + 1) * TILE_R, :]

    @pl.when(j == pl.num_programs(0) - 1)
    def _():
        s_out[...] = s_scr[...]
        conv_out[...] = hist_scr[:, CONV_PAD - (CONV_W - 1):, :]


def _gdn_sample_kernel(q_ref, k_ref, v_ref, z_ref, g_ref, gp_ref, cw_ref, onw_ref,
                       s_in, conv_in, y_ref, s_out, conv_out, hist_scr, *, c):
    _TRACE_MEMO.clear()
    nseq = TILE_R // c
    hist = CONV_W - 1
    assert c == CONV_PAD
    pieces = ((q_ref, 0), (k_ref, W_B), (v_ref, 2 * W_B))
    hist_scr[:, 0:CONV_PAD - hist, :] = jnp.zeros((nseq, CONV_PAD - hist, 3 * W_B), F32)
    hist_scr[:, CONV_PAD - hist:, :] = conv_in[...]

    def u_cols(cols):
        ref, off = pieces[cols.start // W_B]
        return ref[:, cols.start - off:cols.stop - off]

    def prev_cols(cols):
        return hist_scr[:, :, cols].reshape(TILE_R, cols.stop - cols.start)

    def put_s(i, h, val):
        s_out[i, h] = val

    _interleave([_gdn_tile(u_cols, prev_cols, cw_ref, z_ref, g_ref, gp_ref, onw_ref, y_ref,
                           lambda i, h: s_in[i, h], put_s, nseq, c)])
    for ref, off in pieces:
        conv_out[:, :, off:off + W_B] = ref[...].reshape(nseq, c, W_B)[:, c - hist:, :]


def _gdn_prompt(proj, gdn_par, conv_w, onw3, layer, bsz, n_chunk):
    t = proj.shape[0]
    tm = bsz * TILE_R
    rows = lambda j: j
    in_specs = [_col_spec(W_B, off, rows, tm) for off in (OFF_BQ, OFF_BK, OFF_BV, OFF_ZB)]
    in_specs += [
        _col_spec(LANES, OFF_G, rows, tm),
        pl.BlockSpec((None, 8, LANES), lambda j: (layer, 0, 0)),
        pl.BlockSpec((None, CONV_W, 3 * W_B), lambda j: (layer, 0, 0)),
        _param_spec(W_B, W_A, layer),
    ]
    return pl.pallas_call(
        _gdn_prompt_kernel,
        grid=(n_chunk,),
        in_specs=in_specs,
        out_specs=[
            pl.BlockSpec((tm, W_B), lambda j: (j, 0)),
            _whole(bsz, H_B, HEAD_DIM, HEAD_DIM),
            _whole(bsz, CONV_W - 1, 3 * W_B),
        ],
        out_shape=[
            jax.ShapeDtypeStruct((t, W_B), BF16),
            jax.ShapeDtypeStruct((bsz, H_B, HEAD_DIM, HEAD_DIM), F32),
            jax.ShapeDtypeStruct((bsz, CONV_W - 1, 3 * W_B), F32),
        ],
        scratch_shapes=[pltpu.VMEM((bsz, H_B, HEAD_DIM, HEAD_DIM), F32),
                        pltpu.VMEM((bsz, CONV_PAD, 3 * W_B), F32)],
        compiler_params=pltpu.CompilerParams(
            dimension_semantics=("arbitrary",), vmem_limit_bytes=VMEM_LIMIT),
        name="gdn_prompt",
    )(proj, proj, proj, proj, proj, gdn_par, conv_w, onw3)


def _gdn_sample(proj, y_prev, s_stack, gdn_par, conv_w, onw3, s_state, conv_state, layer,
                dec_b, c, rows):
    nseq = TILE_R // c
    depth = s_state.shape[0]
    in_specs = [_col_spec(W_B, off, rows) for off in (OFF_BQ, OFF_BK, OFF_BV, OFF_ZB)]
    in_specs += [
        _col_spec(LANES, OFF_G, rows),
        pl.BlockSpec((None, 8, LANES), lambda i: (layer, 0, 0)),
        pl.BlockSpec((None, CONV_W, 3 * W_B), lambda i: (layer, 0, 0)),
        _param_spec(W_B, W_A, layer),
        pl.BlockSpec((None, nseq, H_B, HEAD_DIM, HEAD_DIM), lambda i: (layer, i, 0, 0, 0)),
        pl.BlockSpec((None, nseq, CONV_W - 1, 3 * W_B), lambda i: (layer, i, 0, 0)),
    ]
    operands = (proj, proj, proj, proj, proj, gdn_par, conv_w, onw3, s_state, conv_state)
    operands, extra, aliases, strip = _in_place(operands, len(in_specs), {0: y_prev, 1: s_stack})
    return pl.pallas_call(
        strip(functools.partial(_gdn_sample_kernel, c=c)),
        grid=(dec_b // nseq,),
        in_specs=in_specs + extra,
        out_specs=[
            pl.BlockSpec((TILE_R, W_B), lambda i: (rows(i), 0)),
            pl.BlockSpec((None, nseq, H_B, HEAD_DIM, HEAD_DIM), lambda i: (layer, i, 0, 0, 0)),
            pl.BlockSpec((nseq, CONV_W - 1, 3 * W_B), lambda i: (i, 0, 0)),
        ],
        out_shape=[
            jax.ShapeDtypeStruct(y_prev.shape, BF16),
            jax.ShapeDtypeStruct((depth, dec_b, H_B, HEAD_DIM, HEAD_DIM), F32),
            jax.ShapeDtypeStruct((dec_b, CONV_W - 1, 3 * W_B), F32),
        ],
        scratch_shapes=[pltpu.VMEM((nseq, CONV_PAD, 3 * W_B), F32)],
        input_output_aliases=aliases,
        compiler_params=pltpu.CompilerParams(
            dimension_semantics=("parallel",), vmem_limit_bytes=VMEM_LIMIT),
        name="gdn_sample",
    )(*operands)


def _hgrn_tile(q_ref, f_ref, i_ref, z_ref, lbp_ref, onw_ref, y_ref, get_s, put_s,
               nseq, c, layer):
    r = nseq * c
    row, col, same = _seq_masks(nseq, c)
    incl = _memo(("incl", nseq, c), lambda: same & (col <= row))
    rowv = lax.broadcasted_iota(jnp.int32, (r, 1), 0)
    lbp = lbp_ref[...]
    e = jnp.exp(lbp - jnp.max(lbp, axis=0, keepdims=True))
    sm = e / jnp.sum(e, axis=0, keepdims=True)
    lb = jnp.zeros((1, W_C), F32)
    for l in range(1, layer + 1):
        lb = lb + sm[l:l + 1, :]

    cf = f_ref[...]
    logf = _log_sigmoid(cf)
    if layer > 0:
        logf = logf + _log1p(lb * jnp.exp(jnp.minimum(-cf, EXP_CLIP)))
    kall = (1.0 - lb) * _sigmoid(-cf)
    bcum = _dot_exact(incl.astype(F32), logf)
    if nseq == 1:
        blast = jnp.broadcast_to(bcum[r - 1:r, :], (r, W_C))
    else:
        blast = _dot_exact(same.astype(F32), logf)

    levels = []
    w = c // 2
    while w >= 1:
        levels.append(w)
        w //= 2
    rowf = lax.broadcasted_iota(jnp.int32, (r, W_C), 0)
    refs = []
    for w in levels:
        if 2 * w >= 8:
            refs.append(_cat_rows([
                jnp.broadcast_to(bcum[g * 2 * w + w - 1:g * 2 * w + w, :], (2 * w, W_C))
                for g in range(r // (2 * w))]))
        else:
            pos = rowf & (2 * w - 1)
            ref = bcum
            for off in range(2 * w):
                if off != w - 1:
                    ref = jnp.where(pos == off, pltpu.roll(bcum, (off - (w - 1)) % r, axis=0), ref)
            refs.append(ref)

    heads = range(H_C)
    hsl = [slice(h * HEAD_DIM, (h + 1) * HEAD_DIM) for h in heads]
    q = [_silu(q_ref[:, hs]) for hs in hsl]
    k = [kall[:, hs] for hs in hsl]
    b = [bcum[:, hs] for hs in hsl]
    diag = _memo(("diag", nseq, c), lambda: row == col)
    yield
    amat = [jnp.where(diag, _dot_nt(q[h], k[h]), 0.0) for h in heads]
    for w, ref_all in zip(levels, refs):
        yield
        lw = _log2(w)
        level = _memo(("lower_left", c, w), lambda: _half_block_mask(row, col, lw))
        e = [jnp.exp(-jnp.abs(b[h] - ref_all[:, hsl[h]])) for h in heads]
        amat = [jnp.where(level, _dot_nt(q[h] * e[h], k[h] * e[h]), amat[h]) for h in heads]
    yield
    qs = [_cat_rows([_dot(_rows(q[h] * jnp.exp(b[h]), i, c), get_s(i, h)) for i in range(nseq)])
          for h in heads]
    av = [_dot(amat[h], i_ref[:, hsl[h]]) for h in heads]
    yield
    for h in heads:
        hs = hsl[h]
        y_ref[:, hs] = _head_norm_gate(qs[h] + av[h], onw_ref[:, hs], z_ref[:, hs]).astype(BF16)
    yield
    for h in heads:
        hs = hsl[h]
        vh = i_ref[:, hs]
        bl = blast[:, hs]
        kd_t = (k[h] * jnp.exp(bl - b[h])).T.astype(BF16)
        eb_t = jnp.exp(bl).T
        upd = _seq_outer(kd_t, vh, nseq, c)
        for i in range(nseq):
            put_s(i, h, eb_t[:, i * c:i * c + 1] * get_s(i, h) + upd[i])


class _ColView:
    def __init__(self, parts):
        self.parts = []
        pos = 0
        for ref, lo, hi in parts:
            self.parts.append((pos, ref, lo, hi - lo))
            pos += hi - lo
        self.width = pos

    def __getitem__(self, idx):
        if idx is Ellipsis:
            rows, cols = slice(None), slice(0, self.width)
        else:
            rows, cols = idx
            cols = slice(cols.start or 0, self.width if cols.stop is None else cols.stop)
        out = []
        for pos, ref, lo, n in self.parts:
            a, b = max(cols.start, pos), min(cols.stop, pos + n)
            if a < b:
                out.append(ref[rows, lo + a - pos:lo + b - pos])
        return out[0] if len(out) == 1 else jnp.concatenate(out, axis=1)


def _hgrn_views(ca_ref, cb_ref, zc_ref):
    two = 2 * W_C - ca_ref.shape[1]
    q = _ColView([(ca_ref, 0, W_C)])
    f = _ColView([(ca_ref, W_C, ca_ref.shape[1]), (cb_ref, 0, two)])
    i = _ColView([(cb_ref, two, two + W_C)])
    z = _ColView([(zc_ref, 0, W_C)])
    return q, f, i, z


def _hgrn_prompt_kernel(ca_ref, cb_ref, zc_ref, lbp_ref, onw_ref, y_ref, s_out, s_scr,
                        *, layer):
    _TRACE_MEMO.clear()
    j = pl.program_id(0)
    bsz = s_scr.shape[0]

    @pl.when(j == 0)
    def _():
        s_scr[...] = jnp.zeros_like(s_scr)

    def tile(u):
        ca, cb, zc, y = _tile_views((ca_ref, cb_ref, zc_ref, y_ref), u)

        def put_s(i, h, val):
            s_scr[u, h] = val

        return _hgrn_tile(*_hgrn_views(ca, cb, zc), lbp_ref, onw_ref, y,
                          lambda i, h: s_scr[u, h], put_s, 1, TILE_R, layer)

    def share(u):
        y_ref[u * TILE_R:(u + 1) * TILE_R, :] = y_ref[0:TILE_R, :]
        s_scr[u] = s_scr[0]

    _prompt_step(j, bsz, tile, share)

    @pl.when(j == pl.num_programs(0) - 1)
    def _():
        s_out[...] = s_scr[...]


def _hgrn_sample_kernel(ca_ref, cb_ref, zc_ref, lbp_ref, onw_ref, s_in, y_ref, s_out,
                        *, c, layer):
    _TRACE_MEMO.clear()

    def put_s(i, h, val):
        s_out[i, h] = val

    _interleave([_hgrn_tile(*_hgrn_views(ca_ref, cb_ref, zc_ref), lbp_ref, onw_ref, y_ref,
                            lambda i, h: s_in[i, h], put_s, TILE_R // c, c, layer)])


HGRN_BLK = W_A
assert OFF_C % HGRN_BLK == 0 and 3 * W_C == 2 * HGRN_BLK and OFF_ZC % HGRN_BLK == 0


def _hgrn_prompt(proj, lb_par, onw3, layer, bsz, n_chunk):
    t = proj.shape[0]
    tm = bsz * TILE_R
    depth = lb_par.shape[0]
    rows = lambda j: j
    in_specs = [_col_spec(HGRN_BLK, off, rows, tm) for off in (OFF_C, OFF_C + HGRN_BLK, OFF_ZC)]
    in_specs += [pl.BlockSpec((depth, W_C), lambda j: (0, 0)),
                 _param_spec(W_C, W_A + W_B, layer)]
    return pl.pallas_call(
        functools.partial(_hgrn_prompt_kernel, layer=layer),
        grid=(n_chunk,),
        in_specs=in_specs,
        out_specs=[
            pl.BlockSpec((tm, W_C), lambda j: (j, 0)),
            _whole(bsz, H_C, HEAD_DIM, HEAD_DIM),
        ],
        out_shape=[
            jax.ShapeDtypeStruct((t, W_C), BF16),
            jax.ShapeDtypeStruct((bsz, H_C, HEAD_DIM, HEAD_DIM), F32),
        ],
        scratch_shapes=[pltpu.VMEM((bsz, H_C, HEAD_DIM, HEAD_DIM), F32)],
        compiler_params=pltpu.CompilerParams(
            dimension_semantics=("arbitrary",), vmem_limit_bytes=VMEM_LIMIT),
        name="hgrn_prompt",
    )(proj, proj, proj, lb_par, onw3)


def _hgrn_sample(proj, y_prev, s_stack, lb_par, onw3, s_state, layer, dec_b, c, rows):
    nseq = TILE_R // c
    depth = lb_par.shape[0]
    in_specs = [_col_spec(HGRN_BLK, off, rows) for off in (OFF_C, OFF_C + HGRN_BLK, OFF_ZC)]
    in_specs += [
        pl.BlockSpec((depth, W_C), lambda i: (0, 0)),
        _param_spec(W_C, W_A + W_B, layer),
        pl.BlockSpec((None, nseq, H_C, HEAD_DIM, HEAD_DIM), lambda i: (layer, i, 0, 0, 0)),
    ]
    operands = (proj, proj, proj, lb_par, onw3, s_state)
    operands, extra, aliases, strip = _in_place(operands, len(in_specs), {0: y_prev, 1: s_stack})
    return pl.pallas_call(
        strip(functools.partial(_hgrn_sample_kernel, c=c, layer=layer)),
        grid=(dec_b // nseq,),
        in_specs=in_specs + extra,
        out_specs=[
            pl.BlockSpec((TILE_R, W_C), lambda i: (rows(i), 0)),
            pl.BlockSpec((None, nseq, H_C, HEAD_DIM, HEAD_DIM), lambda i: (layer, i, 0, 0, 0)),
        ],
        out_shape=[
            jax.ShapeDtypeStruct(y_prev.shape, BF16),
            jax.ShapeDtypeStruct((depth, dec_b, H_C, HEAD_DIM, HEAD_DIM), F32),
        ],
        input_output_aliases=aliases,
        compiler_params=pltpu.CompilerParams(
            dimension_semantics=("parallel",), vmem_limit_bytes=VMEM_LIMIT),
        name="hgrn_sample",
    )(*operands)


def _gate_weight(w_t):
    ai0 = 4 * W_A
    ba0 = ai0 + 2 * H_A + 3 * W_B
    n_gate = 2 * H_A + 2 * H_B
    pad = jnp.zeros((w_t.shape[0], LANES - n_gate, w_t.shape[2]), w_t.dtype)
    return jnp.concatenate([w_t[:, ai0:ai0 + 2 * H_A], w_t[:, ba0:ba0 + 2 * H_B], pad], axis=1)


def _lane_row(depth, pieces):
    row = jnp.zeros((depth, LANES), F32)
    for off, val in pieces:
        row = lax.dynamic_update_slice(row, val.astype(F32), (0, off))
    return row[:, None, :]


def kernel(x_prompt, x_sample, state_mlstm_C, state_mlstm_n, state_mlstm_m, state_gdn_S,
           state_gdn_conv, state_hgrn_S, meta_tokens, norm_w, w_in, mlstm_gate_b, gdn_A_log,
           gdn_dt_bias, gdn_conv_w, hgrn_lower_bounds, out_norm_w, w_out, final_norm_w):
    bsz, seq_len, d = x_prompt.shape
    dec_b, dec_seq, _ = x_sample.shape
    depth = w_in.shape[0]
    assert d == D_MODEL and seq_len % TILE_R == 0 and TILE_R % dec_seq == 0
    assert w_in.shape[2] == W_SEGMENTS[-1][1] + W_SEGMENTS[-1][2]
    nseq = TILE_R // dec_seq
    assert dec_b % nseq == 0

    chunks = seq_len // TILE_R
    n_chunk = chunks + 1
    tm = bsz * TILE_R
    n_dec = dec_b * dec_seq
    dec_blk0 = n_chunk * bsz

    def dec_rows(i):
        return dec_blk0 + i

    w_t = jnp.swapaxes(w_in.astype(F32), 1, 2)
    w_gate = _gate_weight(w_t)
    w_out_bf = w_out.astype(BF16)
    norm_w3 = norm_w.astype(F32)[:, None, :]
    onw3 = out_norm_w.astype(F32)[:, None, :]
    final_nw = final_norm_w.astype(F32)[None, :]
    gate_bias = _lane_row(depth, [(G_AI, mlstm_gate_b[:, 0]), (G_AF, mlstm_gate_b[:, 1])])
    gdn_par = jnp.concatenate([
        _lane_row(depth, [(G_BA, gdn_dt_bias)]), _lane_row(depth, [(G_BA, gdn_A_log)]),
        jnp.zeros((depth, 6, LANES), F32)], axis=1)
    conv_w = gdn_conv_w.astype(F32)
    lb_par = hgrn_lower_bounds.astype(F32)
    m_rows = jnp.repeat(
        jnp.pad(state_mlstm_m.astype(F32), ((0, 0), (0, 0), (0, LANES - H_A))), dec_seq, axis=1)

    x, h = _prep(x_prompt.astype(F32), x_sample.astype(F32), meta_tokens.astype(F32), norm_w3,
                 bsz, n_chunk)
    t_all = x.shape[0]
    tm_in = max(m for m in range(16, INPROJ_MAX_ROWS + 1, 16) if t_all % m == 0)

    new_p = [[] for _ in range(6)]
    new_s = [[] for _ in range(3)]
    c_s = s_s = h_s = None
    y_prompt = y_sample = None
    for l in range(depth):
        proj = _inproj(h, w_t, w_gate, l, tm_in)

        ya, c_p, n_p, m_p = _mlstm_prompt(proj, gate_bias, onw3, l, bsz, n_chunk)
        ya, c_s, n_s, m_s = _mlstm_sample(proj, ya, c_s, gate_bias, onw3, state_mlstm_C,
                                          state_mlstm_n, m_rows, l, dec_b, dec_seq, dec_rows)
        yb, s_p, cv_p = _gdn_prompt(proj, gdn_par, conv_w, onw3, l, bsz, n_chunk)
        yb, s_s, cv_s = _gdn_sample(proj, yb, s_s, gdn_par, conv_w, onw3, state_gdn_S,
                                    state_gdn_conv, l, dec_b, dec_seq, dec_rows)
        yc, h_p = _hgrn_prompt(proj, lb_par, onw3, l, bsz, n_chunk)
        yc, h_s = _hgrn_sample(proj, yc, h_s, lb_par, onw3, state_hgrn_S, l, dec_b, dec_seq,
                               dec_rows)

        for lst, val in zip(new_p, (c_p, n_p, m_p[:, 0, :H_A], s_p, cv_p, h_p)):
            lst.append(val)
        for lst, val in zip(new_s, (n_s, m_s[::dec_seq, :H_A], cv_s)):
            lst.append(val)

        if l + 1 < depth:
            x, h = _outproj_mid(ya, yb, yc, x, w_out_bf, l, norm_w3, tm)
        else:
            y_prompt = _outproj_final(
                ya, yb, yc, x, w_out_bf, l, final_nw, tm, 1, chunks, (bsz, chunks, TILE_R, d),
                pl.BlockSpec((bsz, None, TILE_R, d), lambda i: (0, i, 0, 0)))
            y_sample = _outproj_final(
                ya, yb, yc, x, w_out_bf, l, final_nw, tm, n_chunk, n_dec // tm, (n_dec, d),
                pl.BlockSpec((tm, d), lambda i: (i, 0)))

    outs_p = [jnp.stack(a, axis=0) for a in new_p]
    n_all, m_all, cv_all = [jnp.stack(a, axis=0) for a in new_s]
    return (y_prompt.reshape(bsz, seq_len, d), y_sample.reshape(dec_b, dec_seq, d),
            *outs_p, c_s, n_all, m_all, s_s, cv_all, h_s)
```
